```python
import math
import jax, jax.numpy as jnp
from jax import lax
import numpy as np

D_MODEL = 1024
BATCH = 4
SEQ = 4096
DEPTH = 1
DEC_BATCH = 128
DEC_SEQ = 4
PAST_LEN = 2048
PAGE_SIZE = 128

MIX_WIDTH = D_MODEL
ML_HEADS = 4
ML_WIDTH = MIX_WIDTH // 2
ML_HEAD_DIM = ML_WIDTH // ML_HEADS
ML_CHUNK = 64
ML_COLS = 4 * ML_WIDTH + 2 * ML_HEADS
DA_HEADS = 4
DA_WIDTH = MIX_WIDTH - ML_WIDTH
DA_V_DIM = DA_WIDTH // DA_HEADS
DA_QK_DIM = DA_V_DIM // 2
DA_COLS = 3 * DA_WIDTH
IN_COLS = ML_COLS + DA_COLS
ROPE_DIM = DA_QK_DIM // 4
ROPE_THETA = 500000.0
Q_BLOCK = 128
MEM_TOKENS = 256
MEM_HEADS = 4
MEM_HEAD_DIM = D_MODEL // MEM_HEADS
D_FF = (8 * D_MODEL // 3 + 63) // 64 * 64
CONV_W = 3
RMS_EPS = 1e-6

kernel_name = 'hymba_mlstm_diffattn_decoder'


def rms_norm(x, g):
    xf = x.astype(jnp.float32)
    y = xf * lax.rsqrt(jnp.mean(xf * xf, axis=-1, keepdims=True) + RMS_EPS)
    return (y * g.astype(jnp.float32)).astype(x.dtype)


def lambda_init_fn(layer_idx):
    return 0.8 - 0.6 * math.exp(-0.3 * layer_idx)


def rope_partial(x, pos):
    half = ROPE_DIM // 2
    inv = jnp.exp(-math.log(ROPE_THETA) * (2.0 * jnp.arange(half, dtype=jnp.float32) / ROPE_DIM))
    ang = pos.astype(jnp.float32)[:, None] * inv[None, :]
    ang = ang.reshape((1, pos.shape[0]) + (1,) * (x.ndim - 3) + (half,))
    cos, sin = jnp.cos(ang), jnp.sin(ang)
    x1 = x[..., :half].astype(jnp.float32)
    x2 = x[..., half:ROPE_DIM].astype(jnp.float32)
    rot = jnp.concatenate([x1 * cos - x2 * sin, x2 * cos + x1 * sin], axis=-1).astype(x.dtype)
    return jnp.concatenate([rot, x[..., ROPE_DIM:]], axis=-1)


def diff_attend(q, k, v, mask, lam):
    s = jnp.einsum('bqhcd,bkhcd->bhcqk', q.astype(jnp.float32), k.astype(jnp.float32)) * (DA_QK_DIM ** -0.5)
    s = jnp.where(mask[None, None, None], s, -jnp.inf)
    p = jax.nn.softmax(s, axis=-1)
    a = p[:, :, 0] - lam * p[:, :, 1]
    return jnp.einsum('bhqk,bkhd->bqhd', a, v.astype(jnp.float32))


def da_attend_prompt(q, k, v, lam):
    B, T = q.shape[0], q.shape[1]
    nb = T // Q_BLOCK
    qb = jnp.moveaxis(q.reshape(B, nb, Q_BLOCK, DA_HEADS, 2, DA_QK_DIM), 1, 0)
    kpos = jnp.arange(T)

    def blk(args):
        qi, bi = args
        qpos = bi * Q_BLOCK + jnp.arange(Q_BLOCK)
        return diff_attend(qi, k, v, kpos[None, :] <= qpos[:, None], lam)

    out = lax.map(blk, (qb, jnp.arange(nb)))
    return jnp.moveaxis(out, 0, 1).reshape(B, T, DA_HEADS, DA_V_DIM)


def da_attend_sample(q, k, v, lam, past_k, past_v):
    T, P = q.shape[1], past_k.shape[1]
    kk = jnp.concatenate([past_k.astype(k.dtype), k], axis=1)
    vv = jnp.concatenate([past_v.astype(v.dtype), v], axis=1)
    qpos = P + jnp.arange(T)
    kpos = jnp.arange(P + T)
    return diff_attend(q, kk, vv, kpos[None, :] <= qpos[:, None], lam)


def mlstm_chunkwise(q, k, v, i_pre, logf, C0, n0, m0):
    B, T, H, D = q.shape
    L = ML_CHUNK if T % ML_CHUNK == 0 else T
    nc = T // L

    def to_chunks(a):
        a = a.reshape((B, nc, L, H) + a.shape[3:])
        return jnp.moveaxis(jnp.moveaxis(a, 1, 0), 3, 2)

    qs = to_chunks(q.astype(jnp.float32))
    ks = to_chunks(k.astype(jnp.float32) * (D ** -0.5))
    vs = to_chunks(v.astype(jnp.float32))
    is_ = to_chunks(i_pre)
    fs = to_chunks(logf)
    causal = jnp.tril(jnp.ones((L, L), dtype=bool))

    def step(carry, inp):
        C, n, m = carry
        qc, kc, vc, ic, fc = inp
        b = jnp.cumsum(fc, axis=-1)
        logD = jnp.where(causal, b[..., :, None] - b[..., None, :] + ic[..., None, :], -jnp.inf)
        inter = b + m[..., None]
        m_row = jnp.maximum(jnp.max(logD, axis=-1), inter)
        w_inter = jnp.exp(inter - m_row)
        s = jnp.einsum('bhtd,bhsd->bhts', qc, kc) * jnp.exp(logD - m_row[..., None])
        num = w_inter[..., None] * jnp.einsum('bhtd,bhde->bhte', qc, C) + jnp.einsum('bhts,bhse->bhte', s, vc)
        den = w_inter * jnp.einsum('bhtd,bhd->bht', qc, n) + jnp.sum(s, axis=-1)
        h = num / jnp.maximum(jnp.abs(den), jnp.exp(-m_row))[..., None]
        bL = b[..., -1]
        log_w = bL[..., None] - b + ic
        m_new = jnp.maximum(bL + m, jnp.max(log_w, axis=-1))
        decay = jnp.exp(bL + m - m_new)
        wk = jnp.exp(log_w - m_new[..., None])[..., None] * kc
        C_new = decay[..., None, None] * C + jnp.einsum('bhsd,bhse->bhde', wk, vc)
        n_new = decay[..., None] * n + jnp.sum(wk, axis=2)
        return (C_new, n_new, m_new), h

    carry0 = (C0.astype(jnp.float32), n0.astype(jnp.float32), m0.astype(jnp.float32))
    (C, n, m), hs = lax.scan(step, carry0, (qs, ks, vs, is_, fs))
    h = jnp.swapaxes(jnp.moveaxis(hs, 0, 1), 2, 3).reshape(B, T, H, D)
    return h, C, n, m


def mem_kv(mem, g_mem_src, w_mk, w_mv):
    B = mem.shape[0]
    mn = rms_norm(mem, g_mem_src)
    mk = (mn @ w_mk).reshape(B, MEM_TOKENS, MEM_HEADS, MEM_HEAD_DIM)
    mv = (mn @ w_mv).reshape(B, MEM_TOKENS, MEM_HEADS, MEM_HEAD_DIM)
    return mk, mv


def mem_attend(h, mk, mv, w_mq, w_mo):
    B, T, _ = h.shape
    q = (h @ w_mq).reshape(B, T, MEM_HEADS, MEM_HEAD_DIM)
    s = jnp.einsum('bqhd,bkhd->bhqk', q.astype(jnp.float32), mk.astype(jnp.float32)) * (MEM_HEAD_DIM ** -0.5)
    p = jax.nn.softmax(s, axis=-1)
    o = jnp.einsum('bhqk,bkhd->bqhd', p, mv.astype(jnp.float32)).reshape(B, T, D_MODEL)
    return o.astype(h.dtype) @ w_mo


def conv_ffn(h, conv_buf, w_up, w_dw, b_dw, w_down):
    T = h.shape[1]
    u = h @ w_up
    ext = jnp.concatenate([conv_buf.astype(u.dtype), u], axis=1)
    c = b_dw
    for j in range(CONV_W):
        c = c + ext[:, j:j + T] * w_dw[j]
    a, g = c[..., :D_FF], c[..., D_FF:]
    y = (jax.nn.silu(g) * a) @ w_down
    return y, ext[:, ext.shape[1] - (CONV_W - 1):]


def layer(x, pos, attend, ml_C, ml_n, ml_m, mk, mv, conv_buf, lam_init,
          g_mix_pre, g_mix_post, w_in, b_if, g_ml_head, da_lambda, g_da_head, w_out,
          g_mem_pre, g_mem_post, w_mq, w_mo, g_ffn_pre, g_ffn_post, w_up, w_dw, b_dw, w_down):
    B, T, _ = x.shape
    h = rms_norm(x, g_mix_pre)
    z = h @ w_in
    zm, zd = z[..., :ML_COLS], z[..., ML_COLS:]
    q, k, v, o = [zm[..., j * ML_WIDTH:(j + 1) * ML_WIDTH].reshape(B, T, ML_HEADS, ML_HEAD_DIM) for j in range(4)]
    gates = (zm[..., 4 * ML_WIDTH:] + b_if).astype(jnp.float32)
    i_pre, logf = gates[..., :ML_HEADS], jax.nn.log_sigmoid(gates[..., ML_HEADS:])
    h_ml, C_new, n_new, m_new = mlstm_chunkwise(q, k, v, i_pre, logf, ml_C, ml_n, ml_m)
    h_ml = rms_norm(h_ml, g_ml_head) * jax.nn.sigmoid(o.astype(jnp.float32))
    dq = rope_partial(zd[..., :DA_WIDTH].reshape(B, T, DA_HEADS, 2, DA_QK_DIM), pos)
    dk = rope_partial(zd[..., DA_WIDTH:2 * DA_WIDTH].reshape(B, T, DA_HEADS, 2, DA_QK_DIM), pos)
    dv = zd[..., 2 * DA_WIDTH:].reshape(B, T, DA_HEADS, DA_V_DIM)
    lamv = da_lambda.astype(jnp.float32)
    lam = jnp.exp(jnp.sum(lamv[0] * lamv[1])) - jnp.exp(jnp.sum(lamv[2] * lamv[3])) + lam_init
    a = attend(dq, dk, dv, lam)
    h_da = rms_norm(a, g_da_head) * (1.0 - lam_init)
    mix = jnp.concatenate([h_ml.reshape(B, T, ML_WIDTH), h_da.reshape(B, T, DA_WIDTH)], axis=-1).astype(x.dtype)
    x = x + rms_norm(mix @ w_out, g_mix_post)
    x = x + rms_norm(mem_attend(rms_norm(x, g_mem_pre), mk, mv, w_mq, w_mo), g_mem_post)
    f, conv_new = conv_ffn(rms_norm(x, g_ffn_pre), conv_buf, w_up, w_dw, b_dw, w_down)
    x = x + rms_norm(f, g_ffn_post)
    return x, dk.reshape(B, T, DA_HEADS, 2 * DA_QK_DIM), dv, C_new, n_new, m_new, conv_new


def setup_inputs(seed: int = 0) -> dict:
    key = jax.random.key(seed)
    kit = iter(list(jax.random.split(key, 48)))
    f32 = jnp.float32

    def nrm(shape, scale=1.0):
        return scale * jax.random.normal(next(kit), shape, f32)

    def gain(shape):
        return 1.0 + nrm(shape, 0.05)

    L = DEPTH
    n_pages = PAST_LEN // PAGE_SIZE
    n_pool = (DEC_BATCH * n_pages * 5) // 4
    x_prompt = nrm((BATCH, SEQ, D_MODEL))
    x_sample = nrm((DEC_BATCH, DEC_SEQ, D_MODEL))
    cache_dk = nrm((L, n_pool, PAGE_SIZE, DA_HEADS, 2 * DA_QK_DIM))
    cache_dv = nrm((L, n_pool, PAGE_SIZE, DA_HEADS, DA_V_DIM))
    cache_mem_k = nrm((L, DEC_BATCH, MEM_TOKENS, MEM_HEADS, MEM_HEAD_DIM))
    cache_mem_v = nrm((L, DEC_BATCH, MEM_TOKENS, MEM_HEADS, MEM_HEAD_DIM))
    state_ml_C = nrm((L, DEC_BATCH, ML_HEADS, ML_HEAD_DIM, ML_HEAD_DIM), 0.5)
    state_ml_n = nrm((L, DEC_BATCH, ML_HEADS, ML_HEAD_DIM), 0.5)
    state_ml_m = nrm((L, DEC_BATCH, ML_HEADS))
    state_conv = nrm((L, DEC_BATCH, CONV_W - 1, 2 * D_FF))
    page_table = jax.random.permutation(next(kit), n_pool)[:DEC_BATCH * n_pages].reshape(DEC_BATCH, n_pages).astype(jnp.int32)
    mem_prompt = nrm((BATCH, MEM_TOKENS, D_MODEL))
    b_if = jnp.concatenate([nrm((L, ML_HEADS), 0.1), 3.0 + 3.0 * jax.random.uniform(next(kit), (L, ML_HEADS), f32)], axis=-1)
    return {
        'x_prompt': x_prompt, 'x_sample': x_sample,
        'cache_dk': cache_dk, 'cache_dv': cache_dv,
        'cache_mem_k': cache_mem_k, 'cache_mem_v': cache_mem_v,
        'state_ml_C': state_ml_C, 'state_ml_n': state_ml_n, 'state_ml_m': state_ml_m,
        'state_conv': state_conv, 'page_table': page_table, 'mem_prompt': mem_prompt,
        'g_mix_pre': gain((L, D_MODEL)), 'g_mix_post': gain((L, D_MODEL)),
        'w_in': nrm((L, D_MODEL, IN_COLS), D_MODEL ** -0.5),
        'b_if': b_if,
        'g_ml_head': gain((L, ML_HEADS, ML_HEAD_DIM)),
        'da_lambda': nrm((L, 4, DA_QK_DIM), 0.1),
        'g_da_head': gain((L, DA_HEADS, DA_V_DIM)),
        'w_out': nrm((L, MIX_WIDTH, D_MODEL), MIX_WIDTH ** -0.5),
        'g_mem_pre': gain((L, D_MODEL)), 'g_mem_post': gain((L, D_MODEL)), 'g_mem_src': gain((L, D_MODEL)),
        'w_mq': nrm((L, D_MODEL, D_MODEL), D_MODEL ** -0.5),
        'w_mk': nrm((L, D_MODEL, D_MODEL), D_MODEL ** -0.5),
        'w_mv': nrm((L, D_MODEL, D_MODEL), D_MODEL ** -0.5),
        'w_mo': nrm((L, D_MODEL, D_MODEL), D_MODEL ** -0.5),
        'g_ffn_pre': gain((L, D_MODEL)), 'g_ffn_post': gain((L, D_MODEL)),
        'w_up': nrm((L, D_MODEL, 2 * D_FF), D_MODEL ** -0.5),
        'w_dw': nrm((L, CONV_W, 2 * D_FF), CONV_W ** -0.5),
        'b_dw': nrm((L, 2 * D_FF), 0.01),
        'w_down': nrm((L, D_FF, D_MODEL), D_FF ** -0.5),
    }


def reference(x_prompt, x_sample, cache_dk, cache_dv, cache_mem_k, cache_mem_v,
              state_ml_C, state_ml_n, state_ml_m, state_conv, page_table, mem_prompt,
              g_mix_pre, g_mix_post, w_in, b_if, g_ml_head, da_lambda, g_da_head, w_out,
              g_mem_pre, g_mem_post, g_mem_src, w_mq, w_mk, w_mv, w_mo,
              g_ffn_pre, g_ffn_post, w_up, w_dw, b_dw, w_down):
    Bp, Tp = x_prompt.shape[0], x_prompt.shape[1]
    Bs, Ts = x_sample.shape[0], x_sample.shape[1]
    n_pages = page_table.shape[1]
    past_len = n_pages * cache_dk.shape[2]
    pos_p = jnp.arange(Tp)
    pos_s = past_len + jnp.arange(Ts)
    yp, ys = x_prompt, x_sample
    p_dk, p_dv, p_mk, p_mv, p_C, p_n, p_m, p_conv = [], [], [], [], [], [], [], []
    s_dk, s_dv, s_C, s_n, s_m, s_conv = [], [], [], [], [], []
    for l in range(DEPTH):
        lam_init = lambda_init_fn(l)
        lw = (g_mix_pre[l], g_mix_post[l], w_in[l], b_if[l], g_ml_head[l], da_lambda[l], g_da_head[l], w_out[l],
              g_mem_pre[l], g_mem_post[l], w_mq[l], w_mo[l], g_ffn_pre[l], g_ffn_post[l],
              w_up[l], w_dw[l], b_dw[l], w_down[l])
        mk_p, mv_p = mem_kv(mem_prompt, g_mem_src[l], w_mk[l], w_mv[l])
        yp, dk_p, dv_p, C_p, n_p, m_p, cv_p = layer(
            yp, pos_p, da_attend_prompt,
            jnp.zeros((Bp, ML_HEADS, ML_HEAD_DIM, ML_HEAD_DIM), jnp.float32),
            jnp.zeros((Bp, ML_HEADS, ML_HEAD_DIM), jnp.float32),
            jnp.zeros((Bp, ML_HEADS), jnp.float32),
            mk_p, mv_p, jnp.zeros((Bp, CONV_W - 1, 2 * D_FF), x_prompt.dtype), lam_init, *lw)
        past_k = cache_dk[l][page_table].reshape(Bs, past_len, DA_HEADS, 2, DA_QK_DIM)
        past_v = cache_dv[l][page_table].reshape(Bs, past_len, DA_HEADS, DA_V_DIM)

        def attend_s(q, k, v, lam, past_k=past_k, past_v=past_v):
            return da_attend_sample(q, k, v, lam, past_k, past_v)

        ys, dk_s, dv_s, C_s, n_s, m_s, cv_s = layer(
            ys, pos_s, attend_s, state_ml_C[l], state_ml_n[l], state_ml_m[l],
            cache_mem_k[l], cache_mem_v[l], state_conv[l], lam_init, *lw)
        p_dk.append(dk_p); p_dv.append(dv_p); p_mk.append(mk_p); p_mv.append(mv_p)
        p_C.append(C_p); p_n.append(n_p); p_m.append(m_p); p_conv.append(cv_p)
        s_dk.append(dk_s); s_dv.append(dv_s); s_C.append(C_s); s_n.append(n_s); s_m.append(m_s); s_conv.append(cv_s)
    return (yp, ys,
            jnp.stack(p_dk), jnp.stack(p_dv), jnp.stack(p_mk), jnp.stack(p_mv),
            jnp.stack(p_C), jnp.stack(p_n), jnp.stack(p_m), jnp.stack(p_conv),
            jnp.stack(s_dk), jnp.stack(s_dv), jnp.stack(s_C), jnp.stack(s_n), jnp.stack(s_m), jnp.stack(s_conv))
```

```python
import functools
import math

import jax
import jax.numpy as jnp
from jax import lax
from jax.experimental import pallas as pl
from jax.experimental.pallas import tpu as pltpu

F32 = jnp.float32
BF16 = jnp.bfloat16

ML_HEADS = 4
ML_HEAD_DIM = 128
ML_WIDTH = ML_HEADS * ML_HEAD_DIM
DA_HEADS = 4
DA_V_DIM = 128
DA_QK_DIM = 64
DA_WIDTH = DA_HEADS * DA_V_DIM
ROPE_DIM = 16
ROPE_THETA = 500000.0
MEM_HEADS = 4
CONV_W = 3
RMS_EPS = 1e-6
LANES = 128
NEG_BIG = -1e30
VMEM_LIMIT = 56 * 1024 * 1024


def _params(*sem, vmem=VMEM_LIMIT):
    return pltpu.CompilerParams(dimension_semantics=sem, vmem_limit_bytes=vmem)


def _rms(x, g):
    return x * lax.rsqrt(jnp.mean(x * x, axis=-1, keepdims=True) + RMS_EPS) * g


def _log_sigmoid(x):
    return jnp.minimum(x, 0.0) - jnp.log1p(jnp.exp(-jnp.abs(x)))


def _dot(a, b):
    return jnp.dot(a, b, preferred_element_type=F32)


def _dot_nt(a, b):
    return lax.dot_general(a, b, (((1,), (1,)), ((), ())), preferred_element_type=F32)


def _dot_tn(a, b):
    return lax.dot_general(a, b, (((0,), (0,)), ((), ())), preferred_element_type=F32)


def _rope_table_kernel(cos_ref, sa_ref, sb_ref, *, period, offset):
    rows = cos_ref.shape[0]
    half = ROPE_DIM // 2
    r = lax.broadcasted_iota(jnp.int32, (rows, LANES), 0) + pl.program_id(0) * rows
    lane = lax.broadcasted_iota(jnp.int32, (rows, LANES), 1)
    pos = (offset + r % period).astype(F32)
    c = lane % DA_QK_DIM
    j = (c % half).astype(F32)
    inv = jnp.exp(-math.log(ROPE_THETA) * (2.0 * j / ROPE_DIM))
    ang = pos * inv
    cos, sin = jnp.cos(ang), jnp.sin(ang)
    cos_ref[...] = jnp.where(c < ROPE_DIM, cos, 1.0)
    sa_ref[...] = jnp.where(c < half, -sin, 0.0)
    sb_ref[...] = jnp.where((c >= half) & (c < ROPE_DIM), sin, 0.0)


def _rope_tables(rows, period, offset):
    blk = min(rows, 512)
    spec = pl.BlockSpec((blk, LANES), lambda i: (i, 0))
    shp = jax.ShapeDtypeStruct((rows, LANES), F32)
    return pl.pallas_call(
        functools.partial(_rope_table_kernel, period=period, offset=offset),
        grid=(rows // blk,), out_specs=[spec] * 3, out_shape=[shp] * 3,
        compiler_params=_params("parallel"), name="rope_tables")()


def _rope(x, cos, sa, sb):
    outs = []
    for j in range(x.shape[1] // LANES):
        xj = x[:, j * LANES:(j + 1) * LANES]
        up = pltpu.roll(xj, LANES - ROPE_DIM // 2, axis=1)
        dn = pltpu.roll(xj, ROPE_DIM // 2, axis=1)
        outs.append(xj * cos + up * sa + dn * sb)
    return outs


def _in_proj_kernel(x_ref, g_ref, w_ref, wg_ref, wgt_ref, bcol_ref, brow_ref, cos_ref, sa_ref, sb_ref,
                    q_ref, k_ref, v_ref, o_ref, gc_ref, gr_ref, dq_ref, dk_ref, dv_ref):
    h = _rms(x_ref[...], g_ref[...]).astype(BF16)
    for j, ref in enumerate((q_ref, k_ref, v_ref, o_ref)):
        ref[...] = _dot(h, w_ref[:, j * ML_WIDTH:(j + 1) * ML_WIDTH]).astype(ref.dtype)
    gc = _dot(h, wg_ref[...]) + bcol_ref[...]
    lane = lax.broadcasted_iota(jnp.int32, gc.shape, 1)
    gc_ref[...] = jnp.where(lane < ML_HEADS, gc, _log_sigmoid(gc))
    gr = _dot_nt(wgt_ref[...], h) + brow_ref[...]
    row = lax.broadcasted_iota(jnp.int32, gr.shape, 0)
    gr_ref[...] = jnp.where(row < ML_HEADS, gr, _log_sigmoid(gr))
    base = 4 * ML_WIDTH
    cos, sa, sb = cos_ref[...], sa_ref[...], sb_ref[...]
    dq = _rope(_dot(h, w_ref[:, base:base + DA_WIDTH]), cos, sa, sb)
    dk = _rope(_dot(h, w_ref[:, base + DA_WIDTH:base + 2 * DA_WIDTH]), cos, sa, sb)
    for j in range(DA_HEADS):
        dq_ref[:, j * LANES:(j + 1) * LANES] = dq[j].astype(dq_ref.dtype)
        dk_ref[:, j * LANES:(j + 1) * LANES] = dk[j]
    dv_ref[...] = _dot(h, w_ref[:, base + 2 * DA_WIDTH:base + 3 * DA_WIDTH])


def _in_proj(x, g, w_main, w_g, w_gt, b_col, b_row, tables, act_dtype, tm):
    rows, d = x.shape
    cos, sa, sb = tables
    nt = cos.shape[0] // tm
    row_spec = lambda w: pl.BlockSpec((tm, w), lambda i: (i, 0))
    full = lambda a: pl.BlockSpec(a.shape, lambda i: (0,) * a.ndim)
    tab_spec = pl.BlockSpec((tm, LANES), lambda i: (i % nt, 0))
    out_shape = [jax.ShapeDtypeStruct((rows, ML_WIDTH), act_dtype)] * 4 + [
        jax.ShapeDtypeStruct((rows, LANES), F32), jax.ShapeDtypeStruct((8, rows), F32),
        jax.ShapeDtypeStruct((rows, DA_WIDTH), act_dtype),
        jax.ShapeDtypeStruct((rows, DA_WIDTH), F32), jax.ShapeDtypeStruct((rows, DA_WIDTH), F32)]
    out_specs = [row_spec(ML_WIDTH)] * 4 + [row_spec(LANES), pl.BlockSpec((8, tm), lambda i: (0, i)),
                                            row_spec(DA_WIDTH), row_spec(DA_WIDTH), row_spec(DA_WIDTH)]
    return pl.pallas_call(
        _in_proj_kernel, grid=(rows // tm,),
        in_specs=[row_spec(d), full(g), full(w_main), full(w_g), full(w_gt), full(b_col), full(b_row),
                  tab_spec, tab_spec, tab_spec],
        out_specs=out_specs, out_shape=out_shape,
        compiler_params=_params("parallel"), name="in_proj")(
            x, g, w_main, w_g, w_gt, b_col, b_row, cos, sa, sb)


def _mlstm_head(q, k, v, i_col, f_col, i_row, f_row, C, n, m):
    L, D = q.shape
    scale = D ** -0.5
    r = lax.broadcasted_iota(jnp.int32, (L, L), 0)
    c = lax.broadcasted_iota(jnp.int32, (L, L), 1)
    tri = c <= r
    b_col = jnp.sum(jnp.where(tri, f_row, 0.0), axis=1, keepdims=True)
    b_row = jnp.sum(jnp.where(r <= c, f_col, 0.0), axis=0, keepdims=True)
    log_d = jnp.where(tri, b_col - b_row + i_row, -jnp.inf)
    inter = b_col + m
    m_row = jnp.maximum(jnp.max(log_d, axis=1, keepdims=True), inter)
    w_inter = jnp.exp(inter - m_row)
    qb, kb, vb = q.astype(BF16), k.astype(BF16), v.astype(BF16)
    qf, kf = q.astype(F32), k.astype(F32)
    s = _dot_nt(qb, kb) * (scale * jnp.exp(log_d - m_row))
    num = w_inter * _dot(qb, C.astype(BF16)) + _dot(s.astype(BF16), vb)
    den = w_inter * jnp.sum(qf * n, axis=1, keepdims=True) + jnp.sum(s, axis=1, keepdims=True)
    h = num / jnp.maximum(jnp.abs(den), jnp.exp(-m_row))
    b_last = b_col[L - 1:L, :]
    log_w = b_last - b_col + i_col
    m_new = jnp.maximum(b_last + m, jnp.max(log_w, axis=0, keepdims=True))
    decay = jnp.exp(b_last + m - m_new)
    wk = (scale * jnp.exp(log_w - m_new)) * kf
    c_new = decay * C + _dot_tn(wk.astype(BF16), vb)
    n_new = decay * n + jnp.sum(wk, axis=0, keepdims=True)
    return h, c_new, n_new, m_new


def _ml_head_out(h, o, g):
    return _rms(h, g) * jax.nn.sigmoid(o.astype(F32))


def _mlstm_prompt_kernel(q_ref, k_ref, v_ref, o_ref, gc_ref, gr_ref, gh_ref,
                         h_ref, c_out, n_out, m_out, c_scr, n_scr, m_scr):
    ci = pl.program_id(1)

    @pl.when(ci == 0)
    def _():
        c_scr[...] = jnp.zeros_like(c_scr)
        n_scr[...] = jnp.zeros_like(n_scr)
        m_scr[...] = jnp.zeros_like(m_scr)

    gc, gr = gc_ref[...], gr_ref[...]
    for hh in range(ML_HEADS):
        cols = slice(hh * ML_HEAD_DIM, (hh + 1) * ML_HEAD_DIM)
        h, c_new, n_new, m_new = _mlstm_head(
            q_ref[:, cols], k_ref[:, cols], v_ref[:, cols],
            gc[:, hh:hh + 1], gc[:, ML_HEADS + hh:ML_HEADS + hh + 1],
            gr[hh:hh + 1, :], gr[ML_HEADS + hh:ML_HEADS + hh + 1, :],
            c_scr[hh], n_scr[hh], m_scr[hh])
        c_scr[hh], n_scr[hh], m_scr[hh] = c_new, n_new, m_new
        h_ref[:, cols] = _ml_head_out(h, o_ref[:, cols], gh_ref[hh:hh + 1, :]).astype(h_ref.dtype)

    @pl.when(ci == pl.num_programs(1) - 1)
    def _():
        c_out[0] = c_scr[...]
        n_out[0] = n_scr[...]
        m_out[0] = m_scr[...]


def _mlstm_prompt(q, k, v, o, gc, gr, g_head, B, T, L):
    nc = T // L
    rows = B * T
    blk = lambda w: pl.BlockSpec((L, w), lambda b, c: (b * nc + c, 0))
    H, D = ML_HEADS, ML_HEAD_DIM
    return pl.pallas_call(
        _mlstm_prompt_kernel, grid=(B, nc),
        in_specs=[blk(ML_WIDTH)] * 4 + [blk(LANES), pl.BlockSpec((8, L), lambda b, c: (0, b * nc + c)),
                                        pl.BlockSpec(g_head.shape, lambda b, c: (0, 0))],
        out_specs=[blk(ML_WIDTH),
                   pl.BlockSpec((1, H, D, D), lambda b, c: (b, 0, 0, 0)),
                   pl.BlockSpec((1, H, 1, D), lambda b, c: (b, 0, 0, 0)),
                   pl.BlockSpec((1, H, 1, 1), lambda b, c: (b, 0, 0, 0))],
        out_shape=[jax.ShapeDtypeStruct((rows, ML_WIDTH), BF16),
                   jax.ShapeDtypeStruct((B, H, D, D), F32),
                   jax.ShapeDtypeStruct((B, H, 1, D), F32),
                   jax.ShapeDtypeStruct((B, H, 1, 1), F32)],
        scratch_shapes=[pltpu.VMEM((H, D, D), F32), pltpu.VMEM((H, 1, D), F32), pltpu.VMEM((H, 1, 1), F32)],
        compiler_params=_params("parallel", "arbitrary"), name="mlstm_prompt")(q, k, v, o, gc, gr, g_head)


def _mlstm_sample_kernel(q_ref, k_ref, v_ref, o_ref, gc_ref, gr_ref, gh_ref, c_in, n_in, m_in,
                         h_ref, c_out, n_out, m_out):
    def body(bi, carry):
        gc, gr = gc_ref[bi], gr_ref[bi]
        for hh in range(ML_HEADS):
            cols = slice(hh * ML_HEAD_DIM, (hh + 1) * ML_HEAD_DIM)
            h, c_new, n_new, m_new = _mlstm_head(
                q_ref[bi, :, cols], k_ref[bi, :, cols], v_ref[bi, :, cols],
                gc[:, hh:hh + 1], gc[:, ML_HEADS + hh:ML_HEADS + hh + 1],
                gr[hh:hh + 1, :], gr[ML_HEADS + hh:ML_HEADS + hh + 1, :],
                c_in[bi, hh], n_in[bi, hh], m_in[bi, hh])
            c_out[bi, hh], n_out[bi, hh], m_out[bi, hh] = c_new, n_new, m_new
            h_ref[bi, :, cols] = _ml_head_out(h, o_ref[bi, :, cols], gh_ref[hh:hh + 1, :]).astype(h_ref.dtype)
        return carry

    lax.fori_loop(0, q_ref.shape[0], body, 0)


def _mlstm_sample(q, k, v, o, gc, gr, g_head, c0, n0, m0, bb):
    B, T, _ = q.shape
    H, D = ML_HEADS, ML_HEAD_DIM
    b3 = lambda s: pl.BlockSpec((bb,) + s, lambda i: (i, 0, 0))
    b4 = lambda s: pl.BlockSpec((bb,) + s, lambda i: (i, 0, 0, 0))
    return pl.pallas_call(
        _mlstm_sample_kernel, grid=(B // bb,),
        in_specs=[b3((T, ML_WIDTH))] * 4 + [b3((T, LANES)), b3((8, T)),
                                            pl.BlockSpec(g_head.shape, lambda i: (0, 0)),
                                            b4((H, D, D)), b4((H, 1, D)), b4((H, 1, 1))],
        out_specs=[b3((T, ML_WIDTH)), b4((H, D, D)), b4((H, 1, D)), b4((H, 1, 1))],
        out_shape=[jax.ShapeDtypeStruct((B, T, ML_WIDTH), BF16),
                   jax.ShapeDtypeStruct((B, H, D, D), F32),
                   jax.ShapeDtypeStruct((B, H, 1, D), F32),
                   jax.ShapeDtypeStruct((B, H, 1, 1), F32)],
        compiler_params=_params("parallel"), name="mlstm_sample")(q, k, v, o, gc, gr, g_head, c0, n0, m0)


def _da_lambda(lam_ref, lam_init):
    lv = lam_ref[...]
    a = jnp.sum(lv[0:1, :] * lv[1:2, :], axis=1, keepdims=True)
    b = jnp.sum(lv[2:3, :] * lv[3:4, :], axis=1, keepdims=True)
    return jnp.exp(a) - jnp.exp(b) + lam_init


def _stack_components(q):
    lane = lax.broadcasted_iota(jnp.int32, q.shape, 1)
    zero = jnp.zeros_like(q)
    return jnp.concatenate([jnp.where(lane < DA_QK_DIM, q, zero), jnp.where(lane >= DA_QK_DIM, q, zero)], axis=0)


def _da_prompt_kernel(q_ref, k_ref, v_ref, lam_ref, gh_ref, out_ref, kb_scr, vb_scr, m_scr, l_scr, acc_scr,
                      *, lam_init):
    i = pl.program_id(2)
    tq = q_ref.shape[0]
    scale = DA_QK_DIM ** -0.5

    @pl.when(i == 0)
    def _():
        kb_scr[...] = k_ref[...].astype(BF16)
        vb_scr[...] = v_ref[...].astype(BF16)

    qq = _stack_components(q_ref[...])
    m_scr[...] = jnp.full_like(m_scr, NEG_BIG)
    l_scr[...] = jnp.zeros_like(l_scr)
    acc_scr[...] = jnp.zeros_like(acc_scr)
    qpos = i * tq + lax.broadcasted_iota(jnp.int32, (2 * tq, tq), 0) % tq

    def body(j, carry):
        off = pl.multiple_of(j * tq, tq)
        s = _dot_nt(qq, kb_scr[pl.ds(off, tq), :]) * scale
        kpos = off + lax.broadcasted_iota(jnp.int32, (2 * tq, tq), 1)
        s = jnp.where(kpos <= qpos, s, NEG_BIG)
        m_old = m_scr[...]
        m_new = jnp.maximum(m_old, jnp.max(s, axis=1, keepdims=True))
        alpha = jnp.exp(m_old - m_new)
        p = jnp.exp(s - m_new)
        l_scr[...] = alpha * l_scr[...] + jnp.sum(p, axis=1, keepdims=True)
        acc_scr[...] = alpha * acc_scr[...] + _dot(p.astype(BF16), vb_scr[pl.ds(off, tq), :])
        m_scr[...] = m_new
        return carry

    lax.fori_loop(0, i + 1, body, 0)
    o = acc_scr[...] / l_scr[...]
    a = o[:tq] - _da_lambda(lam_ref, lam_init) * o[tq:]
    out_ref[...] = (_rms(a, gh_ref[0]) * (1.0 - lam_init)).astype(out_ref.dtype)


def _da_prompt(dq, dk, dv, da_lambda, g_head3, B, T, tq, lam_init):
    nq = T // tq
    rows = B * T
    kv_spec = pl.BlockSpec((T, DA_V_DIM), lambda b, h, i: (b, h))
    return pl.pallas_call(
        functools.partial(_da_prompt_kernel, lam_init=lam_init), grid=(B, DA_HEADS, nq),
        in_specs=[pl.BlockSpec((tq, DA_V_DIM), lambda b, h, i: (b * nq + i, h)), kv_spec, kv_spec,
                  pl.BlockSpec(da_lambda.shape, lambda b, h, i: (0, 0)),
                  pl.BlockSpec((1, 1, DA_V_DIM), lambda b, h, i: (h, 0, 0))],
        out_specs=pl.BlockSpec((tq, DA_V_DIM), lambda b, h, i: (b * nq + i, h)),
        out_shape=jax.ShapeDtypeStruct((rows, DA_WIDTH), BF16),
        scratch_shapes=[pltpu.VMEM((T, DA_V_DIM), BF16), pltpu.VMEM((T, DA_V_DIM), BF16),
                        pltpu.VMEM((2 * tq, 1), F32), pltpu.VMEM((2 * tq, 1), F32),
                        pltpu.VMEM((2 * tq, DA_V_DIM), F32)],
        compiler_params=_params("parallel", "parallel", "arbitrary"), name="da_prompt")(
            dq, dk, dv, da_lambda, g_head3)


def _da_sample_kernel(pt_ref, q_ref, kn_ref, vn_ref, lam_ref, gh_ref, ck_hbm, cv_hbm, out_ref,
                      kbuf, vbuf, sem, *, lam_init):
    b = pl.program_id(0)
    nb = pl.num_programs(0)
    n_pages, page = kbuf.shape[1], kbuf.shape[2]
    T = q_ref.shape[1]
    scale = DA_QK_DIM ** -0.5

    def copies(bi, slot):
        out = []
        for p in range(n_pages):
            pg = pt_ref[bi, p]
            out.append(pltpu.make_async_copy(ck_hbm.at[pg], kbuf.at[slot, p], sem.at[slot, 0]))
            out.append(pltpu.make_async_copy(cv_hbm.at[pg], vbuf.at[slot, p], sem.at[slot, 1]))
        return out

    @pl.when(b == 0)
    def _():
        for cp in copies(0, 0):
            cp.start()

    @pl.when(b + 1 < nb)
    def _():
        for cp in copies(b + 1, (b + 1) % 2):
            cp.start()

    slot = b % 2
    for cp in copies(b, slot):
        cp.wait()

    lam = _da_lambda(lam_ref, lam_init)
    q_all, kn_all, vn_all = q_ref[0], kn_ref[0], vn_ref[0]
    trow = lax.broadcasted_iota(jnp.int32, (2 * T, 1), 0) % T
    for hh in range(DA_HEADS):
        cols = slice(hh * DA_V_DIM, (hh + 1) * DA_V_DIM)
        qq = _stack_components(q_all[:, cols])
        kp = kbuf[slot, :, :, cols].reshape(n_pages * page, DA_V_DIM).astype(BF16)
        vp = vbuf[slot, :, :, cols].reshape(n_pages * page, DA_V_DIM).astype(BF16)
        s_past = _dot_nt(qq.astype(BF16), kp) * scale
        kn, vn = kn_all[:, cols], vn_all[:, cols]
        s_new = [jnp.where(trow >= t, jnp.sum(qq * kn[t:t + 1, :], axis=1, keepdims=True) * scale, NEG_BIG)
                 for t in range(T)]
        m = jnp.max(s_past, axis=1, keepdims=True)
        for t in range(T):
            m = jnp.maximum(m, s_new[t])
        p_past = jnp.exp(s_past - m)
        l = jnp.sum(p_past, axis=1, keepdims=True)
        acc = _dot(p_past.astype(BF16), vp)
        for t in range(T):
            p_t = jnp.exp(s_new[t] - m)
            l = l + p_t
            acc = acc + p_t * vn[t:t + 1, :]
        o = acc / l
        a = o[:T] - lam * o[T:]
        out_ref[0, :, cols] = (_rms(a, gh_ref[hh:hh + 1, :]) * (1.0 - lam_init)).astype(out_ref.dtype)


def _da_sample(page_table, dq, dk, dv, da_lambda, g_head, cache_k, cache_v, lam_init):
    B, T, _ = dq.shape
    n_pages = page_table.shape[1]
    page = cache_k.shape[1]
    blk = pl.BlockSpec((1, T, DA_WIDTH), lambda b, pt: (b, 0, 0))
    grid_spec = pltpu.PrefetchScalarGridSpec(
        num_scalar_prefetch=1, grid=(B,),
        in_specs=[blk, blk, blk, pl.BlockSpec(da_lambda.shape, lambda b, pt: (0, 0)),
                  pl.BlockSpec(g_head.shape, lambda b, pt: (0, 0)),
                  pl.BlockSpec(memory_space=pl.ANY), pl.BlockSpec(memory_space=pl.ANY)],
        out_specs=blk,
        scratch_shapes=[pltpu.VMEM((2, n_pages, page, DA_WIDTH), F32),
                        pltpu.VMEM((2, n_pages, page, DA_WIDTH), F32),
                        pltpu.SemaphoreType.DMA((2, 2))])
    return pl.pallas_call(
        functools.partial(_da_sample_kernel, lam_init=lam_init), grid_spec=grid_spec,
        out_shape=jax.ShapeDtypeStruct((B, T, DA_WIDTH), BF16),
        compiler_params=_params("arbitrary"), name="da_sample")(
            page_table, dq, dk, dv, da_lambda, g_head, cache_k, cache_v)


def _proj_norm_kernel(*refs, n_in, has_next):
    a_refs, w_refs = refs[:n_in], refs[n_in:2 * n_in]
    x_ref, gpost_ref, gpre_ref = refs[2 * n_in:2 * n_in + 3]
    rest = refs[2 * n_in + 3:]
    wn_ref = rest[0] if has_next else None
    xo_ref, ho_ref = rest[-2:]
    acc = _dot(a_refs[0][...], w_refs[0][...])
    for a, w in zip(a_refs[1:], w_refs[1:]):
        acc = acc + _dot(a[...], w[...])
    x1 = x_ref[...] + _rms(acc, gpost_ref[...])
    xo_ref[...] = x1
    hn = _rms(x1, gpre_ref[...]).astype(BF16)
    ho_ref[...] = (_dot(hn, wn_ref[...]) if has_next else hn).astype(ho_ref.dtype)


def _proj_norm(a_list, w_list, x, g_post, g_pre, w_next, tm, h_dtype=BF16):
    rows, d = x.shape
    n_in = len(a_list)
    has_next = w_next is not None
    row_spec = lambda w: pl.BlockSpec((tm, w), lambda i: (i, 0))
    full = lambda a: pl.BlockSpec(a.shape, lambda i: (0,) * a.ndim)
    ins = list(a_list) + list(w_list) + [x, g_post, g_pre] + ([w_next] if has_next else [])
    in_specs = ([row_spec(a.shape[1]) for a in a_list] + [full(w) for w in w_list]
                + [row_spec(d), full(g_post), full(g_pre)] + ([full(w_next)] if has_next else []))
    n_out = w_next.shape[1] if has_next else d
    return pl.pallas_call(
        functools.partial(_proj_norm_kernel, n_in=n_in, has_next=has_next), grid=(rows // tm,),
        in_specs=in_specs, out_specs=[row_spec(d), row_spec(n_out)],
        out_shape=[jax.ShapeDtypeStruct((rows, d), F32), jax.ShapeDtypeStruct((rows, n_out), h_dtype)],
        compiler_params=_params("parallel"), name="proj_norm")(*ins)


def _norm_matmul_kernel(x_ref, g_ref, w_ref, o_ref):
    o_ref[...] = _dot(_rms(x_ref[...], g_ref[...]).astype(BF16), w_ref[...])


def _norm_matmul(x, g, w, tm, tn):
    rows, d = x.shape
    n = w.shape[1]
    return pl.pallas_call(
        _norm_matmul_kernel, grid=(rows // tm, n // tn),
        in_specs=[pl.BlockSpec((tm, d), lambda i, j: (i, 0)), pl.BlockSpec(g.shape, lambda i, j: (0, 0)),
                  pl.BlockSpec((d, tn), lambda i, j: (0, j))],
        out_specs=pl.BlockSpec((tm, tn), lambda i, j: (i, j)),
        out_shape=jax.ShapeDtypeStruct((rows, n), F32),
        compiler_params=_params("parallel", "parallel"), name="mem_kv")(x, g, w)


def _mem_attn_kernel(q_ref, mk_ref, mv_ref, o_ref):
    hd = q_ref.shape[2] // MEM_HEADS
    scale = hd ** -0.5
    for bi in range(q_ref.shape[0]):
        for hh in range(MEM_HEADS):
            cols = slice(hh * hd, (hh + 1) * hd)
            s = _dot_nt(q_ref[bi, :, cols].astype(BF16), mk_ref[bi, :, cols].astype(BF16)) * scale
            p = jnp.exp(s - jnp.max(s, axis=1, keepdims=True))
            o = _dot(p.astype(BF16), mv_ref[bi, :, cols].astype(BF16)) / jnp.sum(p, axis=1, keepdims=True)
            o_ref[bi, :, cols] = o.astype(o_ref.dtype)


def _mem_attn(q, mk, mv, bb, tm):
    B, T, d = q.shape
    M = mk.shape[1]
    kv_spec = pl.BlockSpec((bb, M, d), lambda b, i: (b, 0, 0))
    return pl.pallas_call(
        _mem_attn_kernel, grid=(B // bb, T // tm),
        in_specs=[pl.BlockSpec((bb, tm, d), lambda b, i: (b, i, 0)), kv_spec, kv_spec],
        out_specs=pl.BlockSpec((bb, tm, d), lambda b, i: (b, i, 0)),
        out_shape=jax.ShapeDtypeStruct((B, T, d), BF16),
        compiler_params=_params("parallel", "parallel"), name="mem_attn")(q, mk, mv)


def _ffn_chunks(nh):
    nchunk = 2 if (nh // LANES) % 2 == 0 else 1
    cw = nh // nchunk
    return [(j * cw, cw) for j in range(nchunk)]


def _ffn_prompt_kernel(hf_ref, x_ref, wup_ref, wdw_ref, bdw_ref, wdn_ref, gpost_ref, y_ref, ulast_ref, ubuf,
                       *, tiles_per_seq):
    i = pl.program_id(0)
    tm = hf_ref.shape[0]
    nh = wdn_ref.shape[0]
    halo = ubuf.shape[0] - tm

    @pl.when(i % tiles_per_seq == 0)
    def _():
        ubuf[0:halo, :] = jnp.zeros((halo, ubuf.shape[1]), F32)

    hf = hf_ref[...]
    f = jnp.zeros((tm, y_ref.shape[1]), F32)
    for c0, cw in _ffn_chunks(nh):
        cg = []
        for base in (c0, nh + c0):
            cs = slice(base, base + cw)
            ubuf[halo:halo + tm, cs] = _dot(hf, wup_ref[:, cs])
            c = bdw_ref[:, cs]
            for j in range(CONV_W):
                lo = halo - (CONV_W - 1) + j
                c = c + ubuf[lo:lo + tm, cs] * wdw_ref[j:j + 1, cs]
            cg.append(c)
        act = (jax.nn.silu(cg[1]) * cg[0]).astype(BF16)
        f = f + _dot(act, wdn_ref[c0:c0 + cw, :])
    y_ref[...] = x_ref[...] + _rms(f, gpost_ref[...])
    tail = ubuf[tm:tm + halo, :]
    ubuf[0:halo, :] = tail
    ulast_ref[0] = tail


def _ffn_prompt(hf, x, w_up, w_dw, b_dw, w_down, g_post, B, T, tm):
    rows, d = x.shape
    npad = w_up.shape[1]
    tiles = T // tm
    halo = 8
    row_spec = lambda w: pl.BlockSpec((tm, w), lambda i: (i, 0))
    full = lambda a: pl.BlockSpec(a.shape, lambda i: (0,) * a.ndim)
    return pl.pallas_call(
        functools.partial(_ffn_prompt_kernel, tiles_per_seq=tiles), grid=(rows // tm,),
        in_specs=[row_spec(d), row_spec(d), full(w_up), full(w_dw), full(b_dw), full(w_down), full(g_post)],
        out_specs=[row_spec(d), pl.BlockSpec((1, halo, npad), lambda i: (i // tiles, 0, 0))],
        out_shape=[jax.ShapeDtypeStruct((rows, d), F32), jax.ShapeDtypeStruct((B, halo, npad), F32)],
        scratch_shapes=[pltpu.VMEM((tm + halo, npad), F32)],
        compiler_params=_params("arbitrary"), name="ffn_prompt")(hf, x, w_up, w_dw, b_dw, w_down, g_post)


def _ffn_sample_kernel(hf_ref, x_ref, cb_ref, wup_ref, wdw_ref, bdw_ref, wdn_ref, gpost_ref, y_ref, unew_ref,
                       *, T):
    nb = hf_ref.shape[0] // T
    nh = wdn_ref.shape[0]
    hf = hf_ref[...]
    f = jnp.zeros(y_ref.shape, F32)
    for c0, cw in _ffn_chunks(nh):
        cg = []
        for base in (c0, nh + c0):
            cs = slice(base, base + cw)
            u = _dot(hf, wup_ref[:, cs])
            ext = [cb_ref[j, :, cs] for j in range(CONV_W - 1)] + [u[t * nb:(t + 1) * nb] for t in range(T)]
            for j in range(CONV_W - 1):
                unew_ref[j, :, cs] = ext[len(ext) - (CONV_W - 1) + j]
            rows = []
            for t in range(T):
                c = bdw_ref[:, cs]
                for j in range(CONV_W):
                    c = c + ext[t + j] * wdw_ref[j:j + 1, cs]
                rows.append(c)
            cg.append(jnp.concatenate(rows, axis=0))
        act = (jax.nn.silu(cg[1]) * cg[0]).astype(BF16)
        f = f + _dot(act, wdn_ref[c0:c0 + cw, :])
    y_ref[...] = x_ref[...] + _rms(f, gpost_ref[...])


def _ffn_sample(hf, x, cb, w_up, w_dw, b_dw, w_down, g_post, T):
    rows, d = x.shape
    npad = w_up.shape[1]
    ins = (hf, x, cb, w_up, w_dw, b_dw, w_down, g_post)
    full = lambda a: pl.BlockSpec(a.shape, lambda i: (0,) * a.ndim)
    return pl.pallas_call(
        functools.partial(_ffn_sample_kernel, T=T), grid=(1,),
        in_specs=[full(a) for a in ins],
        out_specs=[pl.BlockSpec((rows, d), lambda i: (0, 0)),
                   pl.BlockSpec((CONV_W - 1, rows // T, npad), lambda i: (0, 0, 0))],
        out_shape=[jax.ShapeDtypeStruct((rows, d), F32),
                   jax.ShapeDtypeStruct((CONV_W - 1, rows // T, npad), F32)],
        compiler_params=_params("arbitrary"), name="ffn_sample")(*ins)


def _pad_halves(a, nh, nh_pad):
    pad = [(0, 0)] * (a.ndim - 1) + [(0, nh_pad - nh)]
    return jnp.concatenate([jnp.pad(a[..., :nh], pad), jnp.pad(a[..., nh:], pad)], axis=-1)


def _unpad_halves(a, nh, nh_pad):
    return jnp.concatenate([a[..., :nh], a[..., nh_pad:nh_pad + nh]], axis=-1)


def _row_tile(rows, want):
    t = min(rows, want)
    while rows % t:
        t //= 2
    return t


def kernel(x_prompt, x_sample, cache_dk, cache_dv, cache_mem_k, cache_mem_v, state_ml_C, state_ml_n, state_ml_m, state_conv, page_table, mem_prompt, g_mix_pre, g_mix_post, w_in, b_if, g_ml_head, da_lambda, g_da_head, w_out, g_mem_pre, g_mem_post, g_mem_src, w_mq, w_mk, w_mv, w_mo, g_ffn_pre, g_ffn_post, w_up, w_dw, b_dw, w_down):
    depth = w_in.shape[0]
    Bp, Tp, d = x_prompt.shape
    Bs, Ts, _ = x_sample.shape
    n_pages, page = page_table.shape[1], cache_dk.shape[2]
    past_len = n_pages * page
    n_mem = mem_prompt.shape[1]
    nh = w_down.shape[1]
    nh_pad = -(-nh // LANES) * LANES
    H, D = ML_HEADS, ML_HEAD_DIM
    rows_p, rows_s = Bp * Tp, Bs * Ts

    tm_p = _row_tile(Tp, 256)
    tm_s = _row_tile(rows_s, 256)
    tab_p = _rope_tables(Tp, Tp, 0)
    tab_s = _rope_tables(tm_s, Ts, past_len)

    yp = x_prompt.reshape(rows_p, d)
    ys = x_sample.reshape(rows_s, d)
    outs = [[] for _ in range(14)]
    row = lambda a: a.reshape(1, -1)
    for l in range(depth):
        lam_init = 0.8 - 0.6 * math.exp(-0.3 * l)
        wi = w_in[l]
        w_main = jnp.concatenate([wi[:, :4 * ML_WIDTH], wi[:, 4 * ML_WIDTH + 2 * H:]], axis=1).astype(BF16)
        w_gate = wi[:, 4 * ML_WIDTH:4 * ML_WIDTH + 2 * H]
        w_g = jnp.pad(w_gate, ((0, 0), (0, LANES - 2 * H))).astype(BF16)
        w_gt = w_gate.T.astype(BF16)
        b_col = jnp.pad(b_if[l], (0, LANES - 2 * H)).reshape(1, LANES)
        b_row = b_if[l].reshape(2 * H, 1)
        wo_ml, wo_da = w_out[l][:ML_WIDTH].astype(BF16), w_out[l][ML_WIDTH:].astype(BF16)
        wq_b, wo_b = w_mq[l].astype(BF16), w_mo[l].astype(BF16)
        wkv_b = jnp.concatenate([w_mk[l], w_mv[l]], axis=1).astype(BF16)
        wup_b = _pad_halves(w_up[l], nh, nh_pad).astype(BF16)
        wdw_p = _pad_halves(w_dw[l], nh, nh_pad)
        bdw_p = _pad_halves(b_dw[l].reshape(1, -1), nh, nh_pad)
        wdn_b = jnp.pad(w_down[l], ((0, nh_pad - nh), (0, 0))).astype(BF16)
        g_da3 = g_da_head[l].reshape(DA_HEADS, 1, DA_V_DIM)

        q, k, v, o, gc, gr, dq, dk, dv = _in_proj(
            yp, row(g_mix_pre[l]), w_main, w_g, w_gt, b_col, b_row, tab_p, BF16, tm_p)
        h_ml, C_p, n_p, m_p = _mlstm_prompt(q, k, v, o, gc, gr, g_ml_head[l], Bp, Tp, _row_tile(Tp, 128))
        h_da = _da_prompt(dq, dk, dv, da_lambda[l], g_da3, Bp, Tp, tm_p, lam_init)
        x1, qm = _proj_norm([h_ml, h_da], [wo_ml, wo_da], yp, row(g_mix_post[l]), row(g_mem_pre[l]), wq_b, tm_p)
        mkv = _norm_matmul(mem_prompt.reshape(Bp * n_mem, d), row(g_mem_src[l]), wkv_b,
                           _row_tile(Bp * n_mem, 256), 512)
        mk_p, mv_p = mkv[:, :d].reshape(Bp, n_mem, d), mkv[:, d:].reshape(Bp, n_mem, d)
        om = _mem_attn(qm.reshape(Bp, Tp, d), mk_p, mv_p, 1, tm_p)
        x2, hf = _proj_norm([om.reshape(rows_p, d)], [wo_b], x1, row(g_mem_post[l]), row(g_ffn_pre[l]), None, tm_p)
        yp, ulast = _ffn_prompt(hf, x2, wup_b, wdw_p, bdw_p, wdn_b, row(g_ffn_post[l]), Bp, Tp, tm_p)
        cv_p = _unpad_halves(ulast[:, ulast.shape[1] - (CONV_W - 1):], nh, nh_pad)

        q, k, v, o, gc, gr, dq, dk_s, dv_s = _in_proj(
            ys, row(g_mix_pre[l]), w_main, w_g, w_gt, b_col, b_row, tab_s, F32, tm_s)
        r3 = lambda a: a.reshape(Bs, Ts, a.shape[-1])
        gr3 = gr.reshape(2 * H, Bs, Ts).transpose(1, 0, 2)
        h_ml, C_s, n_s, m_s = _mlstm_sample(
            r3(q), r3(k), r3(v), r3(o), r3(gc), gr3, g_ml_head[l],
            state_ml_C[l], state_ml_n[l].reshape(Bs, H, 1, D), state_ml_m[l].reshape(Bs, H, 1, 1),
            _row_tile(Bs, 8))
        h_da = _da_sample(page_table, r3(dq), r3(dk_s), r3(dv_s), da_lambda[l], g_da_head[l],
                          cache_dk[l].reshape(-1, page, DA_WIDTH), cache_dv[l].reshape(-1, page, DA_WIDTH),
                          lam_init)
        x1, qm = _proj_norm([h_ml.reshape(rows_s, ML_WIDTH), h_da.reshape(rows_s, DA_WIDTH)], [wo_ml, wo_da], ys,
                            row(g_mix_post[l]), row(g_mem_pre[l]), wq_b, tm_s, h_dtype=F32)
        om = _mem_attn(qm.reshape(Bs, Ts, d), cache_mem_k[l].reshape(Bs, n_mem, d),
                       cache_mem_v[l].reshape(Bs, n_mem, d), _row_tile(Bs, 8), Ts)
        x2, hf = _proj_norm([om.reshape(rows_s, d)], [wo_b], x1, row(g_mem_post[l]), row(g_ffn_pre[l]), None, tm_s)
        tmaj = lambda a: a.reshape(Bs, Ts, -1).transpose(1, 0, 2).reshape(rows_s, -1)
        cb = _pad_halves(state_conv[l], nh, nh_pad).transpose(1, 0, 2)
        y_t, unew = _ffn_sample(tmaj(hf), tmaj(x2), cb, wup_b, wdw_p, bdw_p, wdn_b, row(g_ffn_post[l]), Ts)
        ys = y_t.reshape(Ts, Bs, d).transpose(1, 0, 2).reshape(rows_s, d)
        cv_s = _unpad_halves(unew.transpose(1, 0, 2), nh, nh_pad)

        vals = (dk.reshape(Bp, Tp, DA_HEADS, DA_V_DIM), dv.reshape(Bp, Tp, DA_HEADS, DA_V_DIM),
                mk_p.reshape(Bp, n_mem, MEM_HEADS, d // MEM_HEADS), mv_p.reshape(Bp, n_mem, MEM_HEADS, d // MEM_HEADS),
                C_p, n_p.reshape(Bp, H, D), m_p.reshape(Bp, H), cv_p,
                dk_s.reshape(Bs, Ts, DA_HEADS, DA_V_DIM), dv_s.reshape(Bs, Ts, DA_HEADS, DA_V_DIM),
                C_s, n_s.reshape(Bs, H, D), m_s.reshape(Bs, H), cv_s)
        for acc, val in zip(outs, vals):
            acc.append(val)
    return (yp.reshape(Bp, Tp, d), ys.reshape(Bs, Ts, d)) + tuple(jnp.stack(a) for a in outs)
```

```python
import functools
import math

import jax
import jax.numpy as jnp
from jax import lax
from jax.experimental import pallas as pl
from jax.experimental.pallas import tpu as pltpu

F32 = jnp.float32
BF16 = jnp.bfloat16

ML_HEADS = 4
ML_HEAD_DIM = 128
ML_WIDTH = ML_HEADS * ML_HEAD_DIM
DA_HEADS = 4
DA_V_DIM = 128
DA_QK_DIM = 64
DA_WIDTH = DA_HEADS * DA_V_DIM
ROPE_DIM = 16
ROPE_THETA = 500000.0
MEM_HEADS = 4
CONV_W = 3
RMS_EPS = 1e-6
LANES = 128
NEG_BIG = -1e30
VMEM_LIMIT = 56 * 1024 * 1024


def _params(*sem, vmem=VMEM_LIMIT):
    return pltpu.CompilerParams(dimension_semantics=sem, vmem_limit_bytes=vmem)


def _rms(x, g):
    return x * lax.rsqrt(jnp.mean(x * x, axis=-1, keepdims=True) + RMS_EPS) * g


def _log_sigmoid(x):
    return jnp.minimum(x, 0.0) - jnp.log1p(jnp.exp(-jnp.abs(x)))


def _dot(a, b):
    return jnp.dot(a, b, preferred_element_type=F32)


def _dot_nt(a, b):
    return lax.dot_general(a, b, (((1,), (1,)), ((), ())), preferred_element_type=F32)


def _dot_tn(a, b):
    return lax.dot_general(a, b, (((0,), (0,)), ((), ())), preferred_element_type=F32)


def _rope_table_kernel(cos_ref, sa_ref, sb_ref, *, period, offset):
    rows = cos_ref.shape[0]
    half = ROPE_DIM // 2
    r = lax.broadcasted_iota(jnp.int32, (rows, LANES), 0) + pl.program_id(0) * rows
    lane = lax.broadcasted_iota(jnp.int32, (rows, LANES), 1)
    pos = (offset + r % period).astype(F32)
    c = lane % DA_QK_DIM
    j = (c % half).astype(F32)
    inv = jnp.exp(-math.log(ROPE_THETA) * (2.0 * j / ROPE_DIM))
    ang = pos * inv
    cos, sin = jnp.cos(ang), jnp.sin(ang)
    cos_ref[...] = jnp.where(c < ROPE_DIM, cos, 1.0)
    sa_ref[...] = jnp.where(c < half, -sin, 0.0)
    sb_ref[...] = jnp.where((c >= half) & (c < ROPE_DIM), sin, 0.0)


def _rope_tables(rows, period, offset):
    blk = min(rows, 512)
    spec = pl.BlockSpec((blk, LANES), lambda i: (i, 0))
    shp = jax.ShapeDtypeStruct((rows, LANES), F32)
    return pl.pallas_call(
        functools.partial(_rope_table_kernel, period=period, offset=offset),
        grid=(rows // blk,), out_specs=[spec] * 3, out_shape=[shp] * 3,
        compiler_params=_params("parallel"), name="rope_tables")()


def _rope(x, cos, sa, sb):
    outs = []
    for j in range(x.shape[1] // LANES):
        xj = x[:, j * LANES:(j + 1) * LANES]
        up = pltpu.roll(xj, LANES - ROPE_DIM // 2, axis=1)
        dn = pltpu.roll(xj, ROPE_DIM // 2, axis=1)
        outs.append(xj * cos + up * sa + dn * sb)
    return outs


def _in_proj_kernel(x_ref, g_ref, w_ref, wg_ref, wgt_ref, bcol_ref, brow_ref, cos_ref, sa_ref, sb_ref,
                    q_ref, k_ref, v_ref, o_ref, gc_ref, gr_ref, dq_ref, dk_ref, dv_ref, dkb_ref, dvb_ref,
                    *, dq_scale):
    h = _rms(x_ref[...], g_ref[...]).astype(BF16)
    for j, ref in enumerate((q_ref, k_ref, v_ref, o_ref)):
        ref[...] = _dot(h, w_ref[:, j * ML_WIDTH:(j + 1) * ML_WIDTH]).astype(ref.dtype)
    gc = _dot(h, wg_ref[...]) + bcol_ref[...]
    lane = lax.broadcasted_iota(jnp.int32, gc.shape, 1)
    gc_ref[...] = jnp.where(lane < ML_HEADS, gc, _log_sigmoid(gc))
    gr = _dot_nt(wgt_ref[...], h) + brow_ref[...]
    row = lax.broadcasted_iota(jnp.int32, gr.shape, 0)
    gr_ref[...] = jnp.where(row < ML_HEADS, gr, _log_sigmoid(gr))
    base = 4 * ML_WIDTH
    cos, sa, sb = cos_ref[...], sa_ref[...], sb_ref[...]
    dq = _rope(_dot(h, w_ref[:, base:base + DA_WIDTH]), cos, sa, sb)
    dk = _rope(_dot(h, w_ref[:, base + DA_WIDTH:base + 2 * DA_WIDTH]), cos, sa, sb)
    dv = _dot(h, w_ref[:, base + 2 * DA_WIDTH:base + 3 * DA_WIDTH])
    dvb_ref[...] = dv.astype(BF16)
    for j in range(DA_HEADS):
        cols = slice(j * LANES, (j + 1) * LANES)
        dq_ref[:, cols] = (dq[j] * dq_scale).astype(dq_ref.dtype)
        dk_ref[:, j, :] = dk[j]
        dkb_ref[:, cols] = dk[j].astype(BF16)
        dv_ref[:, j, :] = dv[:, cols]


def _in_proj(x, g, w_main, w_g, w_gt, b_col, b_row, tables, act_dtype, tm, dq_scale):
    rows, d = x.shape
    cos, sa, sb = tables
    nt = cos.shape[0] // tm
    row_spec = lambda w: pl.BlockSpec((tm, w), lambda i: (i, 0))
    full = lambda a: pl.BlockSpec(a.shape, lambda i: (0,) * a.ndim)
    tab_spec = pl.BlockSpec((tm, LANES), lambda i: (i % nt, 0))
    out_shape = [jax.ShapeDtypeStruct((rows, ML_WIDTH), act_dtype)] * 4 + [
        jax.ShapeDtypeStruct((rows, LANES), F32), jax.ShapeDtypeStruct((8, rows), F32),
        jax.ShapeDtypeStruct((rows, DA_WIDTH), act_dtype),
        jax.ShapeDtypeStruct((rows, DA_HEADS, DA_V_DIM), F32), jax.ShapeDtypeStruct((rows, DA_HEADS, DA_V_DIM), F32),
        jax.ShapeDtypeStruct((rows, DA_WIDTH), BF16), jax.ShapeDtypeStruct((rows, DA_WIDTH), BF16)]
    kv_spec = pl.BlockSpec((tm, DA_HEADS, DA_V_DIM), lambda i: (i, 0, 0))
    out_specs = [row_spec(ML_WIDTH)] * 4 + [row_spec(LANES), pl.BlockSpec((8, tm), lambda i: (0, i)),
                                            row_spec(DA_WIDTH), kv_spec, kv_spec,
                                            row_spec(DA_WIDTH), row_spec(DA_WIDTH)]
    return pl.pallas_call(
        functools.partial(_in_proj_kernel, dq_scale=dq_scale), grid=(rows // tm,),
        in_specs=[row_spec(d), full(g), full(w_main), full(w_g), full(w_gt), full(b_col), full(b_row),
                  tab_spec, tab_spec, tab_spec],
        out_specs=out_specs, out_shape=out_shape,
        compiler_params=_params("parallel"), name="in_proj")(
            x, g, w_main, w_g, w_gt, b_col, b_row, cos, sa, sb)


def _mlstm_head(q, k, v, i_col, f_col, i_row, f_row, C, n, m):
    L, D = q.shape
    scale = D ** -0.5
    r = lax.broadcasted_iota(jnp.int32, (L, L), 0)
    c = lax.broadcasted_iota(jnp.int32, (L, L), 1)
    tri = c <= r
    b_col = jnp.sum(jnp.where(tri, f_row, 0.0), axis=1, keepdims=True)
    b_row = jnp.sum(jnp.where(r <= c, f_col, 0.0), axis=0, keepdims=True)
    log_d = jnp.where(tri, b_col - b_row + i_row, -jnp.inf)
    inter = b_col + m
    m_row = jnp.maximum(jnp.max(log_d, axis=1, keepdims=True), inter)
    w_inter = jnp.exp(inter - m_row)
    qb, kb, vb = q.astype(BF16), k.astype(BF16), v.astype(BF16)
    qf, kf = q.astype(F32), k.astype(F32)
    s = _dot_nt(qb, kb) * (scale * jnp.exp(log_d - m_row))
    num = w_inter * _dot(qb, C.astype(BF16)) + _dot(s.astype(BF16), vb)
    den = w_inter * jnp.sum(qf * n, axis=1, keepdims=True) + jnp.sum(s, axis=1, keepdims=True)
    h = num / jnp.maximum(jnp.abs(den), jnp.exp(-m_row))
    b_last = b_col[L - 1:L, :]
    log_w = b_last - b_col + i_col
    m_new = jnp.maximum(b_last + m, jnp.max(log_w, axis=0, keepdims=True))
    decay = jnp.exp(b_last + m - m_new)
    wk = (scale * jnp.exp(log_w - m_new)) * kf
    c_new = decay * C + _dot_tn(wk.astype(BF16), vb)
    n_new = decay * n + jnp.sum(wk, axis=0, keepdims=True)
    return h, c_new, n_new, m_new


def _ml_head_out(h, o, g):
    return _rms(h, g) * jax.nn.sigmoid(o.astype(F32))


def _mlstm_prompt_kernel(q_ref, k_ref, v_ref, o_ref, gc_ref, gr_ref, gh_ref,
                         h_ref, c_out, n_out, m_out, c_scr, n_scr, m_scr):
    ci = pl.program_id(1)

    @pl.when(ci == 0)
    def _():
        c_scr[...] = jnp.zeros_like(c_scr)
        n_scr[...] = jnp.zeros_like(n_scr)
        m_scr[...] = jnp.zeros_like(m_scr)

    gc, gr = gc_ref[...], gr_ref[...]
    for hh in range(ML_HEADS):
        cols = slice(hh * ML_HEAD_DIM, (hh + 1) * ML_HEAD_DIM)
        h, c_new, n_new, m_new = _mlstm_head(
            q_ref[:, cols], k_ref[:, cols], v_ref[:, cols],
            gc[:, hh:hh + 1], gc[:, ML_HEADS + hh:ML_HEADS + hh + 1],
            gr[hh:hh + 1, :], gr[ML_HEADS + hh:ML_HEADS + hh + 1, :],
            c_scr[hh], n_scr[hh], m_scr[hh])
        c_scr[hh], n_scr[hh], m_scr[hh] = c_new, n_new, m_new
        h_ref[:, cols] = _ml_head_out(h, o_ref[:, cols], gh_ref[hh:hh + 1, :]).astype(h_ref.dtype)

    @pl.when(ci == pl.num_programs(1) - 1)
    def _():
        c_out[0] = c_scr[...]
        n_out[0] = n_scr[...]
        m_out[0] = m_scr[...]


def _mlstm_prompt(q, k, v, o, gc, gr, g_head, B, T, L):
    nc = T // L
    rows = B * T
    blk = lambda w: pl.BlockSpec((L, w), lambda b, c: (b * nc + c, 0))
    H, D = ML_HEADS, ML_HEAD_DIM
    return pl.pallas_call(
        _mlstm_prompt_kernel, grid=(B, nc),
        in_specs=[blk(ML_WIDTH)] * 4 + [blk(LANES), pl.BlockSpec((8, L), lambda b, c: (0, b * nc + c)),
                                        pl.BlockSpec(g_head.shape, lambda b, c: (0, 0))],
        out_specs=[blk(ML_WIDTH),
                   pl.BlockSpec((1, H, D, D), lambda b, c: (b, 0, 0, 0)),
                   pl.BlockSpec((1, H, 1, D), lambda b, c: (b, 0, 0, 0)),
                   pl.BlockSpec((1, H, 1, 1), lambda b, c: (b, 0, 0, 0))],
        out_shape=[jax.ShapeDtypeStruct((rows, ML_WIDTH), BF16),
                   jax.ShapeDtypeStruct((B, H, D, D), F32),
                   jax.ShapeDtypeStruct((B, H, 1, D), F32),
                   jax.ShapeDtypeStruct((B, H, 1, 1), F32)],
        scratch_shapes=[pltpu.VMEM((H, D, D), F32), pltpu.VMEM((H, 1, D), F32), pltpu.VMEM((H, 1, 1), F32)],
        compiler_params=_params("parallel", "arbitrary"), name="mlstm_prompt")(q, k, v, o, gc, gr, g_head)


def _mlstm_sample_kernel(q_ref, k_ref, v_ref, o_ref, gc_ref, gr_ref, gh_ref, c_in, n_in, m_in,
                         h_ref, c_out, n_out, m_out):
    def body(bi, carry):
        gc, gr = gc_ref[bi], gr_ref[bi]
        for hh in range(ML_HEADS):
            cols = slice(hh * ML_HEAD_DIM, (hh + 1) * ML_HEAD_DIM)
            h, c_new, n_new, m_new = _mlstm_head(
                q_ref[bi, :, cols], k_ref[bi, :, cols], v_ref[bi, :, cols],
                gc[:, hh:hh + 1], gc[:, ML_HEADS + hh:ML_HEADS + hh + 1],
                gr[hh:hh + 1, :], gr[ML_HEADS + hh:ML_HEADS + hh + 1, :],
                c_in[bi, hh], n_in[bi, hh], m_in[bi, hh])
            c_out[bi, hh], n_out[bi, hh], m_out[bi, hh] = c_new, n_new, m_new
            h_ref[bi, :, cols] = _ml_head_out(h, o_ref[bi, :, cols], gh_ref[hh:hh + 1, :]).astype(h_ref.dtype)
        return carry

    lax.fori_loop(0, q_ref.shape[0], body, 0)


def _mlstm_sample(q, k, v, o, gc, gr, g_head, c0, n0, m0, bb):
    B, T, _ = q.shape
    H, D = ML_HEADS, ML_HEAD_DIM
    b3 = lambda s: pl.BlockSpec((bb,) + s, lambda i: (i, 0, 0))
    b4 = lambda s: pl.BlockSpec((bb,) + s, lambda i: (i, 0, 0, 0))
    return pl.pallas_call(
        _mlstm_sample_kernel, grid=(B // bb,),
        in_specs=[b3((T, ML_WIDTH))] * 4 + [b3((T, LANES)), b3((8, T)),
                                            pl.BlockSpec(g_head.shape, lambda i: (0, 0)),
                                            b4((H, D, D)), b4((H, 1, D)), b4((H, 1, 1))],
        out_specs=[b3((T, ML_WIDTH)), b4((H, D, D)), b4((H, 1, D)), b4((H, 1, 1))],
        out_shape=[jax.ShapeDtypeStruct((B, T, ML_WIDTH), BF16),
                   jax.ShapeDtypeStruct((B, H, D, D), F32),
                   jax.ShapeDtypeStruct((B, H, 1, D), F32),
                   jax.ShapeDtypeStruct((B, H, 1, 1), F32)],
        compiler_params=_params("parallel"), name="mlstm_sample")(q, k, v, o, gc, gr, g_head, c0, n0, m0)


def _da_lambda(lam_ref, lam_init):
    lv = lam_ref[...]
    a = jnp.sum(lv[0:1, :] * lv[1:2, :], axis=1, keepdims=True)
    b = jnp.sum(lv[2:3, :] * lv[3:4, :], axis=1, keepdims=True)
    return jnp.exp(a) - jnp.exp(b) + lam_init


def _stack_components(q):
    lane = lax.broadcasted_iota(jnp.int32, q.shape, 1)
    zero = jnp.zeros_like(q)
    return jnp.concatenate([jnp.where(lane < DA_QK_DIM, q, zero), jnp.where(lane >= DA_QK_DIM, q, zero)], axis=0)


def _lane_fold(x, op):
    out = x[:, :LANES]
    for c in range(1, x.shape[1] // LANES):
        out = op(out, x[:, c * LANES:(c + 1) * LANES])
    return out


def _da_prompt_kernel(q_ref, kb_scr, vb_scr, lam_ref, gh_ref, out_ref, m_scr, l_scr, acc_scr, *, lam_init):
    i = pl.program_id(2)
    tq = q_ref.shape[0]
    rows = 2 * tq
    qq = _stack_components(q_ref[...])
    causal = (lax.broadcasted_iota(jnp.int32, (rows, tq), 1)
              <= lax.broadcasted_iota(jnp.int32, (rows, tq), 0) % tq)

    def scores(j):
        off = pl.multiple_of(j * tq, tq)
        return _dot_nt(qq, kb_scr[pl.ds(off, tq), :]), off

    s_diag, off_diag = scores(i)
    s_diag = jnp.where(causal, s_diag, NEG_BIG)
    m_scr[...] = _lane_fold(s_diag, jnp.maximum)

    def max_body(j, carry):
        m_scr[...] = jnp.maximum(m_scr[...], _lane_fold(scores(j)[0], jnp.maximum))
        return carry

    lax.fori_loop(0, i, max_body, 0)
    m_rep = jnp.broadcast_to(jnp.max(m_scr[...], axis=1, keepdims=True), (rows, LANES))
    m_scr[...] = m_rep

    def weights(s, m):
        p = [jnp.exp2(s[:, c * LANES:(c + 1) * LANES] - m) for c in range(tq // LANES)]
        l = p[0]
        for pc in p[1:]:
            l = l + pc
        return jnp.concatenate([pc.astype(BF16) for pc in p], axis=1), l

    p, l = weights(s_diag, m_rep)
    l_scr[...] = l
    acc_scr[...] = _dot(p, vb_scr[pl.ds(off_diag, tq), :])

    def sum_body(j, carry):
        s, off = scores(j)
        p, l = weights(s, m_scr[...])
        l_scr[...] += l
        acc_scr[...] += _dot(p, vb_scr[pl.ds(off, tq), :])
        return carry

    lax.fori_loop(0, i, sum_body, 0)
    o = acc_scr[...] / jnp.sum(l_scr[...], axis=1, keepdims=True)
    a = o[:tq] - _da_lambda(lam_ref, lam_init) * o[tq:]
    out_ref[...] = (_rms(a, gh_ref[0]) * (1.0 - lam_init)).astype(out_ref.dtype)


def _da_prompt(dq, dk, dv, da_lambda, g_head3, B, T, tq, lam_init):
    nq = T // tq
    rows = B * T
    kv_spec = pl.BlockSpec((T, DA_V_DIM), lambda b, h, i: (b, h))
    return pl.pallas_call(
        functools.partial(_da_prompt_kernel, lam_init=lam_init), grid=(B, DA_HEADS, nq),
        in_specs=[pl.BlockSpec((tq, DA_V_DIM), lambda b, h, i: (b * nq + i, h)), kv_spec, kv_spec,
                  pl.BlockSpec(da_lambda.shape, lambda b, h, i: (0, 0)),
                  pl.BlockSpec((1, 1, DA_V_DIM), lambda b, h, i: (h, 0, 0))],
        out_specs=pl.BlockSpec((tq, DA_V_DIM), lambda b, h, i: (b * nq + i, h)),
        out_shape=jax.ShapeDtypeStruct((rows, DA_WIDTH), BF16),
        scratch_shapes=[pltpu.VMEM((2 * tq, LANES), F32), pltpu.VMEM((2 * tq, LANES), F32),
                        pltpu.VMEM((2 * tq, DA_V_DIM), F32)],
        compiler_params=_params("parallel", "parallel", "arbitrary"), name="da_prompt")(
            dq, dk, dv, da_lambda, g_head3)


def _da_sample_kernel(pt_ref, q_ref, kn_ref, vn_ref, lam_ref, gh_ref, ck_hbm, cv_hbm, out_ref,
                      kbuf, vbuf, sem, *, lam_init, layer):
    b = pl.program_id(0)
    nb = pl.num_programs(0)
    n_pages, page = kbuf.shape[1], kbuf.shape[2] // DA_HEADS
    T = q_ref.shape[1]
    scale = DA_QK_DIM ** -0.5

    def copies(bi, slot):
        out = []
        for p in range(n_pages):
            pg = pt_ref[bi, p]
            out.append(pltpu.make_async_copy(ck_hbm.at[layer, pg], kbuf.at[slot, p], sem.at[slot, 0]))
            out.append(pltpu.make_async_copy(cv_hbm.at[layer, pg], vbuf.at[slot, p], sem.at[slot, 1]))
        return out

    @pl.when(b == 0)
    def _():
        for cp in copies(0, 0):
            cp.start()

    @pl.when(b + 1 < nb)
    def _():
        for cp in copies(b + 1, (b + 1) % 2):
            cp.start()

    slot = b % 2
    for cp in copies(b, slot):
        cp.wait()

    lam = _da_lambda(lam_ref, lam_init)
    q_all = q_ref[0]
    n_rows = n_pages * page * DA_HEADS
    qq = jnp.concatenate([_stack_components(q_all[:, hh * DA_V_DIM:(hh + 1) * DA_V_DIM])
                          for hh in range(DA_HEADS)], axis=0)
    k_all = kbuf[slot].reshape(n_rows, DA_V_DIM).astype(BF16)
    v_all = vbuf[slot].reshape(n_rows, DA_V_DIM).astype(BF16)
    rq = 2 * T * DA_HEADS
    own_head = (lax.broadcasted_iota(jnp.int32, (rq, n_rows), 1) % DA_HEADS
                == lax.broadcasted_iota(jnp.int32, (rq, n_rows), 0) // (2 * T))
    s_past = jnp.where(own_head, _dot_nt(qq.astype(BF16), k_all) * scale, NEG_BIG)
    trow = lax.broadcasted_iota(jnp.int32, (rq, 1), 0) % T

    def per_query_row(new_ref, t):
        x = new_ref[0, t]
        return jnp.concatenate([jnp.broadcast_to(x[hh:hh + 1, :], (2 * T, DA_V_DIM))
                                for hh in range(DA_HEADS)], axis=0)

    s_new = [jnp.where(trow >= t,
                       jnp.sum(qq * per_query_row(kn_ref, t), axis=1, keepdims=True) * scale, NEG_BIG)
             for t in range(T)]
    m = jnp.max(s_past, axis=1, keepdims=True)
    for t in range(T):
        m = jnp.maximum(m, s_new[t])
    p_past = jnp.exp(s_past - m)
    l = jnp.sum(p_past, axis=1, keepdims=True)
    acc = _dot(p_past.astype(BF16), v_all)
    for t in range(T):
        p_t = jnp.exp(s_new[t] - m)
        l = l + p_t
        acc = acc + p_t * per_query_row(vn_ref, t)
    o = acc / l
    for hh in range(DA_HEADS):
        r0 = hh * 2 * T
        a = o[r0:r0 + T] - lam * o[r0 + T:r0 + 2 * T]
        out_ref[0, :, hh * DA_V_DIM:(hh + 1) * DA_V_DIM] = (
            _rms(a, gh_ref[hh:hh + 1, :]) * (1.0 - lam_init)).astype(out_ref.dtype)


def _da_sample(page_table, dq, dk, dv, da_lambda, g_head, cache_k, cache_v, lam_init, layer):
    B, T, _ = dq.shape
    n_pages = page_table.shape[1]
    page_rows = cache_k.shape[2]
    blk = pl.BlockSpec((1, T, DA_WIDTH), lambda b, pt: (b, 0, 0))
    kv_blk = pl.BlockSpec((1, T, DA_HEADS, DA_V_DIM), lambda b, pt: (b, 0, 0, 0))
    grid_spec = pltpu.PrefetchScalarGridSpec(
        num_scalar_prefetch=1, grid=(B,),
        in_specs=[blk, kv_blk, kv_blk, pl.BlockSpec(da_lambda.shape, lambda b, pt: (0, 0)),
                  pl.BlockSpec(g_head.shape, lambda b, pt: (0, 0)),
                  pl.BlockSpec(memory_space=pl.ANY), pl.BlockSpec(memory_space=pl.ANY)],
        out_specs=blk,
        scratch_shapes=[pltpu.VMEM((2, n_pages, page_rows, DA_V_DIM), F32),
                        pltpu.VMEM((2, n_pages, page_rows, DA_V_DIM), F32),
                        pltpu.SemaphoreType.DMA((2, 2))])
    return pl.pallas_call(
        functools.partial(_da_sample_kernel, lam_init=lam_init, layer=layer), grid_spec=grid_spec,
        out_shape=jax.ShapeDtypeStruct((B, T, DA_WIDTH), BF16),
        compiler_params=_params("arbitrary"), name="da_sample")(
            page_table, dq, dk, dv, da_lambda, g_head, cache_k, cache_v)


def _proj_norm_kernel(*refs, n_in, has_next):
    a_refs, w_refs = refs[:n_in], refs[n_in:2 * n_in]
    x_ref, gpost_ref, gpre_ref = refs[2 * n_in:2 * n_in + 3]
    rest = refs[2 * n_in + 3:]
    wn_ref = rest[0] if has_next else None
    xo_ref, ho_ref = rest[-2:]
    acc = _dot(a_refs[0][...], w_refs[0][...])
    for a, w in zip(a_refs[1:], w_refs[1:]):
        acc = acc + _dot(a[...], w[...])
    x1 = x_ref[...] + _rms(acc, gpost_ref[...])
    xo_ref[...] = x1
    hn = _rms(x1, gpre_ref[...]).astype(BF16)
    ho_ref[...] = (_dot(hn, wn_ref[...]) if has_next else hn).astype(ho_ref.dtype)


def _proj_norm(a_list, w_list, x, g_post, g_pre, w_next, tm, h_dtype=BF16):
    rows, d = x.shape
    n_in = len(a_list)
    has_next = w_next is not None
    row_spec = lambda w: pl.BlockSpec((tm, w), lambda i: (i, 0))
    full = lambda a: pl.BlockSpec(a.shape, lambda i: (0,) * a.ndim)
    ins = list(a_list) + list(w_list) + [x, g_post, g_pre] + ([w_next] if has_next else [])
    in_specs = ([row_spec(a.shape[1]) for a in a_list] + [full(w) for w in w_list]
                + [row_spec(d), full(g_post), full(g_pre)] + ([full(w_next)] if has_next else []))
    n_out = w_next.shape[1] if has_next else d
    return pl.pallas_call(
        functools.partial(_proj_norm_kernel, n_in=n_in, has_next=has_next), grid=(rows // tm,),
        in_specs=in_specs, out_specs=[row_spec(d), row_spec(n_out)],
        out_shape=[jax.ShapeDtypeStruct((rows, d), F32), jax.ShapeDtypeStruct((rows, n_out), h_dtype)],
        compiler_params=_params("parallel"), name="proj_norm")(*ins)


def _norm_matmul_kernel(x_ref, g_ref, w_ref, o_ref):
    o_ref[...] = _dot(_rms(x_ref[...], g_ref[...]).astype(BF16), w_ref[...])


def _norm_matmul(x, g, w, tm, tn):
    rows, d = x.shape
    n = w.shape[1]
    return pl.pallas_call(
        _norm_matmul_kernel, grid=(rows // tm, n // tn),
        in_specs=[pl.BlockSpec((tm, d), lambda i, j: (i, 0)), pl.BlockSpec(g.shape, lambda i, j: (0, 0)),
                  pl.BlockSpec((d, tn), lambda i, j: (0, j))],
        out_specs=pl.BlockSpec((tm, tn), lambda i, j: (i, j)),
        out_shape=jax.ShapeDtypeStruct((rows, n), F32),
        compiler_params=_params("parallel", "parallel"), name="mem_kv")(x, g, w)


def _mem_attn_kernel(q_ref, mk_ref, mv_ref, o_ref):
    hd = q_ref.shape[2] // MEM_HEADS
    scale = hd ** -0.5
    for bi in range(q_ref.shape[0]):
        for hh in range(MEM_HEADS):
            cols = slice(hh * hd, (hh + 1) * hd)
            s = _dot_nt(q_ref[bi, :, cols].astype(BF16), mk_ref[bi, :, hh, :].astype(BF16)) * scale
            p = jnp.exp(s - jnp.max(s, axis=1, keepdims=True))
            o = _dot(p.astype(BF16), mv_ref[bi, :, hh, :].astype(BF16)) / jnp.sum(p, axis=1, keepdims=True)
            o_ref[bi, :, cols] = o.astype(o_ref.dtype)


def _mem_attn(q, mk, mv, layer, bb, tm):
    B, T, d = q.shape
    kv_spec = pl.BlockSpec((None, bb) + mk.shape[2:], lambda b, i: (layer, b, 0, 0, 0))
    return pl.pallas_call(
        _mem_attn_kernel, grid=(B // bb, T // tm),
        in_specs=[pl.BlockSpec((bb, tm, d), lambda b, i: (b, i, 0)), kv_spec, kv_spec],
        out_specs=pl.BlockSpec((bb, tm, d), lambda b, i: (b, i, 0)),
        out_shape=jax.ShapeDtypeStruct((B, T, d), BF16),
        compiler_params=_params("parallel", "parallel"), name="mem_attn")(q, mk, mv)


def _ffn_chunks(nh):
    nchunk = 2 if (nh // LANES) % 2 == 0 else 1
    cw = nh // nchunk
    return [(j * cw, cw) for j in range(nchunk)]


def _ffn_prompt_kernel(hf_ref, x_ref, wup_ref, wdw_ref, bdw_ref, wdn_ref, gpost_ref, y_ref, ulast_ref, ubuf,
                       *, tiles_per_seq):
    i = pl.program_id(0)
    tm = hf_ref.shape[0]
    nh = wdn_ref.shape[0]
    halo = ubuf.shape[0] - tm

    @pl.when(i % tiles_per_seq == 0)
    def _():
        ubuf[0:halo, :] = jnp.zeros((halo, ubuf.shape[1]), F32)

    hf = hf_ref[...]
    f = jnp.zeros((tm, y_ref.shape[1]), F32)
    for c0, cw in _ffn_chunks(nh):
        cg = []
        for base in (c0, nh + c0):
            cs = slice(base, base + cw)
            ubuf[halo:halo + tm, cs] = _dot(hf, wup_ref[:, cs])
            c = bdw_ref[:, cs]
            for j in range(CONV_W):
                lo = halo - (CONV_W - 1) + j
                c = c + ubuf[lo:lo + tm, cs] * wdw_ref[j:j + 1, cs]
            cg.append(c)
        act = (jax.nn.silu(cg[1]) * cg[0]).astype(BF16)
        f = f + _dot(act, wdn_ref[c0:c0 + cw, :])
    y_ref[...] = x_ref[...] + _rms(f, gpost_ref[...])
    tail = ubuf[tm:tm + halo, :]
    ubuf[0:halo, :] = tail
    ulast_ref[0] = tail


def _ffn_prompt(hf, x, w_up, w_dw, b_dw, w_down, g_post, B, T, tm):
    rows, d = x.shape
    npad = w_up.shape[1]
    tiles = T // tm
    halo = 8
    row_spec = lambda w: pl.BlockSpec((tm, w), lambda i: (i, 0))
    full = lambda a: pl.BlockSpec(a.shape, lambda i: (0,) * a.ndim)
    return pl.pallas_call(
        functools.partial(_ffn_prompt_kernel, tiles_per_seq=tiles), grid=(rows // tm,),
        in_specs=[row_spec(d), row_spec(d), full(w_up), full(w_dw), full(b_dw), full(w_down), full(g_post)],
        out_specs=[row_spec(d), pl.BlockSpec((1, halo, npad), lambda i: (i // tiles, 0, 0))],
        out_shape=[jax.ShapeDtypeStruct((rows, d), F32), jax.ShapeDtypeStruct((B, halo, npad), F32)],
        scratch_shapes=[pltpu.VMEM((tm + halo, npad), F32)],
        compiler_params=_params("arbitrary"), name="ffn_prompt")(hf, x, w_up, w_dw, b_dw, w_down, g_post)


def _ffn_sample_kernel(hf_ref, x_ref, cb_ref, wup_ref, wdw_ref, bdw_ref, wdn_ref, gpost_ref, y_ref, unew_ref,
                       *, T):
    nb = hf_ref.shape[0] // T
    nh = wdn_ref.shape[0]
    hf = hf_ref[...]
    f = jnp.zeros(y_ref.shape, F32)
    for c0, cw in _ffn_chunks(nh):
        cg = []
        for base in (c0, nh + c0):
            cs = slice(base, base + cw)
            u = _dot(hf, wup_ref[:, cs])
            ext = [cb_ref[j, :, cs] for j in range(CONV_W - 1)] + [u[t * nb:(t + 1) * nb] for t in range(T)]
            for j in range(CONV_W - 1):
                unew_ref[j, :, cs] = ext[len(ext) - (CONV_W - 1) + j]
            rows = []
            for t in range(T):
                c = bdw_ref[:, cs]
                for j in range(CONV_W):
                    c = c + ext[t + j] * wdw_ref[j:j + 1, cs]
                rows.append(c)
            cg.append(jnp.concatenate(rows, axis=0))
        act = (jax.nn.silu(cg[1]) * cg[0]).astype(BF16)
        f = f + _dot(act, wdn_ref[c0:c0 + cw, :])
    y_ref[...] = x_ref[...] + _rms(f, gpost_ref[...])


def _ffn_sample(hf, x, cb, w_up, w_dw, b_dw, w_down, g_post, T):
    rows, d = x.shape
    npad = w_up.shape[1]
    ins = (hf, x, cb, w_up, w_dw, b_dw, w_down, g_post)
    full = lambda a: pl.BlockSpec(a.shape, lambda i: (0,) * a.ndim)
    return pl.pallas_call(
        functools.partial(_ffn_sample_kernel, T=T), grid=(1,),
        in_specs=[full(a) for a in ins],
        out_specs=[pl.BlockSpec((rows, d), lambda i: (0, 0)),
                   pl.BlockSpec((CONV_W - 1, rows // T, npad), lambda i: (0, 0, 0))],
        out_shape=[jax.ShapeDtypeStruct((rows, d), F32),
                   jax.ShapeDtypeStruct((CONV_W - 1, rows // T, npad), F32)],
        compiler_params=_params("arbitrary"), name="ffn_sample")(*ins)


def _pad_halves(a, nh, nh_pad):
    pad = [(0, 0)] * (a.ndim - 1) + [(0, nh_pad - nh)]
    return jnp.concatenate([jnp.pad(a[..., :nh], pad), jnp.pad(a[..., nh:], pad)], axis=-1)


def _unpad_halves(a, nh, nh_pad):
    return jnp.concatenate([a[..., :nh], a[..., nh_pad:nh_pad + nh]], axis=-1)


def _row_tile(rows, want):
    t = min(rows, want)
    while rows % t:
        t //= 2
    return t


def kernel(x_prompt, x_sample, cache_dk, cache_dv, cache_mem_k, cache_mem_v, state_ml_C, state_ml_n, state_ml_m, state_conv, page_table, mem_prompt, g_mix_pre, g_mix_post, w_in, b_if, g_ml_head, da_lambda, g_da_head, w_out, g_mem_pre, g_mem_post, g_mem_src, w_mq, w_mk, w_mv, w_mo, g_ffn_pre, g_ffn_post, w_up, w_dw, b_dw, w_down):
    depth = w_in.shape[0]
    Bp, Tp, d = x_prompt.shape
    Bs, Ts, _ = x_sample.shape
    n_pages, page = page_table.shape[1], cache_dk.shape[2]
    past_len = n_pages * page
    n_mem = mem_prompt.shape[1]
    nh = w_down.shape[1]
    nh_pad = -(-nh // LANES) * LANES
    H, D = ML_HEADS, ML_HEAD_DIM
    rows_p, rows_s = Bp * Tp, Bs * Ts

    tm_p = _row_tile(Tp, 256)
    tm_s = _row_tile(rows_s, 256)
    tab_p = _rope_tables(Tp, Tp, 0)
    tab_s = _rope_tables(tm_s, Ts, past_len)

    yp = x_prompt.reshape(rows_p, d)
    ys = x_sample.reshape(rows_s, d)
    outs = [[] for _ in range(14)]
    row = lambda a: a.reshape(1, -1)
    for l in range(depth):
        lam_init = 0.8 - 0.6 * math.exp(-0.3 * l)
        wi = w_in[l]
        w_main = jnp.concatenate([wi[:, :4 * ML_WIDTH], wi[:, 4 * ML_WIDTH + 2 * H:]], axis=1).astype(BF16)
        w_gate = wi[:, 4 * ML_WIDTH:4 * ML_WIDTH + 2 * H]
        w_g = jnp.pad(w_gate, ((0, 0), (0, LANES - 2 * H))).astype(BF16)
        w_gt = w_gate.T.astype(BF16)
        b_col = jnp.pad(b_if[l], (0, LANES - 2 * H)).reshape(1, LANES)
        b_row = b_if[l].reshape(2 * H, 1)
        wo_ml, wo_da = w_out[l][:ML_WIDTH].astype(BF16), w_out[l][ML_WIDTH:].astype(BF16)
        wq_b, wo_b = w_mq[l].astype(BF16), w_mo[l].astype(BF16)
        wkv_b = jnp.concatenate([w_mk[l], w_mv[l]], axis=1).astype(BF16)
        wup_b = _pad_halves(w_up[l], nh, nh_pad).astype(BF16)
        wdw_p = _pad_halves(w_dw[l], nh, nh_pad)
        bdw_p = _pad_halves(b_dw[l].reshape(1, -1), nh, nh_pad)
        wdn_b = jnp.pad(w_down[l], ((0, nh_pad - nh), (0, 0))).astype(BF16)
        g_da3 = g_da_head[l].reshape(DA_HEADS, 1, DA_V_DIM)

        q, k, v, o, gc, gr, dq, dk, dv, dkb, dvb = _in_proj(
            yp, row(g_mix_pre[l]), w_main, w_g, w_gt, b_col, b_row, tab_p, BF16, tm_p,
            DA_QK_DIM ** -0.5 * math.log2(math.e))
        h_ml, C_p, n_p, m_p = _mlstm_prompt(q, k, v, o, gc, gr, g_ml_head[l], Bp, Tp, _row_tile(Tp, 128))
        h_da = _da_prompt(dq, dkb, dvb, da_lambda[l], g_da3, Bp, Tp, tm_p, lam_init)
        x1, qm = _proj_norm([h_ml, h_da], [wo_ml, wo_da], yp, row(g_mix_post[l]), row(g_mem_pre[l]), wq_b, tm_p)
        mkv = _norm_matmul(mem_prompt.reshape(Bp * n_mem, d), row(g_mem_src[l]), wkv_b,
                           _row_tile(Bp * n_mem, 256), 512)
        mem_shape = (1, Bp, n_mem, MEM_HEADS, d // MEM_HEADS)
        mk_p, mv_p = mkv[:, :d].reshape(mem_shape), mkv[:, d:].reshape(mem_shape)
        om = _mem_attn(qm.reshape(Bp, Tp, d), mk_p, mv_p, 0, 1, tm_p)
        x2, hf = _proj_norm([om.reshape(rows_p, d)], [wo_b], x1, row(g_mem_post[l]), row(g_ffn_pre[l]), None, tm_p)
        yp, ulast = _ffn_prompt(hf, x2, wup_b, wdw_p, bdw_p, wdn_b, row(g_ffn_post[l]), Bp, Tp, tm_p)
        cv_p = _unpad_halves(ulast[:, ulast.shape[1] - (CONV_W - 1):], nh, nh_pad)

        q, k, v, o, gc, gr, dq, dk_s, dv_s, _, _ = _in_proj(
            ys, row(g_mix_pre[l]), w_main, w_g, w_gt, b_col, b_row, tab_s, F32, tm_s, 1.0)
        r3 = lambda a: a.reshape(Bs, Ts, a.shape[-1])
        r4 = lambda a: a.reshape(Bs, Ts, DA_HEADS, DA_V_DIM)
        gr3 = gr.reshape(2 * H, Bs, Ts).transpose(1, 0, 2)
        h_ml, C_s, n_s, m_s = _mlstm_sample(
            r3(q), r3(k), r3(v), r3(o), r3(gc), gr3, g_ml_head[l],
            state_ml_C[l], state_ml_n[l].reshape(Bs, H, 1, D), state_ml_m[l].reshape(Bs, H, 1, 1),
            _row_tile(Bs, 8))
        h_da = _da_sample(page_table, r3(dq), r4(dk_s), r4(dv_s), da_lambda[l], g_da_head[l],
                          cache_dk.reshape(depth, -1, page * DA_HEADS, DA_V_DIM),
                          cache_dv.reshape(depth, -1, page * DA_HEADS, DA_V_DIM), lam_init, l)
        x1, qm = _proj_norm([h_ml.reshape(rows_s, ML_WIDTH), h_da.reshape(rows_s, DA_WIDTH)], [wo_ml, wo_da], ys,
                            row(g_mix_post[l]), row(g_mem_pre[l]), wq_b, tm_s, h_dtype=F32)
        om = _mem_attn(qm.reshape(Bs, Ts, d), cache_mem_k, cache_mem_v, l, _row_tile(Bs, 8), Ts)
        x2, hf = _proj_norm([om.reshape(rows_s, d)], [wo_b], x1, row(g_mem_post[l]), row(g_ffn_pre[l]), None, tm_s)
        tmaj = lambda a: a.reshape(Bs, Ts, -1).transpose(1, 0, 2).reshape(rows_s, -1)
        cb = _pad_halves(state_conv[l], nh, nh_pad).transpose(1, 0, 2)
        y_t, unew = _ffn_sample(tmaj(hf), tmaj(x2), cb, wup_b, wdw_p, bdw_p, wdn_b, row(g_ffn_post[l]), Ts)
        ys = y_t.reshape(Ts, Bs, d).transpose(1, 0, 2).reshape(rows_s, d)
        cv_s = _unpad_halves(unew.transpose(1, 0, 2), nh, nh_pad)

        vals = (dk.reshape(Bp, Tp, DA_HEADS, DA_V_DIM), dv.reshape(Bp, Tp, DA_HEADS, DA_V_DIM),
                mk_p[0], mv_p[0],
                C_p, n_p.reshape(Bp, H, D), m_p.reshape(Bp, H), cv_p,
                dk_s.reshape(Bs, Ts, DA_HEADS, DA_V_DIM), dv_s.reshape(Bs, Ts, DA_HEADS, DA_V_DIM),
                C_s, n_s.reshape(Bs, H, D), m_s.reshape(Bs, H), cv_s)
        for acc, val in zip(outs, vals):
            acc.append(val)
    return (yp.reshape(Bp, Tp, d), ys.reshape(Bs, Ts, d)) + tuple(jnp.stack(a) for a in outs)
```

```python
import functools
import math

import jax
import jax.numpy as jnp
from jax import lax
from jax.experimental import pallas as pl
from jax.experimental.pallas import tpu as pltpu

F32 = jnp.float32
BF16 = jnp.bfloat16

ML_HEADS = 4
ML_HEAD_DIM = 128
ML_WIDTH = ML_HEADS * ML_HEAD_DIM
DA_HEADS = 4
DA_V_DIM = 128
DA_QK_DIM = 64
DA_WIDTH = DA_HEADS * DA_V_DIM
ROPE_DIM = 16
ROPE_THETA = 500000.0
MEM_HEADS = 4
CONV_W = 3
RMS_EPS = 1e-6
LANES = 128
NEG_BIG = -1e30
VMEM_LIMIT = 56 * 1024 * 1024


def _params(*sem, vmem=VMEM_LIMIT):
    return pltpu.CompilerParams(dimension_semantics=sem, vmem_limit_bytes=vmem)


def _rms(x, g):
    return x * lax.rsqrt(jnp.mean(x * x, axis=-1, keepdims=True) + RMS_EPS) * g


def _log_sigmoid(x):
    return jnp.minimum(x, 0.0) - jnp.log1p(jnp.exp(-jnp.abs(x)))


def _dot(a, b):
    return jnp.dot(a, b, preferred_element_type=F32)


def _dot_nt(a, b):
    return lax.dot_general(a, b, (((1,), (1,)), ((), ())), preferred_element_type=F32)


def _dot_tn(a, b):
    return lax.dot_general(a, b, (((0,), (0,)), ((), ())), preferred_element_type=F32)


def _rope_table_kernel(cos_ref, sa_ref, sb_ref, *, period, offset):
    rows = cos_ref.shape[0]
    half = ROPE_DIM // 2
    r = lax.broadcasted_iota(jnp.int32, (rows, LANES), 0) + pl.program_id(0) * rows
    lane = lax.broadcasted_iota(jnp.int32, (rows, LANES), 1)
    pos = (offset + r % period).astype(F32)
    c = lane % DA_QK_DIM
    j = (c % half).astype(F32)
    inv = jnp.exp(-math.log(ROPE_THETA) * (2.0 * j / ROPE_DIM))
    ang = pos * inv
    cos, sin = jnp.cos(ang), jnp.sin(ang)
    cos_ref[...] = jnp.where(c < ROPE_DIM, cos, 1.0)
    sa_ref[...] = jnp.where(c < half, -sin, 0.0)
    sb_ref[...] = jnp.where((c >= half) & (c < ROPE_DIM), sin, 0.0)


def _rope_tables(rows, period, offset):
    blk = min(rows, 512)
    spec = pl.BlockSpec((blk, LANES), lambda i: (i, 0))
    shp = jax.ShapeDtypeStruct((rows, LANES), F32)
    return pl.pallas_call(
        functools.partial(_rope_table_kernel, period=period, offset=offset),
        grid=(rows // blk,), out_specs=[spec] * 3, out_shape=[shp] * 3,
        compiler_params=_params("parallel"), name="rope_tables")()


def _rope(x, cos, sa, sb):
    outs = []
    for j in range(x.shape[1] // LANES):
        xj = x[:, j * LANES:(j + 1) * LANES]
        up = pltpu.roll(xj, LANES - ROPE_DIM // 2, axis=1)
        dn = pltpu.roll(xj, ROPE_DIM // 2, axis=1)
        outs.append(xj * cos + up * sa + dn * sb)
    return outs


def _in_proj_kernel(x_ref, g_ref, w_ref, wg_ref, wgt_ref, bcol_ref, brow_ref, cos_ref, sa_ref, sb_ref,
                    q_ref, k_ref, v_ref, o_ref, gc_ref, gr_ref, dq_ref, dk_ref, dv_ref, dkb_ref, dvb_ref,
                    *, dq_scale):
    h = _rms(x_ref[...], g_ref[...]).astype(BF16)
    for j, ref in enumerate((q_ref, k_ref, v_ref, o_ref)):
        ref[...] = _dot(h, w_ref[:, j * ML_WIDTH:(j + 1) * ML_WIDTH]).astype(ref.dtype)
    gc = _dot(h, wg_ref[...]) + bcol_ref[...]
    lane = lax.broadcasted_iota(jnp.int32, gc.shape, 1)
    gc_ref[...] = jnp.where(lane < ML_HEADS, gc, _log_sigmoid(gc))
    gr = _dot_nt(wgt_ref[...], h) + brow_ref[...]
    row = lax.broadcasted_iota(jnp.int32, gr.shape, 0)
    gr_ref[...] = jnp.where(row < ML_HEADS, gr, _log_sigmoid(gr))
    base = 4 * ML_WIDTH
    cos, sa, sb = cos_ref[...], sa_ref[...], sb_ref[...]
    dq = _rope(_dot(h, w_ref[:, base:base + DA_WIDTH]), cos, sa, sb)
    dk = _rope(_dot(h, w_ref[:, base + DA_WIDTH:base + 2 * DA_WIDTH]), cos, sa, sb)
    dv = _dot(h, w_ref[:, base + 2 * DA_WIDTH:base + 3 * DA_WIDTH])
    dvb_ref[...] = dv.astype(BF16)
    for j in range(DA_HEADS):
        cols = slice(j * LANES, (j + 1) * LANES)
        dq_ref[:, cols] = (dq[j] * dq_scale).astype(dq_ref.dtype)
        dk_ref[:, j, :] = dk[j]
        dkb_ref[:, cols] = dk[j].astype(BF16)
        dv_ref[:, j, :] = dv[:, cols]


def _in_proj(x, g, w_main, w_g, w_gt, b_col, b_row, tables, act_dtype, tm, dq_scale):
    rows, d = x.shape
    cos, sa, sb = tables
    nt = cos.shape[0] // tm
    row_spec = lambda w: pl.BlockSpec((tm, w), lambda i: (i, 0))
    full = lambda a: pl.BlockSpec(a.shape, lambda i: (0,) * a.ndim)
    tab_spec = pl.BlockSpec((tm, LANES), lambda i: (i % nt, 0))
    out_shape = [jax.ShapeDtypeStruct((rows, ML_WIDTH), act_dtype)] * 4 + [
        jax.ShapeDtypeStruct((rows, LANES), F32), jax.ShapeDtypeStruct((8, rows), F32),
        jax.ShapeDtypeStruct((rows, DA_WIDTH), act_dtype),
        jax.ShapeDtypeStruct((rows, DA_HEADS, DA_V_DIM), F32), jax.ShapeDtypeStruct((rows, DA_HEADS, DA_V_DIM), F32),
        jax.ShapeDtypeStruct((rows, DA_WIDTH), BF16), jax.ShapeDtypeStruct((rows, DA_WIDTH), BF16)]
    kv_spec = pl.BlockSpec((tm, DA_HEADS, DA_V_DIM), lambda i: (i, 0, 0))
    out_specs = [row_spec(ML_WIDTH)] * 4 + [row_spec(LANES), pl.BlockSpec((8, tm), lambda i: (0, i)),
                                            row_spec(DA_WIDTH), kv_spec, kv_spec,
                                            row_spec(DA_WIDTH), row_spec(DA_WIDTH)]
    return pl.pallas_call(
        functools.partial(_in_proj_kernel, dq_scale=dq_scale), grid=(rows // tm,),
        in_specs=[row_spec(d), full(g), full(w_main), full(w_g), full(w_gt), full(b_col), full(b_row),
                  tab_spec, tab_spec, tab_spec],
        out_specs=out_specs, out_shape=out_shape,
        compiler_params=_params("parallel"), name="in_proj")(
            x, g, w_main, w_g, w_gt, b_col, b_row, cos, sa, sb)


def _mlstm_head(q, k, v, i_col, f_col, i_row, f_row, C, n, m):
    L, D = q.shape
    scale = D ** -0.5
    r = lax.broadcasted_iota(jnp.int32, (L, L), 0)
    c = lax.broadcasted_iota(jnp.int32, (L, L), 1)
    tri = c <= r
    b_col = jnp.sum(jnp.where(tri, f_row, 0.0), axis=1, keepdims=True)
    b_row = jnp.sum(jnp.where(r <= c, f_col, 0.0), axis=0, keepdims=True)
    log_d = jnp.where(tri, b_col - b_row + i_row, -jnp.inf)
    inter = b_col + m
    m_row = jnp.maximum(jnp.max(log_d, axis=1, keepdims=True), inter)
    w_inter = jnp.exp(inter - m_row)
    qb, kb, vb = q.astype(BF16), k.astype(BF16), v.astype(BF16)
    qf, kf = q.astype(F32), k.astype(F32)
    s = _dot_nt(qb, kb) * (scale * jnp.exp(log_d - m_row))
    num = w_inter * _dot(qb, C.astype(BF16)) + _dot(s.astype(BF16), vb)
    den = w_inter * jnp.sum(qf * n, axis=1, keepdims=True) + jnp.sum(s, axis=1, keepdims=True)
    h = num / jnp.maximum(jnp.abs(den), jnp.exp(-m_row))
    b_last = b_col[L - 1:L, :]
    log_w = b_last - b_col + i_col
    m_new = jnp.maximum(b_last + m, jnp.max(log_w, axis=0, keepdims=True))
    decay = jnp.exp(b_last + m - m_new)
    wk = (scale * jnp.exp(log_w - m_new)) * kf
    c_new = decay * C + _dot_tn(wk.astype(BF16), vb)
    n_new = decay * n + jnp.sum(wk, axis=0, keepdims=True)
    return h, c_new, n_new, m_new


def _ml_head_out(h, o, g):
    return _rms(h, g) * jax.nn.sigmoid(o.astype(F32))


def _mlstm_prompt_kernel(q_ref, k_ref, v_ref, o_ref, gc_ref, gr_ref, gh_ref,
                         h_ref, c_out, n_out, m_out, c_scr, n_scr, m_scr):
    ci = pl.program_id(1)

    @pl.when(ci == 0)
    def _():
        c_scr[...] = jnp.zeros_like(c_scr)
        n_scr[...] = jnp.zeros_like(n_scr)
        m_scr[...] = jnp.zeros_like(m_scr)

    gc, gr = gc_ref[...], gr_ref[...]
    for hh in range(ML_HEADS):
        cols = slice(hh * ML_HEAD_DIM, (hh + 1) * ML_HEAD_DIM)
        h, c_new, n_new, m_new = _mlstm_head(
            q_ref[:, cols], k_ref[:, cols], v_ref[:, cols],
            gc[:, hh:hh + 1], gc[:, ML_HEADS + hh:ML_HEADS + hh + 1],
            gr[hh:hh + 1, :], gr[ML_HEADS + hh:ML_HEADS + hh + 1, :],
            c_scr[hh], n_scr[hh], m_scr[hh])
        c_scr[hh], n_scr[hh], m_scr[hh] = c_new, n_new, m_new
        h_ref[:, cols] = _ml_head_out(h, o_ref[:, cols], gh_ref[hh:hh + 1, :]).astype(h_ref.dtype)

    @pl.when(ci == pl.num_programs(1) - 1)
    def _():
        c_out[0] = c_scr[...]
        n_out[0] = n_scr[...]
        m_out[0] = m_scr[...]


def _mlstm_prompt(q, k, v, o, gc, gr, g_head, B, T, L):
    nc = T // L
    rows = B * T
    blk = lambda w: pl.BlockSpec((L, w), lambda b, c: (b * nc + c, 0))
    H, D = ML_HEADS, ML_HEAD_DIM
    return pl.pallas_call(
        _mlstm_prompt_kernel, grid=(B, nc),
        in_specs=[blk(ML_WIDTH)] * 4 + [blk(LANES), pl.BlockSpec((8, L), lambda b, c: (0, b * nc + c)),
                                        pl.BlockSpec(g_head.shape, lambda b, c: (0, 0))],
        out_specs=[blk(ML_WIDTH),
                   pl.BlockSpec((1, H, D, D), lambda b, c: (b, 0, 0, 0)),
                   pl.BlockSpec((1, H, 1, D), lambda b, c: (b, 0, 0, 0)),
                   pl.BlockSpec((1, H, 1, 1), lambda b, c: (b, 0, 0, 0))],
        out_shape=[jax.ShapeDtypeStruct((rows, ML_WIDTH), BF16),
                   jax.ShapeDtypeStruct((B, H, D, D), F32),
                   jax.ShapeDtypeStruct((B, H, 1, D), F32),
                   jax.ShapeDtypeStruct((B, H, 1, 1), F32)],
        scratch_shapes=[pltpu.VMEM((H, D, D), F32), pltpu.VMEM((H, 1, D), F32), pltpu.VMEM((H, 1, 1), F32)],
        compiler_params=_params("parallel", "arbitrary"), name="mlstm_prompt")(q, k, v, o, gc, gr, g_head)


def _mlstm_sample_kernel(q_ref, k_ref, v_ref, o_ref, gc_ref, gr_ref, gh_ref, c_in, n_in, m_in,
                         h_ref, c_out, n_out, m_out):
    def body(bi, carry):
        gc, gr = gc_ref[bi], gr_ref[bi]
        for hh in range(ML_HEADS):
            cols = slice(hh * ML_HEAD_DIM, (hh + 1) * ML_HEAD_DIM)
            h, c_new, n_new, m_new = _mlstm_head(
                q_ref[bi, :, cols], k_ref[bi, :, cols], v_ref[bi, :, cols],
                gc[:, hh:hh + 1], gc[:, ML_HEADS + hh:ML_HEADS + hh + 1],
                gr[hh:hh + 1, :], gr[ML_HEADS + hh:ML_HEADS + hh + 1, :],
                c_in[bi, hh], n_in[bi, hh], m_in[bi, hh])
            c_out[bi, hh], n_out[bi, hh], m_out[bi, hh] = c_new, n_new, m_new
            h_ref[bi, :, cols] = _ml_head_out(h, o_ref[bi, :, cols], gh_ref[hh:hh + 1, :]).astype(h_ref.dtype)
        return carry

    lax.fori_loop(0, q_ref.shape[0], body, 0)


def _mlstm_sample(q, k, v, o, gc, gr, g_head, c0, n0, m0, bb):
    B, T, _ = q.shape
    H, D = ML_HEADS, ML_HEAD_DIM
    b3 = lambda s: pl.BlockSpec((bb,) + s, lambda i: (i, 0, 0))
    b4 = lambda s: pl.BlockSpec((bb,) + s, lambda i: (i, 0, 0, 0))
    return pl.pallas_call(
        _mlstm_sample_kernel, grid=(B // bb,),
        in_specs=[b3((T, ML_WIDTH))] * 4 + [b3((T, LANES)), b3((8, T)),
                                            pl.BlockSpec(g_head.shape, lambda i: (0, 0)),
                                            b4((H, D, D)), b4((H, 1, D)), b4((H, 1, 1))],
        out_specs=[b3((T, ML_WIDTH)), b4((H, D, D)), b4((H, 1, D)), b4((H, 1, 1))],
        out_shape=[jax.ShapeDtypeStruct((B, T, ML_WIDTH), BF16),
                   jax.ShapeDtypeStruct((B, H, D, D), F32),
                   jax.ShapeDtypeStruct((B, H, 1, D), F32),
                   jax.ShapeDtypeStruct((B, H, 1, 1), F32)],
        compiler_params=_params("parallel"), name="mlstm_sample")(q, k, v, o, gc, gr, g_head, c0, n0, m0)


def _da_lambda(lam_ref, lam_init):
    lv = lam_ref[...]
    a = jnp.sum(lv[0:1, :] * lv[1:2, :], axis=1, keepdims=True)
    b = jnp.sum(lv[2:3, :] * lv[3:4, :], axis=1, keepdims=True)
    return jnp.exp(a) - jnp.exp(b) + lam_init


def _stack_components(q):
    lane = lax.broadcasted_iota(jnp.int32, q.shape, 1)
    zero = jnp.zeros_like(q)
    return jnp.concatenate([jnp.where(lane < DA_QK_DIM, q, zero), jnp.where(lane >= DA_QK_DIM, q, zero)], axis=0)


def _lane_fold(x, op):
    out = x[:, :LANES]
    for c in range(1, x.shape[1] // LANES):
        out = op(out, x[:, c * LANES:(c + 1) * LANES])
    return out


def _da_prompt_kernel(q_ref, kb_scr, vb_scr, lam_ref, gh_ref, out_ref, m_scr, l_scr, acc_scr, *, lam_init):
    i = pl.program_id(2)
    tq = q_ref.shape[0]
    rows = 2 * tq
    qq = _stack_components(q_ref[...])
    causal = (lax.broadcasted_iota(jnp.int32, (rows, tq), 1)
              <= lax.broadcasted_iota(jnp.int32, (rows, tq), 0) % tq)

    def scores(j):
        off = pl.multiple_of(j * tq, tq)
        return _dot_nt(qq, kb_scr[pl.ds(off, tq), :]), off

    s_diag, off_diag = scores(i)
    s_diag = jnp.where(causal, s_diag, NEG_BIG)
    m_scr[...] = _lane_fold(s_diag, jnp.maximum)

    def max_body(j, carry):
        m_scr[...] = jnp.maximum(m_scr[...], _lane_fold(scores(j)[0], jnp.maximum))
        return carry

    lax.fori_loop(0, i, max_body, 0)
    m_rep = jnp.broadcast_to(jnp.max(m_scr[...], axis=1, keepdims=True), (rows, LANES))
    m_scr[...] = m_rep

    def weights(s, m):
        p = [jnp.exp2(s[:, c * LANES:(c + 1) * LANES] - m) for c in range(tq // LANES)]
        l = p[0]
        for pc in p[1:]:
            l = l + pc
        return jnp.concatenate([pc.astype(BF16) for pc in p], axis=1), l

    p, l = weights(s_diag, m_rep)
    l_scr[...] = l
    acc_scr[...] = _dot(p, vb_scr[pl.ds(off_diag, tq), :])

    def sum_body(j, carry):
        s, off = scores(j)
        p, l = weights(s, m_scr[...])
        l_scr[...] += l
        acc_scr[...] += _dot(p, vb_scr[pl.ds(off, tq), :])
        return carry

    lax.fori_loop(0, i, sum_body, 0)
    o = acc_scr[...] / jnp.sum(l_scr[...], axis=1, keepdims=True)
    a = o[:tq] - _da_lambda(lam_ref, lam_init) * o[tq:]
    out_ref[...] = (_rms(a, gh_ref[0]) * (1.0 - lam_init)).astype(out_ref.dtype)


def _da_prompt(dq, dk, dv, da_lambda, g_head3, B, T, tq, lam_init):
    nq = T // tq
    rows = B * T
    kv_spec = pl.BlockSpec((T, DA_V_DIM), lambda b, h, i: (b, h))
    return pl.pallas_call(
        functools.partial(_da_prompt_kernel, lam_init=lam_init), grid=(B, DA_HEADS, nq),
        in_specs=[pl.BlockSpec((tq, DA_V_DIM), lambda b, h, i: (b * nq + i, h)), kv_spec, kv_spec,
                  pl.BlockSpec(da_lambda.shape, lambda b, h, i: (0, 0)),
                  pl.BlockSpec((1, 1, DA_V_DIM), lambda b, h, i: (h, 0, 0))],
        out_specs=pl.BlockSpec((tq, DA_V_DIM), lambda b, h, i: (b * nq + i, h)),
        out_shape=jax.ShapeDtypeStruct((rows, DA_WIDTH), BF16),
        scratch_shapes=[pltpu.VMEM((2 * tq, LANES), F32), pltpu.VMEM((2 * tq, LANES), F32),
                        pltpu.VMEM((2 * tq, DA_V_DIM), F32)],
        compiler_params=_params("parallel", "parallel", "arbitrary"), name="da_prompt")(
            dq, dk, dv, da_lambda, g_head3)


def _da_sample_kernel(pt_ref, q_ref, kn_ref, vn_ref, lam_ref, gh_ref, ck_hbm, cv_hbm, out_ref,
                      kbuf, vbuf, sem, *, lam_init, layer):
    b = pl.program_id(0)
    nb = pl.num_programs(0)
    n_pages, page = kbuf.shape[1], kbuf.shape[2] // DA_HEADS
    T = q_ref.shape[1]
    scale = DA_QK_DIM ** -0.5

    def copies(bi, slot):
        out = []
        for p in range(n_pages):
            pg = pt_ref[bi, p]
            out.append(pltpu.make_async_copy(ck_hbm.at[layer, pg], kbuf.at[slot, p], sem.at[slot, 0]))
            out.append(pltpu.make_async_copy(cv_hbm.at[layer, pg], vbuf.at[slot, p], sem.at[slot, 1]))
        return out

    @pl.when(b == 0)
    def _():
        for cp in copies(0, 0):
            cp.start()

    @pl.when(b + 1 < nb)
    def _():
        for cp in copies(b + 1, (b + 1) % 2):
            cp.start()

    slot = b % 2
    for cp in copies(b, slot):
        cp.wait()

    lam = _da_lambda(lam_ref, lam_init)
    q_all = q_ref[0]
    n_rows = n_pages * page * DA_HEADS
    qq = jnp.concatenate([_stack_components(q_all[:, hh * DA_V_DIM:(hh + 1) * DA_V_DIM])
                          for hh in range(DA_HEADS)], axis=0)
    k_all = kbuf[slot].reshape(n_rows, DA_V_DIM).astype(BF16)
    v_all = vbuf[slot].reshape(n_rows, DA_V_DIM).astype(BF16)
    rq = 2 * T * DA_HEADS
    own_head = (lax.broadcasted_iota(jnp.int32, (rq, n_rows), 1) % DA_HEADS
                == lax.broadcasted_iota(jnp.int32, (rq, n_rows), 0) // (2 * T))
    s_past = jnp.where(own_head, _dot_nt(qq.astype(BF16), k_all) * scale, NEG_BIG)
    trow = lax.broadcasted_iota(jnp.int32, (rq, 1), 0) % T

    def per_query_row(new_ref, t):
        x = new_ref[0, t]
        return jnp.concatenate([jnp.broadcast_to(x[hh:hh + 1, :], (2 * T, DA_V_DIM))
                                for hh in range(DA_HEADS)], axis=0)

    s_new = [jnp.where(trow >= t,
                       jnp.sum(qq * per_query_row(kn_ref, t), axis=1, keepdims=True) * scale, NEG_BIG)
             for t in range(T)]
    m = jnp.max(s_past, axis=1, keepdims=True)
    for t in range(T):
        m = jnp.maximum(m, s_new[t])
    p_past = jnp.exp(s_past - m)
    l = jnp.sum(p_past, axis=1, keepdims=True)
    acc = _dot(p_past.astype(BF16), v_all)
    for t in range(T):
        p_t = jnp.exp(s_new[t] - m)
        l = l + p_t
        acc = acc + p_t * per_query_row(vn_ref, t)
    o = acc / l
    for hh in range(DA_HEADS):
        r0 = hh * 2 * T
        a = o[r0:r0 + T] - lam * o[r0 + T:r0 + 2 * T]
        out_ref[0, :, hh * DA_V_DIM:(hh + 1) * DA_V_DIM] = (
            _rms(a, gh_ref[hh:hh + 1, :]) * (1.0 - lam_init)).astype(out_ref.dtype)


def _da_sample(page_table, dq, dk, dv, da_lambda, g_head, cache_k, cache_v, lam_init, layer):
    B, T, _ = dq.shape
    n_pages = page_table.shape[1]
    page_rows = cache_k.shape[2]
    blk = pl.BlockSpec((1, T, DA_WIDTH), lambda b, pt: (b, 0, 0))
    kv_blk = pl.BlockSpec((1, T, DA_HEADS, DA_V_DIM), lambda b, pt: (b, 0, 0, 0))
    grid_spec = pltpu.PrefetchScalarGridSpec(
        num_scalar_prefetch=1, grid=(B,),
        in_specs=[blk, kv_blk, kv_blk, pl.BlockSpec(da_lambda.shape, lambda b, pt: (0, 0)),
                  pl.BlockSpec(g_head.shape, lambda b, pt: (0, 0)),
                  pl.BlockSpec(memory_space=pl.ANY), pl.BlockSpec(memory_space=pl.ANY)],
        out_specs=blk,
        scratch_shapes=[pltpu.VMEM((2, n_pages, page_rows, DA_V_DIM), F32),
                        pltpu.VMEM((2, n_pages, page_rows, DA_V_DIM), F32),
                        pltpu.SemaphoreType.DMA((2, 2))])
    return pl.pallas_call(
        functools.partial(_da_sample_kernel, lam_init=lam_init, layer=layer), grid_spec=grid_spec,
        out_shape=jax.ShapeDtypeStruct((B, T, DA_WIDTH), BF16),
        compiler_params=_params("arbitrary"), name="da_sample")(
            page_table, dq, dk, dv, da_lambda, g_head, cache_k, cache_v)


def _proj_norm_kernel(*refs, n_in, has_next):
    a_refs, w_refs = refs[:n_in], refs[n_in:2 * n_in]
    x_ref, gpost_ref, gpre_ref = refs[2 * n_in:2 * n_in + 3]
    rest = refs[2 * n_in + 3:]
    wn_ref = rest[0] if has_next else None
    xo_ref, ho_ref = rest[-2:]
    acc = _dot(a_refs[0][...], w_refs[0][...])
    for a, w in zip(a_refs[1:], w_refs[1:]):
        acc = acc + _dot(a[...], w[...])
    x1 = x_ref[...] + _rms(acc, gpost_ref[...])
    xo_ref[...] = x1
    hn = _rms(x1, gpre_ref[...]).astype(BF16)
    ho_ref[...] = (_dot(hn, wn_ref[...]) if has_next else hn).astype(ho_ref.dtype)


def _proj_norm(a_list, w_list, x, g_post, g_pre, w_next, tm, h_dtype=BF16):
    rows, d = x.shape
    n_in = len(a_list)
    has_next = w_next is not None
    row_spec = lambda w: pl.BlockSpec((tm, w), lambda i: (i, 0))
    full = lambda a: pl.BlockSpec(a.shape, lambda i: (0,) * a.ndim)
    ins = list(a_list) + list(w_list) + [x, g_post, g_pre] + ([w_next] if has_next else [])
    in_specs = ([row_spec(a.shape[1]) for a in a_list] + [full(w) for w in w_list]
                + [row_spec(d), full(g_post), full(g_pre)] + ([full(w_next)] if has_next else []))
    n_out = w_next.shape[1] if has_next else d
    return pl.pallas_call(
        functools.partial(_proj_norm_kernel, n_in=n_in, has_next=has_next), grid=(rows // tm,),
        in_specs=in_specs, out_specs=[row_spec(d), row_spec(n_out)],
        out_shape=[jax.ShapeDtypeStruct((rows, d), F32), jax.ShapeDtypeStruct((rows, n_out), h_dtype)],
        compiler_params=_params("parallel"), name="proj_norm")(*ins)


def _norm_matmul_kernel(x_ref, g_ref, w_ref, o_ref):
    o_ref[...] = _dot(_rms(x_ref[...], g_ref[...]).astype(BF16), w_ref[...])


def _norm_matmul(x, g, w, tm, tn):
    rows, d = x.shape
    n = w.shape[1]
    return pl.pallas_call(
        _norm_matmul_kernel, grid=(rows // tm, n // tn),
        in_specs=[pl.BlockSpec((tm, d), lambda i, j: (i, 0)), pl.BlockSpec(g.shape, lambda i, j: (0, 0)),
                  pl.BlockSpec((d, tn), lambda i, j: (0, j))],
        out_specs=pl.BlockSpec((tm, tn), lambda i, j: (i, j)),
        out_shape=jax.ShapeDtypeStruct((rows, n), F32),
        compiler_params=_params("parallel", "parallel"), name="mem_kv")(x, g, w)


def _mem_attn_head(q, k, v):
    s = _dot_nt(q.astype(BF16), k.astype(BF16)) * (q.shape[1] ** -0.5)
    p = jnp.exp(s - jnp.max(s, axis=1, keepdims=True))
    return _dot(p.astype(BF16), v.astype(BF16)) / jnp.sum(p, axis=1, keepdims=True)


def _mem_attn_kernel(q_ref, mk_ref, mv_ref, o_ref):
    hd = q_ref.shape[2] // MEM_HEADS
    for bi in range(q_ref.shape[0]):
        for hh in range(MEM_HEADS):
            cols = slice(hh * hd, (hh + 1) * hd)
            o_ref[bi, :, cols] = _mem_attn_head(
                q_ref[bi, :, cols], mk_ref[bi, :, cols], mv_ref[bi, :, cols]).astype(o_ref.dtype)


def _mem_attn(q, mk, mv, bb, tm):
    B, T, d = q.shape
    kv_spec = pl.BlockSpec((bb,) + mk.shape[1:], lambda b, i: (b, 0, 0))
    return pl.pallas_call(
        _mem_attn_kernel, grid=(B // bb, T // tm),
        in_specs=[pl.BlockSpec((bb, tm, d), lambda b, i: (b, i, 0)), kv_spec, kv_spec],
        out_specs=pl.BlockSpec((bb, tm, d), lambda b, i: (b, i, 0)),
        out_shape=jax.ShapeDtypeStruct((B, T, d), BF16),
        compiler_params=_params("parallel", "parallel"), name="mem_attn")(q, mk, mv)


def _mem_attn_cache_kernel(q_ref, mk_hbm, mv_hbm, o_ref, kbuf, vbuf, sem, *, layer):
    g = pl.program_id(0)
    bb = q_ref.shape[0]
    hd = kbuf.shape[-1]

    def copies(gi, slot):
        out = []
        for bi in range(bb):
            for hh in range(MEM_HEADS):
                b = gi * bb + bi
                out.append(pltpu.make_async_copy(mk_hbm.at[layer, b, :, hh, :], kbuf.at[slot, bi, hh],
                                                 sem.at[slot, 0]))
                out.append(pltpu.make_async_copy(mv_hbm.at[layer, b, :, hh, :], vbuf.at[slot, bi, hh],
                                                 sem.at[slot, 1]))
        return out

    @pl.when(g == 0)
    def _():
        for cp in copies(0, 0):
            cp.start()

    @pl.when(g + 1 < pl.num_programs(0))
    def _():
        for cp in copies(g + 1, (g + 1) % 2):
            cp.start()

    slot = g % 2
    for cp in copies(g, slot):
        cp.wait()
    for bi in range(bb):
        for hh in range(MEM_HEADS):
            cols = slice(hh * hd, (hh + 1) * hd)
            o_ref[bi, :, cols] = _mem_attn_head(
                q_ref[bi, :, cols], kbuf[slot, bi, hh], vbuf[slot, bi, hh]).astype(o_ref.dtype)


def _mem_attn_cache(q, mk, mv, layer, bb):
    B, T, d = q.shape
    M, H, hd = mk.shape[2:]
    return pl.pallas_call(
        functools.partial(_mem_attn_cache_kernel, layer=layer), grid=(B // bb,),
        in_specs=[pl.BlockSpec((bb, T, d), lambda g: (g, 0, 0)),
                  pl.BlockSpec(memory_space=pl.ANY), pl.BlockSpec(memory_space=pl.ANY)],
        out_specs=pl.BlockSpec((bb, T, d), lambda g: (g, 0, 0)),
        out_shape=jax.ShapeDtypeStruct((B, T, d), BF16),
        scratch_shapes=[pltpu.VMEM((2, bb, H, M, hd), F32), pltpu.VMEM((2, bb, H, M, hd), F32),
                        pltpu.SemaphoreType.DMA((2, 2))],
        compiler_params=_params("arbitrary"), name="mem_attn_cache")(q, mk, mv)


def _ffn_chunks(nh):
    nchunk = 2 if (nh // LANES) % 2 == 0 else 1
    cw = nh // nchunk
    return [(j * cw, cw) for j in range(nchunk)]


def _ffn_prompt_kernel(hf_ref, x_ref, wup_ref, wdw_ref, bdw_ref, wdn_ref, gpost_ref, y_ref, ulast_ref, ubuf,
                       *, tiles_per_seq):
    i = pl.program_id(0)
    tm = hf_ref.shape[0]
    nh = wdn_ref.shape[0]
    halo = ubuf.shape[0] - tm

    @pl.when(i % tiles_per_seq == 0)
    def _():
        ubuf[0:halo, :] = jnp.zeros((halo, ubuf.shape[1]), F32)

    hf = hf_ref[...]
    f = jnp.zeros((tm, y_ref.shape[1]), F32)
    for c0, cw in _ffn_chunks(nh):
        cg = []
        for base in (c0, nh + c0):
            cs = slice(base, base + cw)
            ubuf[halo:halo + tm, cs] = _dot(hf, wup_ref[:, cs])
            c = bdw_ref[:, cs]
            for j in range(CONV_W):
                lo = halo - (CONV_W - 1) + j
                c = c + ubuf[lo:lo + tm, cs] * wdw_ref[j:j + 1, cs]
            cg.append(c)
        act = (jax.nn.silu(cg[1]) * cg[0]).astype(BF16)
        f = f + _dot(act, wdn_ref[c0:c0 + cw, :])
    y_ref[...] = x_ref[...] + _rms(f, gpost_ref[...])
    tail = ubuf[tm:tm + halo, :]
    ubuf[0:halo, :] = tail
    ulast_ref[0] = tail


def _ffn_prompt(hf, x, w_up, w_dw, b_dw, w_down, g_post, B, T, tm):
    rows, d = x.shape
    npad = w_up.shape[1]
    tiles = T // tm
    halo = 8
    row_spec = lambda w: pl.BlockSpec((tm, w), lambda i: (i, 0))
    full = lambda a: pl.BlockSpec(a.shape, lambda i: (0,) * a.ndim)
    return pl.pallas_call(
        functools.partial(_ffn_prompt_kernel, tiles_per_seq=tiles), grid=(rows // tm,),
        in_specs=[row_spec(d), row_spec(d), full(w_up), full(w_dw), full(b_dw), full(w_down), full(g_post)],
        out_specs=[row_spec(d), pl.BlockSpec((1, halo, npad), lambda i: (i // tiles, 0, 0))],
        out_shape=[jax.ShapeDtypeStruct((rows, d), F32), jax.ShapeDtypeStruct((B, halo, npad), F32)],
        scratch_shapes=[pltpu.VMEM((tm + halo, npad), F32)],
        compiler_params=_params("arbitrary"), name="ffn_prompt")(hf, x, w_up, w_dw, b_dw, w_down, g_post)


def _ffn_sample_kernel(hf_ref, x_ref, cb_ref, wup_ref, wdw_ref, bdw_ref, wdn_ref, gpost_ref, y_ref, unew_ref,
                       *, T):
    nb = hf_ref.shape[0] // T
    nh = wdn_ref.shape[0]
    hf = hf_ref[...]
    f = jnp.zeros(y_ref.shape, F32)
    for c0, cw in _ffn_chunks(nh):
        cg = []
        for base in (c0, nh + c0):
            cs = slice(base, base + cw)
            u = _dot(hf, wup_ref[:, cs])
            ext = [cb_ref[j, :, cs] for j in range(CONV_W - 1)] + [u[t * nb:(t + 1) * nb] for t in range(T)]
            for j in range(CONV_W - 1):
                unew_ref[j, :, cs] = ext[len(ext) - (CONV_W - 1) + j]
            rows = []
            for t in range(T):
                c = bdw_ref[:, cs]
                for j in range(CONV_W):
                    c = c + ext[t + j] * wdw_ref[j:j + 1, cs]
                rows.append(c)
            cg.append(jnp.concatenate(rows, axis=0))
        act = (jax.nn.silu(cg[1]) * cg[0]).astype(BF16)
        f = f + _dot(act, wdn_ref[c0:c0 + cw, :])
    y_ref[...] = x_ref[...] + _rms(f, gpost_ref[...])


def _ffn_sample(hf, x, cb, w_up, w_dw, b_dw, w_down, g_post, T):
    rows, d = x.shape
    npad = w_up.shape[1]
    ins = (hf, x, cb, w_up, w_dw, b_dw, w_down, g_post)
    full = lambda a: pl.BlockSpec(a.shape, lambda i: (0,) * a.ndim)
    return pl.pallas_call(
        functools.partial(_ffn_sample_kernel, T=T), grid=(1,),
        in_specs=[full(a) for a in ins],
        out_specs=[pl.BlockSpec((rows, d), lambda i: (0, 0)),
                   pl.BlockSpec((CONV_W - 1, rows // T, npad), lambda i: (0, 0, 0))],
        out_shape=[jax.ShapeDtypeStruct((rows, d), F32),
                   jax.ShapeDtypeStruct((CONV_W - 1, rows // T, npad), F32)],
        compiler_params=_params("arbitrary"), name="ffn_sample")(*ins)


def _pad_halves(a, nh, nh_pad):
    pad = [(0, 0)] * (a.ndim - 1) + [(0, nh_pad - nh)]
    return jnp.concatenate([jnp.pad(a[..., :nh], pad), jnp.pad(a[..., nh:], pad)], axis=-1)


def _unpad_halves(a, nh, nh_pad):
    return jnp.concatenate([a[..., :nh], a[..., nh_pad:nh_pad + nh]], axis=-1)


def _row_tile(rows, want):
    t = min(rows, want)
    while rows % t:
        t //= 2
    return t


def kernel(x_prompt, x_sample, cache_dk, cache_dv, cache_mem_k, cache_mem_v, state_ml_C, state_ml_n, state_ml_m, state_conv, page_table, mem_prompt, g_mix_pre, g_mix_post, w_in, b_if, g_ml_head, da_lambda, g_da_head, w_out, g_mem_pre, g_mem_post, g_mem_src, w_mq, w_mk, w_mv, w_mo, g_ffn_pre, g_ffn_post, w_up, w_dw, b_dw, w_down):
    depth = w_in.shape[0]
    Bp, Tp, d = x_prompt.shape
    Bs, Ts, _ = x_sample.shape
    n_pages, page = page_table.shape[1], cache_dk.shape[2]
    past_len = n_pages * page
    n_mem = mem_prompt.shape[1]
    nh = w_down.shape[1]
    nh_pad = -(-nh // LANES) * LANES
    H, D = ML_HEADS, ML_HEAD_DIM
    rows_p, rows_s = Bp * Tp, Bs * Ts

    tm_p = _row_tile(Tp, 256)
    tm_s = _row_tile(rows_s, 256)
    tab_p = _rope_tables(Tp, Tp, 0)
    tab_s = _rope_tables(tm_s, Ts, past_len)

    yp = x_prompt.reshape(rows_p, d)
    ys = x_sample.reshape(rows_s, d)
    outs = [[] for _ in range(14)]
    row = lambda a: a.reshape(1, -1)
    for l in range(depth):
        lam_init = 0.8 - 0.6 * math.exp(-0.3 * l)
        wi = w_in[l]
        w_main = jnp.concatenate([wi[:, :4 * ML_WIDTH], wi[:, 4 * ML_WIDTH + 2 * H:]], axis=1).astype(BF16)
        w_gate = wi[:, 4 * ML_WIDTH:4 * ML_WIDTH + 2 * H]
        w_g = jnp.pad(w_gate, ((0, 0), (0, LANES - 2 * H))).astype(BF16)
        w_gt = w_gate.T.astype(BF16)
        b_col = jnp.pad(b_if[l], (0, LANES - 2 * H)).reshape(1, LANES)
        b_row = b_if[l].reshape(2 * H, 1)
        wo_ml, wo_da = w_out[l][:ML_WIDTH].astype(BF16), w_out[l][ML_WIDTH:].astype(BF16)
        wq_b, wo_b = w_mq[l].astype(BF16), w_mo[l].astype(BF16)
        wkv_b = jnp.concatenate([w_mk[l], w_mv[l]], axis=1).astype(BF16)
        wup_b = _pad_halves(w_up[l], nh, nh_pad).astype(BF16)
        wdw_p = _pad_halves(w_dw[l], nh, nh_pad)
        bdw_p = _pad_halves(b_dw[l].reshape(1, -1), nh, nh_pad)
        wdn_b = jnp.pad(w_down[l], ((0, nh_pad - nh), (0, 0))).astype(BF16)
        g_da3 = g_da_head[l].reshape(DA_HEADS, 1, DA_V_DIM)

        q, k, v, o, gc, gr, dq, dk, dv, dkb, dvb = _in_proj(
            yp, row(g_mix_pre[l]), w_main, w_g, w_gt, b_col, b_row, tab_p, BF16, tm_p,
            DA_QK_DIM ** -0.5 * math.log2(math.e))
        h_ml, C_p, n_p, m_p = _mlstm_prompt(q, k, v, o, gc, gr, g_ml_head[l], Bp, Tp, _row_tile(Tp, 128))
        h_da = _da_prompt(dq, dkb, dvb, da_lambda[l], g_da3, Bp, Tp, _row_tile(Tp, 512), lam_init)
        x1, qm = _proj_norm([h_ml, h_da], [wo_ml, wo_da], yp, row(g_mix_post[l]), row(g_mem_pre[l]), wq_b, tm_p)
        mkv = _norm_matmul(mem_prompt.reshape(Bp * n_mem, d), row(g_mem_src[l]), wkv_b,
                           _row_tile(Bp * n_mem, 256), 512)
        mk_p, mv_p = mkv[:, :d].reshape(Bp, n_mem, d), mkv[:, d:].reshape(Bp, n_mem, d)
        om = _mem_attn(qm.reshape(Bp, Tp, d), mk_p, mv_p, 1, tm_p)
        x2, hf = _proj_norm([om.reshape(rows_p, d)], [wo_b], x1, row(g_mem_post[l]), row(g_ffn_pre[l]), None, tm_p)
        yp, ulast = _ffn_prompt(hf, x2, wup_b, wdw_p, bdw_p, wdn_b, row(g_ffn_post[l]), Bp, Tp, tm_p)
        cv_p = _unpad_halves(ulast[:, ulast.shape[1] - (CONV_W - 1):], nh, nh_pad)

        q, k, v, o, gc, gr, dq, dk_s, dv_s, _, _ = _in_proj(
            ys, row(g_mix_pre[l]), w_main, w_g, w_gt, b_col, b_row, tab_s, F32, tm_s, 1.0)
        r3 = lambda a: a.reshape(Bs, Ts, a.shape[-1])
        r4 = lambda a: a.reshape(Bs, Ts, DA_HEADS, DA_V_DIM)
        gr3 = gr.reshape(2 * H, Bs, Ts).transpose(1, 0, 2)
        h_ml, C_s, n_s, m_s = _mlstm_sample(
            r3(q), r3(k), r3(v), r3(o), r3(gc), gr3, g_ml_head[l],
            state_ml_C[l], state_ml_n[l].reshape(Bs, H, 1, D), state_ml_m[l].reshape(Bs, H, 1, 1),
            _row_tile(Bs, 8))
        h_da = _da_sample(page_table, r3(dq), r4(dk_s), r4(dv_s), da_lambda[l], g_da_head[l],
                          cache_dk.reshape(depth, -1, page * DA_HEADS, DA_V_DIM),
                          cache_dv.reshape(depth, -1, page * DA_HEADS, DA_V_DIM), lam_init, l)
        x1, qm = _proj_norm([h_ml.reshape(rows_s, ML_WIDTH), h_da.reshape(rows_s, DA_WIDTH)], [wo_ml, wo_da], ys,
                            row(g_mix_post[l]), row(g_mem_pre[l]), wq_b, tm_s, h_dtype=F32)
        om = _mem_attn_cache(qm.reshape(Bs, Ts, d), cache_mem_k, cache_mem_v, l, _row_tile(Bs, 4))
        x2, hf = _proj_norm([om.reshape(rows_s, d)], [wo_b], x1, row(g_mem_post[l]), row(g_ffn_pre[l]), None, tm_s)
        tmaj = lambda a: a.reshape(Bs, Ts, -1).transpose(1, 0, 2).reshape(rows_s, -1)
        cb = _pad_halves(state_conv[l], nh, nh_pad).transpose(1, 0, 2)
        y_t, unew = _ffn_sample(tmaj(hf), tmaj(x2), cb, wup_b, wdw_p, bdw_p, wdn_b, row(g_ffn_post[l]), Ts)
        ys = y_t.reshape(Ts, Bs, d).transpose(1, 0, 2).reshape(rows_s, d)
        cv_s = _unpad_halves(unew.transpose(1, 0, 2), nh, nh_pad)

        vals = (dk.reshape(Bp, Tp, DA_HEADS, DA_V_DIM), dv.reshape(Bp, Tp, DA_HEADS, DA_V_DIM),
                mk_p.reshape(Bp, n_mem, MEM_HEADS, d // MEM_HEADS), mv_p.reshape(Bp, n_mem, MEM_HEADS, d // MEM_HEADS),
                C_p, n_p.reshape(Bp, H, D), m_p.reshape(Bp, H), cv_p,
                dk_s.reshape(Bs, Ts, DA_HEADS, DA_V_DIM), dv_s.reshape(Bs, Ts, DA_HEADS, DA_V_DIM),
                C_s, n_s.reshape(Bs, H, D), m_s.reshape(Bs, H), cv_s)
        for acc, val in zip(outs, vals):
            acc.append(val)
    return (yp.reshape(Bp, Tp, d), ys.reshape(Bs, Ts, d)) + tuple(jnp.stack(a) for a in outs)
```

```python
import functools
import math

import jax
import jax.numpy as jnp
from jax import lax
from jax.experimental import pallas as pl
from jax.experimental.pallas import tpu as pltpu

F32 = jnp.float32
BF16 = jnp.bfloat16

ML_HEADS = 4
ML_HEAD_DIM = 128
ML_WIDTH = ML_HEADS * ML_HEAD_DIM
DA_HEADS = 4
DA_V_DIM = 128
DA_QK_DIM = 64
DA_WIDTH = DA_HEADS * DA_V_DIM
ROPE_DIM = 16
ROPE_THETA = 500000.0
MEM_HEADS = 4
CONV_W = 3
RMS_EPS = 1e-6
LANES = 128
NEG_BIG = -1e30
VMEM_LIMIT = 56 * 1024 * 1024


def _params(*sem, vmem=VMEM_LIMIT):
    return pltpu.CompilerParams(dimension_semantics=sem, vmem_limit_bytes=vmem)


def _resident(a):
    return pl.BlockSpec(a.shape, lambda *_: (0,) * a.ndim, pipeline_mode=pl.Buffered(1))


def _rms(x, g):
    return x * lax.rsqrt(jnp.mean(x * x, axis=-1, keepdims=True) + RMS_EPS) * g


def _log_sigmoid(x):
    return jnp.minimum(x, 0.0) - jnp.log1p(jnp.exp(-jnp.abs(x)))


def _dot(a, b):
    return jnp.dot(a, b, preferred_element_type=F32)


def _dot_nt(a, b):
    return lax.dot_general(a, b, (((1,), (1,)), ((), ())), preferred_element_type=F32)


def _dot_tn(a, b):
    return lax.dot_general(a, b, (((0,), (0,)), ((), ())), preferred_element_type=F32)


def _rope_table_kernel(cos_ref, sa_ref, sb_ref, *, period, offset):
    rows = cos_ref.shape[0]
    half = ROPE_DIM // 2
    r = lax.broadcasted_iota(jnp.int32, (rows, LANES), 0) + pl.program_id(0) * rows
    lane = lax.broadcasted_iota(jnp.int32, (rows, LANES), 1)
    pos = (offset + r % period).astype(F32)
    c = lane % DA_QK_DIM
    j = (c % half).astype(F32)
    inv = jnp.exp(-math.log(ROPE_THETA) * (2.0 * j / ROPE_DIM))
    ang = pos * inv
    cos, sin = jnp.cos(ang), jnp.sin(ang)
    cos_ref[...] = jnp.where(c < ROPE_DIM, cos, 1.0)
    sa_ref[...] = jnp.where(c < half, -sin, 0.0)
    sb_ref[...] = jnp.where((c >= half) & (c < ROPE_DIM), sin, 0.0)


def _rope_tables(rows, period, offset):
    blk = min(rows, 512)
    spec = pl.BlockSpec((blk, LANES), lambda i: (i, 0))
    shp = jax.ShapeDtypeStruct((rows, LANES), F32)
    return pl.pallas_call(
        functools.partial(_rope_table_kernel, period=period, offset=offset),
        grid=(rows // blk,), out_specs=[spec] * 3, out_shape=[shp] * 3,
        compiler_params=_params("parallel"), name="rope_tables")()


def _rope(x, cos, sa, sb):
    outs = []
    for j in range(x.shape[1] // LANES):
        xj = x[:, j * LANES:(j + 1) * LANES]
        up = pltpu.roll(xj, LANES - ROPE_DIM // 2, axis=1)
        dn = pltpu.roll(xj, ROPE_DIM // 2, axis=1)
        outs.append(xj * cos + up * sa + dn * sb)
    return outs


def _in_proj_kernel(x_ref, g_ref, w_ref, wg_ref, wgt_ref, bcol_ref, brow_ref, cos_ref, sa_ref, sb_ref,
                    q_ref, k_ref, v_ref, o_ref, gc_ref, gr_ref, dq_ref, dk_ref, dv_ref, dkb_ref, dvb_ref,
                    *, dq_scale):
    h = _rms(x_ref[...], g_ref[...]).astype(BF16)
    for j, ref in enumerate((q_ref, k_ref, v_ref, o_ref)):
        ref[...] = _dot(h, w_ref[:, j * ML_WIDTH:(j + 1) * ML_WIDTH]).astype(ref.dtype)
    gc = _dot(h, wg_ref[...]) + bcol_ref[...]
    lane = lax.broadcasted_iota(jnp.int32, gc.shape, 1)
    gc_ref[...] = jnp.where(lane < ML_HEADS, gc, _log_sigmoid(gc))
    gr = _dot_nt(wgt_ref[...], h) + brow_ref[...]
    row = lax.broadcasted_iota(jnp.int32, gr.shape, 0)
    gr_ref[...] = jnp.where(row < ML_HEADS, gr, _log_sigmoid(gr))
    base = 4 * ML_WIDTH
    cos, sa, sb = cos_ref[...], sa_ref[...], sb_ref[...]
    dq = _rope(_dot(h, w_ref[:, base:base + DA_WIDTH]), cos, sa, sb)
    dk = _rope(_dot(h, w_ref[:, base + DA_WIDTH:base + 2 * DA_WIDTH]), cos, sa, sb)
    dv = _dot(h, w_ref[:, base + 2 * DA_WIDTH:base + 3 * DA_WIDTH])
    dvb_ref[...] = dv.astype(BF16)
    for j in range(DA_HEADS):
        cols = slice(j * LANES, (j + 1) * LANES)
        dq_ref[:, cols] = (dq[j] * dq_scale).astype(dq_ref.dtype)
        dk_ref[:, j, :] = dk[j]
        dkb_ref[:, cols] = dk[j].astype(BF16)
        dv_ref[:, j, :] = dv[:, cols]


def _in_proj(x, g, w_main, w_g, w_gt, b_col, b_row, tables, act_dtype, tm, dq_scale):
    rows, d = x.shape
    cos, sa, sb = tables
    nt = cos.shape[0] // tm
    row_spec = lambda w: pl.BlockSpec((tm, w), lambda i: (i, 0))
    full = lambda a: pl.BlockSpec(a.shape, lambda i: (0,) * a.ndim)
    tab_spec = pl.BlockSpec((tm, LANES), lambda i: (i % nt, 0))
    out_shape = [jax.ShapeDtypeStruct((rows, ML_WIDTH), act_dtype)] * 4 + [
        jax.ShapeDtypeStruct((rows, LANES), F32), jax.ShapeDtypeStruct((8, rows), F32),
        jax.ShapeDtypeStruct((rows, DA_WIDTH), act_dtype),
        jax.ShapeDtypeStruct((rows, DA_HEADS, DA_V_DIM), F32), jax.ShapeDtypeStruct((rows, DA_HEADS, DA_V_DIM), F32),
        jax.ShapeDtypeStruct((rows, DA_WIDTH), BF16), jax.ShapeDtypeStruct((rows, DA_WIDTH), BF16)]
    kv_spec = pl.BlockSpec((tm, DA_HEADS, DA_V_DIM), lambda i: (i, 0, 0))
    out_specs = [row_spec(ML_WIDTH)] * 4 + [row_spec(LANES), pl.BlockSpec((8, tm), lambda i: (0, i)),
                                            row_spec(DA_WIDTH), kv_spec, kv_spec,
                                            row_spec(DA_WIDTH), row_spec(DA_WIDTH)]
    return pl.pallas_call(
        functools.partial(_in_proj_kernel, dq_scale=dq_scale), grid=(rows // tm,),
        in_specs=[row_spec(d), full(g), full(w_main), full(w_g), full(w_gt), full(b_col), full(b_row),
                  tab_spec, tab_spec, tab_spec],
        out_specs=out_specs, out_shape=out_shape,
        compiler_params=_params("parallel"), name="in_proj")(
            x, g, w_main, w_g, w_gt, b_col, b_row, cos, sa, sb)


def _mlstm_heads(heads):
    L, D = heads[0][0].shape
    scale = D ** -0.5
    r = lax.broadcasted_iota(jnp.int32, (L, L), 0)
    c = lax.broadcasted_iota(jnp.int32, (L, L), 1)
    tri = c <= r
    gate = []
    for q, k, v, i_col, f_col, i_row, f_row, C, n, m in heads:
        b_col = jnp.sum(jnp.where(tri, f_row, 0.0), axis=1, keepdims=True)
        b_row = jnp.sum(jnp.where(r <= c, f_col, 0.0), axis=0, keepdims=True)
        log_d = jnp.where(tri, b_col - b_row + i_row, -jnp.inf)
        inter = b_col + m
        m_row = jnp.maximum(jnp.max(log_d, axis=1, keepdims=True), inter)
        b_last = b_col[L - 1:L, :]
        log_w = b_last - b_col + i_col
        m_new = jnp.maximum(b_last + m, jnp.max(log_w, axis=0, keepdims=True))
        gate.append(dict(
            m_row=m_row, w_inter=jnp.exp(inter - m_row), d=scale * jnp.exp(log_d - m_row), m_new=m_new,
            decay=jnp.exp(b_last + m - m_new), wk=(scale * jnp.exp(log_w - m_new)) * k.astype(F32)))
    qb = [h[0].astype(BF16) for h in heads]
    vb = [h[2].astype(BF16) for h in heads]
    qk = [_dot_nt(qb[j], heads[j][1].astype(BF16)) for j in range(len(heads))]
    qc = [_dot(qb[j], heads[j][7].astype(BF16)) for j in range(len(heads))]
    kv = [_dot_tn(gate[j]["wk"].astype(BF16), vb[j]) for j in range(len(heads))]
    s = [qk[j] * gate[j]["d"] for j in range(len(heads))]
    sv = [_dot(s[j].astype(BF16), vb[j]) for j in range(len(heads))]
    out = []
    for j, (q, k, v, i_col, f_col, i_row, f_row, C, n, m) in enumerate(heads):
        g = gate[j]
        num = g["w_inter"] * qc[j] + sv[j]
        den = (g["w_inter"] * jnp.sum(q.astype(F32) * n, axis=1, keepdims=True)
               + jnp.sum(s[j], axis=1, keepdims=True))
        h = num / jnp.maximum(jnp.abs(den), jnp.exp(-g["m_row"]))
        c_new = g["decay"] * C + kv[j]
        n_new = g["decay"] * n + jnp.sum(g["wk"], axis=0, keepdims=True)
        out.append((h, c_new, n_new, g["m_new"]))
    return out


def _ml_head_out(h, o, g):
    return _rms(h, g) * jax.nn.sigmoid(o.astype(F32))


def _mlstm_prompt_kernel(q_ref, k_ref, v_ref, o_ref, gc_ref, gr_ref, gh_ref,
                         h_ref, c_out, n_out, m_out, c_scr, n_scr, m_scr):
    ci = pl.program_id(1)

    @pl.when(ci == 0)
    def _():
        c_scr[...] = jnp.zeros_like(c_scr)
        n_scr[...] = jnp.zeros_like(n_scr)
        m_scr[...] = jnp.zeros_like(m_scr)

    gc, gr = gc_ref[...], gr_ref[...]
    cols = [slice(hh * ML_HEAD_DIM, (hh + 1) * ML_HEAD_DIM) for hh in range(ML_HEADS)]
    res = _mlstm_heads([
        (q_ref[:, cols[hh]], k_ref[:, cols[hh]], v_ref[:, cols[hh]],
         gc[:, hh:hh + 1], gc[:, ML_HEADS + hh:ML_HEADS + hh + 1],
         gr[hh:hh + 1, :], gr[ML_HEADS + hh:ML_HEADS + hh + 1, :],
         c_scr[hh], n_scr[hh], m_scr[hh]) for hh in range(ML_HEADS)])
    for hh, (h, c_new, n_new, m_new) in enumerate(res):
        c_scr[hh], n_scr[hh], m_scr[hh] = c_new, n_new, m_new
        h_ref[:, cols[hh]] = _ml_head_out(h, o_ref[:, cols[hh]], gh_ref[hh:hh + 1, :]).astype(h_ref.dtype)

    @pl.when(ci == pl.num_programs(1) - 1)
    def _():
        c_out[0] = c_scr[...]
        n_out[0] = n_scr[...]
        m_out[0] = m_scr[...]


def _mlstm_prompt(q, k, v, o, gc, gr, g_head, B, T, L):
    nc = T // L
    rows = B * T
    blk = lambda w: pl.BlockSpec((L, w), lambda b, c: (b * nc + c, 0))
    H, D = ML_HEADS, ML_HEAD_DIM
    return pl.pallas_call(
        _mlstm_prompt_kernel, grid=(B, nc),
        in_specs=[blk(ML_WIDTH)] * 4 + [blk(LANES), pl.BlockSpec((8, L), lambda b, c: (0, b * nc + c)),
                                        pl.BlockSpec(g_head.shape, lambda b, c: (0, 0))],
        out_specs=[blk(ML_WIDTH),
                   pl.BlockSpec((1, H, D, D), lambda b, c: (b, 0, 0, 0)),
                   pl.BlockSpec((1, H, 1, D), lambda b, c: (b, 0, 0, 0)),
                   pl.BlockSpec((1, H, 1, 1), lambda b, c: (b, 0, 0, 0))],
        out_shape=[jax.ShapeDtypeStruct((rows, ML_WIDTH), BF16),
                   jax.ShapeDtypeStruct((B, H, D, D), F32),
                   jax.ShapeDtypeStruct((B, H, 1, D), F32),
                   jax.ShapeDtypeStruct((B, H, 1, 1), F32)],
        scratch_shapes=[pltpu.VMEM((H, D, D), F32), pltpu.VMEM((H, 1, D), F32), pltpu.VMEM((H, 1, 1), F32)],
        compiler_params=_params("parallel", "arbitrary"), name="mlstm_prompt")(q, k, v, o, gc, gr, g_head)


def _mlstm_sample_kernel(q_ref, k_ref, v_ref, o_ref, gc_ref, gr_ref, gh_ref, c_in, n_in, m_in,
                         h_ref, c_out, n_out, m_out):
    cols = [slice(hh * ML_HEAD_DIM, (hh + 1) * ML_HEAD_DIM) for hh in range(ML_HEADS)]
    per_trip = 2 if q_ref.shape[0] % 2 == 0 else 1

    def body(it, carry):
        elems = [it * per_trip + u for u in range(per_trip)]
        gates = [(gc_ref[bi], gr_ref[bi]) for bi in elems]
        items = [(bi, hh) for bi in elems for hh in range(ML_HEADS)]
        res = _mlstm_heads([
            (q_ref[bi, :, cols[hh]], k_ref[bi, :, cols[hh]], v_ref[bi, :, cols[hh]],
             gc[:, hh:hh + 1], gc[:, ML_HEADS + hh:ML_HEADS + hh + 1],
             gr[hh:hh + 1, :], gr[ML_HEADS + hh:ML_HEADS + hh + 1, :],
             c_in[bi, hh], n_in[bi, hh], m_in[bi, hh])
            for bi, (gc, gr) in zip(elems, gates) for hh in range(ML_HEADS)])
        for (bi, hh), (h, c_new, n_new, m_new) in zip(items, res):
            c_out[bi, hh], n_out[bi, hh], m_out[bi, hh] = c_new, n_new, m_new
            h_ref[bi, :, cols[hh]] = _ml_head_out(
                h, o_ref[bi, :, cols[hh]], gh_ref[hh:hh + 1, :]).astype(h_ref.dtype)
        return carry

    lax.fori_loop(0, q_ref.shape[0] // per_trip, body, 0)


def _mlstm_sample(q, k, v, o, gc, gr, g_head, c0, n0, m0, bb):
    B, T, _ = q.shape
    H, D = ML_HEADS, ML_HEAD_DIM
    b3 = lambda s: pl.BlockSpec((bb,) + s, lambda i: (i, 0, 0))
    b4 = lambda s: pl.BlockSpec((bb,) + s, lambda i: (i, 0, 0, 0))
    return pl.pallas_call(
        _mlstm_sample_kernel, grid=(B // bb,),
        in_specs=[b3((T, ML_WIDTH))] * 4 + [b3((T, LANES)), b3((8, T)),
                                            pl.BlockSpec(g_head.shape, lambda i: (0, 0)),
                                            b4((H, D, D)), b4((H, 1, D)), b4((H, 1, 1))],
        out_specs=[b3((T, ML_WIDTH)), b4((H, D, D)), b4((H, 1, D)), b4((H, 1, 1))],
        out_shape=[jax.ShapeDtypeStruct((B, T, ML_WIDTH), BF16),
                   jax.ShapeDtypeStruct((B, H, D, D), F32),
                   jax.ShapeDtypeStruct((B, H, 1, D), F32),
                   jax.ShapeDtypeStruct((B, H, 1, 1), F32)],
        compiler_params=_params("parallel"), name="mlstm_sample")(q, k, v, o, gc, gr, g_head, c0, n0, m0)


def _da_lambda(lam_ref, lam_init):
    lv = lam_ref[...]
    a = jnp.sum(lv[0:1, :] * lv[1:2, :], axis=1, keepdims=True)
    b = jnp.sum(lv[2:3, :] * lv[3:4, :], axis=1, keepdims=True)
    return jnp.exp(a) - jnp.exp(b) + lam_init


def _stack_components(q):
    lane = lax.broadcasted_iota(jnp.int32, q.shape, 1)
    zero = jnp.zeros_like(q)
    return jnp.concatenate([jnp.where(lane < DA_QK_DIM, q, zero), jnp.where(lane >= DA_QK_DIM, q, zero)], axis=0)


def _lane_fold(x, op):
    out = x[:, :LANES]
    for c in range(1, x.shape[1] // LANES):
        out = op(out, x[:, c * LANES:(c + 1) * LANES])
    return out


def _da_prompt_kernel(q_ref, kb_scr, vb_scr, lam_ref, gh_ref, out_ref, m_scr, l_scr, acc_scr, *, lam_init):
    i = pl.program_id(2)
    tq = q_ref.shape[0]
    rows = 2 * tq
    qq = _stack_components(q_ref[...])
    causal = (lax.broadcasted_iota(jnp.int32, (rows, tq), 1)
              <= lax.broadcasted_iota(jnp.int32, (rows, tq), 0) % tq)

    def scores(j):
        off = pl.multiple_of(j * tq, tq)
        return _dot_nt(qq, kb_scr[pl.ds(off, tq), :]), off

    s_diag, off_diag = scores(i)
    s_diag = jnp.where(causal, s_diag, NEG_BIG)
    m_scr[...] = _lane_fold(s_diag, jnp.maximum)

    def max_body(j, carry):
        m_scr[...] = jnp.maximum(m_scr[...], _lane_fold(scores(j)[0], jnp.maximum))
        return carry

    lax.fori_loop(0, i, max_body, 0)
    m_rep = jnp.broadcast_to(jnp.max(m_scr[...], axis=1, keepdims=True), (rows, LANES))
    m_scr[...] = m_rep

    def weights(s, m):
        p = [jnp.exp2(s[:, c * LANES:(c + 1) * LANES] - m) for c in range(tq // LANES)]
        l = p[0]
        for pc in p[1:]:
            l = l + pc
        return jnp.concatenate([pc.astype(BF16) for pc in p], axis=1), l

    p, l = weights(s_diag, m_rep)
    l_scr[...] = l
    acc_scr[...] = _dot(p, vb_scr[pl.ds(off_diag, tq), :])

    def sum_body(j, carry):
        s, off = scores(j)
        p, l = weights(s, m_scr[...])
        l_scr[...] += l
        acc_scr[...] += _dot(p, vb_scr[pl.ds(off, tq), :])
        return carry

    lax.fori_loop(0, i, sum_body, 0)
    o = acc_scr[...] / jnp.sum(l_scr[...], axis=1, keepdims=True)
    a = o[:tq] - _da_lambda(lam_ref, lam_init) * o[tq:]
    out_ref[...] = (_rms(a, gh_ref[0]) * (1.0 - lam_init)).astype(out_ref.dtype)


def _da_prompt(dq, dk, dv, da_lambda, g_head3, B, T, tq, lam_init):
    nq = T // tq
    rows = B * T
    kv_spec = pl.BlockSpec((T, DA_V_DIM), lambda b, h, i: (b, h))
    return pl.pallas_call(
        functools.partial(_da_prompt_kernel, lam_init=lam_init), grid=(B, DA_HEADS, nq),
        in_specs=[pl.BlockSpec((tq, DA_V_DIM), lambda b, h, i: (b * nq + i, h)), kv_spec, kv_spec,
                  pl.BlockSpec(da_lambda.shape, lambda b, h, i: (0, 0)),
                  pl.BlockSpec((1, 1, DA_V_DIM), lambda b, h, i: (h, 0, 0))],
        out_specs=pl.BlockSpec((tq, DA_V_DIM), lambda b, h, i: (b * nq + i, h)),
        out_shape=jax.ShapeDtypeStruct((rows, DA_WIDTH), BF16),
        scratch_shapes=[pltpu.VMEM((2 * tq, LANES), F32), pltpu.VMEM((2 * tq, LANES), F32),
                        pltpu.VMEM((2 * tq, DA_V_DIM), F32)],
        compiler_params=_params("parallel", "parallel", "arbitrary"), name="da_prompt")(
            dq, dk, dv, da_lambda, g_head3)


def _da_sample_kernel(pt_ref, q_ref, kn_ref, vn_ref, lam_ref, gh_ref, ck_hbm, cv_hbm, out_ref,
                      kbuf, vbuf, sem, *, lam_init, layer):
    b = pl.program_id(0)
    nb = pl.num_programs(0)
    n_pages, page = kbuf.shape[1], kbuf.shape[2] // DA_HEADS
    T = q_ref.shape[1]
    scale = DA_QK_DIM ** -0.5

    def copies(bi, slot):
        out = []
        for p in range(n_pages):
            pg = pt_ref[bi, p]
            out.append(pltpu.make_async_copy(ck_hbm.at[layer, pg], kbuf.at[slot, p], sem.at[slot, 0]))
            out.append(pltpu.make_async_copy(cv_hbm.at[layer, pg], vbuf.at[slot, p], sem.at[slot, 1]))
        return out

    @pl.when(b == 0)
    def _():
        for cp in copies(0, 0):
            cp.start()

    @pl.when(b + 1 < nb)
    def _():
        for cp in copies(b + 1, (b + 1) % 2):
            cp.start()

    slot = b % 2
    for cp in copies(b, slot):
        cp.wait()

    lam = _da_lambda(lam_ref, lam_init)
    q_all = q_ref[0]
    n_rows = n_pages * page * DA_HEADS
    qq = jnp.concatenate([_stack_components(q_all[:, hh * DA_V_DIM:(hh + 1) * DA_V_DIM])
                          for hh in range(DA_HEADS)], axis=0)
    k_all = kbuf[slot].reshape(n_rows, DA_V_DIM).astype(BF16)
    v_all = vbuf[slot].reshape(n_rows, DA_V_DIM).astype(BF16)
    rq = 2 * T * DA_HEADS
    own_head = (lax.broadcasted_iota(jnp.int32, (rq, n_rows), 1) % DA_HEADS
                == lax.broadcasted_iota(jnp.int32, (rq, n_rows), 0) // (2 * T))
    s_past = jnp.where(own_head, _dot_nt(qq.astype(BF16), k_all) * scale, NEG_BIG)
    trow = lax.broadcasted_iota(jnp.int32, (rq, 1), 0) % T

    def per_query_row(new_ref, t):
        x = new_ref[0, t]
        return jnp.concatenate([jnp.broadcast_to(x[hh:hh + 1, :], (2 * T, DA_V_DIM))
                                for hh in range(DA_HEADS)], axis=0)

    s_new = [jnp.where(trow >= t,
                       jnp.sum(qq * per_query_row(kn_ref, t), axis=1, keepdims=True) * scale, NEG_BIG)
             for t in range(T)]
    m = jnp.max(s_past, axis=1, keepdims=True)
    for t in range(T):
        m = jnp.maximum(m, s_new[t])
    p_past = jnp.exp(s_past - m)
    l = jnp.sum(p_past, axis=1, keepdims=True)
    acc = _dot(p_past.astype(BF16), v_all)
    for t in range(T):
        p_t = jnp.exp(s_new[t] - m)
        l = l + p_t
        acc = acc + p_t * per_query_row(vn_ref, t)
    o = acc / l
    for hh in range(DA_HEADS):
        r0 = hh * 2 * T
        a = o[r0:r0 + T] - lam * o[r0 + T:r0 + 2 * T]
        out_ref[0, :, hh * DA_V_DIM:(hh + 1) * DA_V_DIM] = (
            _rms(a, gh_ref[hh:hh + 1, :]) * (1.0 - lam_init)).astype(out_ref.dtype)


def _da_sample(page_table, dq, dk, dv, da_lambda, g_head, cache_k, cache_v, lam_init, layer):
    B, T, _ = dq.shape
    n_pages = page_table.shape[1]
    page_rows = cache_k.shape[2]
    blk = pl.BlockSpec((1, T, DA_WIDTH), lambda b, pt: (b, 0, 0))
    kv_blk = pl.BlockSpec((1, T, DA_HEADS, DA_V_DIM), lambda b, pt: (b, 0, 0, 0))
    grid_spec = pltpu.PrefetchScalarGridSpec(
        num_scalar_prefetch=1, grid=(B,),
        in_specs=[blk, kv_blk, kv_blk, pl.BlockSpec(da_lambda.shape, lambda b, pt: (0, 0)),
                  pl.BlockSpec(g_head.shape, lambda b, pt: (0, 0)),
                  pl.BlockSpec(memory_space=pl.ANY), pl.BlockSpec(memory_space=pl.ANY)],
        out_specs=blk,
        scratch_shapes=[pltpu.VMEM((2, n_pages, page_rows, DA_V_DIM), F32),
                        pltpu.VMEM((2, n_pages, page_rows, DA_V_DIM), F32),
                        pltpu.SemaphoreType.DMA((2, 2))])
    return pl.pallas_call(
        functools.partial(_da_sample_kernel, lam_init=lam_init, layer=layer), grid_spec=grid_spec,
        out_shape=jax.ShapeDtypeStruct((B, T, DA_WIDTH), BF16),
        compiler_params=_params("arbitrary"), name="da_sample")(
            page_table, dq, dk, dv, da_lambda, g_head, cache_k, cache_v)


def _proj_norm_kernel(*refs, n_in, has_next):
    a_refs, w_refs = refs[:n_in], refs[n_in:2 * n_in]
    x_ref, gpost_ref, gpre_ref = refs[2 * n_in:2 * n_in + 3]
    rest = refs[2 * n_in + 3:]
    wn_ref = rest[0] if has_next else None
    xo_ref, ho_ref = rest[-2:]
    acc = _dot(a_refs[0][...], w_refs[0][...])
    for a, w in zip(a_refs[1:], w_refs[1:]):
        acc = acc + _dot(a[...], w[...])
    x1 = x_ref[...] + _rms(acc, gpost_ref[...])
    xo_ref[...] = x1
    hn = _rms(x1, gpre_ref[...]).astype(BF16)
    ho_ref[...] = (_dot(hn, wn_ref[...]) if has_next else hn).astype(ho_ref.dtype)


def _proj_norm(a_list, w_list, x, g_post, g_pre, w_next, tm, h_dtype=BF16):
    rows, d = x.shape
    n_in = len(a_list)
    has_next = w_next is not None
    row_spec = lambda w: pl.BlockSpec((tm, w), lambda i: (i, 0))
    full = lambda a: pl.BlockSpec(a.shape, lambda i: (0,) * a.ndim)
    ins = list(a_list) + list(w_list) + [x, g_post, g_pre] + ([w_next] if has_next else [])
    in_specs = ([row_spec(a.shape[1]) for a in a_list] + [full(w) for w in w_list]
                + [row_spec(d), full(g_post), full(g_pre)] + ([full(w_next)] if has_next else []))
    n_out = w_next.shape[1] if has_next else d
    return pl.pallas_call(
        functools.partial(_proj_norm_kernel, n_in=n_in, has_next=has_next), grid=(rows // tm,),
        in_specs=in_specs, out_specs=[row_spec(d), row_spec(n_out)],
        out_shape=[jax.ShapeDtypeStruct((rows, d), F32), jax.ShapeDtypeStruct((rows, n_out), h_dtype)],
        compiler_params=_params("parallel"), name="proj_norm")(*ins)


def _norm_matmul_kernel(x_ref, g_ref, w_ref, o_ref):
    o_ref[...] = _dot(_rms(x_ref[...], g_ref[...]).astype(BF16), w_ref[...])


def _norm_matmul(x, g, w, tm, tn):
    rows, d = x.shape
    n = w.shape[1]
    return pl.pallas_call(
        _norm_matmul_kernel, grid=(rows // tm, n // tn),
        in_specs=[pl.BlockSpec((tm, d), lambda i, j: (i, 0)), pl.BlockSpec(g.shape, lambda i, j: (0, 0)),
                  pl.BlockSpec((d, tn), lambda i, j: (0, j))],
        out_specs=pl.BlockSpec((tm, tn), lambda i, j: (i, j)),
        out_shape=jax.ShapeDtypeStruct((rows, n), F32),
        compiler_params=_params("parallel", "parallel"), name="mem_kv")(x, g, w)


def _mem_attn_heads(qkv):
    s = [_dot_nt(q.astype(BF16), k.astype(BF16)) * (q.shape[1] ** -0.5) for q, k, _ in qkv]
    p = [jnp.exp(x - jnp.max(x, axis=1, keepdims=True)) for x in s]
    pv = [_dot(x.astype(BF16), v.astype(BF16)) for x, (_, _, v) in zip(p, qkv)]
    return [o / jnp.sum(x, axis=1, keepdims=True) for o, x in zip(pv, p)]


def _mem_attn_kernel(q_ref, mk_ref, mv_ref, o_ref):
    hd = q_ref.shape[2] // MEM_HEADS
    items = [(bi, slice(hh * hd, (hh + 1) * hd)) for bi in range(q_ref.shape[0]) for hh in range(MEM_HEADS)]
    outs = _mem_attn_heads([(q_ref[bi, :, cols], mk_ref[bi, :, cols], mv_ref[bi, :, cols]) for bi, cols in items])
    for (bi, cols), o in zip(items, outs):
        o_ref[bi, :, cols] = o.astype(o_ref.dtype)


def _mem_attn(q, mk, mv, bb, tm):
    B, T, d = q.shape
    kv_spec = pl.BlockSpec((bb,) + mk.shape[1:], lambda b, i: (b, 0, 0))
    return pl.pallas_call(
        _mem_attn_kernel, grid=(B // bb, T // tm),
        in_specs=[pl.BlockSpec((bb, tm, d), lambda b, i: (b, i, 0)), kv_spec, kv_spec],
        out_specs=pl.BlockSpec((bb, tm, d), lambda b, i: (b, i, 0)),
        out_shape=jax.ShapeDtypeStruct((B, T, d), BF16),
        compiler_params=_params("parallel", "parallel"), name="mem_attn")(q, mk, mv)


def _mem_attn_cache_kernel(q_ref, mk_hbm, mv_hbm, o_ref, kbuf, vbuf, sem, *, layer):
    g = pl.program_id(0)
    bb = q_ref.shape[0]
    hd = kbuf.shape[-1]

    def copies(gi, slot):
        out = []
        for bi in range(bb):
            for hh in range(MEM_HEADS):
                b = gi * bb + bi
                out.append(pltpu.make_async_copy(mk_hbm.at[layer, b, :, hh, :], kbuf.at[slot, bi, hh],
                                                 sem.at[slot, 0]))
                out.append(pltpu.make_async_copy(mv_hbm.at[layer, b, :, hh, :], vbuf.at[slot, bi, hh],
                                                 sem.at[slot, 1]))
        return out

    @pl.when(g == 0)
    def _():
        for cp in copies(0, 0):
            cp.start()

    @pl.when(g + 1 < pl.num_programs(0))
    def _():
        for cp in copies(g + 1, (g + 1) % 2):
            cp.start()

    slot = g % 2
    for cp in copies(g, slot):
        cp.wait()
    for bi in range(bb):
        outs = _mem_attn_heads([(q_ref[bi, :, hh * hd:(hh + 1) * hd], kbuf[slot, bi, hh], vbuf[slot, bi, hh])
                                for hh in range(MEM_HEADS)])
        for hh, o in enumerate(outs):
            o_ref[bi, :, hh * hd:(hh + 1) * hd] = o.astype(o_ref.dtype)


def _mem_attn_cache(q, mk, mv, layer, bb):
    B, T, d = q.shape
    M, H, hd = mk.shape[2:]
    return pl.pallas_call(
        functools.partial(_mem_attn_cache_kernel, layer=layer), grid=(B // bb,),
        in_specs=[pl.BlockSpec((bb, T, d), lambda g: (g, 0, 0)),
                  pl.BlockSpec(memory_space=pl.ANY), pl.BlockSpec(memory_space=pl.ANY)],
        out_specs=pl.BlockSpec((bb, T, d), lambda g: (g, 0, 0)),
        out_shape=jax.ShapeDtypeStruct((B, T, d), BF16),
        scratch_shapes=[pltpu.VMEM((2, bb, H, M, hd), F32), pltpu.VMEM((2, bb, H, M, hd), F32),
                        pltpu.SemaphoreType.DMA((2, 2))],
        compiler_params=_params("arbitrary"), name="mem_attn_cache")(q, mk, mv)


def _ffn_chunks(nh):
    nchunk = 2 if (nh // LANES) % 2 == 0 else 1
    cw = nh // nchunk
    return [(j * cw, cw) for j in range(nchunk)]


def _ffn_prompt_kernel(hf_ref, x_ref, wup_ref, wdw_ref, bdw_ref, wdn_ref, gpost_ref, y_ref, ulast_ref, ubuf,
                       *, tiles_per_seq):
    i = pl.program_id(0)
    tm = hf_ref.shape[0]
    nh = wdn_ref.shape[0]
    halo = ubuf.shape[0] - tm

    @pl.when(i % tiles_per_seq == 0)
    def _():
        ubuf[0:halo, :] = jnp.zeros((halo, ubuf.shape[1]), F32)

    hf = hf_ref[...]
    f = jnp.zeros((tm, y_ref.shape[1]), F32)
    for c0, cw in _ffn_chunks(nh):
        cg = []
        for base in (c0, nh + c0):
            cs = slice(base, base + cw)
            ubuf[halo:halo + tm, cs] = _dot(hf, wup_ref[:, cs])
            c = bdw_ref[:, cs]
            for j in range(CONV_W):
                lo = halo - (CONV_W - 1) + j
                c = c + ubuf[lo:lo + tm, cs] * wdw_ref[j:j + 1, cs]
            cg.append(c)
        act = (jax.nn.silu(cg[1]) * cg[0]).astype(BF16)
        f = f + _dot(act, wdn_ref[c0:c0 + cw, :])
    y_ref[...] = x_ref[...] + _rms(f, gpost_ref[...])
    tail = ubuf[tm:tm + halo, :]
    ubuf[0:halo, :] = tail
    ulast_ref[0] = tail


def _ffn_prompt(hf, x, w_up, w_dw, b_dw, w_down, g_post, B, T, tm):
    rows, d = x.shape
    npad = w_up.shape[1]
    tiles = T // tm
    halo = 8
    row_spec = lambda w: pl.BlockSpec((tm, w), lambda i: (i, 0))
    full = _resident
    return pl.pallas_call(
        functools.partial(_ffn_prompt_kernel, tiles_per_seq=tiles), grid=(rows // tm,),
        in_specs=[row_spec(d), row_spec(d), full(w_up), full(w_dw), full(b_dw), full(w_down), full(g_post)],
        out_specs=[row_spec(d), pl.BlockSpec((1, halo, npad), lambda i: (i // tiles, 0, 0))],
        out_shape=[jax.ShapeDtypeStruct((rows, d), F32), jax.ShapeDtypeStruct((B, halo, npad), F32)],
        scratch_shapes=[pltpu.VMEM((tm + halo, npad), F32)],
        compiler_params=_params("arbitrary"), name="ffn_prompt")(hf, x, w_up, w_dw, b_dw, w_down, g_post)


def _ffn_sample_kernel(hf_ref, x_ref, cb_ref, wup_ref, wdw_ref, bdw_ref, wdn_ref, gpost_ref, y_ref, unew_ref,
                       *, T):
    nb = hf_ref.shape[0] // T
    nh = wdn_ref.shape[0]
    hf = hf_ref[...]
    f = jnp.zeros(y_ref.shape, F32)
    for c0, cw in _ffn_chunks(nh):
        cg = []
        for base in (c0, nh + c0):
            cs = slice(base, base + cw)
            u = _dot(hf, wup_ref[:, cs])
            ext = [cb_ref[j, :, cs] for j in range(CONV_W - 1)] + [u[t * nb:(t + 1) * nb] for t in range(T)]
            for j in range(CONV_W - 1):
                unew_ref[j, :, cs] = ext[len(ext) - (CONV_W - 1) + j]
            rows = []
            for t in range(T):
                c = bdw_ref[:, cs]
                for j in range(CONV_W):
                    c = c + ext[t + j] * wdw_ref[j:j + 1, cs]
                rows.append(c)
            cg.append(jnp.concatenate(rows, axis=0))
        act = (jax.nn.silu(cg[1]) * cg[0]).astype(BF16)
        f = f + _dot(act, wdn_ref[c0:c0 + cw, :])
    y_ref[...] = x_ref[...] + _rms(f, gpost_ref[...])


def _ffn_sample(hf, x, cb, w_up, w_dw, b_dw, w_down, g_post, T):
    rows, d = x.shape
    npad = w_up.shape[1]
    ins = (hf, x, cb, w_up, w_dw, b_dw, w_down, g_post)
    full = lambda a: pl.BlockSpec(a.shape, lambda i: (0,) * a.ndim)
    return pl.pallas_call(
        functools.partial(_ffn_sample_kernel, T=T), grid=(1,),
        in_specs=[full(a) for a in ins],
        out_specs=[pl.BlockSpec((rows, d), lambda i: (0, 0)),
                   pl.BlockSpec((CONV_W - 1, rows // T, npad), lambda i: (0, 0, 0))],
        out_shape=[jax.ShapeDtypeStruct((rows, d), F32),
                   jax.ShapeDtypeStruct((CONV_W - 1, rows // T, npad), F32)],
        compiler_params=_params("arbitrary"), name="ffn_sample")(*ins)


def _pad_halves(a, nh, nh_pad):
    pad = [(0, 0)] * (a.ndim - 1) + [(0, nh_pad - nh)]
    return jnp.concatenate([jnp.pad(a[..., :nh], pad), jnp.pad(a[..., nh:], pad)], axis=-1)


def _unpad_halves(a, nh, nh_pad):
    return jnp.concatenate([a[..., :nh], a[..., nh_pad:nh_pad + nh]], axis=-1)


def _row_tile(rows, want):
    t = min(rows, want)
    while rows % t:
        t //= 2
    return t


def kernel(x_prompt, x_sample, cache_dk, cache_dv, cache_mem_k, cache_mem_v, state_ml_C, state_ml_n, state_ml_m, state_conv, page_table, mem_prompt, g_mix_pre, g_mix_post, w_in, b_if, g_ml_head, da_lambda, g_da_head, w_out, g_mem_pre, g_mem_post, g_mem_src, w_mq, w_mk, w_mv, w_mo, g_ffn_pre, g_ffn_post, w_up, w_dw, b_dw, w_down):
    depth = w_in.shape[0]
    Bp, Tp, d = x_prompt.shape
    Bs, Ts, _ = x_sample.shape
    n_pages, page = page_table.shape[1], cache_dk.shape[2]
    past_len = n_pages * page
    n_mem = mem_prompt.shape[1]
    nh = w_down.shape[1]
    nh_pad = -(-nh // LANES) * LANES
    H, D = ML_HEADS, ML_HEAD_DIM
    rows_p, rows_s = Bp * Tp, Bs * Ts

    tm_p = _row_tile(Tp, 512)
    tm_s = _row_tile(rows_s, 256)
    tab_p = _rope_tables(Tp, Tp, 0)
    tab_s = _rope_tables(tm_s, Ts, past_len)

    yp = x_prompt.reshape(rows_p, d)
    ys = x_sample.reshape(rows_s, d)
    outs = [[] for _ in range(14)]
    row = lambda a: a.reshape(1, -1)
    for l in range(depth):
        lam_init = 0.8 - 0.6 * math.exp(-0.3 * l)
        wi = w_in[l]
        w_main = jnp.concatenate([wi[:, :4 * ML_WIDTH], wi[:, 4 * ML_WIDTH + 2 * H:]], axis=1).astype(BF16)
        w_gate = wi[:, 4 * ML_WIDTH:4 * ML_WIDTH + 2 * H]
        w_g = jnp.pad(w_gate, ((0, 0), (0, LANES - 2 * H))).astype(BF16)
        w_gt = w_gate.T.astype(BF16)
        b_col = jnp.pad(b_if[l], (0, LANES - 2 * H)).reshape(1, LANES)
        b_row = b_if[l].reshape(2 * H, 1)
        wo_ml, wo_da = w_out[l][:ML_WIDTH].astype(BF16), w_out[l][ML_WIDTH:].astype(BF16)
        wq_b, wo_b = w_mq[l].astype(BF16), w_mo[l].astype(BF16)
        wkv_b = jnp.concatenate([w_mk[l], w_mv[l]], axis=1).astype(BF16)
        wup_b = _pad_halves(w_up[l], nh, nh_pad).astype(BF16)
        wdw_p = _pad_halves(w_dw[l], nh, nh_pad)
        bdw_p = _pad_halves(b_dw[l].reshape(1, -1), nh, nh_pad)
        wdn_b = jnp.pad(w_down[l], ((0, nh_pad - nh), (0, 0))).astype(BF16)
        g_da3 = g_da_head[l].reshape(DA_HEADS, 1, DA_V_DIM)

        q, k, v, o, gc, gr, dq, dk, dv, dkb, dvb = _in_proj(
            yp, row(g_mix_pre[l]), w_main, w_g, w_gt, b_col, b_row, tab_p, BF16, tm_p,
            DA_QK_DIM ** -0.5 * math.log2(math.e))
        h_ml, C_p, n_p, m_p = _mlstm_prompt(q, k, v, o, gc, gr, g_ml_head[l], Bp, Tp, _row_tile(Tp, 128))
        h_da = _da_prompt(dq, dkb, dvb, da_lambda[l], g_da3, Bp, Tp, _row_tile(Tp, 512), lam_init)
        x1, qm = _proj_norm([h_ml, h_da], [wo_ml, wo_da], yp, row(g_mix_post[l]), row(g_mem_pre[l]), wq_b, tm_p)
        mkv = _norm_matmul(mem_prompt.reshape(Bp * n_mem, d), row(g_mem_src[l]), wkv_b,
                           _row_tile(Bp * n_mem, 256), 512)
        mk_p, mv_p = mkv[:, :d].reshape(Bp, n_mem, d), mkv[:, d:].reshape(Bp, n_mem, d)
        om = _mem_attn(qm.reshape(Bp, Tp, d), mk_p, mv_p, 1, tm_p)
        x2, hf = _proj_norm([om.reshape(rows_p, d)], [wo_b], x1, row(g_mem_post[l]), row(g_ffn_pre[l]), None, tm_p)
        yp, ulast = _ffn_prompt(hf, x2, wup_b, wdw_p, bdw_p, wdn_b, row(g_ffn_post[l]), Bp, Tp, tm_p)
        cv_p = _unpad_halves(ulast[:, ulast.shape[1] - (CONV_W - 1):], nh, nh_pad)

        q, k, v, o, gc, gr, dq, dk_s, dv_s, _, _ = _in_proj(
            ys, row(g_mix_pre[l]), w_main, w_g, w_gt, b_col, b_row, tab_s, F32, tm_s, 1.0)
        r3 = lambda a: a.reshape(Bs, Ts, a.shape[-1])
        r4 = lambda a: a.reshape(Bs, Ts, DA_HEADS, DA_V_DIM)
        gr3 = gr.reshape(2 * H, Bs, Ts).transpose(1, 0, 2)
        h_ml, C_s, n_s, m_s = _mlstm_sample(
            r3(q), r3(k), r3(v), r3(o), r3(gc), gr3, g_ml_head[l],
            state_ml_C[l], state_ml_n[l].reshape(Bs, H, 1, D), state_ml_m[l].reshape(Bs, H, 1, 1),
            _row_tile(Bs, 8))
        h_da = _da_sample(page_table, r3(dq), r4(dk_s), r4(dv_s), da_lambda[l], g_da_head[l],
                          cache_dk.reshape(depth, -1, page * DA_HEADS, DA_V_DIM),
                          cache_dv.reshape(depth, -1, page * DA_HEADS, DA_V_DIM), lam_init, l)
        x1, qm = _proj_norm([h_ml.reshape(rows_s, ML_WIDTH), h_da.reshape(rows_s, DA_WIDTH)], [wo_ml, wo_da], ys,
                            row(g_mix_post[l]), row(g_mem_pre[l]), wq_b, tm_s, h_dtype=F32)
        om = _mem_attn_cache(qm.reshape(Bs, Ts, d), cache_mem_k, cache_mem_v, l, _row_tile(Bs, 4))
        x2, hf = _proj_norm([om.reshape(rows_s, d)], [wo_b], x1, row(g_mem_post[l]), row(g_ffn_pre[l]), None, tm_s)
        tmaj = lambda a: a.reshape(Bs, Ts, -1).transpose(1, 0, 2).reshape(rows_s, -1)
        cb = _pad_halves(state_conv[l], nh, nh_pad).transpose(1, 0, 2)
        y_t, unew = _ffn_sample(tmaj(hf), tmaj(x2), cb, wup_b, wdw_p, bdw_p, wdn_b, row(g_ffn_post[l]), Ts)
        ys = y_t.reshape(Ts, Bs, d).transpose(1, 0, 2).reshape(rows_s, d)
        cv_s = _unpad_halves(unew.transpose(1, 0, 2), nh, nh_pad)

        vals = (dk.reshape(Bp, Tp, DA_HEADS, DA_V_DIM), dv.reshape(Bp, Tp, DA_HEADS, DA_V_DIM),
                mk_p.reshape(Bp, n_mem, MEM_HEADS, d // MEM_HEADS), mv_p.reshape(Bp, n_mem, MEM_HEADS, d // MEM_HEADS),
                C_p, n_p.reshape(Bp, H, D), m_p.reshape(Bp, H), cv_p,
                dk_s.reshape(Bs, Ts, DA_HEADS, DA_V_DIM), dv_s.reshape(Bs, Ts, DA_HEADS, DA_V_DIM),
                C_s, n_s.reshape(Bs, H, D), m_s.reshape(Bs, H), cv_s)
        for acc, val in zip(outs, vals):
            acc.append(val)
    return (yp.reshape(Bp, Tp, d), ys.reshape(Bs, Ts, d)) + tuple(jnp.stack(a) for a in outs)
```

```python
import functools
import math

import jax
import jax.numpy as jnp
from jax import lax
from jax.experimental import pallas as pl
from jax.experimental.pallas import tpu as pltpu

F32 = jnp.float32
BF16 = jnp.bfloat16

ML_HEADS = 4
ML_HEAD_DIM = 128
ML_WIDTH = ML_HEADS * ML_HEAD_DIM
DA_HEADS = 4
DA_V_DIM = 128
DA_QK_DIM = 64
DA_WIDTH = DA_HEADS * DA_V_DIM
ROPE_DIM = 16
ROPE_THETA = 500000.0
MEM_HEADS = 4
CONV_W = 3
RMS_EPS = 1e-6
LANES = 128
NEG_BIG = -1e30
VMEM_LIMIT = 56 * 1024 * 1024


def _params(*sem, vmem=VMEM_LIMIT):
    return pltpu.CompilerParams(dimension_semantics=sem, vmem_limit_bytes=vmem)


def _resident(a):
    return pl.BlockSpec(a.shape, lambda *_: (0,) * a.ndim, pipeline_mode=pl.Buffered(1))


def _rms(x, g):
    return x * lax.rsqrt(jnp.mean(x * x, axis=-1, keepdims=True) + RMS_EPS) * g


def _log_sigmoid(x):
    return jnp.minimum(x, 0.0) - jnp.log1p(jnp.exp(-jnp.abs(x)))


def _dot(a, b):
    return jnp.dot(a, b, preferred_element_type=F32)


def _dot_nt(a, b):
    return lax.dot_general(a, b, (((1,), (1,)), ((), ())), preferred_element_type=F32)


def _dot_tn(a, b):
    return lax.dot_general(a, b, (((0,), (0,)), ((), ())), preferred_element_type=F32)


def _rope_table_kernel(cos_ref, sa_ref, sb_ref, *, period, offset):
    rows = cos_ref.shape[0]
    half = ROPE_DIM // 2
    r = lax.broadcasted_iota(jnp.int32, (rows, LANES), 0) + pl.program_id(0) * rows
    lane = lax.broadcasted_iota(jnp.int32, (rows, LANES), 1)
    pos = (offset + r % period).astype(F32)
    c = lane % DA_QK_DIM
    j = (c % half).astype(F32)
    inv = jnp.exp(-math.log(ROPE_THETA) * (2.0 * j / ROPE_DIM))
    ang = pos * inv
    cos, sin = jnp.cos(ang), jnp.sin(ang)
    cos_ref[...] = jnp.where(c < ROPE_DIM, cos, 1.0)
    sa_ref[...] = jnp.where(c < half, -sin, 0.0)
    sb_ref[...] = jnp.where((c >= half) & (c < ROPE_DIM), sin, 0.0)


def _rope_tables(rows, period, offset):
    blk = min(rows, 512)
    spec = pl.BlockSpec((blk, LANES), lambda i: (i, 0))
    shp = jax.ShapeDtypeStruct((rows, LANES), F32)
    return pl.pallas_call(
        functools.partial(_rope_table_kernel, period=period, offset=offset),
        grid=(rows // blk,), out_specs=[spec] * 3, out_shape=[shp] * 3,
        compiler_params=_params("parallel"), name="rope_tables")()


def _rope(x, cos, sa, sb):
    outs = []
    for j in range(x.shape[1] // LANES):
        xj = x[:, j * LANES:(j + 1) * LANES]
        up = pltpu.roll(xj, LANES - ROPE_DIM // 2, axis=1)
        dn = pltpu.roll(xj, ROPE_DIM // 2, axis=1)
        outs.append(xj * cos + up * sa + dn * sb)
    return outs


def _in_proj_kernel(x_ref, g_ref, w_ref, wg_ref, wgt_ref, bcol_ref, brow_ref, cos_ref, sa_ref, sb_ref,
                    q_ref, k_ref, v_ref, o_ref, gc_ref, gr_ref, dq_ref, dk_ref, dv_ref, dkb_ref, dvb_ref,
                    *, dq_scale):
    h = _rms(x_ref[...], g_ref[...]).astype(BF16)
    for j, ref in enumerate((q_ref, k_ref, v_ref, o_ref)):
        ref[...] = _dot(h, w_ref[:, j * ML_WIDTH:(j + 1) * ML_WIDTH]).astype(ref.dtype)
    gc = _dot(h, wg_ref[...]) + bcol_ref[...]
    lane = lax.broadcasted_iota(jnp.int32, gc.shape, 1)
    gc_ref[...] = jnp.where(lane < ML_HEADS, gc, _log_sigmoid(gc))
    gr = _dot_nt(wgt_ref[...], h) + brow_ref[...]
    row = lax.broadcasted_iota(jnp.int32, gr.shape, 0)
    gr_ref[...] = jnp.where(row < ML_HEADS, gr, _log_sigmoid(gr))
    base = 4 * ML_WIDTH
    cos, sa, sb = cos_ref[...], sa_ref[...], sb_ref[...]
    dq = _rope(_dot(h, w_ref[:, base:base + DA_WIDTH]), cos, sa, sb)
    dk = _rope(_dot(h, w_ref[:, base + DA_WIDTH:base + 2 * DA_WIDTH]), cos, sa, sb)
    dv = _dot(h, w_ref[:, base + 2 * DA_WIDTH:base + 3 * DA_WIDTH])
    dvb_ref[...] = dv.astype(BF16)
    for j in range(DA_HEADS):
        cols = slice(j * LANES, (j + 1) * LANES)
        dq_ref[:, cols] = (dq[j] * dq_scale).astype(dq_ref.dtype)
        dk_ref[:, j, :] = dk[j]
        dkb_ref[:, cols] = dk[j].astype(BF16)
        dv_ref[:, j, :] = dv[:, cols]


def _in_proj(x, g, w_main, w_g, w_gt, b_col, b_row, tables, act_dtype, tm, dq_scale):
    rows, d = x.shape
    cos, sa, sb = tables
    nt = cos.shape[0] // tm
    row_spec = lambda w: pl.BlockSpec((tm, w), lambda i: (i, 0))
    full = lambda a: pl.BlockSpec(a.shape, lambda i: (0,) * a.ndim)
    tab_spec = pl.BlockSpec((tm, LANES), lambda i: (i % nt, 0))
    out_shape = [jax.ShapeDtypeStruct((rows, ML_WIDTH), act_dtype)] * 4 + [
        jax.ShapeDtypeStruct((rows, LANES), F32), jax.ShapeDtypeStruct((8, rows), F32),
        jax.ShapeDtypeStruct((rows, DA_WIDTH), act_dtype),
        jax.ShapeDtypeStruct((rows, DA_HEADS, DA_V_DIM), F32), jax.ShapeDtypeStruct((rows, DA_HEADS, DA_V_DIM), F32),
        jax.ShapeDtypeStruct((rows, DA_WIDTH), BF16), jax.ShapeDtypeStruct((rows, DA_WIDTH), BF16)]
    kv_spec = pl.BlockSpec((tm, DA_HEADS, DA_V_DIM), lambda i: (i, 0, 0))
    out_specs = [row_spec(ML_WIDTH)] * 4 + [row_spec(LANES), pl.BlockSpec((8, tm), lambda i: (0, i)),
                                            row_spec(DA_WIDTH), kv_spec, kv_spec,
                                            row_spec(DA_WIDTH), row_spec(DA_WIDTH)]
    return pl.pallas_call(
        functools.partial(_in_proj_kernel, dq_scale=dq_scale), grid=(rows // tm,),
        in_specs=[row_spec(d), full(g), full(w_main), full(w_g), full(w_gt), full(b_col), full(b_row),
                  tab_spec, tab_spec, tab_spec],
        out_specs=out_specs, out_shape=out_shape,
        compiler_params=_params("parallel"), name="in_proj")(
            x, g, w_main, w_g, w_gt, b_col, b_row, cos, sa, sb)


def _mlstm_heads(heads):
    L, D = heads[0][0].shape
    scale = D ** -0.5
    r = lax.broadcasted_iota(jnp.int32, (L, L), 0)
    c = lax.broadcasted_iota(jnp.int32, (L, L), 1)
    tri = c <= r
    gate = []
    for q, k, v, i_col, f_col, i_row, f_row, C, n, m in heads:
        b_col = jnp.sum(jnp.where(tri, f_row, 0.0), axis=1, keepdims=True)
        b_row = jnp.sum(jnp.where(r <= c, f_col, 0.0), axis=0, keepdims=True)
        log_d = jnp.where(tri, b_col - b_row + i_row, -jnp.inf)
        inter = b_col + m
        m_row = jnp.maximum(jnp.max(log_d, axis=1, keepdims=True), inter)
        b_last = b_col[L - 1:L, :]
        log_w = b_last - b_col + i_col
        m_new = jnp.maximum(b_last + m, jnp.max(log_w, axis=0, keepdims=True))
        gate.append(dict(
            m_row=m_row, w_inter=jnp.exp(inter - m_row), d=scale * jnp.exp(log_d - m_row), m_new=m_new,
            decay=jnp.exp(b_last + m - m_new), wk=(scale * jnp.exp(log_w - m_new)) * k.astype(F32)))
    qb = [h[0].astype(BF16) for h in heads]
    vb = [h[2].astype(BF16) for h in heads]
    qk = [_dot_nt(qb[j], heads[j][1].astype(BF16)) for j in range(len(heads))]
    qc = [_dot(qb[j], heads[j][7].astype(BF16)) for j in range(len(heads))]
    kv = [_dot_tn(gate[j]["wk"].astype(BF16), vb[j]) for j in range(len(heads))]
    s = [qk[j] * gate[j]["d"] for j in range(len(heads))]
    sv = [_dot(s[j].astype(BF16), vb[j]) for j in range(len(heads))]
    out = []
    for j, (q, k, v, i_col, f_col, i_row, f_row, C, n, m) in enumerate(heads):
        g = gate[j]
        num = g["w_inter"] * qc[j] + sv[j]
        den = (g["w_inter"] * jnp.sum(q.astype(F32) * n, axis=1, keepdims=True)
               + jnp.sum(s[j], axis=1, keepdims=True))
        h = num / jnp.maximum(jnp.abs(den), jnp.exp(-g["m_row"]))
        c_new = g["decay"] * C + kv[j]
        n_new = g["decay"] * n + jnp.sum(g["wk"], axis=0, keepdims=True)
        out.append((h, c_new, n_new, g["m_new"]))
    return out


def _ml_head_out(h, o, g):
    return _rms(h, g) * jax.nn.sigmoid(o.astype(F32))


def _mlstm_prompt_kernel(q_ref, k_ref, v_ref, o_ref, gc_ref, gr_ref, gh_ref,
                         h_ref, c_out, n_out, m_out, c_scr, n_scr, m_scr):
    ci = pl.program_id(1)

    @pl.when(ci == 0)
    def _():
        c_scr[...] = jnp.zeros_like(c_scr)
        n_scr[...] = jnp.zeros_like(n_scr)
        m_scr[...] = jnp.zeros_like(m_scr)

    gc, gr = gc_ref[...], gr_ref[...]
    cols = [slice(hh * ML_HEAD_DIM, (hh + 1) * ML_HEAD_DIM) for hh in range(ML_HEADS)]
    res = _mlstm_heads([
        (q_ref[:, cols[hh]], k_ref[:, cols[hh]], v_ref[:, cols[hh]],
         gc[:, hh:hh + 1], gc[:, ML_HEADS + hh:ML_HEADS + hh + 1],
         gr[hh:hh + 1, :], gr[ML_HEADS + hh:ML_HEADS + hh + 1, :],
         c_scr[hh], n_scr[hh], m_scr[hh]) for hh in range(ML_HEADS)])
    for hh, (h, c_new, n_new, m_new) in enumerate(res):
        c_scr[hh], n_scr[hh], m_scr[hh] = c_new, n_new, m_new
        h_ref[:, cols[hh]] = _ml_head_out(h, o_ref[:, cols[hh]], gh_ref[hh:hh + 1, :]).astype(h_ref.dtype)

    @pl.when(ci == pl.num_programs(1) - 1)
    def _():
        c_out[0] = c_scr[...]
        n_out[0] = n_scr[...]
        m_out[0] = m_scr[...]


def _mlstm_prompt(q, k, v, o, gc, gr, g_head, B, T, L):
    nc = T // L
    rows = B * T
    blk = lambda w: pl.BlockSpec((L, w), lambda b, c: (b * nc + c, 0))
    H, D = ML_HEADS, ML_HEAD_DIM
    return pl.pallas_call(
        _mlstm_prompt_kernel, grid=(B, nc),
        in_specs=[blk(ML_WIDTH)] * 4 + [blk(LANES), pl.BlockSpec((8, L), lambda b, c: (0, b * nc + c)),
                                        pl.BlockSpec(g_head.shape, lambda b, c: (0, 0))],
        out_specs=[blk(ML_WIDTH),
                   pl.BlockSpec((1, H, D, D), lambda b, c: (b, 0, 0, 0)),
                   pl.BlockSpec((1, H, 1, D), lambda b, c: (b, 0, 0, 0)),
                   pl.BlockSpec((1, H, 1, 1), lambda b, c: (b, 0, 0, 0))],
        out_shape=[jax.ShapeDtypeStruct((rows, ML_WIDTH), BF16),
                   jax.ShapeDtypeStruct((B, H, D, D), F32),
                   jax.ShapeDtypeStruct((B, H, 1, D), F32),
                   jax.ShapeDtypeStruct((B, H, 1, 1), F32)],
        scratch_shapes=[pltpu.VMEM((H, D, D), F32), pltpu.VMEM((H, 1, D), F32), pltpu.VMEM((H, 1, 1), F32)],
        compiler_params=_params("parallel", "arbitrary"), name="mlstm_prompt")(q, k, v, o, gc, gr, g_head)


def _mlstm_sample_kernel(q_ref, k_ref, v_ref, o_ref, gc_ref, gr_ref, gh_ref, c_in, n_in, m_in,
                         h_ref, c_out, n_out, m_out):
    cols = [slice(hh * ML_HEAD_DIM, (hh + 1) * ML_HEAD_DIM) for hh in range(ML_HEADS)]
    per_trip = 2 if q_ref.shape[0] % 2 == 0 else 1

    def body(it, carry):
        elems = [it * per_trip + u for u in range(per_trip)]
        gates = [(gc_ref[bi], gr_ref[bi]) for bi in elems]
        items = [(bi, hh) for bi in elems for hh in range(ML_HEADS)]
        res = _mlstm_heads([
            (q_ref[bi, :, cols[hh]], k_ref[bi, :, cols[hh]], v_ref[bi, :, cols[hh]],
             gc[:, hh:hh + 1], gc[:, ML_HEADS + hh:ML_HEADS + hh + 1],
             gr[hh:hh + 1, :], gr[ML_HEADS + hh:ML_HEADS + hh + 1, :],
             c_in[bi, hh], n_in[bi, hh], m_in[bi, hh])
            for bi, (gc, gr) in zip(elems, gates) for hh in range(ML_HEADS)])
        for (bi, hh), (h, c_new, n_new, m_new) in zip(items, res):
            c_out[bi, hh], n_out[bi, hh], m_out[bi, hh] = c_new, n_new, m_new
            h_ref[bi, :, cols[hh]] = _ml_head_out(
                h, o_ref[bi, :, cols[hh]], gh_ref[hh:hh + 1, :]).astype(h_ref.dtype)
        return carry

    lax.fori_loop(0, q_ref.shape[0] // per_trip, body, 0)


def _mlstm_sample(q, k, v, o, gc, gr, g_head, c0, n0, m0, bb):
    B, T, _ = q.shape
    H, D = ML_HEADS, ML_HEAD_DIM
    b3 = lambda s: pl.BlockSpec((bb,) + s, lambda i: (i, 0, 0))
    b4 = lambda s: pl.BlockSpec((bb,) + s, lambda i: (i, 0, 0, 0))
    return pl.pallas_call(
        _mlstm_sample_kernel, grid=(B // bb,),
        in_specs=[b3((T, ML_WIDTH))] * 4 + [b3((T, LANES)), b3((8, T)),
                                            pl.BlockSpec(g_head.shape, lambda i: (0, 0)),
                                            b4((H, D, D)), b4((H, 1, D)), b4((H, 1, 1))],
        out_specs=[b3((T, ML_WIDTH)), b4((H, D, D)), b4((H, 1, D)), b4((H, 1, 1))],
        out_shape=[jax.ShapeDtypeStruct((B, T, ML_WIDTH), BF16),
                   jax.ShapeDtypeStruct((B, H, D, D), F32),
                   jax.ShapeDtypeStruct((B, H, 1, D), F32),
                   jax.ShapeDtypeStruct((B, H, 1, 1), F32)],
        compiler_params=_params("parallel"), name="mlstm_sample")(q, k, v, o, gc, gr, g_head, c0, n0, m0)


def _da_lambda(lam_ref, lam_init):
    lv = lam_ref[...]
    a = jnp.sum(lv[0:1, :] * lv[1:2, :], axis=1, keepdims=True)
    b = jnp.sum(lv[2:3, :] * lv[3:4, :], axis=1, keepdims=True)
    return jnp.exp(a) - jnp.exp(b) + lam_init


def _stack_components(q):
    lane = lax.broadcasted_iota(jnp.int32, q.shape, 1)
    zero = jnp.zeros_like(q)
    return jnp.concatenate([jnp.where(lane < DA_QK_DIM, q, zero), jnp.where(lane >= DA_QK_DIM, q, zero)], axis=0)


def _da_prompt_kernel(q_ref, k_ref, v_ref, lam_ref, gh_ref, out_ref, vt_scr, s_a, s_b, m_scr, l_scr, acc_scr,
                      *, lam_init):
    i = pl.program_id(2)
    tq = q_ref.shape[0]
    nblk = vt_scr.shape[0]

    @pl.when(i == 0)
    def _():
        for j in range(nblk):
            vt_scr[j] = v_ref[j * tq:(j + 1) * tq, :].astype(F32).T.astype(BF16)

    qt = q_ref[...].astype(F32).T
    dim = lax.broadcasted_iota(jnp.int32, qt.shape, 0)
    qqt = jnp.concatenate([jnp.where(dim < DA_QK_DIM, qt, 0.0), jnp.where(dim >= DA_QK_DIM, qt, 0.0)],
                          axis=1).astype(BF16)

    def scores(j, s_ref):
        off = pl.multiple_of(j * tq, tq)
        s_ref[...] = _dot(k_ref[pl.ds(off, tq), :], qqt)

    def accumulate(j, s_ref, masked):
        st = s_ref[...]
        if masked:
            key = lax.broadcasted_iota(jnp.int32, st.shape, 0)
            query = lax.broadcasted_iota(jnp.int32, st.shape, 1) % tq
            st = jnp.where(key <= query, st, NEG_BIG)
        m_old = m_scr[...]
        m_new = jnp.maximum(m_old, jnp.max(st, axis=0, keepdims=True))
        alpha = jnp.exp2(m_old - m_new)
        p = jnp.exp2(st - m_new)
        l_scr[...] = alpha * l_scr[...] + jnp.sum(p, axis=0, keepdims=True)
        acc_scr[...] = alpha * acc_scr[...] + _dot(vt_scr[j], p.astype(BF16))
        m_scr[...] = m_new

    m_scr[...] = jnp.full_like(m_scr, NEG_BIG)
    l_scr[...] = jnp.zeros_like(l_scr)
    acc_scr[...] = jnp.zeros_like(acc_scr)
    scores(0, s_a)

    def body(t, carry):
        scores(2 * t + 1, s_b)
        accumulate(2 * t, s_a, False)
        scores(2 * t + 2, s_a)
        accumulate(2 * t + 1, s_b, False)
        return carry

    lax.fori_loop(0, i // 2, body, 0)

    @pl.when(i % 2 == 1)
    def _():
        scores(i, s_b)
        accumulate(i - 1, s_a, False)
        accumulate(i, s_b, True)

    @pl.when(i % 2 == 0)
    def _():
        accumulate(i, s_a, True)

    ot = acc_scr[...] / l_scr[...]
    at = ot[:, :tq] - _da_lambda(lam_ref, lam_init) * ot[:, tq:]
    norm = at * lax.rsqrt(jnp.mean(at * at, axis=0, keepdims=True) + RMS_EPS)
    out_ref[...] = ((norm.T * gh_ref[0]) * (1.0 - lam_init)).astype(out_ref.dtype)


def _da_prompt(dq, dk, dv, da_lambda, g_head3, B, T, tq, lam_init):
    nq = T // tq
    rows = B * T
    kv_spec = pl.BlockSpec((T, DA_V_DIM), lambda b, h, i: (b, h))
    return pl.pallas_call(
        functools.partial(_da_prompt_kernel, lam_init=lam_init), grid=(B, DA_HEADS, nq),
        in_specs=[pl.BlockSpec((tq, DA_V_DIM), lambda b, h, i: (b * nq + i, h)), kv_spec, kv_spec,
                  pl.BlockSpec(da_lambda.shape, lambda b, h, i: (0, 0)),
                  pl.BlockSpec((1, 1, DA_V_DIM), lambda b, h, i: (h, 0, 0))],
        out_specs=pl.BlockSpec((tq, DA_V_DIM), lambda b, h, i: (b * nq + i, h)),
        out_shape=jax.ShapeDtypeStruct((rows, DA_WIDTH), BF16),
        scratch_shapes=[pltpu.VMEM((nq, DA_V_DIM, tq), BF16),
                        pltpu.VMEM((tq, 2 * tq), F32), pltpu.VMEM((tq, 2 * tq), F32),
                        pltpu.VMEM((1, 2 * tq), F32), pltpu.VMEM((1, 2 * tq), F32),
                        pltpu.VMEM((DA_V_DIM, 2 * tq), F32)],
        compiler_params=_params("parallel", "parallel", "arbitrary"), name="da_prompt")(
            dq, dk, dv, da_lambda, g_head3)


def _da_sample_kernel(pt_ref, q_ref, kn_ref, vn_ref, lam_ref, gh_ref, ck_hbm, cv_hbm, out_ref,
                      kbuf, vbuf, sem, *, lam_init, layer):
    b = pl.program_id(0)
    nb = pl.num_programs(0)
    n_pages, page = kbuf.shape[1], kbuf.shape[2] // DA_HEADS
    T = q_ref.shape[1]
    scale = DA_QK_DIM ** -0.5

    def copies(bi, slot):
        out = []
        for p in range(n_pages):
            pg = pt_ref[bi, p]
            out.append(pltpu.make_async_copy(ck_hbm.at[layer, pg], kbuf.at[slot, p], sem.at[slot, 0]))
            out.append(pltpu.make_async_copy(cv_hbm.at[layer, pg], vbuf.at[slot, p], sem.at[slot, 1]))
        return out

    @pl.when(b == 0)
    def _():
        for cp in copies(0, 0):
            cp.start()

    @pl.when(b + 1 < nb)
    def _():
        for cp in copies(b + 1, (b + 1) % 2):
            cp.start()

    slot = b % 2
    for cp in copies(b, slot):
        cp.wait()

    lam = _da_lambda(lam_ref, lam_init)
    q_all = q_ref[0]
    n_rows = n_pages * page * DA_HEADS
    qq = jnp.concatenate([_stack_components(q_all[:, hh * DA_V_DIM:(hh + 1) * DA_V_DIM])
                          for hh in range(DA_HEADS)], axis=0)
    k_all = kbuf[slot].reshape(n_rows, DA_V_DIM).astype(BF16)
    v_all = vbuf[slot].reshape(n_rows, DA_V_DIM).astype(BF16)
    rq = 2 * T * DA_HEADS
    own_head = (lax.broadcasted_iota(jnp.int32, (rq, n_rows), 1) % DA_HEADS
                == lax.broadcasted_iota(jnp.int32, (rq, n_rows), 0) // (2 * T))
    s_past = jnp.where(own_head, _dot_nt(qq.astype(BF16), k_all) * scale, NEG_BIG)
    trow = lax.broadcasted_iota(jnp.int32, (rq, 1), 0) % T

    def per_query_row(new_ref, t):
        x = new_ref[0, t]
        return jnp.concatenate([jnp.broadcast_to(x[hh:hh + 1, :], (2 * T, DA_V_DIM))
                                for hh in range(DA_HEADS)], axis=0)

    s_new = [jnp.where(trow >= t,
                       jnp.sum(qq * per_query_row(kn_ref, t), axis=1, keepdims=True) * scale, NEG_BIG)
             for t in range(T)]
    m = jnp.max(s_past, axis=1, keepdims=True)
    for t in range(T):
        m = jnp.maximum(m, s_new[t])
    p_past = jnp.exp(s_past - m)
    l = jnp.sum(p_past, axis=1, keepdims=True)
    acc = _dot(p_past.astype(BF16), v_all)
    for t in range(T):
        p_t = jnp.exp(s_new[t] - m)
        l = l + p_t
        acc = acc + p_t * per_query_row(vn_ref, t)
    o = acc / l
    for hh in range(DA_HEADS):
        r0 = hh * 2 * T
        a = o[r0:r0 + T] - lam * o[r0 + T:r0 + 2 * T]
        out_ref[0, :, hh * DA_V_DIM:(hh + 1) * DA_V_DIM] = (
            _rms(a, gh_ref[hh:hh + 1, :]) * (1.0 - lam_init)).astype(out_ref.dtype)


def _da_sample(page_table, dq, dk, dv, da_lambda, g_head, cache_k, cache_v, lam_init, layer):
    B, T, _ = dq.shape
    n_pages = page_table.shape[1]
    page_rows = cache_k.shape[2]
    blk = pl.BlockSpec((1, T, DA_WIDTH), lambda b, pt: (b, 0, 0))
    kv_blk = pl.BlockSpec((1, T, DA_HEADS, DA_V_DIM), lambda b, pt: (b, 0, 0, 0))
    grid_spec = pltpu.PrefetchScalarGridSpec(
        num_scalar_prefetch=1, grid=(B,),
        in_specs=[blk, kv_blk, kv_blk, pl.BlockSpec(da_lambda.shape, lambda b, pt: (0, 0)),
                  pl.BlockSpec(g_head.shape, lambda b, pt: (0, 0)),
                  pl.BlockSpec(memory_space=pl.ANY), pl.BlockSpec(memory_space=pl.ANY)],
        out_specs=blk,
        scratch_shapes=[pltpu.VMEM((2, n_pages, page_rows, DA_V_DIM), F32),
                        pltpu.VMEM((2, n_pages, page_rows, DA_V_DIM), F32),
                        pltpu.SemaphoreType.DMA((2, 2))])
    return pl.pallas_call(
        functools.partial(_da_sample_kernel, lam_init=lam_init, layer=layer), grid_spec=grid_spec,
        out_shape=jax.ShapeDtypeStruct((B, T, DA_WIDTH), BF16),
        compiler_params=_params("arbitrary"), name="da_sample")(
            page_table, dq, dk, dv, da_lambda, g_head, cache_k, cache_v)


def _proj_norm_kernel(*refs, n_in, has_next):
    a_refs, w_refs = refs[:n_in], refs[n_in:2 * n_in]
    x_ref, gpost_ref, gpre_ref = refs[2 * n_in:2 * n_in + 3]
    rest = refs[2 * n_in + 3:]
    wn_ref = rest[0] if has_next else None
    xo_ref, ho_ref = rest[-2:]
    acc = _dot(a_refs[0][...], w_refs[0][...])
    for a, w in zip(a_refs[1:], w_refs[1:]):
        acc = acc + _dot(a[...], w[...])
    x1 = x_ref[...] + _rms(acc, gpost_ref[...])
    xo_ref[...] = x1
    hn = _rms(x1, gpre_ref[...]).astype(BF16)
    ho_ref[...] = (_dot(hn, wn_ref[...]) if has_next else hn).astype(ho_ref.dtype)


def _proj_norm(a_list, w_list, x, g_post, g_pre, w_next, tm, h_dtype=BF16):
    rows, d = x.shape
    n_in = len(a_list)
    has_next = w_next is not None
    row_spec = lambda w: pl.BlockSpec((tm, w), lambda i: (i, 0))
    full = lambda a: pl.BlockSpec(a.shape, lambda i: (0,) * a.ndim)
    ins = list(a_list) + list(w_list) + [x, g_post, g_pre] + ([w_next] if has_next else [])
    in_specs = ([row_spec(a.shape[1]) for a in a_list] + [full(w) for w in w_list]
                + [row_spec(d), full(g_post), full(g_pre)] + ([full(w_next)] if has_next else []))
    n_out = w_next.shape[1] if has_next else d
    return pl.pallas_call(
        functools.partial(_proj_norm_kernel, n_in=n_in, has_next=has_next), grid=(rows // tm,),
        in_specs=in_specs, out_specs=[row_spec(d), row_spec(n_out)],
        out_shape=[jax.ShapeDtypeStruct((rows, d), F32), jax.ShapeDtypeStruct((rows, n_out), h_dtype)],
        compiler_params=_params("parallel"), name="proj_norm")(*ins)


def _norm_matmul_kernel(x_ref, g_ref, w_ref, o_ref):
    o_ref[...] = _dot(_rms(x_ref[...], g_ref[...]).astype(BF16), w_ref[...])


def _norm_matmul(x, g, w, tm, tn):
    rows, d = x.shape
    n = w.shape[1]
    return pl.pallas_call(
        _norm_matmul_kernel, grid=(rows // tm, n // tn),
        in_specs=[pl.BlockSpec((tm, d), lambda i, j: (i, 0)), pl.BlockSpec(g.shape, lambda i, j: (0, 0)),
                  pl.BlockSpec((d, tn), lambda i, j: (0, j))],
        out_specs=pl.BlockSpec((tm, tn), lambda i, j: (i, j)),
        out_shape=jax.ShapeDtypeStruct((rows, n), F32),
        compiler_params=_params("parallel", "parallel"), name="mem_kv")(x, g, w)


def _mem_attn_heads(qkv):
    s = [_dot_nt(q.astype(BF16), k.astype(BF16)) * (q.shape[1] ** -0.5) for q, k, _ in qkv]
    p = [jnp.exp(x - jnp.max(x, axis=1, keepdims=True)) for x in s]
    pv = [_dot(x.astype(BF16), v.astype(BF16)) for x, (_, _, v) in zip(p, qkv)]
    return [o / jnp.sum(x, axis=1, keepdims=True) for o, x in zip(pv, p)]


def _mem_attn_kernel(q_ref, mk_ref, mv_ref, o_ref):
    hd = q_ref.shape[2] // MEM_HEADS
    items = [(bi, slice(hh * hd, (hh + 1) * hd)) for bi in range(q_ref.shape[0]) for hh in range(MEM_HEADS)]
    outs = _mem_attn_heads([(q_ref[bi, :, cols], mk_ref[bi, :, cols], mv_ref[bi, :, cols]) for bi, cols in items])
    for (bi, cols), o in zip(items, outs):
        o_ref[bi, :, cols] = o.astype(o_ref.dtype)


def _mem_attn(q, mk, mv, bb, tm):
    B, T, d = q.shape
    kv_spec = pl.BlockSpec((bb,) + mk.shape[1:], lambda b, i: (b, 0, 0))
    return pl.pallas_call(
        _mem_attn_kernel, grid=(B // bb, T // tm),
        in_specs=[pl.BlockSpec((bb, tm, d), lambda b, i: (b, i, 0)), kv_spec, kv_spec],
        out_specs=pl.BlockSpec((bb, tm, d), lambda b, i: (b, i, 0)),
        out_shape=jax.ShapeDtypeStruct((B, T, d), BF16),
        compiler_params=_params("parallel", "parallel"), name="mem_attn")(q, mk, mv)


def _mem_attn_cache_kernel(q_ref, mk_hbm, mv_hbm, o_ref, kbuf, vbuf, sem, *, layer):
    g = pl.program_id(0)
    bb = q_ref.shape[0]
    hd = kbuf.shape[-1]

    def copies(gi, slot):
        out = []
        for bi in range(bb):
            for hh in range(MEM_HEADS):
                b = gi * bb + bi
                out.append(pltpu.make_async_copy(mk_hbm.at[layer, b, :, hh, :], kbuf.at[slot, bi, hh],
                                                 sem.at[slot, 0]))
                out.append(pltpu.make_async_copy(mv_hbm.at[layer, b, :, hh, :], vbuf.at[slot, bi, hh],
                                                 sem.at[slot, 1]))
        return out

    @pl.when(g == 0)
    def _():
        for cp in copies(0, 0):
            cp.start()

    @pl.when(g + 1 < pl.num_programs(0))
    def _():
        for cp in copies(g + 1, (g + 1) % 2):
            cp.start()

    slot = g % 2
    for cp in copies(g, slot):
        cp.wait()
    for bi in range(bb):
        outs = _mem_attn_heads([(q_ref[bi, :, hh * hd:(hh + 1) * hd], kbuf[slot, bi, hh], vbuf[slot, bi, hh])
                                for hh in range(MEM_HEADS)])
        for hh, o in enumerate(outs):
            o_ref[bi, :, hh * hd:(hh + 1) * hd] = o.astype(o_ref.dtype)


def _mem_attn_cache(q, mk, mv, layer, bb):
    B, T, d = q.shape
    M, H, hd = mk.shape[2:]
    return pl.pallas_call(
        functools.partial(_mem_attn_cache_kernel, layer=layer), grid=(B // bb,),
        in_specs=[pl.BlockSpec((bb, T, d), lambda g: (g, 0, 0)),
                  pl.BlockSpec(memory_space=pl.ANY), pl.BlockSpec(memory_space=pl.ANY)],
        out_specs=pl.BlockSpec((bb, T, d), lambda g: (g, 0, 0)),
        out_shape=jax.ShapeDtypeStruct((B, T, d), BF16),
        scratch_shapes=[pltpu.VMEM((2, bb, H, M, hd), F32), pltpu.VMEM((2, bb, H, M, hd), F32),
                        pltpu.SemaphoreType.DMA((2, 2))],
        compiler_params=_params("arbitrary"), name="mem_attn_cache")(q, mk, mv)


def _ffn_chunks(nh):
    nchunk = 2 if (nh // LANES) % 2 == 0 else 1
    cw = nh // nchunk
    return [(j * cw, cw) for j in range(nchunk)]


def _ffn_prompt_kernel(hf_ref, x_ref, wup_ref, wdw_ref, bdw_ref, wdn_ref, gpost_ref, y_ref, ulast_ref, ubuf,
                       *, tiles_per_seq):
    i = pl.program_id(0)
    tm = hf_ref.shape[0]
    nh = wdn_ref.shape[0]
    halo = ubuf.shape[0] - tm

    @pl.when(i % tiles_per_seq == 0)
    def _():
        ubuf[0:halo, :] = jnp.zeros((halo, ubuf.shape[1]), F32)

    hf = hf_ref[...]
    f = jnp.zeros((tm, y_ref.shape[1]), F32)
    for c0, cw in _ffn_chunks(nh):
        cg = []
        for base in (c0, nh + c0):
            cs = slice(base, base + cw)
            ubuf[halo:halo + tm, cs] = _dot(hf, wup_ref[:, cs])
            c = bdw_ref[:, cs]
            for j in range(CONV_W):
                lo = halo - (CONV_W - 1) + j
                c = c + ubuf[lo:lo + tm, cs] * wdw_ref[j:j + 1, cs]
            cg.append(c)
        act = (jax.nn.silu(cg[1]) * cg[0]).astype(BF16)
        f = f + _dot(act, wdn_ref[c0:c0 + cw, :])
    y_ref[...] = x_ref[...] + _rms(f, gpost_ref[...])
    tail = ubuf[tm:tm + halo, :]
    ubuf[0:halo, :] = tail
    ulast_ref[0] = tail


def _ffn_prompt(hf, x, w_up, w_dw, b_dw, w_down, g_post, B, T, tm):
    rows, d = x.shape
    npad = w_up.shape[1]
    tiles = T // tm
    halo = 8
    row_spec = lambda w: pl.BlockSpec((tm, w), lambda i: (i, 0))
    full = _resident
    return pl.pallas_call(
        functools.partial(_ffn_prompt_kernel, tiles_per_seq=tiles), grid=(rows // tm,),
        in_specs=[row_spec(d), row_spec(d), full(w_up), full(w_dw), full(b_dw), full(w_down), full(g_post)],
        out_specs=[row_spec(d), pl.BlockSpec((1, halo, npad), lambda i: (i // tiles, 0, 0))],
        out_shape=[jax.ShapeDtypeStruct((rows, d), F32), jax.ShapeDtypeStruct((B, halo, npad), F32)],
        scratch_shapes=[pltpu.VMEM((tm + halo, npad), F32)],
        compiler_params=_params("arbitrary"), name="ffn_prompt")(hf, x, w_up, w_dw, b_dw, w_down, g_post)


def _ffn_sample_kernel(hf_ref, x_ref, cb_ref, wup_ref, wdw_ref, bdw_ref, wdn_ref, gpost_ref, y_ref, unew_ref,
                       *, T):
    nb = hf_ref.shape[0] // T
    nh = wdn_ref.shape[0]
    hf = hf_ref[...]
    f = jnp.zeros(y_ref.shape, F32)
    for c0, cw in _ffn_chunks(nh):
        cg = []
        for base in (c0, nh + c0):
            cs = slice(base, base + cw)
            u = _dot(hf, wup_ref[:, cs])
            ext = [cb_ref[j, :, cs] for j in range(CONV_W - 1)] + [u[t * nb:(t + 1) * nb] for t in range(T)]
            for j in range(CONV_W - 1):
                unew_ref[j, :, cs] = ext[len(ext) - (CONV_W - 1) + j]
            rows = []
            for t in range(T):
                c = bdw_ref[:, cs]
                for j in range(CONV_W):
                    c = c + ext[t + j] * wdw_ref[j:j + 1, cs]
                rows.append(c)
            cg.append(jnp.concatenate(rows, axis=0))
        act = (jax.nn.silu(cg[1]) * cg[0]).astype(BF16)
        f = f + _dot(act, wdn_ref[c0:c0 + cw, :])
    y_ref[...] = x_ref[...] + _rms(f, gpost_ref[...])


def _ffn_sample(hf, x, cb, w_up, w_dw, b_dw, w_down, g_post, T):
    rows, d = x.shape
    npad = w_up.shape[1]
    ins = (hf, x, cb, w_up, w_dw, b_dw, w_down, g_post)
    full = lambda a: pl.BlockSpec(a.shape, lambda i: (0,) * a.ndim)
    return pl.pallas_call(
        functools.partial(_ffn_sample_kernel, T=T), grid=(1,),
        in_specs=[full(a) for a in ins],
        out_specs=[pl.BlockSpec((rows, d), lambda i: (0, 0)),
                   pl.BlockSpec((CONV_W - 1, rows // T, npad), lambda i: (0, 0, 0))],
        out_shape=[jax.ShapeDtypeStruct((rows, d), F32),
                   jax.ShapeDtypeStruct((CONV_W - 1, rows // T, npad), F32)],
        compiler_params=_params("arbitrary"), name="ffn_sample")(*ins)


def _pad_halves(a, nh, nh_pad):
    pad = [(0, 0)] * (a.ndim - 1) + [(0, nh_pad - nh)]
    return jnp.concatenate([jnp.pad(a[..., :nh], pad), jnp.pad(a[..., nh:], pad)], axis=-1)


def _unpad_halves(a, nh, nh_pad):
    return jnp.concatenate([a[..., :nh], a[..., nh_pad:nh_pad + nh]], axis=-1)


def _row_tile(rows, want):
    t = min(rows, want)
    while rows % t:
        t //= 2
    return t


def kernel(x_prompt, x_sample, cache_dk, cache_dv, cache_mem_k, cache_mem_v, state_ml_C, state_ml_n, state_ml_m, state_conv, page_table, mem_prompt, g_mix_pre, g_mix_post, w_in, b_if, g_ml_head, da_lambda, g_da_head, w_out, g_mem_pre, g_mem_post, g_mem_src, w_mq, w_mk, w_mv, w_mo, g_ffn_pre, g_ffn_post, w_up, w_dw, b_dw, w_down):
    depth = w_in.shape[0]
    Bp, Tp, d = x_prompt.shape
    Bs, Ts, _ = x_sample.shape
    n_pages, page = page_table.shape[1], cache_dk.shape[2]
    past_len = n_pages * page
    n_mem = mem_prompt.shape[1]
    nh = w_down.shape[1]
    nh_pad = -(-nh // LANES) * LANES
    H, D = ML_HEADS, ML_HEAD_DIM
    rows_p, rows_s = Bp * Tp, Bs * Ts

    tm_p = _row_tile(Tp, 512)
    tm_s = _row_tile(rows_s, 256)
    tab_p = _rope_tables(Tp, Tp, 0)
    tab_s = _rope_tables(tm_s, Ts, past_len)

    yp = x_prompt.reshape(rows_p, d)
    ys = x_sample.reshape(rows_s, d)
    outs = [[] for _ in range(14)]
    row = lambda a: a.reshape(1, -1)
    for l in range(depth):
        lam_init = 0.8 - 0.6 * math.exp(-0.3 * l)
        wi = w_in[l]
        w_main = jnp.concatenate([wi[:, :4 * ML_WIDTH], wi[:, 4 * ML_WIDTH + 2 * H:]], axis=1).astype(BF16)
        w_gate = wi[:, 4 * ML_WIDTH:4 * ML_WIDTH + 2 * H]
        w_g = jnp.pad(w_gate, ((0, 0), (0, LANES - 2 * H))).astype(BF16)
        w_gt = w_gate.T.astype(BF16)
        b_col = jnp.pad(b_if[l], (0, LANES - 2 * H)).reshape(1, LANES)
        b_row = b_if[l].reshape(2 * H, 1)
        wo_ml, wo_da = w_out[l][:ML_WIDTH].astype(BF16), w_out[l][ML_WIDTH:].astype(BF16)
        wq_b, wo_b = w_mq[l].astype(BF16), w_mo[l].astype(BF16)
        wkv_b = jnp.concatenate([w_mk[l], w_mv[l]], axis=1).astype(BF16)
        wup_b = _pad_halves(w_up[l], nh, nh_pad).astype(BF16)
        wdw_p = _pad_halves(w_dw[l], nh, nh_pad)
        bdw_p = _pad_halves(b_dw[l].reshape(1, -1), nh, nh_pad)
        wdn_b = jnp.pad(w_down[l], ((0, nh_pad - nh), (0, 0))).astype(BF16)
        g_da3 = g_da_head[l].reshape(DA_HEADS, 1, DA_V_DIM)

        q, k, v, o, gc, gr, dq, dk, dv, dkb, dvb = _in_proj(
            yp, row(g_mix_pre[l]), w_main, w_g, w_gt, b_col, b_row, tab_p, BF16, tm_p,
            DA_QK_DIM ** -0.5 * math.log2(math.e))
        h_ml, C_p, n_p, m_p = _mlstm_prompt(q, k, v, o, gc, gr, g_ml_head[l], Bp, Tp, _row_tile(Tp, 128))
        h_da = _da_prompt(dq, dkb, dvb, da_lambda[l], g_da3, Bp, Tp, _row_tile(Tp, 512), lam_init)
        x1, qm = _proj_norm([h_ml, h_da], [wo_ml, wo_da], yp, row(g_mix_post[l]), row(g_mem_pre[l]), wq_b, tm_p)
        mkv = _norm_matmul(mem_prompt.reshape(Bp * n_mem, d), row(g_mem_src[l]), wkv_b,
                           _row_tile(Bp * n_mem, 256), 512)
        mk_p, mv_p = mkv[:, :d].reshape(Bp, n_mem, d), mkv[:, d:].reshape(Bp, n_mem, d)
        om = _mem_attn(qm.reshape(Bp, Tp, d), mk_p, mv_p, 1, tm_p)
        x2, hf = _proj_norm([om.reshape(rows_p, d)], [wo_b], x1, row(g_mem_post[l]), row(g_ffn_pre[l]), None, tm_p)
        yp, ulast = _ffn_prompt(hf, x2, wup_b, wdw_p, bdw_p, wdn_b, row(g_ffn_post[l]), Bp, Tp, tm_p)
        cv_p = _unpad_halves(ulast[:, ulast.shape[1] - (CONV_W - 1):], nh, nh_pad)

        q, k, v, o, gc, gr, dq, dk_s, dv_s, _, _ = _in_proj(
            ys, row(g_mix_pre[l]), w_main, w_g, w_gt, b_col, b_row, tab_s, F32, tm_s, 1.0)
        r3 = lambda a: a.reshape(Bs, Ts, a.shape[-1])
        r4 = lambda a: a.reshape(Bs, Ts, DA_HEADS, DA_V_DIM)
        gr3 = gr.reshape(2 * H, Bs, Ts).transpose(1, 0, 2)
        h_ml, C_s, n_s, m_s = _mlstm_sample(
            r3(q), r3(k), r3(v), r3(o), r3(gc), gr3, g_ml_head[l],
            state_ml_C[l], state_ml_n[l].reshape(Bs, H, 1, D), state_ml_m[l].reshape(Bs, H, 1, 1),
            _row_tile(Bs, 8))
        h_da = _da_sample(page_table, r3(dq), r4(dk_s), r4(dv_s), da_lambda[l], g_da_head[l],
                          cache_dk.reshape(depth, -1, page * DA_HEADS, DA_V_DIM),
                          cache_dv.reshape(depth, -1, page * DA_HEADS, DA_V_DIM), lam_init, l)
        x1, qm = _proj_norm([h_ml.reshape(rows_s, ML_WIDTH), h_da.reshape(rows_s, DA_WIDTH)], [wo_ml, wo_da], ys,
                            row(g_mix_post[l]), row(g_mem_pre[l]), wq_b, tm_s, h_dtype=F32)
        om = _mem_attn_cache(qm.reshape(Bs, Ts, d), cache_mem_k, cache_mem_v, l, _row_tile(Bs, 4))
        x2, hf = _proj_norm([om.reshape(rows_s, d)], [wo_b], x1, row(g_mem_post[l]), row(g_ffn_pre[l]), None, tm_s)
        tmaj = lambda a: a.reshape(Bs, Ts, -1).transpose(1, 0, 2).reshape(rows_s, -1)
        cb = _pad_halves(state_conv[l], nh, nh_pad).transpose(1, 0, 2)
        y_t, unew = _ffn_sample(tmaj(hf), tmaj(x2), cb, wup_b, wdw_p, bdw_p, wdn_b, row(g_ffn_post[l]), Ts)
        ys = y_t.reshape(Ts, Bs, d).transpose(1, 0, 2).reshape(rows_s, d)
        cv_s = _unpad_halves(unew.transpose(1, 0, 2), nh, nh_pad)

        vals = (dk.reshape(Bp, Tp, DA_HEADS, DA_V_DIM), dv.reshape(Bp, Tp, DA_HEADS, DA_V_DIM),
                mk_p.reshape(Bp, n_mem, MEM_HEADS, d // MEM_HEADS), mv_p.reshape(Bp, n_mem, MEM_HEADS, d // MEM_HEADS),
                C_p, n_p.reshape(Bp, H, D), m_p.reshape(Bp, H), cv_p,
                dk_s.reshape(Bs, Ts, DA_HEADS, DA_V_DIM), dv_s.reshape(Bs, Ts, DA_HEADS, DA_V_DIM),
                C_s, n_s.reshape(Bs, H, D), m_s.reshape(Bs, H), cv_s)
        for acc, val in zip(outs, vals):
            acc.append(val)
    return (yp.reshape(Bp, Tp, d), ys.reshape(Bs, Ts, d)) + tuple(jnp.stack(a) for a in outs)
```

```python
import functools
import math

import jax
import jax.numpy as jnp
from jax import lax
from jax.experimental import pallas as pl
from jax.experimental.pallas import tpu as pltpu

F32 = jnp.float32
BF16 = jnp.bfloat16

ML_HEADS = 4
ML_HEAD_DIM = 128
ML_WIDTH = ML_HEADS * ML_HEAD_DIM
DA_HEADS = 4
DA_V_DIM = 128
DA_QK_DIM = 64
DA_WIDTH = DA_HEADS * DA_V_DIM
ROPE_DIM = 16
ROPE_THETA = 500000.0
MEM_HEADS = 4
CONV_W = 3
RMS_EPS = 1e-6
LANES = 128
NEG_BIG = -1e30
VMEM_LIMIT = 56 * 1024 * 1024


def _params(*sem, vmem=VMEM_LIMIT):
    return pltpu.CompilerParams(dimension_semantics=sem, vmem_limit_bytes=vmem)


def _resident(a):
    return pl.BlockSpec(a.shape, lambda *_: (0,) * a.ndim, pipeline_mode=pl.Buffered(1))


def _rms(x, g):
    return x * lax.rsqrt(jnp.mean(x * x, axis=-1, keepdims=True) + RMS_EPS) * g


def _log_sigmoid(x):
    return jnp.minimum(x, 0.0) - jnp.log1p(jnp.exp(-jnp.abs(x)))


def _dot(a, b):
    return jnp.dot(a, b, preferred_element_type=F32)


def _dot_nt(a, b):
    return lax.dot_general(a, b, (((1,), (1,)), ((), ())), preferred_element_type=F32)


def _dot_tn(a, b):
    return lax.dot_general(a, b, (((0,), (0,)), ((), ())), preferred_element_type=F32)


def _rope_table_kernel(cos_ref, sa_ref, sb_ref, *, period, offset):
    rows = cos_ref.shape[0]
    half = ROPE_DIM // 2
    r = lax.broadcasted_iota(jnp.int32, (rows, LANES), 0) + pl.program_id(0) * rows
    lane = lax.broadcasted_iota(jnp.int32, (rows, LANES), 1)
    pos = (offset + r % period).astype(F32)
    c = lane % DA_QK_DIM
    j = (c % half).astype(F32)
    inv = jnp.exp(-math.log(ROPE_THETA) * (2.0 * j / ROPE_DIM))
    ang = pos * inv
    cos, sin = jnp.cos(ang), jnp.sin(ang)
    cos_ref[...] = jnp.where(c < ROPE_DIM, cos, 1.0)
    sa_ref[...] = jnp.where(c < half, -sin, 0.0)
    sb_ref[...] = jnp.where((c >= half) & (c < ROPE_DIM), sin, 0.0)


def _rope_tables(rows, period, offset):
    blk = min(rows, 512)
    spec = pl.BlockSpec((blk, LANES), lambda i: (i, 0))
    shp = jax.ShapeDtypeStruct((rows, LANES), F32)
    return pl.pallas_call(
        functools.partial(_rope_table_kernel, period=period, offset=offset),
        grid=(rows // blk,), out_specs=[spec] * 3, out_shape=[shp] * 3,
        compiler_params=_params("parallel"), name="rope_tables")()


def _rope(x, cos, sa, sb):
    outs = []
    for j in range(x.shape[1] // LANES):
        xj = x[:, j * LANES:(j + 1) * LANES]
        up = pltpu.roll(xj, LANES - ROPE_DIM // 2, axis=1)
        dn = pltpu.roll(xj, ROPE_DIM // 2, axis=1)
        outs.append(xj * cos + up * sa + dn * sb)
    return outs


def _in_proj_kernel(x_ref, g_ref, w_ref, wg_ref, wgt_ref, bcol_ref, brow_ref, cos_ref, sa_ref, sb_ref,
                    q_ref, k_ref, v_ref, o_ref, gc_ref, gr_ref, dq_ref, dk_ref, dv_ref, dkb_ref, dvb_ref,
                    *, dq_scale):
    h = _rms(x_ref[...], g_ref[...]).astype(BF16)
    for j, ref in enumerate((q_ref, k_ref, v_ref, o_ref)):
        ref[...] = _dot(h, w_ref[:, j * ML_WIDTH:(j + 1) * ML_WIDTH]).astype(ref.dtype)
    gc = _dot(h, wg_ref[...]) + bcol_ref[...]
    lane = lax.broadcasted_iota(jnp.int32, gc.shape, 1)
    gc_ref[...] = jnp.where(lane < ML_HEADS, gc, _log_sigmoid(gc))
    gr = _dot_nt(wgt_ref[...], h) + brow_ref[...]
    row = lax.broadcasted_iota(jnp.int32, gr.shape, 0)
    gr_ref[...] = jnp.where(row < ML_HEADS, gr, _log_sigmoid(gr))
    base = 4 * ML_WIDTH
    cos, sa, sb = cos_ref[...], sa_ref[...], sb_ref[...]
    dq = _rope(_dot(h, w_ref[:, base:base + DA_WIDTH]), cos, sa, sb)
    dk = _rope(_dot(h, w_ref[:, base + DA_WIDTH:base + 2 * DA_WIDTH]), cos, sa, sb)
    dv = _dot(h, w_ref[:, base + 2 * DA_WIDTH:base + 3 * DA_WIDTH])
    dvb_ref[...] = dv.astype(BF16)
    for j in range(DA_HEADS):
        cols = slice(j * LANES, (j + 1) * LANES)
        dq_ref[:, cols] = (dq[j] * dq_scale).astype(dq_ref.dtype)
        dk_ref[:, j, :] = dk[j]
        dkb_ref[:, cols] = dk[j].astype(BF16)
        dv_ref[:, j, :] = dv[:, cols]


def _in_proj(x, g, w_main, w_g, w_gt, b_col, b_row, tables, act_dtype, tm, dq_scale):
    rows, d = x.shape
    cos, sa, sb = tables
    nt = cos.shape[0] // tm
    row_spec = lambda w: pl.BlockSpec((tm, w), lambda i: (i, 0))
    full = lambda a: pl.BlockSpec(a.shape, lambda i: (0,) * a.ndim)
    tab_spec = pl.BlockSpec((tm, LANES), lambda i: (i % nt, 0))
    out_shape = [jax.ShapeDtypeStruct((rows, ML_WIDTH), act_dtype)] * 4 + [
        jax.ShapeDtypeStruct((rows, LANES), F32), jax.ShapeDtypeStruct((8, rows), F32),
        jax.ShapeDtypeStruct((rows, DA_WIDTH), act_dtype),
        jax.ShapeDtypeStruct((rows, DA_HEADS, DA_V_DIM), F32), jax.ShapeDtypeStruct((rows, DA_HEADS, DA_V_DIM), F32),
        jax.ShapeDtypeStruct((rows, DA_WIDTH), BF16), jax.ShapeDtypeStruct((rows, DA_WIDTH), BF16)]
    kv_spec = pl.BlockSpec((tm, DA_HEADS, DA_V_DIM), lambda i: (i, 0, 0))
    out_specs = [row_spec(ML_WIDTH)] * 4 + [row_spec(LANES), pl.BlockSpec((8, tm), lambda i: (0, i)),
                                            row_spec(DA_WIDTH), kv_spec, kv_spec,
                                            row_spec(DA_WIDTH), row_spec(DA_WIDTH)]
    return pl.pallas_call(
        functools.partial(_in_proj_kernel, dq_scale=dq_scale), grid=(rows // tm,),
        in_specs=[row_spec(d), full(g), full(w_main), full(w_g), full(w_gt), full(b_col), full(b_row),
                  tab_spec, tab_spec, tab_spec],
        out_specs=out_specs, out_shape=out_shape,
        compiler_params=_params("parallel"), name="in_proj")(
            x, g, w_main, w_g, w_gt, b_col, b_row, cos, sa, sb)


def _run_phases(*gens):
    live = list(gens)
    while live:
        for g in list(live):
            try:
                next(g)
            except StopIteration:
                live.remove(g)


def _mlstm_heads(heads):
    out = []
    _run_phases(_mlstm_phases(heads, out))
    return out


def _mlstm_phases(heads, out):
    L, D = heads[0][0].shape
    scale = D ** -0.5
    r = lax.broadcasted_iota(jnp.int32, (L, L), 0)
    c = lax.broadcasted_iota(jnp.int32, (L, L), 1)
    tri = c <= r
    gate = []
    for q, k, v, i_col, f_col, i_row, f_row, C, n, m in heads:
        b_col = jnp.sum(jnp.where(tri, f_row, 0.0), axis=1, keepdims=True)
        b_row = jnp.sum(jnp.where(r <= c, f_col, 0.0), axis=0, keepdims=True)
        log_d = jnp.where(tri, b_col - b_row + i_row, -jnp.inf)
        inter = b_col + m
        m_row = jnp.maximum(jnp.max(log_d, axis=1, keepdims=True), inter)
        b_last = b_col[L - 1:L, :]
        log_w = b_last - b_col + i_col
        m_new = jnp.maximum(b_last + m, jnp.max(log_w, axis=0, keepdims=True))
        gate.append(dict(
            m_row=m_row, w_inter=jnp.exp(inter - m_row), d=scale * jnp.exp(log_d - m_row), m_new=m_new,
            decay=jnp.exp(b_last + m - m_new), wk=(scale * jnp.exp(log_w - m_new)) * k.astype(F32)))
    yield
    qb = [h[0].astype(BF16) for h in heads]
    vb = [h[2].astype(BF16) for h in heads]
    qk = [_dot_nt(qb[j], heads[j][1].astype(BF16)) for j in range(len(heads))]
    qc = [_dot(qb[j], heads[j][7].astype(BF16)) for j in range(len(heads))]
    kv = [_dot_tn(gate[j]["wk"].astype(BF16), vb[j]) for j in range(len(heads))]
    yield
    s = [qk[j] * gate[j]["d"] for j in range(len(heads))]
    yield
    sv = [_dot(s[j].astype(BF16), vb[j]) for j in range(len(heads))]
    yield
    for j, (q, k, v, i_col, f_col, i_row, f_row, C, n, m) in enumerate(heads):
        g = gate[j]
        num = g["w_inter"] * qc[j] + sv[j]
        den = (g["w_inter"] * jnp.sum(q.astype(F32) * n, axis=1, keepdims=True)
               + jnp.sum(s[j], axis=1, keepdims=True))
        h = num / jnp.maximum(jnp.abs(den), jnp.exp(-g["m_row"]))
        c_new = g["decay"] * C + kv[j]
        n_new = g["decay"] * n + jnp.sum(g["wk"], axis=0, keepdims=True)
        out.append((h, c_new, n_new, g["m_new"]))


def _ml_head_out(h, o, g):
    return _rms(h, g) * jax.nn.sigmoid(o.astype(F32))


def _mlstm_prompt_phases(first_chunk, q_ref, k_ref, v_ref, o_ref, gc_ref, gr_ref, gh_ref,
                         h_ref, c_out, n_out, m_out, c_scr, n_scr, m_scr):
    gc, gr = gc_ref[...], gr_ref[...]
    cols = [slice(hh * ML_HEAD_DIM, (hh + 1) * ML_HEAD_DIM) for hh in range(ML_HEADS)]
    prev = lambda ref, hh: jnp.where(first_chunk, 0.0, ref[hh])
    res = []
    yield from _mlstm_phases([
        (q_ref[:, cols[hh]], k_ref[:, cols[hh]], v_ref[:, cols[hh]],
         gc[:, hh:hh + 1], gc[:, ML_HEADS + hh:ML_HEADS + hh + 1],
         gr[hh:hh + 1, :], gr[ML_HEADS + hh:ML_HEADS + hh + 1, :],
         prev(c_scr, hh), prev(n_scr, hh), prev(m_scr, hh)) for hh in range(ML_HEADS)], res)
    for hh, (h, c_new, n_new, m_new) in enumerate(res):
        c_scr[hh], n_scr[hh], m_scr[hh] = c_new, n_new, m_new
        c_out[0, hh], n_out[0, hh], m_out[0, hh] = c_new, n_new, m_new
        h_ref[:, cols[hh]] = _ml_head_out(h, o_ref[:, cols[hh]], gh_ref[hh:hh + 1, :]).astype(h_ref.dtype)


def _mlstm_sample_kernel(q_ref, k_ref, v_ref, o_ref, gc_ref, gr_ref, gh_ref, c_in, n_in, m_in,
                         h_ref, c_out, n_out, m_out):
    cols = [slice(hh * ML_HEAD_DIM, (hh + 1) * ML_HEAD_DIM) for hh in range(ML_HEADS)]
    per_trip = 2 if q_ref.shape[0] % 2 == 0 else 1

    def body(it, carry):
        elems = [it * per_trip + u for u in range(per_trip)]
        gates = [(gc_ref[bi], gr_ref[bi]) for bi in elems]
        items = [(bi, hh) for bi in elems for hh in range(ML_HEADS)]
        res = _mlstm_heads([
            (q_ref[bi, :, cols[hh]], k_ref[bi, :, cols[hh]], v_ref[bi, :, cols[hh]],
             gc[:, hh:hh + 1], gc[:, ML_HEADS + hh:ML_HEADS + hh + 1],
             gr[hh:hh + 1, :], gr[ML_HEADS + hh:ML_HEADS + hh + 1, :],
             c_in[bi, hh], n_in[bi, hh], m_in[bi, hh])
            for bi, (gc, gr) in zip(elems, gates) for hh in range(ML_HEADS)])
        for (bi, hh), (h, c_new, n_new, m_new) in zip(items, res):
            c_out[bi, hh], n_out[bi, hh], m_out[bi, hh] = c_new, n_new, m_new
            h_ref[bi, :, cols[hh]] = _ml_head_out(
                h, o_ref[bi, :, cols[hh]], gh_ref[hh:hh + 1, :]).astype(h_ref.dtype)
        return carry

    lax.fori_loop(0, q_ref.shape[0] // per_trip, body, 0)


def _mlstm_sample(q, k, v, o, gc, gr, g_head, c0, n0, m0, bb):
    B, T, _ = q.shape
    H, D = ML_HEADS, ML_HEAD_DIM
    b3 = lambda s: pl.BlockSpec((bb,) + s, lambda i: (i, 0, 0))
    b4 = lambda s: pl.BlockSpec((bb,) + s, lambda i: (i, 0, 0, 0))
    return pl.pallas_call(
        _mlstm_sample_kernel, grid=(B // bb,),
        in_specs=[b3((T, ML_WIDTH))] * 4 + [b3((T, LANES)), b3((8, T)),
                                            pl.BlockSpec(g_head.shape, lambda i: (0, 0)),
                                            b4((H, D, D)), b4((H, 1, D)), b4((H, 1, 1))],
        out_specs=[b3((T, ML_WIDTH)), b4((H, D, D)), b4((H, 1, D)), b4((H, 1, 1))],
        out_shape=[jax.ShapeDtypeStruct((B, T, ML_WIDTH), BF16),
                   jax.ShapeDtypeStruct((B, H, D, D), F32),
                   jax.ShapeDtypeStruct((B, H, 1, D), F32),
                   jax.ShapeDtypeStruct((B, H, 1, 1), F32)],
        compiler_params=_params("parallel"), name="mlstm_sample")(q, k, v, o, gc, gr, g_head, c0, n0, m0)


def _da_lambda(lam_ref, lam_init):
    lv = lam_ref[...]
    a = jnp.sum(lv[0:1, :] * lv[1:2, :], axis=1, keepdims=True)
    b = jnp.sum(lv[2:3, :] * lv[3:4, :], axis=1, keepdims=True)
    return jnp.exp(a) - jnp.exp(b) + lam_init


def _stack_components(q):
    lane = lax.broadcasted_iota(jnp.int32, q.shape, 1)
    zero = jnp.zeros_like(q)
    return jnp.concatenate([jnp.where(lane < DA_QK_DIM, q, zero), jnp.where(lane >= DA_QK_DIM, q, zero)], axis=0)


def _da_prompt_kernel(q_ref, k_ref, v_ref, lam_ref, gh_ref, out_ref, vt_scr, s_a, s_b, m_scr, l_scr, acc_scr,
                      *, lam_init):
    i = pl.program_id(2)
    tq = q_ref.shape[0]
    nblk = vt_scr.shape[0]

    @pl.when(i == 0)
    def _():
        for j in range(nblk):
            vt_scr[j] = v_ref[j * tq:(j + 1) * tq, :].astype(F32).T.astype(BF16)

    qt = q_ref[...].astype(F32).T
    dim = lax.broadcasted_iota(jnp.int32, qt.shape, 0)
    qqt = jnp.concatenate([jnp.where(dim < DA_QK_DIM, qt, 0.0), jnp.where(dim >= DA_QK_DIM, qt, 0.0)],
                          axis=1).astype(BF16)

    def scores(j, s_ref):
        off = pl.multiple_of(j * tq, tq)
        s_ref[...] = _dot(k_ref[pl.ds(off, tq), :], qqt)

    def accumulate(j, s_ref, masked):
        st = s_ref[...]
        if masked:
            key = lax.broadcasted_iota(jnp.int32, st.shape, 0)
            query = lax.broadcasted_iota(jnp.int32, st.shape, 1) % tq
            st = jnp.where(key <= query, st, NEG_BIG)
        m_old = m_scr[...]
        m_new = jnp.maximum(m_old, jnp.max(st, axis=0, keepdims=True))
        alpha = jnp.exp2(m_old - m_new)
        p = jnp.exp2(st - m_new)
        l_scr[...] = alpha * l_scr[...] + jnp.sum(p, axis=0, keepdims=True)
        acc_scr[...] = alpha * acc_scr[...] + _dot(vt_scr[j], p.astype(BF16))
        m_scr[...] = m_new

    m_scr[...] = jnp.full_like(m_scr, NEG_BIG)
    l_scr[...] = jnp.zeros_like(l_scr)
    acc_scr[...] = jnp.zeros_like(acc_scr)
    scores(0, s_a)

    def body(t, carry):
        scores(2 * t + 1, s_b)
        accumulate(2 * t, s_a, False)
        scores(2 * t + 2, s_a)
        accumulate(2 * t + 1, s_b, False)
        return carry

    lax.fori_loop(0, i // 2, body, 0)

    @pl.when(i % 2 == 1)
    def _():
        scores(i, s_b)
        accumulate(i - 1, s_a, False)
        accumulate(i, s_b, True)

    @pl.when(i % 2 == 0)
    def _():
        accumulate(i, s_a, True)

    ot = acc_scr[...] / l_scr[...]
    at = ot[:, :tq] - _da_lambda(lam_ref, lam_init) * ot[:, tq:]
    norm = at * lax.rsqrt(jnp.mean(at * at, axis=0, keepdims=True) + RMS_EPS)
    out_ref[...] = ((norm.T * gh_ref[0]) * (1.0 - lam_init)).astype(out_ref.dtype)


def _da_prompt(dq, dk, dv, da_lambda, g_head3, B, T, tq, lam_init):
    nq = T // tq
    rows = B * T
    kv_spec = pl.BlockSpec((T, DA_V_DIM), lambda b, h, i: (b, h))
    return pl.pallas_call(
        functools.partial(_da_prompt_kernel, lam_init=lam_init), grid=(B, DA_HEADS, nq),
        in_specs=[pl.BlockSpec((tq, DA_V_DIM), lambda b, h, i: (b * nq + i, h)), kv_spec, kv_spec,
                  pl.BlockSpec(da_lambda.shape, lambda b, h, i: (0, 0)),
                  pl.BlockSpec((1, 1, DA_V_DIM), lambda b, h, i: (h, 0, 0))],
        out_specs=pl.BlockSpec((tq, DA_V_DIM), lambda b, h, i: (b * nq + i, h)),
        out_shape=jax.ShapeDtypeStruct((rows, DA_WIDTH), BF16),
        scratch_shapes=[pltpu.VMEM((nq, DA_V_DIM, tq), BF16),
                        pltpu.VMEM((tq, 2 * tq), F32), pltpu.VMEM((tq, 2 * tq), F32),
                        pltpu.VMEM((1, 2 * tq), F32), pltpu.VMEM((1, 2 * tq), F32),
                        pltpu.VMEM((DA_V_DIM, 2 * tq), F32)],
        compiler_params=_params("parallel", "parallel", "arbitrary"), name="da_prompt")(
            dq, dk, dv, da_lambda, g_head3)


def _paged_fetch(pt_ref, ck_hbm, cv_hbm, kbuf, vbuf, sem, layer):
    b = pl.program_id(0)
    n_pages = kbuf.shape[1]

    def copies(bi, slot):
        out = []
        for p in range(n_pages):
            pg = pt_ref[bi, p]
            out.append(pltpu.make_async_copy(ck_hbm.at[layer, pg], kbuf.at[slot, p], sem.at[slot, 0]))
            out.append(pltpu.make_async_copy(cv_hbm.at[layer, pg], vbuf.at[slot, p], sem.at[slot, 1]))
        return out

    @pl.when(b == 0)
    def _():
        for cp in copies(0, 0):
            cp.start()

    @pl.when(b + 1 < pl.num_programs(0))
    def _():
        for cp in copies(b + 1, (b + 1) % 2):
            cp.start()

    slot = b % 2
    for cp in copies(b, slot):
        cp.wait()
    return slot


def _da_sample_phases(slot, q_ref, kn_ref, vn_ref, lam_ref, gh_ref, out_ref, kbuf, vbuf, lam_init):
    n_pages, page = kbuf.shape[1], kbuf.shape[2] // DA_HEADS
    T = q_ref.shape[1]
    scale = DA_QK_DIM ** -0.5
    lam = _da_lambda(lam_ref, lam_init)
    q_all = q_ref[0]
    n_rows = n_pages * page * DA_HEADS
    qq = jnp.concatenate([_stack_components(q_all[:, hh * DA_V_DIM:(hh + 1) * DA_V_DIM])
                          for hh in range(DA_HEADS)], axis=0)
    k_all = kbuf[slot].reshape(n_rows, DA_V_DIM).astype(BF16)
    s_raw = _dot_nt(qq.astype(BF16), k_all)
    yield
    rq = 2 * T * DA_HEADS
    own_head = (lax.broadcasted_iota(jnp.int32, (rq, n_rows), 1) % DA_HEADS
                == lax.broadcasted_iota(jnp.int32, (rq, n_rows), 0) // (2 * T))
    s_past = jnp.where(own_head, s_raw * scale, NEG_BIG)
    trow = lax.broadcasted_iota(jnp.int32, (rq, 1), 0) % T

    def per_query_row(new_ref, t):
        x = new_ref[0, t]
        return jnp.concatenate([jnp.broadcast_to(x[hh:hh + 1, :], (2 * T, DA_V_DIM))
                                for hh in range(DA_HEADS)], axis=0)

    s_new = [jnp.where(trow >= t,
                       jnp.sum(qq * per_query_row(kn_ref, t), axis=1, keepdims=True) * scale, NEG_BIG)
             for t in range(T)]
    m = jnp.max(s_past, axis=1, keepdims=True)
    for t in range(T):
        m = jnp.maximum(m, s_new[t])
    p_past = jnp.exp(s_past - m)
    l = jnp.sum(p_past, axis=1, keepdims=True)
    v_all = vbuf[slot].reshape(n_rows, DA_V_DIM).astype(BF16)
    yield
    acc = _dot(p_past.astype(BF16), v_all)
    yield
    for t in range(T):
        p_t = jnp.exp(s_new[t] - m)
        l = l + p_t
        acc = acc + p_t * per_query_row(vn_ref, t)
    o = acc / l
    for hh in range(DA_HEADS):
        r0 = hh * 2 * T
        a = o[r0:r0 + T] - lam * o[r0 + T:r0 + 2 * T]
        out_ref[0, :, hh * DA_V_DIM:(hh + 1) * DA_V_DIM] = (
            _rms(a, gh_ref[hh:hh + 1, :]) * (1.0 - lam_init)).astype(out_ref.dtype)


def _da_sample_mlstm_prompt_kernel(
        pt_ref, q_ref, kn_ref, vn_ref, lam_ref, gh_ref, ck_hbm, cv_hbm,
        mq_ref, mk_ref, mv_ref, mo_ref, gc_ref, gr_ref, mgh_ref,
        out_ref, h_ref, c_out, n_out, m_out,
        kbuf, vbuf, sem, c_scr, n_scr, m_scr, *, lam_init, layer, chunks):
    i = pl.program_id(0)

    @pl.when(i == 0)
    def _():
        c_scr[...] = jnp.zeros_like(c_scr)
        n_scr[...] = jnp.zeros_like(n_scr)
        m_scr[...] = jnp.zeros_like(m_scr)

    slot = _paged_fetch(pt_ref, ck_hbm, cv_hbm, kbuf, vbuf, sem, layer)
    _run_phases(
        _da_sample_phases(slot, q_ref, kn_ref, vn_ref, lam_ref, gh_ref, out_ref, kbuf, vbuf, lam_init),
        _mlstm_prompt_phases(i % chunks == 0, mq_ref, mk_ref, mv_ref, mo_ref, gc_ref, gr_ref, mgh_ref,
                             h_ref, c_out, n_out, m_out, c_scr, n_scr, m_scr))


def _da_sample_mlstm_prompt(page_table, dq, dk, dv, da_lambda, g_da, cache_k, cache_v, lam_init, layer,
                            mq, mk, mv, mo, gc, gr, g_ml, Bp, Tp):
    B, T, _ = dq.shape
    rows = Bp * Tp
    assert rows % B == 0 and Tp % (rows // B) == 0, "one mLSTM chunk per sample batch element"
    L = rows // B
    chunks = Tp // L
    H, D = ML_HEADS, ML_HEAD_DIM
    n_pages = page_table.shape[1]
    page_rows = cache_k.shape[2]
    blk = pl.BlockSpec((1, T, DA_WIDTH), lambda i, pt: (i, 0, 0))
    kv_blk = pl.BlockSpec((1, T, DA_HEADS, DA_V_DIM), lambda i, pt: (i, 0, 0, 0))
    const = lambda a: pl.BlockSpec(a.shape, lambda i, pt: (0,) * a.ndim)
    mblk = lambda w: pl.BlockSpec((L, w), lambda i, pt: (i, 0))
    state = lambda *s: pl.BlockSpec((1, H) + s, lambda i, pt: (i // chunks, 0, 0, 0))
    grid_spec = pltpu.PrefetchScalarGridSpec(
        num_scalar_prefetch=1, grid=(B,),
        in_specs=[blk, kv_blk, kv_blk, const(da_lambda), const(g_da),
                  pl.BlockSpec(memory_space=pl.ANY), pl.BlockSpec(memory_space=pl.ANY),
                  mblk(ML_WIDTH), mblk(ML_WIDTH), mblk(ML_WIDTH), mblk(ML_WIDTH), mblk(LANES),
                  pl.BlockSpec((8, L), lambda i, pt: (0, i)), const(g_ml)],
        out_specs=[blk, mblk(ML_WIDTH), state(D, D), state(1, D), state(1, 1)],
        scratch_shapes=[pltpu.VMEM((2, n_pages, page_rows, DA_V_DIM), F32),
                        pltpu.VMEM((2, n_pages, page_rows, DA_V_DIM), F32),
                        pltpu.SemaphoreType.DMA((2, 2)),
                        pltpu.VMEM((H, D, D), F32), pltpu.VMEM((H, 1, D), F32), pltpu.VMEM((H, 1, 1), F32)])
    return pl.pallas_call(
        functools.partial(_da_sample_mlstm_prompt_kernel, lam_init=lam_init, layer=layer, chunks=chunks),
        grid_spec=grid_spec,
        out_shape=[jax.ShapeDtypeStruct((B, T, DA_WIDTH), BF16),
                   jax.ShapeDtypeStruct((rows, ML_WIDTH), BF16),
                   jax.ShapeDtypeStruct((Bp, H, D, D), F32),
                   jax.ShapeDtypeStruct((Bp, H, 1, D), F32),
                   jax.ShapeDtypeStruct((Bp, H, 1, 1), F32)],
        compiler_params=_params("arbitrary"), name="da_sample_mlstm_prompt")(
            page_table, dq, dk, dv, da_lambda, g_da, cache_k, cache_v, mq, mk, mv, mo, gc, gr, g_ml)


def _proj_norm_kernel(*refs, n_in, has_next):
    a_refs, w_refs = refs[:n_in], refs[n_in:2 * n_in]
    x_ref, gpost_ref, gpre_ref = refs[2 * n_in:2 * n_in + 3]
    rest = refs[2 * n_in + 3:]
    wn_ref = rest[0] if has_next else None
    xo_ref, ho_ref = rest[-2:]
    acc = _dot(a_refs[0][...], w_refs[0][...])
    for a, w in zip(a_refs[1:], w_refs[1:]):
        acc = acc + _dot(a[...], w[...])
    x1 = x_ref[...] + _rms(acc, gpost_ref[...])
    xo_ref[...] = x1
    hn = _rms(x1, gpre_ref[...]).astype(BF16)
    ho_ref[...] = (_dot(hn, wn_ref[...]) if has_next else hn).astype(ho_ref.dtype)


def _proj_norm(a_list, w_list, x, g_post, g_pre, w_next, tm, h_dtype=BF16):
    rows, d = x.shape
    n_in = len(a_list)
    has_next = w_next is not None
    row_spec = lambda w: pl.BlockSpec((tm, w), lambda i: (i, 0))
    full = lambda a: pl.BlockSpec(a.shape, lambda i: (0,) * a.ndim)
    ins = list(a_list) + list(w_list) + [x, g_post, g_pre] + ([w_next] if has_next else [])
    in_specs = ([row_spec(a.shape[1]) for a in a_list] + [full(w) for w in w_list]
                + [row_spec(d), full(g_post), full(g_pre)] + ([full(w_next)] if has_next else []))
    n_out = w_next.shape[1] if has_next else d
    return pl.pallas_call(
        functools.partial(_proj_norm_kernel, n_in=n_in, has_next=has_next), grid=(rows // tm,),
        in_specs=in_specs, out_specs=[row_spec(d), row_spec(n_out)],
        out_shape=[jax.ShapeDtypeStruct((rows, d), F32), jax.ShapeDtypeStruct((rows, n_out), h_dtype)],
        compiler_params=_params("parallel"), name="proj_norm")(*ins)


def _norm_matmul_kernel(x_ref, g_ref, w_ref, o_ref):
    o_ref[...] = _dot(_rms(x_ref[...], g_ref[...]).astype(BF16), w_ref[...])


def _norm_matmul(x, g, w, tm, tn):
    rows, d = x.shape
    n = w.shape[1]
    return pl.pallas_call(
        _norm_matmul_kernel, grid=(rows // tm, n // tn),
        in_specs=[pl.BlockSpec((tm, d), lambda i, j: (i, 0)), pl.BlockSpec(g.shape, lambda i, j: (0, 0)),
                  pl.BlockSpec((d, tn), lambda i, j: (0, j))],
        out_specs=pl.BlockSpec((tm, tn), lambda i, j: (i, j)),
        out_shape=jax.ShapeDtypeStruct((rows, n), F32),
        compiler_params=_params("parallel", "parallel"), name="mem_kv")(x, g, w)


def _mem_attn_heads(qkv):
    s = [_dot_nt(q.astype(BF16), k.astype(BF16)) * (q.shape[1] ** -0.5) for q, k, _ in qkv]
    p = [jnp.exp(x - jnp.max(x, axis=1, keepdims=True)) for x in s]
    pv = [_dot(x.astype(BF16), v.astype(BF16)) for x, (_, _, v) in zip(p, qkv)]
    return [o / jnp.sum(x, axis=1, keepdims=True) for o, x in zip(pv, p)]


def _mem_attn_kernel(q_ref, mk_ref, mv_ref, o_ref):
    hd = q_ref.shape[2] // MEM_HEADS
    items = [(bi, slice(hh * hd, (hh + 1) * hd)) for bi in range(q_ref.shape[0]) for hh in range(MEM_HEADS)]
    outs = _mem_attn_heads([(q_ref[bi, :, cols], mk_ref[bi, :, cols], mv_ref[bi, :, cols]) for bi, cols in items])
    for (bi, cols), o in zip(items, outs):
        o_ref[bi, :, cols] = o.astype(o_ref.dtype)


def _mem_attn(q, mk, mv, bb, tm):
    B, T, d = q.shape
    kv_spec = pl.BlockSpec((bb,) + mk.shape[1:], lambda b, i: (b, 0, 0))
    return pl.pallas_call(
        _mem_attn_kernel, grid=(B // bb, T // tm),
        in_specs=[pl.BlockSpec((bb, tm, d), lambda b, i: (b, i, 0)), kv_spec, kv_spec],
        out_specs=pl.BlockSpec((bb, tm, d), lambda b, i: (b, i, 0)),
        out_shape=jax.ShapeDtypeStruct((B, T, d), BF16),
        compiler_params=_params("parallel", "parallel"), name="mem_attn")(q, mk, mv)


def _mem_attn_cache_kernel(q_ref, mk_hbm, mv_hbm, o_ref, kbuf, vbuf, sem, *, layer):
    g = pl.program_id(0)
    bb = q_ref.shape[0]
    hd = kbuf.shape[-1]

    def copies(gi, slot):
        out = []
        for bi in range(bb):
            for hh in range(MEM_HEADS):
                b = gi * bb + bi
                out.append(pltpu.make_async_copy(mk_hbm.at[layer, b, :, hh, :], kbuf.at[slot, bi, hh],
                                                 sem.at[slot, 0]))
                out.append(pltpu.make_async_copy(mv_hbm.at[layer, b, :, hh, :], vbuf.at[slot, bi, hh],
                                                 sem.at[slot, 1]))
        return out

    @pl.when(g == 0)
    def _():
        for cp in copies(0, 0):
            cp.start()

    @pl.when(g + 1 < pl.num_programs(0))
    def _():
        for cp in copies(g + 1, (g + 1) % 2):
            cp.start()

    slot = g % 2
    for cp in copies(g, slot):
        cp.wait()
    for bi in range(bb):
        outs = _mem_attn_heads([(q_ref[bi, :, hh * hd:(hh + 1) * hd], kbuf[slot, bi, hh], vbuf[slot, bi, hh])
                                for hh in range(MEM_HEADS)])
        for hh, o in enumerate(outs):
            o_ref[bi, :, hh * hd:(hh + 1) * hd] = o.astype(o_ref.dtype)


def _mem_attn_cache(q, mk, mv, layer, bb):
    B, T, d = q.shape
    M, H, hd = mk.shape[2:]
    return pl.pallas_call(
        functools.partial(_mem_attn_cache_kernel, layer=layer), grid=(B // bb,),
        in_specs=[pl.BlockSpec((bb, T, d), lambda g: (g, 0, 0)),
                  pl.BlockSpec(memory_space=pl.ANY), pl.BlockSpec(memory_space=pl.ANY)],
        out_specs=pl.BlockSpec((bb, T, d), lambda g: (g, 0, 0)),
        out_shape=jax.ShapeDtypeStruct((B, T, d), BF16),
        scratch_shapes=[pltpu.VMEM((2, bb, H, M, hd), F32), pltpu.VMEM((2, bb, H, M, hd), F32),
                        pltpu.SemaphoreType.DMA((2, 2))],
        compiler_params=_params("arbitrary"), name="mem_attn_cache")(q, mk, mv)


def _ffn_chunks(nh):
    nchunk = 2 if (nh // LANES) % 2 == 0 else 1
    cw = nh // nchunk
    return [(j * cw, cw) for j in range(nchunk)]


def _ffn_prompt_kernel(hf_ref, x_ref, wup_ref, wdw_ref, bdw_ref, wdn_ref, gpost_ref, y_ref, ulast_ref, ubuf,
                       *, tiles_per_seq):
    i = pl.program_id(0)
    tm = hf_ref.shape[0]
    nh = wdn_ref.shape[0]
    halo = ubuf.shape[0] - tm

    @pl.when(i % tiles_per_seq == 0)
    def _():
        ubuf[0:halo, :] = jnp.zeros((halo, ubuf.shape[1]), F32)

    hf = hf_ref[...]
    f = jnp.zeros((tm, y_ref.shape[1]), F32)
    for c0, cw in _ffn_chunks(nh):
        cg = []
        for base in (c0, nh + c0):
            cs = slice(base, base + cw)
            ubuf[halo:halo + tm, cs] = _dot(hf, wup_ref[:, cs])
            c = bdw_ref[:, cs]
            for j in range(CONV_W):
                lo = halo - (CONV_W - 1) + j
                c = c + ubuf[lo:lo + tm, cs] * wdw_ref[j:j + 1, cs]
            cg.append(c)
        act = (jax.nn.silu(cg[1]) * cg[0]).astype(BF16)
        f = f + _dot(act, wdn_ref[c0:c0 + cw, :])
    y_ref[...] = x_ref[...] + _rms(f, gpost_ref[...])
    tail = ubuf[tm:tm + halo, :]
    ubuf[0:halo, :] = tail
    ulast_ref[0] = tail


def _ffn_prompt(hf, x, w_up, w_dw, b_dw, w_down, g_post, B, T, tm):
    rows, d = x.shape
    npad = w_up.shape[1]
    tiles = T // tm
    halo = 8
    row_spec = lambda w: pl.BlockSpec((tm, w), lambda i: (i, 0))
    full = _resident
    return pl.pallas_call(
        functools.partial(_ffn_prompt_kernel, tiles_per_seq=tiles), grid=(rows // tm,),
        in_specs=[row_spec(d), row_spec(d), full(w_up), full(w_dw), full(b_dw), full(w_down), full(g_post)],
        out_specs=[row_spec(d), pl.BlockSpec((1, halo, npad), lambda i: (i // tiles, 0, 0))],
        out_shape=[jax.ShapeDtypeStruct((rows, d), F32), jax.ShapeDtypeStruct((B, halo, npad), F32)],
        scratch_shapes=[pltpu.VMEM((tm + halo, npad), F32)],
        compiler_params=_params("arbitrary"), name="ffn_prompt")(hf, x, w_up, w_dw, b_dw, w_down, g_post)


def _ffn_sample_kernel(hf_ref, x_ref, cb_ref, wup_ref, wdw_ref, bdw_ref, wdn_ref, gpost_ref, y_ref, unew_ref,
                       *, T):
    nb = hf_ref.shape[0] // T
    nh = wdn_ref.shape[0]
    hf = hf_ref[...]
    f = jnp.zeros(y_ref.shape, F32)
    for c0, cw in _ffn_chunks(nh):
        cg = []
        for base in (c0, nh + c0):
            cs = slice(base, base + cw)
            u = _dot(hf, wup_ref[:, cs])
            ext = [cb_ref[j, :, cs] for j in range(CONV_W - 1)] + [u[t * nb:(t + 1) * nb] for t in range(T)]
            for j in range(CONV_W - 1):
                unew_ref[j, :, cs] = ext[len(ext) - (CONV_W - 1) + j]
            rows = []
            for t in range(T):
                c = bdw_ref[:, cs]
                for j in range(CONV_W):
                    c = c + ext[t + j] * wdw_ref[j:j + 1, cs]
                rows.append(c)
            cg.append(jnp.concatenate(rows, axis=0))
        act = (jax.nn.silu(cg[1]) * cg[0]).astype(BF16)
        f = f + _dot(act, wdn_ref[c0:c0 + cw, :])
    y_ref[...] = x_ref[...] + _rms(f, gpost_ref[...])


def _ffn_sample(hf, x, cb, w_up, w_dw, b_dw, w_down, g_post, T):
    rows, d = x.shape
    npad = w_up.shape[1]
    ins = (hf, x, cb, w_up, w_dw, b_dw, w_down, g_post)
    full = lambda a: pl.BlockSpec(a.shape, lambda i: (0,) * a.ndim)
    return pl.pallas_call(
        functools.partial(_ffn_sample_kernel, T=T), grid=(1,),
        in_specs=[full(a) for a in ins],
        out_specs=[pl.BlockSpec((rows, d), lambda i: (0, 0)),
                   pl.BlockSpec((CONV_W - 1, rows // T, npad), lambda i: (0, 0, 0))],
        out_shape=[jax.ShapeDtypeStruct((rows, d), F32),
                   jax.ShapeDtypeStruct((CONV_W - 1, rows // T, npad), F32)],
        compiler_params=_params("arbitrary"), name="ffn_sample")(*ins)


def _pad_halves(a, nh, nh_pad):
    pad = [(0, 0)] * (a.ndim - 1) + [(0, nh_pad - nh)]
    return jnp.concatenate([jnp.pad(a[..., :nh], pad), jnp.pad(a[..., nh:], pad)], axis=-1)


def _unpad_halves(a, nh, nh_pad):
    return jnp.concatenate([a[..., :nh], a[..., nh_pad:nh_pad + nh]], axis=-1)


def _row_tile(rows, want):
    t = min(rows, want)
    while rows % t:
        t //= 2
    return t


def kernel(x_prompt, x_sample, cache_dk, cache_dv, cache_mem_k, cache_mem_v, state_ml_C, state_ml_n, state_ml_m, state_conv, page_table, mem_prompt, g_mix_pre, g_mix_post, w_in, b_if, g_ml_head, da_lambda, g_da_head, w_out, g_mem_pre, g_mem_post, g_mem_src, w_mq, w_mk, w_mv, w_mo, g_ffn_pre, g_ffn_post, w_up, w_dw, b_dw, w_down):
    depth = w_in.shape[0]
    Bp, Tp, d = x_prompt.shape
    Bs, Ts, _ = x_sample.shape
    n_pages, page = page_table.shape[1], cache_dk.shape[2]
    past_len = n_pages * page
    n_mem = mem_prompt.shape[1]
    nh = w_down.shape[1]
    nh_pad = -(-nh // LANES) * LANES
    H, D = ML_HEADS, ML_HEAD_DIM
    rows_p, rows_s = Bp * Tp, Bs * Ts

    tm_p = _row_tile(Tp, 512)
    tm_s = _row_tile(rows_s, 256)
    tab_p = _rope_tables(Tp, Tp, 0)
    tab_s = _rope_tables(tm_s, Ts, past_len)

    yp = x_prompt.reshape(rows_p, d)
    ys = x_sample.reshape(rows_s, d)
    outs = [[] for _ in range(14)]
    row = lambda a: a.reshape(1, -1)
    for l in range(depth):
        lam_init = 0.8 - 0.6 * math.exp(-0.3 * l)
        wi = w_in[l]
        w_main = jnp.concatenate([wi[:, :4 * ML_WIDTH], wi[:, 4 * ML_WIDTH + 2 * H:]], axis=1).astype(BF16)
        w_gate = wi[:, 4 * ML_WIDTH:4 * ML_WIDTH + 2 * H]
        w_g = jnp.pad(w_gate, ((0, 0), (0, LANES - 2 * H))).astype(BF16)
        w_gt = w_gate.T.astype(BF16)
        b_col = jnp.pad(b_if[l], (0, LANES - 2 * H)).reshape(1, LANES)
        b_row = b_if[l].reshape(2 * H, 1)
        wo_ml, wo_da = w_out[l][:ML_WIDTH].astype(BF16), w_out[l][ML_WIDTH:].astype(BF16)
        wq_b, wo_b = w_mq[l].astype(BF16), w_mo[l].astype(BF16)
        wkv_b = jnp.concatenate([w_mk[l], w_mv[l]], axis=1).astype(BF16)
        wup_b = _pad_halves(w_up[l], nh, nh_pad).astype(BF16)
        wdw_p = _pad_halves(w_dw[l], nh, nh_pad)
        bdw_p = _pad_halves(b_dw[l].reshape(1, -1), nh, nh_pad)
        wdn_b = jnp.pad(w_down[l], ((0, nh_pad - nh), (0, 0))).astype(BF16)
        g_da3 = g_da_head[l].reshape(DA_HEADS, 1, DA_V_DIM)

        q, k, v, o, gc, gr, dq, dk, dv, dkb, dvb = _in_proj(
            yp, row(g_mix_pre[l]), w_main, w_g, w_gt, b_col, b_row, tab_p, BF16, tm_p,
            DA_QK_DIM ** -0.5 * math.log2(math.e))
        qs, ks, vs, os_, gcs, grs, dqs, dk_s, dv_s, _, _ = _in_proj(
            ys, row(g_mix_pre[l]), w_main, w_g, w_gt, b_col, b_row, tab_s, F32, tm_s, 1.0)
        r3 = lambda a: a.reshape(Bs, Ts, a.shape[-1])
        r4 = lambda a: a.reshape(Bs, Ts, DA_HEADS, DA_V_DIM)
        h_da_s, h_ml, C_p, n_p, m_p = _da_sample_mlstm_prompt(
            page_table, r3(dqs), r4(dk_s), r4(dv_s), da_lambda[l], g_da_head[l],
            cache_dk.reshape(depth, -1, page * DA_HEADS, DA_V_DIM),
            cache_dv.reshape(depth, -1, page * DA_HEADS, DA_V_DIM), lam_init, l,
            q, k, v, o, gc, gr, g_ml_head[l], Bp, Tp)

        h_da = _da_prompt(dq, dkb, dvb, da_lambda[l], g_da3, Bp, Tp, _row_tile(Tp, 512), lam_init)
        x1, qm = _proj_norm([h_ml, h_da], [wo_ml, wo_da], yp, row(g_mix_post[l]), row(g_mem_pre[l]), wq_b, tm_p)
        mkv = _norm_matmul(mem_prompt.reshape(Bp * n_mem, d), row(g_mem_src[l]), wkv_b,
                           _row_tile(Bp * n_mem, 256), 512)
        mk_p, mv_p = mkv[:, :d].reshape(Bp, n_mem, d), mkv[:, d:].reshape(Bp, n_mem, d)
        om = _mem_attn(qm.reshape(Bp, Tp, d), mk_p, mv_p, 1, tm_p)
        x2, hf = _proj_norm([om.reshape(rows_p, d)], [wo_b], x1, row(g_mem_post[l]), row(g_ffn_pre[l]), None, tm_p)
        yp, ulast = _ffn_prompt(hf, x2, wup_b, wdw_p, bdw_p, wdn_b, row(g_ffn_post[l]), Bp, Tp, tm_p)
        cv_p = _unpad_halves(ulast[:, ulast.shape[1] - (CONV_W - 1):], nh, nh_pad)

        gr3 = grs.reshape(2 * H, Bs, Ts).transpose(1, 0, 2)
        h_ml, C_s, n_s, m_s = _mlstm_sample(
            r3(qs), r3(ks), r3(vs), r3(os_), r3(gcs), gr3, g_ml_head[l],
            state_ml_C[l], state_ml_n[l].reshape(Bs, H, 1, D), state_ml_m[l].reshape(Bs, H, 1, 1),
            _row_tile(Bs, 8))
        x1, qm = _proj_norm([h_ml.reshape(rows_s, ML_WIDTH), h_da_s.reshape(rows_s, DA_WIDTH)], [wo_ml, wo_da], ys,
                            row(g_mix_post[l]), row(g_mem_pre[l]), wq_b, tm_s, h_dtype=F32)
        om = _mem_attn_cache(qm.reshape(Bs, Ts, d), cache_mem_k, cache_mem_v, l, _row_tile(Bs, 4))
        x2, hf = _proj_norm([om.reshape(rows_s, d)], [wo_b], x1, row(g_mem_post[l]), row(g_ffn_pre[l]), None, tm_s)
        tmaj = lambda a: a.reshape(Bs, Ts, -1).transpose(1, 0, 2).reshape(rows_s, -1)
        cb = _pad_halves(state_conv[l], nh, nh_pad).transpose(1, 0, 2)
        y_t, unew = _ffn_sample(tmaj(hf), tmaj(x2), cb, wup_b, wdw_p, bdw_p, wdn_b, row(g_ffn_post[l]), Ts)
        ys = y_t.reshape(Ts, Bs, d).transpose(1, 0, 2).reshape(rows_s, d)
        cv_s = _unpad_halves(unew.transpose(1, 0, 2), nh, nh_pad)

        vals = (dk.reshape(Bp, Tp, DA_HEADS, DA_V_DIM), dv.reshape(Bp, Tp, DA_HEADS, DA_V_DIM),
                mk_p.reshape(Bp, n_mem, MEM_HEADS, d // MEM_HEADS), mv_p.reshape(Bp, n_mem, MEM_HEADS, d // MEM_HEADS),
                C_p, n_p.reshape(Bp, H, D), m_p.reshape(Bp, H), cv_p,
                dk_s.reshape(Bs, Ts, DA_HEADS, DA_V_DIM), dv_s.reshape(Bs, Ts, DA_HEADS, DA_V_DIM),
                C_s, n_s.reshape(Bs, H, D), m_s.reshape(Bs, H), cv_s)
        for acc, val in zip(outs, vals):
            acc.append(val)
    return (yp.reshape(Bp, Tp, d), ys.reshape(Bs, Ts, d)) + tuple(jnp.stack(a) for a in outs)
```

```python
import functools
import math

import jax
import jax.numpy as jnp
from jax import lax
from jax.experimental import pallas as pl
from jax.experimental.pallas import tpu as pltpu

F32 = jnp.float32
BF16 = jnp.bfloat16

ML_HEADS = 4
ML_HEAD_DIM = 128
ML_WIDTH = ML_HEADS * ML_HEAD_DIM
DA_HEADS = 4
DA_V_DIM = 128
DA_QK_DIM = 64
DA_WIDTH = DA_HEADS * DA_V_DIM
ROPE_DIM = 16
ROPE_THETA = 500000.0
MEM_HEADS = 4
CONV_W = 3
RMS_EPS = 1e-6
LANES = 128
NEG_BIG = -1e30
VMEM_LIMIT = 56 * 1024 * 1024


def _params(*sem, vmem=VMEM_LIMIT):
    return pltpu.CompilerParams(dimension_semantics=sem, vmem_limit_bytes=vmem)


def _resident(a):
    return pl.BlockSpec(a.shape, lambda *_: (0,) * a.ndim, pipeline_mode=pl.Buffered(1))


def _rms(x, g):
    return x * lax.rsqrt(jnp.mean(x * x, axis=-1, keepdims=True) + RMS_EPS) * g


def _log_sigmoid(x):
    return jnp.minimum(x, 0.0) - jnp.log1p(jnp.exp(-jnp.abs(x)))


def _dot(a, b):
    return jnp.dot(a, b, preferred_element_type=F32)


def _dot_nt(a, b):
    return lax.dot_general(a, b, (((1,), (1,)), ((), ())), preferred_element_type=F32)


def _dot_tn(a, b):
    return lax.dot_general(a, b, (((0,), (0,)), ((), ())), preferred_element_type=F32)


def _rope_table_kernel(cos_ref, sa_ref, sb_ref, *, period, offset):
    rows = cos_ref.shape[0]
    half = ROPE_DIM // 2
    r = lax.broadcasted_iota(jnp.int32, (rows, LANES), 0) + pl.program_id(0) * rows
    lane = lax.broadcasted_iota(jnp.int32, (rows, LANES), 1)
    pos = (offset + r % period).astype(F32)
    c = lane % DA_QK_DIM
    j = (c % half).astype(F32)
    inv = jnp.exp(-math.log(ROPE_THETA) * (2.0 * j / ROPE_DIM))
    ang = pos * inv
    cos, sin = jnp.cos(ang), jnp.sin(ang)
    cos_ref[...] = jnp.where(c < ROPE_DIM, cos, 1.0)
    sa_ref[...] = jnp.where(c < half, -sin, 0.0)
    sb_ref[...] = jnp.where((c >= half) & (c < ROPE_DIM), sin, 0.0)


def _rope_tables(rows, period, offset):
    blk = min(rows, 512)
    spec = pl.BlockSpec((blk, LANES), lambda i: (i, 0))
    shp = jax.ShapeDtypeStruct((rows, LANES), F32)
    return pl.pallas_call(
        functools.partial(_rope_table_kernel, period=period, offset=offset),
        grid=(rows // blk,), out_specs=[spec] * 3, out_shape=[shp] * 3,
        compiler_params=_params("parallel"), name="rope_tables")()


def _rope(x, cos, sa, sb):
    outs = []
    for j in range(x.shape[1] // LANES):
        xj = x[:, j * LANES:(j + 1) * LANES]
        up = pltpu.roll(xj, LANES - ROPE_DIM // 2, axis=1)
        dn = pltpu.roll(xj, ROPE_DIM // 2, axis=1)
        outs.append(xj * cos + up * sa + dn * sb)
    return outs


def _in_proj_kernel(x_ref, g_ref, w_ref, wda_ref, wg_ref, wgt_ref, bcol_ref, brow_ref, cos_ref, sa_ref, sb_ref,
                    q_ref, k_ref, v_ref, o_ref, gc_ref, gr_ref, dq_ref, dk_ref, dv_ref, dkb_ref, dvb_ref,
                    *, dq_scale):
    h = _rms(x_ref[...], g_ref[...]).astype(BF16)
    for j, ref in enumerate((q_ref, k_ref, v_ref, o_ref)):
        ref[...] = _dot(h, w_ref[:, j * ML_WIDTH:(j + 1) * ML_WIDTH]).astype(ref.dtype)
    gc = _dot(h, wg_ref[...]) + bcol_ref[...]
    lane = lax.broadcasted_iota(jnp.int32, gc.shape, 1)
    gc_ref[...] = jnp.where(lane < ML_HEADS, gc, _log_sigmoid(gc))
    gr = _dot_nt(wgt_ref[...], h) + brow_ref[...]
    row = lax.broadcasted_iota(jnp.int32, gr.shape, 0)
    gr_ref[...] = jnp.where(row < ML_HEADS, gr, _log_sigmoid(gr))
    cos, sa, sb = cos_ref[...], sa_ref[...], sb_ref[...]
    dq = _rope(_dot(h, wda_ref[:, :DA_WIDTH]), cos, sa, sb)
    dk = _rope(_dot(h, wda_ref[:, DA_WIDTH:2 * DA_WIDTH]), cos, sa, sb)
    dv = _dot(h, wda_ref[:, 2 * DA_WIDTH:3 * DA_WIDTH])
    dvb_ref[...] = dv.astype(BF16)
    for j in range(DA_HEADS):
        cols = slice(j * LANES, (j + 1) * LANES)
        dq_ref[:, cols] = (dq[j] * dq_scale).astype(dq_ref.dtype)
        dk_ref[:, j, :] = dk[j]
        dkb_ref[:, cols] = dk[j].astype(BF16)
        dv_ref[:, j, :] = dv[:, cols]


def _in_proj(x, g, w_ml, w_da, w_g, w_gt, b_col, b_row, tables, act_dtype, tm, dq_scale):
    rows, d = x.shape
    cos, sa, sb = tables
    nt = cos.shape[0] // tm
    row_spec = lambda w: pl.BlockSpec((tm, w), lambda i: (i, 0))
    full = lambda a: pl.BlockSpec(a.shape, lambda i: (0,) * a.ndim)
    tab_spec = pl.BlockSpec((tm, LANES), lambda i: (i % nt, 0))
    out_shape = [jax.ShapeDtypeStruct((rows, ML_WIDTH), act_dtype)] * 4 + [
        jax.ShapeDtypeStruct((rows, LANES), F32), jax.ShapeDtypeStruct((8, rows), F32),
        jax.ShapeDtypeStruct((rows, DA_WIDTH), act_dtype),
        jax.ShapeDtypeStruct((rows, DA_HEADS, DA_V_DIM), F32), jax.ShapeDtypeStruct((rows, DA_HEADS, DA_V_DIM), F32),
        jax.ShapeDtypeStruct((rows, DA_WIDTH), BF16), jax.ShapeDtypeStruct((rows, DA_WIDTH), BF16)]
    kv_spec = pl.BlockSpec((tm, DA_HEADS, DA_V_DIM), lambda i: (i, 0, 0))
    out_specs = [row_spec(ML_WIDTH)] * 4 + [row_spec(LANES), pl.BlockSpec((8, tm), lambda i: (0, i)),
                                            row_spec(DA_WIDTH), kv_spec, kv_spec,
                                            row_spec(DA_WIDTH), row_spec(DA_WIDTH)]
    return pl.pallas_call(
        functools.partial(_in_proj_kernel, dq_scale=dq_scale), grid=(rows // tm,),
        in_specs=[row_spec(d), full(g), full(w_ml), full(w_da), full(w_g), full(w_gt), full(b_col), full(b_row),
                  tab_spec, tab_spec, tab_spec],
        out_specs=out_specs, out_shape=out_shape,
        compiler_params=_params("parallel"), name="in_proj")(
            x, g, w_ml, w_da, w_g, w_gt, b_col, b_row, cos, sa, sb)


def _run_phases(*gens):
    live = list(gens)
    while live:
        for g in list(live):
            try:
                next(g)
            except StopIteration:
                live.remove(g)


def _mlstm_heads(heads):
    out = []
    _run_phases(_mlstm_phases(heads, out))
    return out


def _mlstm_phases(heads, out):
    L, D = heads[0][0].shape
    scale = D ** -0.5
    r = lax.broadcasted_iota(jnp.int32, (L, L), 0)
    c = lax.broadcasted_iota(jnp.int32, (L, L), 1)
    tri = c <= r
    gate = []
    for q, k, v, i_col, f_col, i_row, f_row, C, n, m in heads:
        b_col = jnp.sum(jnp.where(tri, f_row, 0.0), axis=1, keepdims=True)
        b_row = jnp.sum(jnp.where(r <= c, f_col, 0.0), axis=0, keepdims=True)
        log_d = jnp.where(tri, b_col - b_row + i_row, -jnp.inf)
        inter = b_col + m
        m_row = jnp.maximum(jnp.max(log_d, axis=1, keepdims=True), inter)
        b_last = b_col[L - 1:L, :]
        log_w = b_last - b_col + i_col
        m_new = jnp.maximum(b_last + m, jnp.max(log_w, axis=0, keepdims=True))
        gate.append(dict(
            m_row=m_row, w_inter=jnp.exp(inter - m_row), d=scale * jnp.exp(log_d - m_row), m_new=m_new,
            decay=jnp.exp(b_last + m - m_new), wk=(scale * jnp.exp(log_w - m_new)) * k.astype(F32)))
    yield
    qb = [h[0].astype(BF16) for h in heads]
    vb = [h[2].astype(BF16) for h in heads]
    qk = [_dot_nt(qb[j], heads[j][1].astype(BF16)) for j in range(len(heads))]
    qc = [_dot(qb[j], heads[j][7].astype(BF16)) for j in range(len(heads))]
    kv = [_dot_tn(gate[j]["wk"].astype(BF16), vb[j]) for j in range(len(heads))]
    yield
    s = [qk[j] * gate[j]["d"] for j in range(len(heads))]
    yield
    sv = [_dot(s[j].astype(BF16), vb[j]) for j in range(len(heads))]
    yield
    for j, (q, k, v, i_col, f_col, i_row, f_row, C, n, m) in enumerate(heads):
        g = gate[j]
        num = g["w_inter"] * qc[j] + sv[j]
        den = (g["w_inter"] * jnp.sum(q.astype(F32) * n, axis=1, keepdims=True)
               + jnp.sum(s[j], axis=1, keepdims=True))
        h = num / jnp.maximum(jnp.abs(den), jnp.exp(-g["m_row"]))
        c_new = g["decay"] * C + kv[j]
        n_new = g["decay"] * n + jnp.sum(g["wk"], axis=0, keepdims=True)
        out.append((h, c_new, n_new, g["m_new"]))


def _ml_head_out(h, o, g):
    return _rms(h, g) * jax.nn.sigmoid(o.astype(F32))


def _mlstm_prompt_phases(first_chunk, q_ref, k_ref, v_ref, o_ref, gc_ref, gr_ref, gh_ref,
                         h_ref, c_out, n_out, m_out, c_scr, n_scr, m_scr):
    gc, gr = gc_ref[...], gr_ref[...]
    cols = [slice(hh * ML_HEAD_DIM, (hh + 1) * ML_HEAD_DIM) for hh in range(ML_HEADS)]
    prev = lambda ref, hh: jnp.where(first_chunk, 0.0, ref[hh])
    res = []
    yield from _mlstm_phases([
        (q_ref[:, cols[hh]], k_ref[:, cols[hh]], v_ref[:, cols[hh]],
         gc[:, hh:hh + 1], gc[:, ML_HEADS + hh:ML_HEADS + hh + 1],
         gr[hh:hh + 1, :], gr[ML_HEADS + hh:ML_HEADS + hh + 1, :],
         prev(c_scr, hh), prev(n_scr, hh), prev(m_scr, hh)) for hh in range(ML_HEADS)], res)
    for hh, (h, c_new, n_new, m_new) in enumerate(res):
        c_scr[hh], n_scr[hh], m_scr[hh] = c_new, n_new, m_new
        c_out[0, hh], n_out[0, hh], m_out[0, hh] = c_new, n_new, m_new
        h_ref[:, cols[hh]] = _ml_head_out(h, o_ref[:, cols[hh]], gh_ref[hh:hh + 1, :]).astype(h_ref.dtype)


def _mlstm_sample_kernel(q_ref, k_ref, v_ref, o_ref, gc_ref, gr_ref, gh_ref, c_in, n_in, m_in,
                         h_ref, c_out, n_out, m_out):
    cols = [slice(hh * ML_HEAD_DIM, (hh + 1) * ML_HEAD_DIM) for hh in range(ML_HEADS)]
    per_trip = 2 if q_ref.shape[0] % 2 == 0 else 1

    def body(it, carry):
        elems = [it * per_trip + u for u in range(per_trip)]
        gates = [(gc_ref[bi], gr_ref[bi]) for bi in elems]
        items = [(bi, hh) for bi in elems for hh in range(ML_HEADS)]
        res = _mlstm_heads([
            (q_ref[bi, :, cols[hh]], k_ref[bi, :, cols[hh]], v_ref[bi, :, cols[hh]],
             gc[:, hh:hh + 1], gc[:, ML_HEADS + hh:ML_HEADS + hh + 1],
             gr[hh:hh + 1, :], gr[ML_HEADS + hh:ML_HEADS + hh + 1, :],
             c_in[bi, hh], n_in[bi, hh], m_in[bi, hh])
            for bi, (gc, gr) in zip(elems, gates) for hh in range(ML_HEADS)])
        for (bi, hh), (h, c_new, n_new, m_new) in zip(items, res):
            c_out[bi, hh], n_out[bi, hh], m_out[bi, hh] = c_new, n_new, m_new
            h_ref[bi, :, cols[hh]] = _ml_head_out(
                h, o_ref[bi, :, cols[hh]], gh_ref[hh:hh + 1, :]).astype(h_ref.dtype)
        return carry

    lax.fori_loop(0, q_ref.shape[0] // per_trip, body, 0)


def _mlstm_sample(q, k, v, o, gc, gr, g_head, c0, n0, m0, bb):
    B, T, _ = q.shape
    H, D = ML_HEADS, ML_HEAD_DIM
    b3 = lambda s: pl.BlockSpec((bb,) + s, lambda i: (i, 0, 0))
    b4 = lambda s: pl.BlockSpec((bb,) + s, lambda i: (i, 0, 0, 0))
    return pl.pallas_call(
        _mlstm_sample_kernel, grid=(B // bb,),
        in_specs=[b3((T, ML_WIDTH))] * 4 + [b3((T, LANES)), b3((8, T)),
                                            pl.BlockSpec(g_head.shape, lambda i: (0, 0)),
                                            b4((H, D, D)), b4((H, 1, D)), b4((H, 1, 1))],
        out_specs=[b3((T, ML_WIDTH)), b4((H, D, D)), b4((H, 1, D)), b4((H, 1, 1))],
        out_shape=[jax.ShapeDtypeStruct((B, T, ML_WIDTH), BF16),
                   jax.ShapeDtypeStruct((B, H, D, D), F32),
                   jax.ShapeDtypeStruct((B, H, 1, D), F32),
                   jax.ShapeDtypeStruct((B, H, 1, 1), F32)],
        compiler_params=_params("parallel"), name="mlstm_sample")(q, k, v, o, gc, gr, g_head, c0, n0, m0)


def _da_lambda(lam_ref, lam_init):
    lv = lam_ref[...]
    a = jnp.sum(lv[0:1, :] * lv[1:2, :], axis=1, keepdims=True)
    b = jnp.sum(lv[2:3, :] * lv[3:4, :], axis=1, keepdims=True)
    return jnp.exp(a) - jnp.exp(b) + lam_init


def _stack_components(q):
    lane = lax.broadcasted_iota(jnp.int32, q.shape, 1)
    zero = jnp.zeros_like(q)
    return jnp.concatenate([jnp.where(lane < DA_QK_DIM, q, zero), jnp.where(lane >= DA_QK_DIM, q, zero)], axis=0)


def _da_prompt_kernel(q_ref, k_ref, v_ref, lam_ref, gh_ref, out_ref, vt_scr, s_a, s_b, m_scr, l_scr, acc_scr,
                      *, lam_init):
    i = pl.program_id(2)
    tq = q_ref.shape[0]
    nblk = vt_scr.shape[1]
    heads = range(vt_scr.shape[0])
    hcols = [slice(hd * DA_V_DIM, (hd + 1) * DA_V_DIM) for hd in heads]

    @pl.when(i == 0)
    def _():
        for hd in heads:
            for j in range(nblk):
                vt_scr[hd, j] = v_ref[j * tq:(j + 1) * tq, hcols[hd]].astype(F32).T.astype(BF16)

    qqt = []
    for hd in heads:
        qt = q_ref[:, hcols[hd]].astype(F32).T
        dim = lax.broadcasted_iota(jnp.int32, qt.shape, 0)
        qqt.append(jnp.concatenate(
            [jnp.where(dim < DA_QK_DIM, qt, 0.0), jnp.where(dim >= DA_QK_DIM, qt, 0.0)],
            axis=1).astype(BF16))

    def scores(j, s_ref):
        off = pl.multiple_of(j * tq, tq)
        for hd in heads:
            s_ref[hd] = _dot(k_ref[pl.ds(off, tq), hcols[hd]], qqt[hd])

    def accumulate(j, s_ref, masked):
        stats = []
        for hd in heads:
            st = s_ref[hd]
            if masked:
                key = lax.broadcasted_iota(jnp.int32, st.shape, 0)
                query = lax.broadcasted_iota(jnp.int32, st.shape, 1) % tq
                st = jnp.where(key <= query, st, NEG_BIG)
            m_old = m_scr[hd]
            m_new = jnp.maximum(m_old, jnp.max(st, axis=0, keepdims=True))
            alpha = jnp.exp2(m_old - m_new)
            p = jnp.exp2(st - m_new)
            l_scr[hd] = alpha * l_scr[hd] + jnp.sum(p, axis=0, keepdims=True)
            m_scr[hd] = m_new
            stats.append((alpha, p.astype(BF16)))
        for hd, (alpha, p) in zip(heads, stats):
            acc_scr[hd] = alpha * acc_scr[hd] + _dot(vt_scr[hd, j], p)

    m_scr[...] = jnp.full_like(m_scr, NEG_BIG)
    l_scr[...] = jnp.zeros_like(l_scr)
    acc_scr[...] = jnp.zeros_like(acc_scr)
    scores(0, s_a)

    def body(t, carry):
        scores(2 * t + 1, s_b)
        accumulate(2 * t, s_a, False)
        scores(2 * t + 2, s_a)
        accumulate(2 * t + 1, s_b, False)
        return carry

    lax.fori_loop(0, i // 2, body, 0)

    @pl.when(i % 2 == 1)
    def _():
        scores(i, s_b)
        accumulate(i - 1, s_a, False)
        accumulate(i, s_b, True)

    @pl.when(i % 2 == 0)
    def _():
        accumulate(i, s_a, True)

    lam = _da_lambda(lam_ref, lam_init)
    for hd in heads:
        ot = acc_scr[hd] / l_scr[hd]
        at = ot[:, :tq] - lam * ot[:, tq:]
        norm = at * lax.rsqrt(jnp.mean(at * at, axis=0, keepdims=True) + RMS_EPS)
        out_ref[:, hcols[hd]] = ((norm.T * gh_ref[hd]) * (1.0 - lam_init)).astype(out_ref.dtype)


def _da_prompt(dq, dk, dv, da_lambda, g_head3, B, T, tq, hp, lam_init):
    nq = T // tq
    rows = B * T
    w = hp * DA_V_DIM
    kv_spec = pl.BlockSpec((T, w), lambda b, h, i: (b, h))
    return pl.pallas_call(
        functools.partial(_da_prompt_kernel, lam_init=lam_init), grid=(B, DA_HEADS // hp, nq),
        in_specs=[pl.BlockSpec((tq, w), lambda b, h, i: (b * nq + i, h)), kv_spec, kv_spec,
                  pl.BlockSpec(da_lambda.shape, lambda b, h, i: (0, 0)),
                  pl.BlockSpec((hp, 1, DA_V_DIM), lambda b, h, i: (h, 0, 0))],
        out_specs=pl.BlockSpec((tq, w), lambda b, h, i: (b * nq + i, h)),
        out_shape=jax.ShapeDtypeStruct((rows, DA_WIDTH), BF16),
        scratch_shapes=[pltpu.VMEM((hp, nq, DA_V_DIM, tq), BF16),
                        pltpu.VMEM((hp, tq, 2 * tq), F32), pltpu.VMEM((hp, tq, 2 * tq), F32),
                        pltpu.VMEM((hp, 1, 2 * tq), F32), pltpu.VMEM((hp, 1, 2 * tq), F32),
                        pltpu.VMEM((hp, DA_V_DIM, 2 * tq), F32)],
        compiler_params=_params("parallel", "parallel", "arbitrary"), name="da_prompt")(
            dq, dk, dv, da_lambda, g_head3)


def _paged_fetch(pt_ref, ck_hbm, cv_hbm, kbuf, vbuf, sem, layer):
    b = pl.program_id(0)
    n_pages = kbuf.shape[1]

    def copies(bi, slot):
        out = []
        for p in range(n_pages):
            pg = pt_ref[bi, p]
            out.append(pltpu.make_async_copy(ck_hbm.at[layer, pg], kbuf.at[slot, p], sem.at[slot, 0]))
            out.append(pltpu.make_async_copy(cv_hbm.at[layer, pg], vbuf.at[slot, p], sem.at[slot, 1]))
        return out

    @pl.when(b == 0)
    def _():
        for cp in copies(0, 0):
            cp.start()

    @pl.when(b + 1 < pl.num_programs(0))
    def _():
        for cp in copies(b + 1, (b + 1) % 2):
            cp.start()

    slot = b % 2
    for cp in copies(b, slot):
        cp.wait()
    return slot


def _da_sample_phases(slot, q_ref, kn_ref, vn_ref, lam_ref, gh_ref, out_ref, kbuf, vbuf, lam_init):
    n_pages, page = kbuf.shape[1], kbuf.shape[2] // DA_HEADS
    T = q_ref.shape[1]
    scale = DA_QK_DIM ** -0.5
    lam = _da_lambda(lam_ref, lam_init)
    q_all = q_ref[0]
    n_rows = n_pages * page * DA_HEADS
    qq = jnp.concatenate([_stack_components(q_all[:, hh * DA_V_DIM:(hh + 1) * DA_V_DIM])
                          for hh in range(DA_HEADS)], axis=0)
    k_all = kbuf[slot].reshape(n_rows, DA_V_DIM).astype(BF16)
    s_raw = _dot_nt(qq.astype(BF16), k_all)
    yield
    rq = 2 * T * DA_HEADS
    own_head = (lax.broadcasted_iota(jnp.int32, (rq, n_rows), 1) % DA_HEADS
                == lax.broadcasted_iota(jnp.int32, (rq, n_rows), 0) // (2 * T))
    s_past = jnp.where(own_head, s_raw * scale, NEG_BIG)
    trow = lax.broadcasted_iota(jnp.int32, (rq, 1), 0) % T

    def per_query_row(new_ref, t):
        x = new_ref[0, t]
        return jnp.concatenate([jnp.broadcast_to(x[hh:hh + 1, :], (2 * T, DA_V_DIM))
                                for hh in range(DA_HEADS)], axis=0)

    s_new = [jnp.where(trow >= t,
                       jnp.sum(qq * per_query_row(kn_ref, t), axis=1, keepdims=True) * scale, NEG_BIG)
             for t in range(T)]
    m = jnp.max(s_past, axis=1, keepdims=True)
    for t in range(T):
        m = jnp.maximum(m, s_new[t])
    p_past = jnp.exp(s_past - m)
    l = jnp.sum(p_past, axis=1, keepdims=True)
    v_all = vbuf[slot].reshape(n_rows, DA_V_DIM).astype(BF16)
    yield
    acc = _dot(p_past.astype(BF16), v_all)
    yield
    for t in range(T):
        p_t = jnp.exp(s_new[t] - m)
        l = l + p_t
        acc = acc + p_t * per_query_row(vn_ref, t)
    o = acc / l
    for hh in range(DA_HEADS):
        r0 = hh * 2 * T
        a = o[r0:r0 + T] - lam * o[r0 + T:r0 + 2 * T]
        out_ref[0, :, hh * DA_V_DIM:(hh + 1) * DA_V_DIM] = (
            _rms(a, gh_ref[hh:hh + 1, :]) * (1.0 - lam_init)).astype(out_ref.dtype)


def _da_sample_mlstm_prompt_kernel(
        pt_ref, q_ref, kn_ref, vn_ref, lam_ref, gh_ref, ck_hbm, cv_hbm,
        mq_ref, mk_ref, mv_ref, mo_ref, gc_ref, gr_ref, mgh_ref,
        out_ref, h_ref, c_out, n_out, m_out,
        kbuf, vbuf, sem, c_scr, n_scr, m_scr, *, lam_init, layer, chunks):
    i = pl.program_id(0)

    @pl.when(i == 0)
    def _():
        c_scr[...] = jnp.zeros_like(c_scr)
        n_scr[...] = jnp.zeros_like(n_scr)
        m_scr[...] = jnp.zeros_like(m_scr)

    slot = _paged_fetch(pt_ref, ck_hbm, cv_hbm, kbuf, vbuf, sem, layer)
    _run_phases(
        _da_sample_phases(slot, q_ref, kn_ref, vn_ref, lam_ref, gh_ref, out_ref, kbuf, vbuf, lam_init),
        _mlstm_prompt_phases(i % chunks == 0, mq_ref, mk_ref, mv_ref, mo_ref, gc_ref, gr_ref, mgh_ref,
                             h_ref, c_out, n_out, m_out, c_scr, n_scr, m_scr))


def _da_sample_mlstm_prompt(page_table, dq, dk, dv, da_lambda, g_da, cache_k, cache_v, lam_init, layer,
                            mq, mk, mv, mo, gc, gr, g_ml, Bp, Tp):
    B, T, _ = dq.shape
    rows = Bp * Tp
    assert rows % B == 0 and Tp % (rows // B) == 0, "one mLSTM chunk per sample batch element"
    L = rows // B
    chunks = Tp // L
    H, D = ML_HEADS, ML_HEAD_DIM
    n_pages = page_table.shape[1]
    page_rows = cache_k.shape[2]
    blk = pl.BlockSpec((1, T, DA_WIDTH), lambda i, pt: (i, 0, 0))
    kv_blk = pl.BlockSpec((1, T, DA_HEADS, DA_V_DIM), lambda i, pt: (i, 0, 0, 0))
    const = lambda a: pl.BlockSpec(a.shape, lambda i, pt: (0,) * a.ndim)
    mblk = lambda w: pl.BlockSpec((L, w), lambda i, pt: (i, 0))
    state = lambda *s: pl.BlockSpec((1, H) + s, lambda i, pt: (i // chunks, 0, 0, 0))
    grid_spec = pltpu.PrefetchScalarGridSpec(
        num_scalar_prefetch=1, grid=(B,),
        in_specs=[blk, kv_blk, kv_blk, const(da_lambda), const(g_da),
                  pl.BlockSpec(memory_space=pl.ANY), pl.BlockSpec(memory_space=pl.ANY),
                  mblk(ML_WIDTH), mblk(ML_WIDTH), mblk(ML_WIDTH), mblk(ML_WIDTH), mblk(LANES),
                  pl.BlockSpec((8, L), lambda i, pt: (0, i)), const(g_ml)],
        out_specs=[blk, mblk(ML_WIDTH), state(D, D), state(1, D), state(1, 1)],
        scratch_shapes=[pltpu.VMEM((2, n_pages, page_rows, DA_V_DIM), F32),
                        pltpu.VMEM((2, n_pages, page_rows, DA_V_DIM), F32),
                        pltpu.SemaphoreType.DMA((2, 2)),
                        pltpu.VMEM((H, D, D), F32), pltpu.VMEM((H, 1, D), F32), pltpu.VMEM((H, 1, 1), F32)])
    return pl.pallas_call(
        functools.partial(_da_sample_mlstm_prompt_kernel, lam_init=lam_init, layer=layer, chunks=chunks),
        grid_spec=grid_spec,
        out_shape=[jax.ShapeDtypeStruct((B, T, DA_WIDTH), BF16),
                   jax.ShapeDtypeStruct((rows, ML_WIDTH), BF16),
                   jax.ShapeDtypeStruct((Bp, H, D, D), F32),
                   jax.ShapeDtypeStruct((Bp, H, 1, D), F32),
                   jax.ShapeDtypeStruct((Bp, H, 1, 1), F32)],
        compiler_params=_params("arbitrary"), name="da_sample_mlstm_prompt")(
            page_table, dq, dk, dv, da_lambda, g_da, cache_k, cache_v, mq, mk, mv, mo, gc, gr, g_ml)


def _proj_norm_kernel(*refs, n_in, has_next):
    a_refs, w_refs = refs[:n_in], refs[n_in:2 * n_in]
    x_ref, gpost_ref, gpre_ref = refs[2 * n_in:2 * n_in + 3]
    rest = refs[2 * n_in + 3:]
    wn_ref = rest[0] if has_next else None
    xo_ref, ho_ref = rest[-2:]
    acc = _dot(a_refs[0][...], w_refs[0][...])
    for a, w in zip(a_refs[1:], w_refs[1:]):
        acc = acc + _dot(a[...], w[...])
    x1 = x_ref[...] + _rms(acc, gpost_ref[...])
    xo_ref[...] = x1
    hn = _rms(x1, gpre_ref[...]).astype(BF16)
    ho_ref[...] = (_dot(hn, wn_ref[...]) if has_next else hn).astype(ho_ref.dtype)


def _proj_norm(a_list, w_list, x, g_post, g_pre, w_next, tm, h_dtype=BF16):
    rows, d = x.shape
    n_in = len(a_list)
    has_next = w_next is not None
    row_spec = lambda w: pl.BlockSpec((tm, w), lambda i: (i, 0))
    full = lambda a: pl.BlockSpec(a.shape, lambda i: (0,) * a.ndim)
    ins = list(a_list) + list(w_list) + [x, g_post, g_pre] + ([w_next] if has_next else [])
    in_specs = ([row_spec(a.shape[1]) for a in a_list] + [full(w) for w in w_list]
                + [row_spec(d), full(g_post), full(g_pre)] + ([full(w_next)] if has_next else []))
    n_out = w_next.shape[1] if has_next else d
    return pl.pallas_call(
        functools.partial(_proj_norm_kernel, n_in=n_in, has_next=has_next), grid=(rows // tm,),
        in_specs=in_specs, out_specs=[row_spec(d), row_spec(n_out)],
        out_shape=[jax.ShapeDtypeStruct((rows, d), F32), jax.ShapeDtypeStruct((rows, n_out), h_dtype)],
        compiler_params=_params("parallel"), name="proj_norm")(*ins)


def _mem_kv_kernel(x_ref, g_ref, wk_ref, wv_ref, k_ref, v_ref):
    h = _rms(x_ref[...], g_ref[...]).astype(BF16)
    k_ref[...] = _dot(h, wk_ref[...])
    v_ref[...] = _dot(h, wv_ref[...])


def _mem_kv(x, g, wk, wv, tm):
    rows, d = x.shape
    n = wk.shape[1]
    row_spec = lambda w: pl.BlockSpec((tm, w), lambda i: (i, 0))
    return pl.pallas_call(
        _mem_kv_kernel, grid=(rows // tm,),
        in_specs=[row_spec(d), _resident(g), _resident(wk), _resident(wv)],
        out_specs=[row_spec(n), row_spec(n)],
        out_shape=[jax.ShapeDtypeStruct((rows, n), F32)] * 2,
        compiler_params=_params("parallel"), name="mem_kv")(x, g, wk, wv)


def _mem_attn_heads(qkv):
    s = [_dot_nt(q.astype(BF16), k.astype(BF16)) * (q.shape[1] ** -0.5) for q, k, _ in qkv]
    p = [jnp.exp(x - jnp.max(x, axis=1, keepdims=True)) for x in s]
    pv = [_dot(x.astype(BF16), v.astype(BF16)) for x, (_, _, v) in zip(p, qkv)]
    return [o / jnp.sum(x, axis=1, keepdims=True) for o, x in zip(pv, p)]


def _mem_attn_kernel(q_ref, mk_ref, mv_ref, o_ref):
    hd = q_ref.shape[2] // MEM_HEADS
    items = [(bi, slice(hh * hd, (hh + 1) * hd)) for bi in range(q_ref.shape[0]) for hh in range(MEM_HEADS)]
    outs = _mem_attn_heads([(q_ref[bi, :, cols], mk_ref[bi, :, cols], mv_ref[bi, :, cols]) for bi, cols in items])
    for (bi, cols), o in zip(items, outs):
        o_ref[bi, :, cols] = o.astype(o_ref.dtype)


def _mem_attn(q, mk, mv, bb, tm):
    B, T, d = q.shape
    kv_spec = pl.BlockSpec((bb,) + mk.shape[1:], lambda b, i: (b, 0, 0))
    return pl.pallas_call(
        _mem_attn_kernel, grid=(B // bb, T // tm),
        in_specs=[pl.BlockSpec((bb, tm, d), lambda b, i: (b, i, 0)), kv_spec, kv_spec],
        out_specs=pl.BlockSpec((bb, tm, d), lambda b, i: (b, i, 0)),
        out_shape=jax.ShapeDtypeStruct((B, T, d), BF16),
        compiler_params=_params("parallel", "parallel"), name="mem_attn")(q, mk, mv)


def _mem_attn_cache_kernel(q_ref, mk_hbm, mv_hbm, o_ref, kbuf, vbuf, sem, *, layer):
    g = pl.program_id(0)
    bb = q_ref.shape[0]
    hd = kbuf.shape[-1]

    def copies(gi, slot):
        out = []
        for bi in range(bb):
            for hh in range(MEM_HEADS):
                b = gi * bb + bi
                out.append(pltpu.make_async_copy(mk_hbm.at[layer, b, :, hh, :], kbuf.at[slot, bi, hh],
                                                 sem.at[slot, 0]))
                out.append(pltpu.make_async_copy(mv_hbm.at[layer, b, :, hh, :], vbuf.at[slot, bi, hh],
                                                 sem.at[slot, 1]))
        return out

    @pl.when(g == 0)
    def _():
        for cp in copies(0, 0):
            cp.start()

    @pl.when(g + 1 < pl.num_programs(0))
    def _():
        for cp in copies(g + 1, (g + 1) % 2):
            cp.start()

    slot = g % 2
    for cp in copies(g, slot):
        cp.wait()
    for bi in range(bb):
        outs = _mem_attn_heads([(q_ref[bi, :, hh * hd:(hh + 1) * hd], kbuf[slot, bi, hh], vbuf[slot, bi, hh])
                                for hh in range(MEM_HEADS)])
        for hh, o in enumerate(outs):
            o_ref[bi, :, hh * hd:(hh + 1) * hd] = o.astype(o_ref.dtype)


def _mem_attn_cache(q, mk, mv, layer, bb):
    B, T, d = q.shape
    M, H, hd = mk.shape[2:]
    return pl.pallas_call(
        functools.partial(_mem_attn_cache_kernel, layer=layer), grid=(B // bb,),
        in_specs=[pl.BlockSpec((bb, T, d), lambda g: (g, 0, 0)),
                  pl.BlockSpec(memory_space=pl.ANY), pl.BlockSpec(memory_space=pl.ANY)],
        out_specs=pl.BlockSpec((bb, T, d), lambda g: (g, 0, 0)),
        out_shape=jax.ShapeDtypeStruct((B, T, d), BF16),
        scratch_shapes=[pltpu.VMEM((2, bb, H, M, hd), F32), pltpu.VMEM((2, bb, H, M, hd), F32),
                        pltpu.SemaphoreType.DMA((2, 2))],
        compiler_params=_params("arbitrary"), name="mem_attn_cache")(q, mk, mv)


def _ffn_chunks(nh):
    nchunk = 2 if (nh // LANES) % 2 == 0 else 1
    cw = nh // nchunk
    return [(j * cw, cw) for j in range(nchunk)]


def _ffn_prompt_kernel(hf_ref, x_ref, wup_ref, wdw_ref, bdw_ref, wdn_ref, gpost_ref, y_ref, ulast_ref, ubuf,
                       *, tiles_per_seq):
    i = pl.program_id(0)
    tm = hf_ref.shape[0]
    nh = wdn_ref.shape[0]
    halo = ubuf.shape[0] - tm

    @pl.when(i % tiles_per_seq == 0)
    def _():
        ubuf[0:halo, :] = jnp.zeros((halo, ubuf.shape[1]), F32)

    hf = hf_ref[...]
    f = jnp.zeros((tm, y_ref.shape[1]), F32)
    for c0, cw in _ffn_chunks(nh):
        cg = []
        for base in (c0, nh + c0):
            cs = slice(base, base + cw)
            ubuf[halo:halo + tm, cs] = _dot(hf, wup_ref[:, cs])
            c = bdw_ref[:, cs]
            for j in range(CONV_W):
                lo = halo - (CONV_W - 1) + j
                c = c + ubuf[lo:lo + tm, cs] * wdw_ref[j:j + 1, cs]
            cg.append(c)
        act = (jax.nn.silu(cg[1]) * cg[0]).astype(BF16)
        f = f + _dot(act, wdn_ref[c0:c0 + cw, :])
    y_ref[...] = x_ref[...] + _rms(f, gpost_ref[...])
    tail = ubuf[tm:tm + halo, :]
    ubuf[0:halo, :] = tail
    ulast_ref[0] = tail


def _ffn_prompt(hf, x, w_up, w_dw, b_dw, w_down, g_post, B, T, tm):
    rows, d = x.shape
    npad = w_up.shape[1]
    tiles = T // tm
    halo = 8
    row_spec = lambda w: pl.BlockSpec((tm, w), lambda i: (i, 0))
    full = _resident
    return pl.pallas_call(
        functools.partial(_ffn_prompt_kernel, tiles_per_seq=tiles), grid=(rows // tm,),
        in_specs=[row_spec(d), row_spec(d), full(w_up), full(w_dw), full(b_dw), full(w_down), full(g_post)],
        out_specs=[row_spec(d), pl.BlockSpec((1, halo, npad), lambda i: (i // tiles, 0, 0))],
        out_shape=[jax.ShapeDtypeStruct((rows, d), F32), jax.ShapeDtypeStruct((B, halo, npad), F32)],
        scratch_shapes=[pltpu.VMEM((tm + halo, npad), F32)],
        compiler_params=_params("arbitrary"), name="ffn_prompt")(hf, x, w_up, w_dw, b_dw, w_down, g_post)


def _ffn_sample_kernel(hf_ref, x_ref, cb_ref, wup_ref, wdw_ref, bdw_ref, wdn_ref, gpost_ref, y_ref, unew_ref,
                       *, T):
    nb = hf_ref.shape[0] // T
    nh = wdn_ref.shape[0]
    hf = hf_ref[...]
    f = jnp.zeros(y_ref.shape, F32)
    for c0, cw in _ffn_chunks(nh):
        cg = []
        for base in (c0, nh + c0):
            cs = slice(base, base + cw)
            u = _dot(hf, wup_ref[:, cs])
            ext = [cb_ref[j, :, cs] for j in range(CONV_W - 1)] + [u[t * nb:(t + 1) * nb] for t in range(T)]
            for j in range(CONV_W - 1):
                unew_ref[j, :, cs] = ext[len(ext) - (CONV_W - 1) + j]
            rows = []
            for t in range(T):
                c = bdw_ref[:, cs]
                for j in range(CONV_W):
                    c = c + ext[t + j] * wdw_ref[j:j + 1, cs]
                rows.append(c)
            cg.append(jnp.concatenate(rows, axis=0))
        act = (jax.nn.silu(cg[1]) * cg[0]).astype(BF16)
        f = f + _dot(act, wdn_ref[c0:c0 + cw, :])
    y_ref[...] = x_ref[...] + _rms(f, gpost_ref[...])


def _ffn_sample(hf, x, cb, w_up, w_dw, b_dw, w_down, g_post, T):
    rows, d = x.shape
    npad = w_up.shape[1]
    ins = (hf, x, cb, w_up, w_dw, b_dw, w_down, g_post)
    full = lambda a: pl.BlockSpec(a.shape, lambda i: (0,) * a.ndim)
    return pl.pallas_call(
        functools.partial(_ffn_sample_kernel, T=T), grid=(1,),
        in_specs=[full(a) for a in ins],
        out_specs=[pl.BlockSpec((rows, d), lambda i: (0, 0)),
                   pl.BlockSpec((CONV_W - 1, rows // T, npad), lambda i: (0, 0, 0))],
        out_shape=[jax.ShapeDtypeStruct((rows, d), F32),
                   jax.ShapeDtypeStruct((CONV_W - 1, rows // T, npad), F32)],
        compiler_params=_params("arbitrary"), name="ffn_sample")(*ins)


def _pad_halves(a, nh, nh_pad):
    pad = [(0, 0)] * (a.ndim - 1) + [(0, nh_pad - nh)]
    return jnp.concatenate([jnp.pad(a[..., :nh], pad), jnp.pad(a[..., nh:], pad)], axis=-1)


def _unpad_halves(a, nh, nh_pad):
    return jnp.concatenate([a[..., :nh], a[..., nh_pad:nh_pad + nh]], axis=-1)


def _row_tile(rows, want):
    t = min(rows, want)
    while rows % t:
        t //= 2
    return t


def kernel(x_prompt, x_sample, cache_dk, cache_dv, cache_mem_k, cache_mem_v, state_ml_C, state_ml_n, state_ml_m, state_conv, page_table, mem_prompt, g_mix_pre, g_mix_post, w_in, b_if, g_ml_head, da_lambda, g_da_head, w_out, g_mem_pre, g_mem_post, g_mem_src, w_mq, w_mk, w_mv, w_mo, g_ffn_pre, g_ffn_post, w_up, w_dw, b_dw, w_down):
    depth = w_in.shape[0]
    Bp, Tp, d = x_prompt.shape
    Bs, Ts, _ = x_sample.shape
    n_pages, page = page_table.shape[1], cache_dk.shape[2]
    past_len = n_pages * page
    n_mem = mem_prompt.shape[1]
    nh = w_down.shape[1]
    nh_pad = -(-nh // LANES) * LANES
    H, D = ML_HEADS, ML_HEAD_DIM
    rows_p, rows_s = Bp * Tp, Bs * Ts

    tm_p = _row_tile(Tp, 512)
    tm_s = _row_tile(rows_s, 256)
    tab_p = _rope_tables(Tp, Tp, 0)
    tab_s = _rope_tables(tm_s, Ts, past_len)

    yp = x_prompt.reshape(rows_p, d)
    ys = x_sample.reshape(rows_s, d)
    outs = [[] for _ in range(14)]
    row = lambda a: a.reshape(1, -1)
    for l in range(depth):
        lam_init = 0.8 - 0.6 * math.exp(-0.3 * l)
        wi = w_in[l]
        w_ml = wi[:, :4 * ML_WIDTH].astype(BF16)
        w_da = wi[:, 4 * ML_WIDTH + 2 * H:].astype(BF16)
        w_gate = wi[:, 4 * ML_WIDTH:4 * ML_WIDTH + 2 * H]
        w_g = jnp.pad(w_gate, ((0, 0), (0, LANES - 2 * H))).astype(BF16)
        w_gt = w_gate.T.astype(BF16)
        b_col = jnp.pad(b_if[l], (0, LANES - 2 * H)).reshape(1, LANES)
        b_row = b_if[l].reshape(2 * H, 1)
        wo_ml, wo_da = w_out[l][:ML_WIDTH].astype(BF16), w_out[l][ML_WIDTH:].astype(BF16)
        wq_b, wo_b = w_mq[l].astype(BF16), w_mo[l].astype(BF16)
        wk_b, wv_b = w_mk[l].astype(BF16), w_mv[l].astype(BF16)
        wup_b = _pad_halves(w_up[l], nh, nh_pad).astype(BF16)
        wdw_p = _pad_halves(w_dw[l], nh, nh_pad)
        bdw_p = _pad_halves(b_dw[l].reshape(1, -1), nh, nh_pad)
        wdn_b = jnp.pad(w_down[l], ((0, nh_pad - nh), (0, 0))).astype(BF16)
        g_da3 = g_da_head[l].reshape(DA_HEADS, 1, DA_V_DIM)

        q, k, v, o, gc, gr, dq, dk, dv, dkb, dvb = _in_proj(
            yp, row(g_mix_pre[l]), w_ml, w_da, w_g, w_gt, b_col, b_row, tab_p, BF16, tm_p,
            DA_QK_DIM ** -0.5 * math.log2(math.e))
        qs, ks, vs, os_, gcs, grs, dqs, dk_s, dv_s, _, _ = _in_proj(
            ys, row(g_mix_pre[l]), w_ml, w_da, w_g, w_gt, b_col, b_row, tab_s, F32, tm_s, 1.0)
        r3 = lambda a: a.reshape(Bs, Ts, a.shape[-1])
        r4 = lambda a: a.reshape(Bs, Ts, DA_HEADS, DA_V_DIM)
        h_da_s, h_ml, C_p, n_p, m_p = _da_sample_mlstm_prompt(
            page_table, r3(dqs), r4(dk_s), r4(dv_s), da_lambda[l], g_da_head[l],
            cache_dk.reshape(depth, -1, page * DA_HEADS, DA_V_DIM),
            cache_dv.reshape(depth, -1, page * DA_HEADS, DA_V_DIM), lam_init, l,
            q, k, v, o, gc, gr, g_ml_head[l], Bp, Tp)

        h_da = _da_prompt(dq, dkb, dvb, da_lambda[l], g_da3, Bp, Tp, _row_tile(Tp, 512), 2, lam_init)
        x1, qm = _proj_norm([h_ml, h_da], [wo_ml, wo_da], yp, row(g_mix_post[l]), row(g_mem_pre[l]), wq_b, tm_p)
        mk_p, mv_p = _mem_kv(mem_prompt.reshape(Bp * n_mem, d), row(g_mem_src[l]), wk_b, wv_b,
                             _row_tile(Bp * n_mem, 256))
        mk_p, mv_p = mk_p.reshape(Bp, n_mem, d), mv_p.reshape(Bp, n_mem, d)
        om = _mem_attn(qm.reshape(Bp, Tp, d), mk_p, mv_p, 1, tm_p)
        x2, hf = _proj_norm([om.reshape(rows_p, d)], [wo_b], x1, row(g_mem_post[l]), row(g_ffn_pre[l]), None, tm_p)
        yp, ulast = _ffn_prompt(hf, x2, wup_b, wdw_p, bdw_p, wdn_b, row(g_ffn_post[l]), Bp, Tp, tm_p)
        cv_p = _unpad_halves(ulast[:, ulast.shape[1] - (CONV_W - 1):], nh, nh_pad)

        gr3 = grs.reshape(2 * H, Bs, Ts).transpose(1, 0, 2)
        h_ml, C_s, n_s, m_s = _mlstm_sample(
            r3(qs), r3(ks), r3(vs), r3(os_), r3(gcs), gr3, g_ml_head[l],
            state_ml_C[l], state_ml_n[l].reshape(Bs, H, 1, D), state_ml_m[l].reshape(Bs, H, 1, 1),
            _row_tile(Bs, 8))
        x1, qm = _proj_norm([h_ml.reshape(rows_s, ML_WIDTH), h_da_s.reshape(rows_s, DA_WIDTH)], [wo_ml, wo_da], ys,
                            row(g_mix_post[l]), row(g_mem_pre[l]), wq_b, tm_s, h_dtype=F32)
        om = _mem_attn_cache(qm.reshape(Bs, Ts, d), cache_mem_k, cache_mem_v, l, _row_tile(Bs, 4))
        x2, hf = _proj_norm([om.reshape(rows_s, d)], [wo_b], x1, row(g_mem_post[l]), row(g_ffn_pre[l]), None, tm_s)
        tmaj = lambda a: a.reshape(Bs, Ts, -1).transpose(1, 0, 2).reshape(rows_s, -1)
        cb = _pad_halves(state_conv[l], nh, nh_pad).transpose(1, 0, 2)
        y_t, unew = _ffn_sample(tmaj(hf), tmaj(x2), cb, wup_b, wdw_p, bdw_p, wdn_b, row(g_ffn_post[l]), Ts)
        ys = y_t.reshape(Ts, Bs, d).transpose(1, 0, 2).reshape(rows_s, d)
        cv_s = _unpad_halves(unew.transpose(1, 0, 2), nh, nh_pad)

        vals = (dk.reshape(Bp, Tp, DA_HEADS, DA_V_DIM), dv.reshape(Bp, Tp, DA_HEADS, DA_V_DIM),
                mk_p.reshape(Bp, n_mem, MEM_HEADS, d // MEM_HEADS), mv_p.reshape(Bp, n_mem, MEM_HEADS, d // MEM_HEADS),
                C_p, n_p.reshape(Bp, H, D), m_p.reshape(Bp, H), cv_p,
                dk_s.reshape(Bs, Ts, DA_HEADS, DA_V_DIM), dv_s.reshape(Bs, Ts, DA_HEADS, DA_V_DIM),
                C_s, n_s.reshape(Bs, H, D), m_s.reshape(Bs, H), cv_s)
        for acc, val in zip(outs, vals):
            acc.append(val)
    return (yp.reshape(Bp, Tp, d), ys.reshape(Bs, Ts, d)) + tuple(jnp.stack(a) for a in outs)
```

```python
import functools
import math

import jax
import jax.numpy as jnp
from jax import lax
from jax.experimental import pallas as pl
from jax.experimental.pallas import tpu as pltpu

F32 = jnp.float32
BF16 = jnp.bfloat16

ML_HEADS = 4
ML_HEAD_DIM = 128
ML_WIDTH = ML_HEADS * ML_HEAD_DIM
DA_HEADS = 4
DA_V_DIM = 128
DA_QK_DIM = 64
DA_WIDTH = DA_HEADS * DA_V_DIM
ROPE_DIM = 16
ROPE_THETA = 500000.0
MEM_HEADS = 4
CONV_W = 3
RMS_EPS = 1e-6
LANES = 128
NEG_BIG = -1e30
VMEM_LIMIT = 56 * 1024 * 1024


def _params(*sem, vmem=VMEM_LIMIT):
    return pltpu.CompilerParams(dimension_semantics=sem, vmem_limit_bytes=vmem)


def _resident(a):
    return pl.BlockSpec(a.shape, lambda *_: (0,) * a.ndim, pipeline_mode=pl.Buffered(1))


def _rms(x, g):
    return x * lax.rsqrt(jnp.mean(x * x, axis=-1, keepdims=True) + RMS_EPS) * g


def _log_sigmoid(x):
    return jnp.minimum(x, 0.0) - jnp.log1p(jnp.exp(-jnp.abs(x)))


def _dot(a, b):
    return jnp.dot(a, b, preferred_element_type=F32)


def _dot_nt(a, b):
    return lax.dot_general(a, b, (((1,), (1,)), ((), ())), preferred_element_type=F32)


def _dot_tn(a, b):
    return lax.dot_general(a, b, (((0,), (0,)), ((), ())), preferred_element_type=F32)


def _rope_table_kernel(cos_ref, sa_ref, sb_ref, *, period, offset):
    rows = cos_ref.shape[0]
    half = ROPE_DIM // 2
    r = lax.broadcasted_iota(jnp.int32, (rows, LANES), 0) + pl.program_id(0) * rows
    lane = lax.broadcasted_iota(jnp.int32, (rows, LANES), 1)
    pos = (offset + r % period).astype(F32)
    c = lane % DA_QK_DIM
    j = (c % half).astype(F32)
    inv = jnp.exp(-math.log(ROPE_THETA) * (2.0 * j / ROPE_DIM))
    ang = pos * inv
    cos, sin = jnp.cos(ang), jnp.sin(ang)
    cos_ref[...] = jnp.where(c < ROPE_DIM, cos, 1.0)
    sa_ref[...] = jnp.where(c < half, -sin, 0.0)
    sb_ref[...] = jnp.where((c >= half) & (c < ROPE_DIM), sin, 0.0)


def _rope_tables(rows, period, offset):
    blk = min(rows, 512)
    spec = pl.BlockSpec((blk, LANES), lambda i: (i, 0))
    shp = jax.ShapeDtypeStruct((rows, LANES), F32)
    return pl.pallas_call(
        functools.partial(_rope_table_kernel, period=period, offset=offset),
        grid=(rows // blk,), out_specs=[spec] * 3, out_shape=[shp] * 3,
        compiler_params=_params("parallel"), name="rope_tables")()


def _rope(x, cos, sa, sb):
    outs = []
    for j in range(x.shape[1] // LANES):
        xj = x[:, j * LANES:(j + 1) * LANES]
        up = pltpu.roll(xj, LANES - ROPE_DIM // 2, axis=1)
        dn = pltpu.roll(xj, ROPE_DIM // 2, axis=1)
        outs.append(xj * cos + up * sa + dn * sb)
    return outs


def _in_proj_kernel(x_ref, g_ref, w_ref, wda_ref, wg_ref, wgt_ref, bcol_ref, brow_ref, cos_ref, sa_ref, sb_ref,
                    q_ref, k_ref, v_ref, o_ref, gc_ref, gr_ref, dq_ref, dk_ref, dv_ref, dkb_ref, dvb_ref,
                    *, dq_scale, n_sub):
    sub = x_ref.shape[0] // n_sub
    for u in range(n_sub):
        rows = slice(u * sub, (u + 1) * sub)
        h = _rms(x_ref[rows, :], g_ref[...]).astype(BF16)
        for j, ref in enumerate((q_ref, k_ref, v_ref, o_ref)):
            ref[rows, :] = _dot(h, w_ref[:, j * ML_WIDTH:(j + 1) * ML_WIDTH]).astype(ref.dtype)
        gc = _dot(h, wg_ref[...]) + bcol_ref[...]
        lane = lax.broadcasted_iota(jnp.int32, gc.shape, 1)
        gc_ref[rows, :] = jnp.where(lane < ML_HEADS, gc, _log_sigmoid(gc))
        gr = _dot_nt(wgt_ref[...], h) + brow_ref[...]
        row = lax.broadcasted_iota(jnp.int32, gr.shape, 0)
        gr_ref[:, rows] = jnp.where(row < ML_HEADS, gr, _log_sigmoid(gr))
        cos, sa, sb = cos_ref[rows, :], sa_ref[rows, :], sb_ref[rows, :]
        dq = _rope(_dot(h, wda_ref[:, :DA_WIDTH]), cos, sa, sb)
        dk = _rope(_dot(h, wda_ref[:, DA_WIDTH:2 * DA_WIDTH]), cos, sa, sb)
        dv = _dot(h, wda_ref[:, 2 * DA_WIDTH:3 * DA_WIDTH])
        dvb_ref[rows, :] = dv.astype(BF16)
        for j in range(DA_HEADS):
            cols = slice(j * LANES, (j + 1) * LANES)
            dq_ref[rows, cols] = (dq[j] * dq_scale).astype(dq_ref.dtype)
            dk_ref[rows, j, :] = dk[j]
            dkb_ref[rows, cols] = dk[j].astype(BF16)
            dv_ref[rows, j, :] = dv[:, cols]


def _in_proj(x, g, w_ml, w_da, w_g, w_gt, b_col, b_row, tables, act_dtype, tm, dq_scale):
    rows, d = x.shape
    cos, sa, sb = tables
    nt = cos.shape[0] // tm
    row_spec = lambda w: pl.BlockSpec((tm, w), lambda i: (i, 0))
    full = lambda a: pl.BlockSpec(a.shape, lambda i: (0,) * a.ndim)
    tab_spec = pl.BlockSpec((tm, LANES), lambda i: (i % nt, 0))
    out_shape = [jax.ShapeDtypeStruct((rows, ML_WIDTH), act_dtype)] * 4 + [
        jax.ShapeDtypeStruct((rows, LANES), F32), jax.ShapeDtypeStruct((8, rows), F32),
        jax.ShapeDtypeStruct((rows, DA_WIDTH), act_dtype),
        jax.ShapeDtypeStruct((rows, DA_HEADS, DA_V_DIM), F32), jax.ShapeDtypeStruct((rows, DA_HEADS, DA_V_DIM), F32),
        jax.ShapeDtypeStruct((rows, DA_WIDTH), BF16), jax.ShapeDtypeStruct((rows, DA_WIDTH), BF16)]
    kv_spec = pl.BlockSpec((tm, DA_HEADS, DA_V_DIM), lambda i: (i, 0, 0))
    out_specs = [row_spec(ML_WIDTH)] * 4 + [row_spec(LANES), pl.BlockSpec((8, tm), lambda i: (0, i)),
                                            row_spec(DA_WIDTH), kv_spec, kv_spec,
                                            row_spec(DA_WIDTH), row_spec(DA_WIDTH)]
    return pl.pallas_call(
        functools.partial(_in_proj_kernel, dq_scale=dq_scale, n_sub=1),
        grid=(rows // tm,),
        in_specs=[row_spec(d), full(g), full(w_ml), full(w_da), full(w_g), full(w_gt), full(b_col), full(b_row),
                  tab_spec, tab_spec, tab_spec],
        out_specs=out_specs, out_shape=out_shape,
        compiler_params=_params("parallel"), name="in_proj")(
            x, g, w_ml, w_da, w_g, w_gt, b_col, b_row, cos, sa, sb)


def _run_phases(*gens):
    live = list(gens)
    while live:
        for g in list(live):
            try:
                next(g)
            except StopIteration:
                live.remove(g)


def _mlstm_heads(heads):
    out = []
    _run_phases(_mlstm_phases(heads, out))
    return out


def _mlstm_phases(heads, out):
    L, D = heads[0][0].shape
    scale = D ** -0.5
    r = lax.broadcasted_iota(jnp.int32, (L, L), 0)
    c = lax.broadcasted_iota(jnp.int32, (L, L), 1)
    tri = c <= r
    gate = []
    for q, k, v, i_col, f_col, i_row, f_row, C, n, m in heads:
        b_col = jnp.sum(jnp.where(tri, f_row, 0.0), axis=1, keepdims=True)
        b_row = jnp.sum(jnp.where(r <= c, f_col, 0.0), axis=0, keepdims=True)
        log_d = jnp.where(tri, b_col - b_row + i_row, -jnp.inf)
        inter = b_col + m
        m_row = jnp.maximum(jnp.max(log_d, axis=1, keepdims=True), inter)
        b_last = b_col[L - 1:L, :]
        log_w = b_last - b_col + i_col
        m_new = jnp.maximum(b_last + m, jnp.max(log_w, axis=0, keepdims=True))
        gate.append(dict(
            m_row=m_row, w_inter=jnp.exp(inter - m_row), d=scale * jnp.exp(log_d - m_row), m_new=m_new,
            decay=jnp.exp(b_last + m - m_new), wk=(scale * jnp.exp(log_w - m_new)) * k.astype(F32)))
    yield
    qb = [h[0].astype(BF16) for h in heads]
    vb = [h[2].astype(BF16) for h in heads]
    qk = [_dot_nt(qb[j], heads[j][1].astype(BF16)) for j in range(len(heads))]
    qc = [_dot(qb[j], heads[j][7].astype(BF16)) for j in range(len(heads))]
    kv = [_dot_tn(gate[j]["wk"].astype(BF16), vb[j]) for j in range(len(heads))]
    yield
    s = [qk[j] * gate[j]["d"] for j in range(len(heads))]
    yield
    sv = [_dot(s[j].astype(BF16), vb[j]) for j in range(len(heads))]
    yield
    for j, (q, k, v, i_col, f_col, i_row, f_row, C, n, m) in enumerate(heads):
        g = gate[j]
        num = g["w_inter"] * qc[j] + sv[j]
        den = (g["w_inter"] * jnp.sum(q.astype(F32) * n, axis=1, keepdims=True)
               + jnp.sum(s[j], axis=1, keepdims=True))
        h = num / jnp.maximum(jnp.abs(den), jnp.exp(-g["m_row"]))
        c_new = g["decay"] * C + kv[j]
        n_new = g["decay"] * n + jnp.sum(g["wk"], axis=0, keepdims=True)
        out.append((h, c_new, n_new, g["m_new"]))


def _ml_head_out(h, o, g):
    return _rms(h, g) * jax.nn.sigmoid(o.astype(F32))


def _mlstm_prompt_phases(first_chunk, q_ref, k_ref, v_ref, o_ref, gc_ref, gr_ref, gh_ref,
                         h_ref, c_out, n_out, m_out, c_scr, n_scr, m_scr):
    gc, gr = gc_ref[...], gr_ref[...]
    cols = [slice(hh * ML_HEAD_DIM, (hh + 1) * ML_HEAD_DIM) for hh in range(ML_HEADS)]
    prev = lambda ref, hh: jnp.where(first_chunk, 0.0, ref[hh])
    res = []
    yield from _mlstm_phases([
        (q_ref[:, cols[hh]], k_ref[:, cols[hh]], v_ref[:, cols[hh]],
         gc[:, hh:hh + 1], gc[:, ML_HEADS + hh:ML_HEADS + hh + 1],
         gr[hh:hh + 1, :], gr[ML_HEADS + hh:ML_HEADS + hh + 1, :],
         prev(c_scr, hh), prev(n_scr, hh), prev(m_scr, hh)) for hh in range(ML_HEADS)], res)
    for hh, (h, c_new, n_new, m_new) in enumerate(res):
        c_scr[hh], n_scr[hh], m_scr[hh] = c_new, n_new, m_new
        c_out[0, hh], n_out[0, hh], m_out[0, hh] = c_new, n_new, m_new
        h_ref[:, cols[hh]] = _ml_head_out(h, o_ref[:, cols[hh]], gh_ref[hh:hh + 1, :]).astype(h_ref.dtype)


def _mlstm_sample_kernel(q_ref, k_ref, v_ref, o_ref, gc_ref, gr_ref, gh_ref, c_in, n_in, m_in,
                         h_ref, c_out, n_out, m_out):
    cols = [slice(hh * ML_HEAD_DIM, (hh + 1) * ML_HEAD_DIM) for hh in range(ML_HEADS)]
    per_trip = math.gcd(q_ref.shape[0], 2)

    def body(it, carry):
        elems = [it * per_trip + u for u in range(per_trip)]
        gates = [(gc_ref[bi], gr_ref[bi]) for bi in elems]
        items = [(bi, hh) for bi in elems for hh in range(ML_HEADS)]
        res = _mlstm_heads([
            (q_ref[bi, :, cols[hh]], k_ref[bi, :, cols[hh]], v_ref[bi, :, cols[hh]],
             gc[:, hh:hh + 1], gc[:, ML_HEADS + hh:ML_HEADS + hh + 1],
             gr[hh:hh + 1, :], gr[ML_HEADS + hh:ML_HEADS + hh + 1, :],
             c_in[bi, hh], n_in[bi, hh], m_in[bi, hh])
            for bi, (gc, gr) in zip(elems, gates) for hh in range(ML_HEADS)])
        for (bi, hh), (h, c_new, n_new, m_new) in zip(items, res):
            c_out[bi, hh], n_out[bi, hh], m_out[bi, hh] = c_new, n_new, m_new
            h_ref[bi, :, cols[hh]] = _ml_head_out(
                h, o_ref[bi, :, cols[hh]], gh_ref[hh:hh + 1, :]).astype(h_ref.dtype)
        return carry

    lax.fori_loop(0, q_ref.shape[0] // per_trip, body, 0)


def _mlstm_sample(q, k, v, o, gc, gr, g_head, c0, n0, m0, bb):
    B, T, _ = q.shape
    H, D = ML_HEADS, ML_HEAD_DIM
    b3 = lambda s: pl.BlockSpec((bb,) + s, lambda i: (i, 0, 0))
    b4 = lambda s: pl.BlockSpec((bb,) + s, lambda i: (i, 0, 0, 0))
    return pl.pallas_call(
        _mlstm_sample_kernel, grid=(B // bb,),
        in_specs=[b3((T, ML_WIDTH))] * 4 + [b3((T, LANES)), b3((8, T)),
                                            pl.BlockSpec(g_head.shape, lambda i: (0, 0)),
                                            b4((H, D, D)), b4((H, 1, D)), b4((H, 1, 1))],
        out_specs=[b3((T, ML_WIDTH)), b4((H, D, D)), b4((H, 1, D)), b4((H, 1, 1))],
        out_shape=[jax.ShapeDtypeStruct((B, T, ML_WIDTH), BF16),
                   jax.ShapeDtypeStruct((B, H, D, D), F32),
                   jax.ShapeDtypeStruct((B, H, 1, D), F32),
                   jax.ShapeDtypeStruct((B, H, 1, 1), F32)],
        compiler_params=_params("parallel"), name="mlstm_sample")(q, k, v, o, gc, gr, g_head, c0, n0, m0)


def _da_lambda(lam_ref, lam_init):
    lv = lam_ref[...]
    a = jnp.sum(lv[0:1, :] * lv[1:2, :], axis=1, keepdims=True)
    b = jnp.sum(lv[2:3, :] * lv[3:4, :], axis=1, keepdims=True)
    return jnp.exp(a) - jnp.exp(b) + lam_init


def _stack_components(q):
    lane = lax.broadcasted_iota(jnp.int32, q.shape, 1)
    zero = jnp.zeros_like(q)
    return jnp.concatenate([jnp.where(lane < DA_QK_DIM, q, zero), jnp.where(lane >= DA_QK_DIM, q, zero)], axis=0)


def _da_prompt_kernel(q_ref, k_ref, v_ref, lam_ref, gh_ref, out_ref, vt_scr, s_a, s_b, m_scr, l_scr, acc_scr,
                      *, lam_init):
    i = pl.program_id(2)
    tq = q_ref.shape[0]
    nblk = vt_scr.shape[1]
    heads = range(vt_scr.shape[0])
    hcols = [slice(hd * DA_V_DIM, (hd + 1) * DA_V_DIM) for hd in heads]

    @pl.when(i == 0)
    def _():
        for hd in heads:
            for j in range(nblk):
                vt_scr[hd, j] = v_ref[j * tq:(j + 1) * tq, hcols[hd]].astype(F32).T.astype(BF16)

    qqt = []
    for hd in heads:
        qt = q_ref[:, hcols[hd]].astype(F32).T
        dim = lax.broadcasted_iota(jnp.int32, qt.shape, 0)
        qqt.append(jnp.concatenate(
            [jnp.where(dim < DA_QK_DIM, qt, 0.0), jnp.where(dim >= DA_QK_DIM, qt, 0.0)],
            axis=1).astype(BF16))

    def scores(j, s_ref):
        off = pl.multiple_of(j * tq, tq)
        for hd in heads:
            s_ref[hd] = _dot(k_ref[pl.ds(off, tq), hcols[hd]], qqt[hd])

    def accumulate(j, s_ref, masked):
        stats = []
        for hd in heads:
            st = s_ref[hd]
            if masked:
                key = lax.broadcasted_iota(jnp.int32, st.shape, 0)
                query = lax.broadcasted_iota(jnp.int32, st.shape, 1) % tq
                st = jnp.where(key <= query, st, NEG_BIG)
            m_old = m_scr[hd]
            m_new = jnp.maximum(m_old, jnp.max(st, axis=0, keepdims=True))
            alpha = jnp.exp2(m_old - m_new)
            p = jnp.exp2(st - m_new)
            l_scr[hd] = alpha * l_scr[hd] + jnp.sum(p, axis=0, keepdims=True)
            m_scr[hd] = m_new
            stats.append((alpha, p.astype(BF16)))
        for hd, (alpha, p) in zip(heads, stats):
            acc_scr[hd] = alpha * acc_scr[hd] + _dot(vt_scr[hd, j], p)

    m_scr[...] = jnp.full_like(m_scr, NEG_BIG)
    l_scr[...] = jnp.zeros_like(l_scr)
    acc_scr[...] = jnp.zeros_like(acc_scr)
    scores(0, s_a)

    def body(t, carry):
        scores(2 * t + 1, s_b)
        accumulate(2 * t, s_a, False)
        scores(2 * t + 2, s_a)
        accumulate(2 * t + 1, s_b, False)
        return carry

    lax.fori_loop(0, i // 2, body, 0)

    @pl.when(i % 2 == 1)
    def _():
        scores(i, s_b)
        accumulate(i - 1, s_a, False)
        accumulate(i, s_b, True)

    @pl.when(i % 2 == 0)
    def _():
        accumulate(i, s_a, True)

    lam = _da_lambda(lam_ref, lam_init)
    for hd in heads:
        ot = acc_scr[hd] / l_scr[hd]
        at = ot[:, :tq] - lam * ot[:, tq:]
        norm = at * lax.rsqrt(jnp.mean(at * at, axis=0, keepdims=True) + RMS_EPS)
        out_ref[:, hcols[hd]] = ((norm.T * gh_ref[hd]) * (1.0 - lam_init)).astype(out_ref.dtype)


def _da_prompt(dq, dk, dv, da_lambda, g_head3, B, T, tq, hp, lam_init):
    nq = T // tq
    rows = B * T
    w = hp * DA_V_DIM
    kv_spec = pl.BlockSpec((T, w), lambda b, h, i: (b, h))
    return pl.pallas_call(
        functools.partial(_da_prompt_kernel, lam_init=lam_init), grid=(B, DA_HEADS // hp, nq),
        in_specs=[pl.BlockSpec((tq, w), lambda b, h, i: (b * nq + i, h)), kv_spec, kv_spec,
                  pl.BlockSpec(da_lambda.shape, lambda b, h, i: (0, 0)),
                  pl.BlockSpec((hp, 1, DA_V_DIM), lambda b, h, i: (h, 0, 0))],
        out_specs=pl.BlockSpec((tq, w), lambda b, h, i: (b * nq + i, h)),
        out_shape=jax.ShapeDtypeStruct((rows, DA_WIDTH), BF16),
        scratch_shapes=[pltpu.VMEM((hp, nq, DA_V_DIM, tq), BF16),
                        pltpu.VMEM((hp, tq, 2 * tq), F32), pltpu.VMEM((hp, tq, 2 * tq), F32),
                        pltpu.VMEM((hp, 1, 2 * tq), F32), pltpu.VMEM((hp, 1, 2 * tq), F32),
                        pltpu.VMEM((hp, DA_V_DIM, 2 * tq), F32)],
        compiler_params=_params("parallel", "parallel", "arbitrary"), name="da_prompt")(
            dq, dk, dv, da_lambda, g_head3)


def _paged_fetch(pt_ref, ck_hbm, cv_hbm, kbuf, vbuf, sem, layer):
    b = pl.program_id(0)
    n_pages = kbuf.shape[1]

    def copies(bi, slot):
        out = []
        for p in range(n_pages):
            pg = pt_ref[bi, p]
            out.append(pltpu.make_async_copy(ck_hbm.at[layer, pg], kbuf.at[slot, p], sem.at[slot, 0]))
            out.append(pltpu.make_async_copy(cv_hbm.at[layer, pg], vbuf.at[slot, p], sem.at[slot, 1]))
        return out

    @pl.when(b == 0)
    def _():
        for cp in copies(0, 0):
            cp.start()

    @pl.when(b + 1 < pl.num_programs(0))
    def _():
        for cp in copies(b + 1, (b + 1) % 2):
            cp.start()

    slot = b % 2
    for cp in copies(b, slot):
        cp.wait()
    return slot


def _da_sample_phases(slot, q_ref, kn_ref, vn_ref, lam_ref, gh_ref, out_ref, kbuf, vbuf, lam_init):
    n_pages, page = kbuf.shape[1], kbuf.shape[2] // DA_HEADS
    T = q_ref.shape[1]
    scale = DA_QK_DIM ** -0.5
    lam = _da_lambda(lam_ref, lam_init)
    q_all = q_ref[0]
    n_rows = n_pages * page * DA_HEADS
    qq = jnp.concatenate([_stack_components(q_all[:, hh * DA_V_DIM:(hh + 1) * DA_V_DIM])
                          for hh in range(DA_HEADS)], axis=0)
    k_all = kbuf[slot].reshape(n_rows, DA_V_DIM).astype(BF16)
    s_raw = _dot_nt(qq.astype(BF16), k_all)
    yield
    rq = 2 * T * DA_HEADS
    own_head = (lax.broadcasted_iota(jnp.int32, (rq, n_rows), 1) % DA_HEADS
                == lax.broadcasted_iota(jnp.int32, (rq, n_rows), 0) // (2 * T))
    s_past = jnp.where(own_head, s_raw * scale, NEG_BIG)
    trow = lax.broadcasted_iota(jnp.int32, (rq, 1), 0) % T

    def per_query_row(new_ref, t):
        x = new_ref[0, t]
        return jnp.concatenate([jnp.broadcast_to(x[hh:hh + 1, :], (2 * T, DA_V_DIM))
                                for hh in range(DA_HEADS)], axis=0)

    s_new = [jnp.where(trow >= t,
                       jnp.sum(qq * per_query_row(kn_ref, t), axis=1, keepdims=True) * scale, NEG_BIG)
             for t in range(T)]
    m = jnp.max(s_past, axis=1, keepdims=True)
    for t in range(T):
        m = jnp.maximum(m, s_new[t])
    p_past = jnp.exp(s_past - m)
    l = jnp.sum(p_past, axis=1, keepdims=True)
    v_all = vbuf[slot].reshape(n_rows, DA_V_DIM).astype(BF16)
    yield
    acc = _dot(p_past.astype(BF16), v_all)
    yield
    for t in range(T):
        p_t = jnp.exp(s_new[t] - m)
        l = l + p_t
        acc = acc + p_t * per_query_row(vn_ref, t)
    o = acc / l
    for hh in range(DA_HEADS):
        r0 = hh * 2 * T
        a = o[r0:r0 + T] - lam * o[r0 + T:r0 + 2 * T]
        out_ref[0, :, hh * DA_V_DIM:(hh + 1) * DA_V_DIM] = (
            _rms(a, gh_ref[hh:hh + 1, :]) * (1.0 - lam_init)).astype(out_ref.dtype)


def _da_sample_mlstm_prompt_kernel(
        pt_ref, q_ref, kn_ref, vn_ref, lam_ref, gh_ref, ck_hbm, cv_hbm,
        mq_ref, mk_ref, mv_ref, mo_ref, gc_ref, gr_ref, mgh_ref,
        out_ref, h_ref, c_out, n_out, m_out,
        kbuf, vbuf, sem, c_scr, n_scr, m_scr, *, lam_init, layer, chunks):
    i = pl.program_id(0)

    @pl.when(i == 0)
    def _():
        c_scr[...] = jnp.zeros_like(c_scr)
        n_scr[...] = jnp.zeros_like(n_scr)
        m_scr[...] = jnp.zeros_like(m_scr)

    slot = _paged_fetch(pt_ref, ck_hbm, cv_hbm, kbuf, vbuf, sem, layer)
    _run_phases(
        _da_sample_phases(slot, q_ref, kn_ref, vn_ref, lam_ref, gh_ref, out_ref, kbuf, vbuf, lam_init),
        _mlstm_prompt_phases(i % chunks == 0, mq_ref, mk_ref, mv_ref, mo_ref, gc_ref, gr_ref, mgh_ref,
                             h_ref, c_out, n_out, m_out, c_scr, n_scr, m_scr))


def _da_sample_mlstm_prompt(page_table, dq, dk, dv, da_lambda, g_da, cache_k, cache_v, lam_init, layer,
                            mq, mk, mv, mo, gc, gr, g_ml, Bp, Tp):
    B, T, _ = dq.shape
    rows = Bp * Tp
    assert rows % B == 0 and Tp % (rows // B) == 0, "one mLSTM chunk per sample batch element"
    L = rows // B
    chunks = Tp // L
    H, D = ML_HEADS, ML_HEAD_DIM
    n_pages = page_table.shape[1]
    page_rows = cache_k.shape[2]
    blk = pl.BlockSpec((1, T, DA_WIDTH), lambda i, pt: (i, 0, 0))
    kv_blk = pl.BlockSpec((1, T, DA_HEADS, DA_V_DIM), lambda i, pt: (i, 0, 0, 0))
    const = lambda a: pl.BlockSpec(a.shape, lambda i, pt: (0,) * a.ndim)
    mblk = lambda w: pl.BlockSpec((L, w), lambda i, pt: (i, 0))
    state = lambda *s: pl.BlockSpec((1, H) + s, lambda i, pt: (i // chunks, 0, 0, 0))
    grid_spec = pltpu.PrefetchScalarGridSpec(
        num_scalar_prefetch=1, grid=(B,),
        in_specs=[blk, kv_blk, kv_blk, const(da_lambda), const(g_da),
                  pl.BlockSpec(memory_space=pl.ANY), pl.BlockSpec(memory_space=pl.ANY),
                  mblk(ML_WIDTH), mblk(ML_WIDTH), mblk(ML_WIDTH), mblk(ML_WIDTH), mblk(LANES),
                  pl.BlockSpec((8, L), lambda i, pt: (0, i)), const(g_ml)],
        out_specs=[blk, mblk(ML_WIDTH), state(D, D), state(1, D), state(1, 1)],
        scratch_shapes=[pltpu.VMEM((2, n_pages, page_rows, DA_V_DIM), F32),
                        pltpu.VMEM((2, n_pages, page_rows, DA_V_DIM), F32),
                        pltpu.SemaphoreType.DMA((2, 2)),
                        pltpu.VMEM((H, D, D), F32), pltpu.VMEM((H, 1, D), F32), pltpu.VMEM((H, 1, 1), F32)])
    return pl.pallas_call(
        functools.partial(_da_sample_mlstm_prompt_kernel, lam_init=lam_init, layer=layer, chunks=chunks),
        grid_spec=grid_spec,
        out_shape=[jax.ShapeDtypeStruct((B, T, DA_WIDTH), BF16),
                   jax.ShapeDtypeStruct((rows, ML_WIDTH), BF16),
                   jax.ShapeDtypeStruct((Bp, H, D, D), F32),
                   jax.ShapeDtypeStruct((Bp, H, 1, D), F32),
                   jax.ShapeDtypeStruct((Bp, H, 1, 1), F32)],
        compiler_params=_params("arbitrary"), name="da_sample_mlstm_prompt")(
            page_table, dq, dk, dv, da_lambda, g_da, cache_k, cache_v, mq, mk, mv, mo, gc, gr, g_ml)


def _proj_norm_kernel(*refs, n_in, has_next):
    a_refs, w_refs = refs[:n_in], refs[n_in:2 * n_in]
    x_ref, gpost_ref, gpre_ref = refs[2 * n_in:2 * n_in + 3]
    rest = refs[2 * n_in + 3:]
    wn_ref = rest[0] if has_next else None
    xo_ref, ho_ref = rest[-2:]
    acc = _dot(a_refs[0][...], w_refs[0][...])
    for a, w in zip(a_refs[1:], w_refs[1:]):
        acc = acc + _dot(a[...], w[...])
    x1 = x_ref[...] + _rms(acc, gpost_ref[...])
    xo_ref[...] = x1
    hn = _rms(x1, gpre_ref[...]).astype(BF16)
    ho_ref[...] = (_dot(hn, wn_ref[...]) if has_next else hn).astype(ho_ref.dtype)


def _proj_norm(a_list, w_list, x, g_post, g_pre, w_next, tm, h_dtype=BF16):
    rows, d = x.shape
    n_in = len(a_list)
    has_next = w_next is not None
    row_spec = lambda w: pl.BlockSpec((tm, w), lambda i: (i, 0))
    full = lambda a: pl.BlockSpec(a.shape, lambda i: (0,) * a.ndim)
    ins = list(a_list) + list(w_list) + [x, g_post, g_pre] + ([w_next] if has_next else [])
    in_specs = ([row_spec(a.shape[1]) for a in a_list] + [full(w) for w in w_list]
                + [row_spec(d), full(g_post), full(g_pre)] + ([full(w_next)] if has_next else []))
    n_out = w_next.shape[1] if has_next else d
    return pl.pallas_call(
        functools.partial(_proj_norm_kernel, n_in=n_in, has_next=has_next), grid=(rows // tm,),
        in_specs=in_specs, out_specs=[row_spec(d), row_spec(n_out)],
        out_shape=[jax.ShapeDtypeStruct((rows, d), F32), jax.ShapeDtypeStruct((rows, n_out), h_dtype)],
        compiler_params=_params("parallel"), name="proj_norm")(*ins)


def _mem_kv_kernel(x_ref, g_ref, wk_ref, wv_ref, k_ref, v_ref, kb_ref, vb_ref):
    h = _rms(x_ref[...], g_ref[...]).astype(BF16)
    k = _dot(h, wk_ref[...])
    v = _dot(h, wv_ref[...])
    k_ref[...], v_ref[...] = k, v
    kb_ref[...], vb_ref[...] = k.astype(BF16), v.astype(BF16)


def _mem_kv(x, g, wk, wv, tm):
    rows, d = x.shape
    n = wk.shape[1]
    row_spec = lambda w: pl.BlockSpec((tm, w), lambda i: (i, 0))
    return pl.pallas_call(
        _mem_kv_kernel, grid=(rows // tm,),
        in_specs=[row_spec(d), _resident(g), _resident(wk), _resident(wv)],
        out_specs=[row_spec(n)] * 4,
        out_shape=[jax.ShapeDtypeStruct((rows, n), F32)] * 2 + [jax.ShapeDtypeStruct((rows, n), BF16)] * 2,
        compiler_params=_params("parallel"), name="mem_kv")(x, g, wk, wv)


def _mem_attn_heads(qkv):
    s = [_dot_nt(q.astype(BF16), k.astype(BF16)) * (q.shape[1] ** -0.5) for q, k, _ in qkv]
    p = [jnp.exp(x - jnp.max(x, axis=1, keepdims=True)) for x in s]
    pv = [_dot(x.astype(BF16), v.astype(BF16)) for x, (_, _, v) in zip(p, qkv)]
    return [o / jnp.sum(x, axis=1, keepdims=True) for o, x in zip(pv, p)]


def _mid_phases(rows, hml_ref, hda_ref, x_ref, woml_ref, woda_ref, gpost1_ref, gpre1_ref, wq_ref,
                mk_ref, mv_ref, wmo_ref, gpost2_ref, gpre2_ref, x2_ref, hf_ref):
    acc = _dot(hml_ref[rows, :], woml_ref[...]) + _dot(hda_ref[rows, :], woda_ref[...])
    yield
    x1 = x_ref[rows, :] + _rms(acc, gpost1_ref[...])
    qm = _dot(_rms(x1, gpre1_ref[...]).astype(BF16), wq_ref[...]).astype(BF16)
    yield
    hd = qm.shape[1] // MEM_HEADS
    hcols = [slice(hh * hd, (hh + 1) * hd) for hh in range(MEM_HEADS)]
    s = [_dot_nt(qm[:, c], mk_ref[0, :, c]) * (hd ** -0.5) for c in hcols]
    yield
    p = [jnp.exp(x - jnp.max(x, axis=1, keepdims=True)) for x in s]
    pv = [_dot(x.astype(BF16), mv_ref[0, :, c]) for x, c in zip(p, hcols)]
    yield
    o = jnp.concatenate([(a / jnp.sum(x, axis=1, keepdims=True)).astype(BF16) for a, x in zip(pv, p)], axis=1)
    acc2 = _dot(o, wmo_ref[...])
    yield
    x2 = x1 + _rms(acc2, gpost2_ref[...])
    x2_ref[rows, :] = x2
    hf_ref[rows, :] = _rms(x2, gpre2_ref[...]).astype(hf_ref.dtype)


def _mid_kernel(*refs, n_sub):
    tm = refs[2].shape[0]
    sub = tm // n_sub
    _run_phases(*[_mid_phases(slice(j * sub, (j + 1) * sub), *refs) for j in range(n_sub)])


def _mid(h_ml, h_da, x, wo_ml, wo_da, g_post1, g_pre1, wq, mk, mv, wmo, g_post2, g_pre2, T, tm):
    rows, d = x.shape
    tiles = T // tm
    row_spec = lambda w: pl.BlockSpec((tm, w), lambda i: (i, 0))
    kv_spec = pl.BlockSpec((1,) + mk.shape[1:], lambda i: (i // tiles, 0, 0))
    ins = (h_ml, h_da, x, wo_ml, wo_da, g_post1, g_pre1, wq, mk, mv, wmo, g_post2, g_pre2)
    in_specs = [row_spec(h_ml.shape[1]), row_spec(h_da.shape[1]), row_spec(d)] + [
        kv_spec if a is mk or a is mv else _resident(a) for a in ins[3:]]
    return pl.pallas_call(
        functools.partial(_mid_kernel, n_sub=2), grid=(rows // tm,),
        in_specs=in_specs, out_specs=[row_spec(d), row_spec(d)],
        out_shape=[jax.ShapeDtypeStruct((rows, d), F32), jax.ShapeDtypeStruct((rows, d), BF16)],
        compiler_params=_params("parallel"), name="mid")(*ins)


def _mem_attn_cache_kernel(q_ref, mk_hbm, mv_hbm, o_ref, kbuf, vbuf, sem, *, layer):
    g = pl.program_id(0)
    bb = q_ref.shape[0]
    hd = kbuf.shape[-1]

    def copies(gi, slot):
        out = []
        for bi in range(bb):
            for hh in range(MEM_HEADS):
                b = gi * bb + bi
                out.append(pltpu.make_async_copy(mk_hbm.at[layer, b, :, hh, :], kbuf.at[slot, bi, hh],
                                                 sem.at[slot, 0]))
                out.append(pltpu.make_async_copy(mv_hbm.at[layer, b, :, hh, :], vbuf.at[slot, bi, hh],
                                                 sem.at[slot, 1]))
        return out

    @pl.when(g == 0)
    def _():
        for cp in copies(0, 0):
            cp.start()

    @pl.when(g + 1 < pl.num_programs(0))
    def _():
        for cp in copies(g + 1, (g + 1) % 2):
            cp.start()

    slot = g % 2
    for cp in copies(g, slot):
        cp.wait()
    for bi in range(bb):
        outs = _mem_attn_heads([(q_ref[bi, :, hh * hd:(hh + 1) * hd], kbuf[slot, bi, hh], vbuf[slot, bi, hh])
                                for hh in range(MEM_HEADS)])
        for hh, o in enumerate(outs):
            o_ref[bi, :, hh * hd:(hh + 1) * hd] = o.astype(o_ref.dtype)


def _mem_attn_cache(q, mk, mv, layer, bb):
    B, T, d = q.shape
    M, H, hd = mk.shape[2:]
    return pl.pallas_call(
        functools.partial(_mem_attn_cache_kernel, layer=layer), grid=(B // bb,),
        in_specs=[pl.BlockSpec((bb, T, d), lambda g: (g, 0, 0)),
                  pl.BlockSpec(memory_space=pl.ANY), pl.BlockSpec(memory_space=pl.ANY)],
        out_specs=pl.BlockSpec((bb, T, d), lambda g: (g, 0, 0)),
        out_shape=jax.ShapeDtypeStruct((B, T, d), BF16),
        scratch_shapes=[pltpu.VMEM((2, bb, H, M, hd), F32), pltpu.VMEM((2, bb, H, M, hd), F32),
                        pltpu.SemaphoreType.DMA((2, 2))],
        compiler_params=_params("arbitrary"), name="mem_attn_cache")(q, mk, mv)


def _ffn_chunks(nh):
    nchunk = 2 if (nh // LANES) % 2 == 0 else 1
    cw = nh // nchunk
    return [(j * cw, cw) for j in range(nchunk)]


def _ffn_prompt_kernel(hf_ref, x_ref, wup_ref, wdw_ref, bdw_ref, wdn_ref, gpost_ref, y_ref, ulast_ref, ubuf,
                       *, tiles_per_seq):
    i = pl.program_id(0)
    tm = hf_ref.shape[0]
    nh = wdn_ref.shape[0]
    halo = ubuf.shape[0] - tm

    @pl.when(i % tiles_per_seq == 0)
    def _():
        ubuf[0:halo, :] = jnp.zeros((halo, ubuf.shape[1]), F32)

    hf = hf_ref[...]
    f = jnp.zeros((tm, y_ref.shape[1]), F32)
    for c0, cw in _ffn_chunks(nh):
        cg = []
        for base in (c0, nh + c0):
            cs = slice(base, base + cw)
            ubuf[halo:halo + tm, cs] = _dot(hf, wup_ref[:, cs])
            c = bdw_ref[:, cs]
            for j in range(CONV_W):
                lo = halo - (CONV_W - 1) + j
                c = c + ubuf[lo:lo + tm, cs] * wdw_ref[j:j + 1, cs]
            cg.append(c)
        act = (jax.nn.silu(cg[1]) * cg[0]).astype(BF16)
        f = f + _dot(act, wdn_ref[c0:c0 + cw, :])
    y_ref[...] = x_ref[...] + _rms(f, gpost_ref[...])
    tail = ubuf[tm:tm + halo, :]
    ubuf[0:halo, :] = tail
    ulast_ref[0] = tail


def _ffn_prompt(hf, x, w_up, w_dw, b_dw, w_down, g_post, B, T, tm):
    rows, d = x.shape
    npad = w_up.shape[1]
    tiles = T // tm
    halo = 8
    row_spec = lambda w: pl.BlockSpec((tm, w), lambda i: (i, 0))
    full = _resident
    return pl.pallas_call(
        functools.partial(_ffn_prompt_kernel, tiles_per_seq=tiles), grid=(rows // tm,),
        in_specs=[row_spec(d), row_spec(d), full(w_up), full(w_dw), full(b_dw), full(w_down), full(g_post)],
        out_specs=[row_spec(d), pl.BlockSpec((1, halo, npad), lambda i: (i // tiles, 0, 0))],
        out_shape=[jax.ShapeDtypeStruct((rows, d), F32), jax.ShapeDtypeStruct((B, halo, npad), F32)],
        scratch_shapes=[pltpu.VMEM((tm + halo, npad), F32)],
        compiler_params=_params("arbitrary"), name="ffn_prompt")(hf, x, w_up, w_dw, b_dw, w_down, g_post)


def _ffn_sample_kernel(hf_ref, x_ref, cb_ref, wup_ref, wdw_ref, bdw_ref, wdn_ref, gpost_ref, y_ref, unew_ref,
                       *, T):
    nb = hf_ref.shape[0] // T
    nh = wdn_ref.shape[0]
    hf = hf_ref[...]
    f = jnp.zeros(y_ref.shape, F32)
    for c0, cw in _ffn_chunks(nh):
        cg = []
        for base in (c0, nh + c0):
            cs = slice(base, base + cw)
            u = _dot(hf, wup_ref[:, cs])
            ext = [cb_ref[j, :, cs] for j in range(CONV_W - 1)] + [u[t * nb:(t + 1) * nb] for t in range(T)]
            for j in range(CONV_W - 1):
                unew_ref[j, :, cs] = ext[len(ext) - (CONV_W - 1) + j]
            rows = []
            for t in range(T):
                c = bdw_ref[:, cs]
                for j in range(CONV_W):
                    c = c + ext[t + j] * wdw_ref[j:j + 1, cs]
                rows.append(c)
            cg.append(jnp.concatenate(rows, axis=0))
        act = (jax.nn.silu(cg[1]) * cg[0]).astype(BF16)
        f = f + _dot(act, wdn_ref[c0:c0 + cw, :])
    y_ref[...] = x_ref[...] + _rms(f, gpost_ref[...])


def _ffn_sample(hf, x, cb, w_up, w_dw, b_dw, w_down, g_post, T):
    rows, d = x.shape
    npad = w_up.shape[1]
    ins = (hf, x, cb, w_up, w_dw, b_dw, w_down, g_post)
    full = lambda a: pl.BlockSpec(a.shape, lambda i: (0,) * a.ndim)
    return pl.pallas_call(
        functools.partial(_ffn_sample_kernel, T=T), grid=(1,),
        in_specs=[full(a) for a in ins],
        out_specs=[pl.BlockSpec((rows, d), lambda i: (0, 0)),
                   pl.BlockSpec((CONV_W - 1, rows // T, npad), lambda i: (0, 0, 0))],
        out_shape=[jax.ShapeDtypeStruct((rows, d), F32),
                   jax.ShapeDtypeStruct((CONV_W - 1, rows // T, npad), F32)],
        compiler_params=_params("arbitrary"), name="ffn_sample")(*ins)


def _pad_halves(a, nh, nh_pad):
    pad = [(0, 0)] * (a.ndim - 1) + [(0, nh_pad - nh)]
    return jnp.concatenate([jnp.pad(a[..., :nh], pad), jnp.pad(a[..., nh:], pad)], axis=-1)


def _unpad_halves(a, nh, nh_pad):
    return jnp.concatenate([a[..., :nh], a[..., nh_pad:nh_pad + nh]], axis=-1)


def _row_tile(rows, want):
    t = min(rows, want)
    while rows % t:
        t //= 2
    return t


def kernel(x_prompt, x_sample, cache_dk, cache_dv, cache_mem_k, cache_mem_v, state_ml_C, state_ml_n, state_ml_m, state_conv, page_table, mem_prompt, g_mix_pre, g_mix_post, w_in, b_if, g_ml_head, da_lambda, g_da_head, w_out, g_mem_pre, g_mem_post, g_mem_src, w_mq, w_mk, w_mv, w_mo, g_ffn_pre, g_ffn_post, w_up, w_dw, b_dw, w_down):
    depth = w_in.shape[0]
    Bp, Tp, d = x_prompt.shape
    Bs, Ts, _ = x_sample.shape
    n_pages, page = page_table.shape[1], cache_dk.shape[2]
    past_len = n_pages * page
    n_mem = mem_prompt.shape[1]
    nh = w_down.shape[1]
    nh_pad = -(-nh // LANES) * LANES
    H, D = ML_HEADS, ML_HEAD_DIM
    rows_p, rows_s = Bp * Tp, Bs * Ts

    tm_p = _row_tile(Tp, 512)
    tm_s = _row_tile(rows_s, 256)
    tab_p = _rope_tables(Tp, Tp, 0)
    tab_s = _rope_tables(tm_s, Ts, past_len)

    yp = x_prompt.reshape(rows_p, d)
    ys = x_sample.reshape(rows_s, d)
    outs = [[] for _ in range(14)]
    row = lambda a: a.reshape(1, -1)
    for l in range(depth):
        lam_init = 0.8 - 0.6 * math.exp(-0.3 * l)
        wi = w_in[l]
        w_ml = wi[:, :4 * ML_WIDTH].astype(BF16)
        w_da = wi[:, 4 * ML_WIDTH + 2 * H:].astype(BF16)
        w_gate = wi[:, 4 * ML_WIDTH:4 * ML_WIDTH + 2 * H]
        w_g = jnp.pad(w_gate, ((0, 0), (0, LANES - 2 * H))).astype(BF16)
        w_gt = w_gate.T.astype(BF16)
        b_col = jnp.pad(b_if[l], (0, LANES - 2 * H)).reshape(1, LANES)
        b_row = b_if[l].reshape(2 * H, 1)
        wo_ml, wo_da = w_out[l][:ML_WIDTH].astype(BF16), w_out[l][ML_WIDTH:].astype(BF16)
        wq_b, wo_b = w_mq[l].astype(BF16), w_mo[l].astype(BF16)
        wk_b, wv_b = w_mk[l].astype(BF16), w_mv[l].astype(BF16)
        wup_b = _pad_halves(w_up[l], nh, nh_pad).astype(BF16)
        wdw_p = _pad_halves(w_dw[l], nh, nh_pad)
        bdw_p = _pad_halves(b_dw[l].reshape(1, -1), nh, nh_pad)
        wdn_b = jnp.pad(w_down[l], ((0, nh_pad - nh), (0, 0))).astype(BF16)
        g_da3 = g_da_head[l].reshape(DA_HEADS, 1, DA_V_DIM)

        q, k, v, o, gc, gr, dq, dk, dv, dkb, dvb = _in_proj(
            yp, row(g_mix_pre[l]), w_ml, w_da, w_g, w_gt, b_col, b_row, tab_p, BF16, tm_p,
            DA_QK_DIM ** -0.5 * math.log2(math.e))
        qs, ks, vs, os_, gcs, grs, dqs, dk_s, dv_s, _, _ = _in_proj(
            ys, row(g_mix_pre[l]), w_ml, w_da, w_g, w_gt, b_col, b_row, tab_s, F32, tm_s, 1.0)
        r3 = lambda a: a.reshape(Bs, Ts, a.shape[-1])
        r4 = lambda a: a.reshape(Bs, Ts, DA_HEADS, DA_V_DIM)
        h_da_s, h_ml, C_p, n_p, m_p = _da_sample_mlstm_prompt(
            page_table, r3(dqs), r4(dk_s), r4(dv_s), da_lambda[l], g_da_head[l],
            cache_dk.reshape(depth, -1, page * DA_HEADS, DA_V_DIM),
            cache_dv.reshape(depth, -1, page * DA_HEADS, DA_V_DIM), lam_init, l,
            q, k, v, o, gc, gr, g_ml_head[l], Bp, Tp)

        h_da = _da_prompt(dq, dkb, dvb, da_lambda[l], g_da3, Bp, Tp, _row_tile(Tp, 512), 2, lam_init)
        mk_p, mv_p, mkb, mvb = _mem_kv(mem_prompt.reshape(Bp * n_mem, d), row(g_mem_src[l]), wk_b, wv_b,
                                       _row_tile(Bp * n_mem, 256))
        x2, hf = _mid(h_ml, h_da, yp, wo_ml, wo_da, row(g_mix_post[l]), row(g_mem_pre[l]), wq_b,
                      mkb.reshape(Bp, n_mem, d), mvb.reshape(Bp, n_mem, d), wo_b,
                      row(g_mem_post[l]), row(g_ffn_pre[l]), Tp, tm_p)
        yp, ulast = _ffn_prompt(hf, x2, wup_b, wdw_p, bdw_p, wdn_b, row(g_ffn_post[l]), Bp, Tp, tm_p)
        cv_p = _unpad_halves(ulast[:, ulast.shape[1] - (CONV_W - 1):], nh, nh_pad)

        gr3 = grs.reshape(2 * H, Bs, Ts).transpose(1, 0, 2)
        h_ml, C_s, n_s, m_s = _mlstm_sample(
            r3(qs), r3(ks), r3(vs), r3(os_), r3(gcs), gr3, g_ml_head[l],
            state_ml_C[l], state_ml_n[l].reshape(Bs, H, 1, D), state_ml_m[l].reshape(Bs, H, 1, 1),
            _row_tile(Bs, 8))
        x1, qm = _proj_norm([h_ml.reshape(rows_s, ML_WIDTH), h_da_s.reshape(rows_s, DA_WIDTH)], [wo_ml, wo_da], ys,
                            row(g_mix_post[l]), row(g_mem_pre[l]), wq_b, tm_s, h_dtype=F32)
        om = _mem_attn_cache(qm.reshape(Bs, Ts, d), cache_mem_k, cache_mem_v, l, _row_tile(Bs, 4))
        x2, hf = _proj_norm([om.reshape(rows_s, d)], [wo_b], x1, row(g_mem_post[l]), row(g_ffn_pre[l]), None, tm_s)
        tmaj = lambda a: a.reshape(Bs, Ts, -1).transpose(1, 0, 2).reshape(rows_s, -1)
        cb = _pad_halves(state_conv[l], nh, nh_pad).transpose(1, 0, 2)
        y_t, unew = _ffn_sample(tmaj(hf), tmaj(x2), cb, wup_b, wdw_p, bdw_p, wdn_b, row(g_ffn_post[l]), Ts)
        ys = y_t.reshape(Ts, Bs, d).transpose(1, 0, 2).reshape(rows_s, d)
        cv_s = _unpad_halves(unew.transpose(1, 0, 2), nh, nh_pad)

        vals = (dk.reshape(Bp, Tp, DA_HEADS, DA_V_DIM), dv.reshape(Bp, Tp, DA_HEADS, DA_V_DIM),
                mk_p.reshape(Bp, n_mem, MEM_HEADS, d // MEM_HEADS), mv_p.reshape(Bp, n_mem, MEM_HEADS, d // MEM_HEADS),
                C_p, n_p.reshape(Bp, H, D), m_p.reshape(Bp, H), cv_p,
                dk_s.reshape(Bs, Ts, DA_HEADS, DA_V_DIM), dv_s.reshape(Bs, Ts, DA_HEADS, DA_V_DIM),
                C_s, n_s.reshape(Bs, H, D), m_s.reshape(Bs, H), cv_s)
        for acc, val in zip(outs, vals):
            acc.append(val)
    return (yp.reshape(Bp, Tp, d), ys.reshape(Bs, Ts, d)) + tuple(jnp.stack(a) for a in outs)
```

```python
import functools
import math

import jax
import jax.numpy as jnp
from jax import lax
from jax.experimental import pallas as pl
from jax.experimental.pallas import tpu as pltpu

F32 = jnp.float32
BF16 = jnp.bfloat16

ML_HEADS = 4
ML_HEAD_DIM = 128
ML_WIDTH = ML_HEADS * ML_HEAD_DIM
DA_HEADS = 4
DA_V_DIM = 128
DA_QK_DIM = 64
DA_WIDTH = DA_HEADS * DA_V_DIM
ROPE_DIM = 16
ROPE_THETA = 500000.0
MEM_HEADS = 4
CONV_W = 3
RMS_EPS = 1e-6
LANES = 128
NEG_BIG = -1e30
VMEM_LIMIT = 56 * 1024 * 1024


def _params(*sem, vmem=VMEM_LIMIT):
    return pltpu.CompilerParams(dimension_semantics=sem, vmem_limit_bytes=vmem)


def _resident(a):
    return pl.BlockSpec(a.shape, lambda *_: (0,) * a.ndim, pipeline_mode=pl.Buffered(1))


def _rms(x, g):
    return x * lax.rsqrt(jnp.mean(x * x, axis=-1, keepdims=True) + RMS_EPS) * g


def _log_sigmoid(x):
    return jnp.minimum(x, 0.0) - jnp.log1p(jnp.exp(-jnp.abs(x)))


def _dot(a, b):
    return jnp.dot(a, b, preferred_element_type=F32)


def _dot_nt(a, b):
    return lax.dot_general(a, b, (((1,), (1,)), ((), ())), preferred_element_type=F32)


def _dot_tn(a, b):
    return lax.dot_general(a, b, (((0,), (0,)), ((), ())), preferred_element_type=F32)


def _rope_table_kernel(cos_ref, sa_ref, sb_ref, *, period, offset):
    rows = cos_ref.shape[0]
    half = ROPE_DIM // 2
    r = lax.broadcasted_iota(jnp.int32, (rows, LANES), 0) + pl.program_id(0) * rows
    lane = lax.broadcasted_iota(jnp.int32, (rows, LANES), 1)
    pos = (offset + r % period).astype(F32)
    c = lane % DA_QK_DIM
    j = (c % half).astype(F32)
    inv = jnp.exp(-math.log(ROPE_THETA) * (2.0 * j / ROPE_DIM))
    ang = pos * inv
    cos, sin = jnp.cos(ang), jnp.sin(ang)
    cos_ref[...] = jnp.where(c < ROPE_DIM, cos, 1.0)
    sa_ref[...] = jnp.where(c < half, -sin, 0.0)
    sb_ref[...] = jnp.where((c >= half) & (c < ROPE_DIM), sin, 0.0)


def _rope_tables(rows, period, offset):
    blk = min(rows, 512)
    spec = pl.BlockSpec((blk, LANES), lambda i: (i, 0))
    shp = jax.ShapeDtypeStruct((rows, LANES), F32)
    return pl.pallas_call(
        functools.partial(_rope_table_kernel, period=period, offset=offset),
        grid=(rows // blk,), out_specs=[spec] * 3, out_shape=[shp] * 3,
        compiler_params=_params("parallel"), name="rope_tables")()


def _rope(x, cos, sa, sb):
    outs = []
    for j in range(x.shape[1] // LANES):
        xj = x[:, j * LANES:(j + 1) * LANES]
        up = pltpu.roll(xj, LANES - ROPE_DIM // 2, axis=1)
        dn = pltpu.roll(xj, ROPE_DIM // 2, axis=1)
        outs.append(xj * cos + up * sa + dn * sb)
    return outs


def _in_proj_kernel(x_ref, g_ref, w_ref, wda_ref, wg_ref, wgt_ref, bcol_ref, brow_ref, cos_ref, sa_ref, sb_ref,
                    q_ref, k_ref, v_ref, o_ref, gc_ref, gr_ref, dq_ref, dk_ref, dv_ref, dkb_ref, dvb_ref,
                    *, dq_scale, n_sub):
    sub = x_ref.shape[0] // n_sub
    for u in range(n_sub):
        rows = slice(u * sub, (u + 1) * sub)
        h = _rms(x_ref[rows, :], g_ref[...]).astype(BF16)
        for j, ref in enumerate((q_ref, k_ref, v_ref, o_ref)):
            ref[rows, :] = _dot(h, w_ref[:, j * ML_WIDTH:(j + 1) * ML_WIDTH]).astype(ref.dtype)
        gc = _dot(h, wg_ref[...]) + bcol_ref[...]
        lane = lax.broadcasted_iota(jnp.int32, gc.shape, 1)
        gc_ref[rows, :] = jnp.where(lane < ML_HEADS, gc, _log_sigmoid(gc))
        gr = _dot_nt(wgt_ref[...], h) + brow_ref[...]
        row = lax.broadcasted_iota(jnp.int32, gr.shape, 0)
        gr_ref[:, rows] = jnp.where(row < ML_HEADS, gr, _log_sigmoid(gr))
        cos, sa, sb = cos_ref[rows, :], sa_ref[rows, :], sb_ref[rows, :]
        dq = _rope(_dot(h, wda_ref[:, :DA_WIDTH]), cos, sa, sb)
        dk = _rope(_dot(h, wda_ref[:, DA_WIDTH:2 * DA_WIDTH]), cos, sa, sb)
        dv = _dot(h, wda_ref[:, 2 * DA_WIDTH:3 * DA_WIDTH])
        dvb_ref[rows, :] = dv.astype(BF16)
        for j in range(DA_HEADS):
            cols = slice(j * LANES, (j + 1) * LANES)
            dq_ref[rows, cols] = (dq[j] * dq_scale).astype(dq_ref.dtype)
            dk_ref[rows, j, :] = dk[j]
            dkb_ref[rows, cols] = dk[j].astype(BF16)
            dv_ref[rows, j, :] = dv[:, cols]


def _in_proj(x, g, w_ml, w_da, w_g, w_gt, b_col, b_row, tables, act_dtype, tm, dq_scale):
    rows, d = x.shape
    cos, sa, sb = tables
    nt = cos.shape[0] // tm
    row_spec = lambda w: pl.BlockSpec((tm, w), lambda i: (i, 0))
    full = lambda a: pl.BlockSpec(a.shape, lambda i: (0,) * a.ndim)
    tab_spec = pl.BlockSpec((tm, LANES), lambda i: (i % nt, 0))
    out_shape = [jax.ShapeDtypeStruct((rows, ML_WIDTH), act_dtype)] * 4 + [
        jax.ShapeDtypeStruct((rows, LANES), F32), jax.ShapeDtypeStruct((8, rows), F32),
        jax.ShapeDtypeStruct((rows, DA_WIDTH), act_dtype),
        jax.ShapeDtypeStruct((rows, DA_HEADS, DA_V_DIM), F32), jax.ShapeDtypeStruct((rows, DA_HEADS, DA_V_DIM), F32),
        jax.ShapeDtypeStruct((rows, DA_WIDTH), BF16), jax.ShapeDtypeStruct((rows, DA_WIDTH), BF16)]
    kv_spec = pl.BlockSpec((tm, DA_HEADS, DA_V_DIM), lambda i: (i, 0, 0))
    out_specs = [row_spec(ML_WIDTH)] * 4 + [row_spec(LANES), pl.BlockSpec((8, tm), lambda i: (0, i)),
                                            row_spec(DA_WIDTH), kv_spec, kv_spec,
                                            row_spec(DA_WIDTH), row_spec(DA_WIDTH)]
    return pl.pallas_call(
        functools.partial(_in_proj_kernel, dq_scale=dq_scale, n_sub=1),
        grid=(rows // tm,),
        in_specs=[row_spec(d), full(g), full(w_ml), full(w_da), full(w_g), full(w_gt), full(b_col), full(b_row),
                  tab_spec, tab_spec, tab_spec],
        out_specs=out_specs, out_shape=out_shape,
        compiler_params=_params("parallel"), name="in_proj")(
            x, g, w_ml, w_da, w_g, w_gt, b_col, b_row, cos, sa, sb)


def _run_phases(*gens):
    live = list(gens)
    while live:
        for g in list(live):
            try:
                next(g)
            except StopIteration:
                live.remove(g)


def _mlstm_heads(heads):
    out = []
    _run_phases(_mlstm_phases(heads, out))
    return out


def _mlstm_phases(heads, out):
    L, D = heads[0][0].shape
    scale = D ** -0.5
    r = lax.broadcasted_iota(jnp.int32, (L, L), 0)
    c = lax.broadcasted_iota(jnp.int32, (L, L), 1)
    tri = c <= r
    gate = []
    for q, k, v, i_col, f_col, i_row, f_row, C, n, m in heads:
        b_col = jnp.sum(jnp.where(tri, f_row, 0.0), axis=1, keepdims=True)
        b_row = jnp.sum(jnp.where(r <= c, f_col, 0.0), axis=0, keepdims=True)
        log_d = jnp.where(tri, b_col - b_row + i_row, -jnp.inf)
        inter = b_col + m
        m_row = jnp.maximum(jnp.max(log_d, axis=1, keepdims=True), inter)
        b_last = b_col[L - 1:L, :]
        log_w = b_last - b_col + i_col
        m_new = jnp.maximum(b_last + m, jnp.max(log_w, axis=0, keepdims=True))
        gate.append(dict(
            m_row=m_row, w_inter=jnp.exp(inter - m_row), d=scale * jnp.exp(log_d - m_row), m_new=m_new,
            decay=jnp.exp(b_last + m - m_new), wk=(scale * jnp.exp(log_w - m_new)) * k.astype(F32)))
    yield
    qb = [h[0].astype(BF16) for h in heads]
    vb = [h[2].astype(BF16) for h in heads]
    qk = [_dot_nt(qb[j], heads[j][1].astype(BF16)) for j in range(len(heads))]
    qc = [_dot(qb[j], heads[j][7].astype(BF16)) for j in range(len(heads))]
    kv = [_dot_tn(gate[j]["wk"].astype(BF16), vb[j]) for j in range(len(heads))]
    yield
    s = [qk[j] * gate[j]["d"] for j in range(len(heads))]
    yield
    sv = [_dot(s[j].astype(BF16), vb[j]) for j in range(len(heads))]
    yield
    for j, (q, k, v, i_col, f_col, i_row, f_row, C, n, m) in enumerate(heads):
        g = gate[j]
        num = g["w_inter"] * qc[j] + sv[j]
        den = (g["w_inter"] * jnp.sum(q.astype(F32) * n, axis=1, keepdims=True)
               + jnp.sum(s[j], axis=1, keepdims=True))
        h = num / jnp.maximum(jnp.abs(den), jnp.exp(-g["m_row"]))
        c_new = g["decay"] * C + kv[j]
        n_new = g["decay"] * n + jnp.sum(g["wk"], axis=0, keepdims=True)
        out.append((h, c_new, n_new, g["m_new"]))


def _ml_head_out(h, o, g):
    return _rms(h, g) * jax.nn.sigmoid(o.astype(F32))


def _mlstm_prompt_phases(first_chunk, q_ref, k_ref, v_ref, o_ref, gc_ref, gr_ref, gh_ref,
                         h_ref, c_out, n_out, m_out, c_scr, n_scr, m_scr):
    gc, gr = gc_ref[...], gr_ref[...]
    cols = [slice(hh * ML_HEAD_DIM, (hh + 1) * ML_HEAD_DIM) for hh in range(ML_HEADS)]
    prev = lambda ref, hh: jnp.where(first_chunk, 0.0, ref[hh])
    res = []
    yield from _mlstm_phases([
        (q_ref[:, cols[hh]], k_ref[:, cols[hh]], v_ref[:, cols[hh]],
         gc[:, hh:hh + 1], gc[:, ML_HEADS + hh:ML_HEADS + hh + 1],
         gr[hh:hh + 1, :], gr[ML_HEADS + hh:ML_HEADS + hh + 1, :],
         prev(c_scr, hh), prev(n_scr, hh), prev(m_scr, hh)) for hh in range(ML_HEADS)], res)
    for hh, (h, c_new, n_new, m_new) in enumerate(res):
        c_scr[hh], n_scr[hh], m_scr[hh] = c_new, n_new, m_new
        c_out[0, hh], n_out[0, hh], m_out[0, hh] = c_new, n_new, m_new
        h_ref[:, cols[hh]] = _ml_head_out(h, o_ref[:, cols[hh]], gh_ref[hh:hh + 1, :]).astype(h_ref.dtype)


def _mlstm_sample_kernel(q_ref, k_ref, v_ref, o_ref, gc_ref, gr_ref, gh_ref, c_in, n_in, m_in,
                         h_ref, c_out, n_out, m_out):
    cols = [slice(hh * ML_HEAD_DIM, (hh + 1) * ML_HEAD_DIM) for hh in range(ML_HEADS)]
    per_trip = math.gcd(q_ref.shape[0], 2)

    def body(it, carry):
        elems = [it * per_trip + u for u in range(per_trip)]
        gates = [(gc_ref[bi], gr_ref[bi]) for bi in elems]
        items = [(bi, hh) for bi in elems for hh in range(ML_HEADS)]
        res = _mlstm_heads([
            (q_ref[bi, :, cols[hh]], k_ref[bi, :, cols[hh]], v_ref[bi, :, cols[hh]],
             gc[:, hh:hh + 1], gc[:, ML_HEADS + hh:ML_HEADS + hh + 1],
             gr[hh:hh + 1, :], gr[ML_HEADS + hh:ML_HEADS + hh + 1, :],
             c_in[bi, hh], n_in[bi, hh], m_in[bi, hh])
            for bi, (gc, gr) in zip(elems, gates) for hh in range(ML_HEADS)])
        for (bi, hh), (h, c_new, n_new, m_new) in zip(items, res):
            c_out[bi, hh], n_out[bi, hh], m_out[bi, hh] = c_new, n_new, m_new
            h_ref[bi, :, cols[hh]] = _ml_head_out(
                h, o_ref[bi, :, cols[hh]], gh_ref[hh:hh + 1, :]).astype(h_ref.dtype)
        return carry

    lax.fori_loop(0, q_ref.shape[0] // per_trip, body, 0)


def _mlstm_sample(q, k, v, o, gc, gr, g_head, c0, n0, m0, bb):
    B, T, _ = q.shape
    H, D = ML_HEADS, ML_HEAD_DIM
    b3 = lambda s: pl.BlockSpec((bb,) + s, lambda i: (i, 0, 0))
    b4 = lambda s: pl.BlockSpec((bb,) + s, lambda i: (i, 0, 0, 0))
    return pl.pallas_call(
        _mlstm_sample_kernel, grid=(B // bb,),
        in_specs=[b3((T, ML_WIDTH))] * 4 + [b3((T, LANES)), b3((8, T)),
                                            pl.BlockSpec(g_head.shape, lambda i: (0, 0)),
                                            b4((H, D, D)), b4((H, 1, D)), b4((H, 1, 1))],
        out_specs=[b3((T, ML_WIDTH)), b4((H, D, D)), b4((H, 1, D)), b4((H, 1, 1))],
        out_shape=[jax.ShapeDtypeStruct((B, T, ML_WIDTH), BF16),
                   jax.ShapeDtypeStruct((B, H, D, D), F32),
                   jax.ShapeDtypeStruct((B, H, 1, D), F32),
                   jax.ShapeDtypeStruct((B, H, 1, 1), F32)],
        compiler_params=_params("parallel"), name="mlstm_sample")(q, k, v, o, gc, gr, g_head, c0, n0, m0)


def _da_lambda(lam_ref, lam_init):
    lv = lam_ref[...]
    a = jnp.sum(lv[0:1, :] * lv[1:2, :], axis=1, keepdims=True)
    b = jnp.sum(lv[2:3, :] * lv[3:4, :], axis=1, keepdims=True)
    return jnp.exp(a) - jnp.exp(b) + lam_init


def _stack_components(q):
    lane = lax.broadcasted_iota(jnp.int32, q.shape, 1)
    zero = jnp.zeros_like(q)
    return jnp.concatenate([jnp.where(lane < DA_QK_DIM, q, zero), jnp.where(lane >= DA_QK_DIM, q, zero)], axis=0)


def _da_prompt_kernel(q_ref, k_ref, v_ref, lam_ref, gh_ref, out_ref, vt_scr, s_a, s_b, m_scr, l_scr, acc_scr,
                      *, lam_init):
    i = pl.program_id(2)
    tq = q_ref.shape[0]
    nblk = vt_scr.shape[1]
    heads = range(vt_scr.shape[0])
    hcols = [slice(hd * DA_V_DIM, (hd + 1) * DA_V_DIM) for hd in heads]

    @pl.when(i == 0)
    def _():
        for hd in heads:
            for j in range(nblk):
                vt_scr[hd, j] = v_ref[j * tq:(j + 1) * tq, hcols[hd]].astype(F32).T.astype(BF16)

    qqt = []
    for hd in heads:
        qt = q_ref[:, hcols[hd]].astype(F32).T
        dim = lax.broadcasted_iota(jnp.int32, qt.shape, 0)
        qqt.append(jnp.concatenate(
            [jnp.where(dim < DA_QK_DIM, qt, 0.0), jnp.where(dim >= DA_QK_DIM, qt, 0.0)],
            axis=1).astype(BF16))

    def scores(j, s_ref):
        off = pl.multiple_of(j * tq, tq)
        for hd in heads:
            s_ref[hd] = _dot(k_ref[pl.ds(off, tq), hcols[hd]], qqt[hd])

    def accumulate(j, s_ref, masked):
        stats = []
        for hd in heads:
            st = s_ref[hd]
            if masked:
                key = lax.broadcasted_iota(jnp.int32, st.shape, 0)
                query = lax.broadcasted_iota(jnp.int32, st.shape, 1) % tq
                st = jnp.where(key <= query, st, NEG_BIG)
            m_old = m_scr[hd]
            m_new = jnp.maximum(m_old, jnp.max(st, axis=0, keepdims=True))
            alpha = jnp.exp2(m_old - m_new)
            p = jnp.exp2(st - m_new)
            l_scr[hd] = alpha * l_scr[hd] + jnp.sum(p, axis=0, keepdims=True)
            m_scr[hd] = m_new
            stats.append((alpha, p.astype(BF16)))
        for hd, (alpha, p) in zip(heads, stats):
            acc_scr[hd] = alpha * acc_scr[hd] + _dot(vt_scr[hd, j], p)

    m_scr[...] = jnp.full_like(m_scr, NEG_BIG)
    l_scr[...] = jnp.zeros_like(l_scr)
    acc_scr[...] = jnp.zeros_like(acc_scr)
    scores(0, s_a)

    def body(t, carry):
        scores(2 * t + 1, s_b)
        accumulate(2 * t, s_a, False)
        scores(2 * t + 2, s_a)
        accumulate(2 * t + 1, s_b, False)
        return carry

    lax.fori_loop(0, i // 2, body, 0)

    @pl.when(i % 2 == 1)
    def _():
        scores(i, s_b)
        accumulate(i - 1, s_a, False)
        accumulate(i, s_b, True)

    @pl.when(i % 2 == 0)
    def _():
        accumulate(i, s_a, True)

    lam = _da_lambda(lam_ref, lam_init)
    for hd in heads:
        ot = acc_scr[hd] / l_scr[hd]
        at = ot[:, :tq] - lam * ot[:, tq:]
        norm = at * lax.rsqrt(jnp.mean(at * at, axis=0, keepdims=True) + RMS_EPS)
        out_ref[:, hcols[hd]] = ((norm.T * gh_ref[hd]) * (1.0 - lam_init)).astype(out_ref.dtype)


def _da_prompt(dq, dk, dv, da_lambda, g_head3, B, T, tq, hp, lam_init):
    nq = T // tq
    rows = B * T
    w = hp * DA_V_DIM
    kv_spec = pl.BlockSpec((T, w), lambda b, h, i: (b, h))
    return pl.pallas_call(
        functools.partial(_da_prompt_kernel, lam_init=lam_init), grid=(B, DA_HEADS // hp, nq),
        in_specs=[pl.BlockSpec((tq, w), lambda b, h, i: (b * nq + i, h)), kv_spec, kv_spec,
                  pl.BlockSpec(da_lambda.shape, lambda b, h, i: (0, 0)),
                  pl.BlockSpec((hp, 1, DA_V_DIM), lambda b, h, i: (h, 0, 0))],
        out_specs=pl.BlockSpec((tq, w), lambda b, h, i: (b * nq + i, h)),
        out_shape=jax.ShapeDtypeStruct((rows, DA_WIDTH), BF16),
        scratch_shapes=[pltpu.VMEM((hp, nq, DA_V_DIM, tq), BF16),
                        pltpu.VMEM((hp, tq, 2 * tq), F32), pltpu.VMEM((hp, tq, 2 * tq), F32),
                        pltpu.VMEM((hp, 1, 2 * tq), F32), pltpu.VMEM((hp, 1, 2 * tq), F32),
                        pltpu.VMEM((hp, DA_V_DIM, 2 * tq), F32)],
        compiler_params=_params("parallel", "parallel", "arbitrary"), name="da_prompt")(
            dq, dk, dv, da_lambda, g_head3)


def _paged_fetch(pt_ref, ck_hbm, cv_hbm, kbuf, vbuf, sem, layer):
    b = pl.program_id(0)
    n_pages = kbuf.shape[1]

    def copies(bi, slot):
        out = []
        for p in range(n_pages):
            pg = pt_ref[bi, p]
            out.append(pltpu.make_async_copy(ck_hbm.at[layer, pg], kbuf.at[slot, p], sem.at[slot, 0]))
            out.append(pltpu.make_async_copy(cv_hbm.at[layer, pg], vbuf.at[slot, p], sem.at[slot, 1]))
        return out

    @pl.when(b == 0)
    def _():
        for cp in copies(0, 0):
            cp.start()

    @pl.when(b + 1 < pl.num_programs(0))
    def _():
        for cp in copies(b + 1, (b + 1) % 2):
            cp.start()

    slot = b % 2
    for cp in copies(b, slot):
        cp.wait()
    return slot


def _da_sample_phases(slot, q_ref, kn_ref, vn_ref, lam_ref, gh_ref, out_ref, kbuf, vbuf, lam_init):
    n_pages, page = kbuf.shape[1], kbuf.shape[2] // DA_HEADS
    T = q_ref.shape[1]
    scale = DA_QK_DIM ** -0.5
    lam = _da_lambda(lam_ref, lam_init)
    q_all = q_ref[0]
    n_rows = n_pages * page * DA_HEADS
    qq = jnp.concatenate([_stack_components(q_all[:, hh * DA_V_DIM:(hh + 1) * DA_V_DIM])
                          for hh in range(DA_HEADS)], axis=0)
    k_all = kbuf[slot].reshape(n_rows, DA_V_DIM).astype(BF16)
    s_raw = _dot_nt(qq.astype(BF16), k_all)
    yield
    rq = 2 * T * DA_HEADS
    own_head = (lax.broadcasted_iota(jnp.int32, (rq, n_rows), 1) % DA_HEADS
                == lax.broadcasted_iota(jnp.int32, (rq, n_rows), 0) // (2 * T))
    s_past = jnp.where(own_head, s_raw * scale, NEG_BIG)
    trow = lax.broadcasted_iota(jnp.int32, (rq, 1), 0) % T

    def per_query_row(new_ref, t):
        x = new_ref[0, t]
        return jnp.concatenate([jnp.broadcast_to(x[hh:hh + 1, :], (2 * T, DA_V_DIM))
                                for hh in range(DA_HEADS)], axis=0)

    s_new = [jnp.where(trow >= t,
                       jnp.sum(qq * per_query_row(kn_ref, t), axis=1, keepdims=True) * scale, NEG_BIG)
             for t in range(T)]
    m = jnp.max(s_past, axis=1, keepdims=True)
    for t in range(T):
        m = jnp.maximum(m, s_new[t])
    p_past = jnp.exp(s_past - m)
    l = jnp.sum(p_past, axis=1, keepdims=True)
    v_all = vbuf[slot].reshape(n_rows, DA_V_DIM).astype(BF16)
    yield
    acc = _dot(p_past.astype(BF16), v_all)
    yield
    for t in range(T):
        p_t = jnp.exp(s_new[t] - m)
        l = l + p_t
        acc = acc + p_t * per_query_row(vn_ref, t)
    o = acc / l
    for hh in range(DA_HEADS):
        r0 = hh * 2 * T
        a = o[r0:r0 + T] - lam * o[r0 + T:r0 + 2 * T]
        out_ref[0, :, hh * DA_V_DIM:(hh + 1) * DA_V_DIM] = (
            _rms(a, gh_ref[hh:hh + 1, :]) * (1.0 - lam_init)).astype(out_ref.dtype)


def _da_sample_mlstm_prompt_kernel(
        pt_ref, q_ref, kn_ref, vn_ref, lam_ref, gh_ref, ck_hbm, cv_hbm,
        mq_ref, mk_ref, mv_ref, mo_ref, gc_ref, gr_ref, mgh_ref,
        out_ref, h_ref, c_out, n_out, m_out,
        kbuf, vbuf, sem, c_scr, n_scr, m_scr, *, lam_init, layer, chunks):
    i = pl.program_id(0)

    @pl.when(i == 0)
    def _():
        c_scr[...] = jnp.zeros_like(c_scr)
        n_scr[...] = jnp.zeros_like(n_scr)
        m_scr[...] = jnp.zeros_like(m_scr)

    slot = _paged_fetch(pt_ref, ck_hbm, cv_hbm, kbuf, vbuf, sem, layer)
    _run_phases(
        _da_sample_phases(slot, q_ref, kn_ref, vn_ref, lam_ref, gh_ref, out_ref, kbuf, vbuf, lam_init),
        _mlstm_prompt_phases(i % chunks == 0, mq_ref, mk_ref, mv_ref, mo_ref, gc_ref, gr_ref, mgh_ref,
                             h_ref, c_out, n_out, m_out, c_scr, n_scr, m_scr))


def _da_sample_mlstm_prompt(page_table, dq, dk, dv, da_lambda, g_da, cache_k, cache_v, lam_init, layer,
                            mq, mk, mv, mo, gc, gr, g_ml, Bp, Tp):
    B, T, _ = dq.shape
    rows = Bp * Tp
    assert rows % B == 0 and Tp % (rows // B) == 0, "one mLSTM chunk per sample batch element"
    L = rows // B
    chunks = Tp // L
    H, D = ML_HEADS, ML_HEAD_DIM
    n_pages = page_table.shape[1]
    page_rows = cache_k.shape[2]
    blk = pl.BlockSpec((1, T, DA_WIDTH), lambda i, pt: (i, 0, 0))
    kv_blk = pl.BlockSpec((1, T, DA_HEADS, DA_V_DIM), lambda i, pt: (i, 0, 0, 0))
    const = lambda a: pl.BlockSpec(a.shape, lambda i, pt: (0,) * a.ndim)
    mblk = lambda w: pl.BlockSpec((L, w), lambda i, pt: (i, 0))
    state = lambda *s: pl.BlockSpec((1, H) + s, lambda i, pt: (i // chunks, 0, 0, 0))
    grid_spec = pltpu.PrefetchScalarGridSpec(
        num_scalar_prefetch=1, grid=(B,),
        in_specs=[blk, kv_blk, kv_blk, const(da_lambda), const(g_da),
                  pl.BlockSpec(memory_space=pl.ANY), pl.BlockSpec(memory_space=pl.ANY),
                  mblk(ML_WIDTH), mblk(ML_WIDTH), mblk(ML_WIDTH), mblk(ML_WIDTH), mblk(LANES),
                  pl.BlockSpec((8, L), lambda i, pt: (0, i)), const(g_ml)],
        out_specs=[blk, mblk(ML_WIDTH), state(D, D), state(1, D), state(1, 1)],
        scratch_shapes=[pltpu.VMEM((2, n_pages, page_rows, DA_V_DIM), F32),
                        pltpu.VMEM((2, n_pages, page_rows, DA_V_DIM), F32),
                        pltpu.SemaphoreType.DMA((2, 2)),
                        pltpu.VMEM((H, D, D), F32), pltpu.VMEM((H, 1, D), F32), pltpu.VMEM((H, 1, 1), F32)])
    return pl.pallas_call(
        functools.partial(_da_sample_mlstm_prompt_kernel, lam_init=lam_init, layer=layer, chunks=chunks),
        grid_spec=grid_spec,
        out_shape=[jax.ShapeDtypeStruct((B, T, DA_WIDTH), BF16),
                   jax.ShapeDtypeStruct((rows, ML_WIDTH), BF16),
                   jax.ShapeDtypeStruct((Bp, H, D, D), F32),
                   jax.ShapeDtypeStruct((Bp, H, 1, D), F32),
                   jax.ShapeDtypeStruct((Bp, H, 1, 1), F32)],
        compiler_params=_params("arbitrary"), name="da_sample_mlstm_prompt")(
            page_table, dq, dk, dv, da_lambda, g_da, cache_k, cache_v, mq, mk, mv, mo, gc, gr, g_ml)


def _proj_norm_kernel(*refs, n_in, has_next):
    a_refs, w_refs = refs[:n_in], refs[n_in:2 * n_in]
    x_ref, gpost_ref, gpre_ref = refs[2 * n_in:2 * n_in + 3]
    rest = refs[2 * n_in + 3:]
    wn_ref = rest[0] if has_next else None
    xo_ref, ho_ref = rest[-2:]
    acc = _dot(a_refs[0][...], w_refs[0][...])
    for a, w in zip(a_refs[1:], w_refs[1:]):
        acc = acc + _dot(a[...], w[...])
    x1 = x_ref[...] + _rms(acc, gpost_ref[...])
    xo_ref[...] = x1
    hn = _rms(x1, gpre_ref[...]).astype(BF16)
    ho_ref[...] = (_dot(hn, wn_ref[...]) if has_next else hn).astype(ho_ref.dtype)


def _proj_norm(a_list, w_list, x, g_post, g_pre, w_next, tm, h_dtype=BF16):
    rows, d = x.shape
    n_in = len(a_list)
    has_next = w_next is not None
    row_spec = lambda w: pl.BlockSpec((tm, w), lambda i: (i, 0))
    full = lambda a: pl.BlockSpec(a.shape, lambda i: (0,) * a.ndim)
    ins = list(a_list) + list(w_list) + [x, g_post, g_pre] + ([w_next] if has_next else [])
    in_specs = ([row_spec(a.shape[1]) for a in a_list] + [full(w) for w in w_list]
                + [row_spec(d), full(g_post), full(g_pre)] + ([full(w_next)] if has_next else []))
    n_out = w_next.shape[1] if has_next else d
    return pl.pallas_call(
        functools.partial(_proj_norm_kernel, n_in=n_in, has_next=has_next), grid=(rows // tm,),
        in_specs=in_specs, out_specs=[row_spec(d), row_spec(n_out)],
        out_shape=[jax.ShapeDtypeStruct((rows, d), F32), jax.ShapeDtypeStruct((rows, n_out), h_dtype)],
        compiler_params=_params("parallel"), name="proj_norm")(*ins)


def _mem_kv_kernel(x_ref, g_ref, wk_ref, wv_ref, k_ref, v_ref, kb_ref, vb_ref):
    h = _rms(x_ref[...], g_ref[...]).astype(BF16)
    k = _dot(h, wk_ref[...])
    v = _dot(h, wv_ref[...])
    k_ref[...], v_ref[...] = k, v
    kb_ref[...], vb_ref[...] = k.astype(BF16), v.astype(BF16)


def _mem_kv(x, g, wk, wv, tm):
    rows, d = x.shape
    n = wk.shape[1]
    row_spec = lambda w: pl.BlockSpec((tm, w), lambda i: (i, 0))
    return pl.pallas_call(
        _mem_kv_kernel, grid=(rows // tm,),
        in_specs=[row_spec(d), _resident(g), _resident(wk), _resident(wv)],
        out_specs=[row_spec(n)] * 4,
        out_shape=[jax.ShapeDtypeStruct((rows, n), F32)] * 2 + [jax.ShapeDtypeStruct((rows, n), BF16)] * 2,
        compiler_params=_params("parallel"), name="mem_kv")(x, g, wk, wv)


def _mid_phases(rows, hml_ref, hda_ref, x_ref, woml_ref, woda_ref, gpost1_ref, gpre1_ref, wq_ref,
                mk_ref, mv_ref, wmo_ref, gpost2_ref, gpre2_ref, x2_ref, hf_ref):
    acc = _dot(hml_ref[rows, :], woml_ref[...]) + _dot(hda_ref[rows, :], woda_ref[...])
    yield
    x1 = x_ref[rows, :] + _rms(acc, gpost1_ref[...])
    qm = _dot(_rms(x1, gpre1_ref[...]).astype(BF16), wq_ref[...]).astype(BF16)
    yield
    hd = qm.shape[1] // MEM_HEADS
    hcols = [slice(hh * hd, (hh + 1) * hd) for hh in range(MEM_HEADS)]
    s = [_dot_nt(qm[:, c], mk_ref[0, :, c]) * (hd ** -0.5) for c in hcols]
    yield
    p = [jnp.exp(x - jnp.max(x, axis=1, keepdims=True)) for x in s]
    pv = [_dot(x.astype(BF16), mv_ref[0, :, c]) for x, c in zip(p, hcols)]
    yield
    o = jnp.concatenate([(a / jnp.sum(x, axis=1, keepdims=True)).astype(BF16) for a, x in zip(pv, p)], axis=1)
    acc2 = _dot(o, wmo_ref[...])
    yield
    x2 = x1 + _rms(acc2, gpost2_ref[...])
    x2_ref[rows, :] = x2
    hf_ref[rows, :] = _rms(x2, gpre2_ref[...]).astype(hf_ref.dtype)


def _cache_fetch(mk_hbm, mv_hbm, kbuf, vbuf, sem, layer):
    g = pl.program_id(0)
    bb = kbuf.shape[1]

    def copies(gi, slot):
        out = []
        for bi in range(bb):
            for hh in range(MEM_HEADS):
                b = gi * bb + bi
                out.append(pltpu.make_async_copy(mk_hbm.at[layer, b, :, hh, :], kbuf.at[slot, bi, hh],
                                                 sem.at[slot, 0]))
                out.append(pltpu.make_async_copy(mv_hbm.at[layer, b, :, hh, :], vbuf.at[slot, bi, hh],
                                                 sem.at[slot, 1]))
        return out

    @pl.when(g == 0)
    def _():
        for cp in copies(0, 0):
            cp.start()

    @pl.when(g + 1 < pl.num_programs(0))
    def _():
        for cp in copies(g + 1, (g + 1) % 2):
            cp.start()

    slot = g % 2
    for cp in copies(g, slot):
        cp.wait()
    return slot


def _mem_attn_cache_phases(slot, q_ref, o_ref, kbuf, vbuf):
    bb, hd = q_ref.shape[0], kbuf.shape[-1]
    items = [(bi, hh) for bi in range(bb) for hh in range(MEM_HEADS)]
    cols = lambda hh: slice(hh * hd, (hh + 1) * hd)
    s = [_dot_nt(q_ref[bi, :, cols(hh)].astype(BF16), kbuf[slot, bi, hh].astype(BF16)) * (hd ** -0.5)
         for bi, hh in items]
    yield
    p = [jnp.exp(x - jnp.max(x, axis=1, keepdims=True)) for x in s]
    pv = [_dot(x.astype(BF16), vbuf[slot, bi, hh].astype(BF16)) for x, (bi, hh) in zip(p, items)]
    yield
    for (bi, hh), a, x in zip(items, pv, p):
        o_ref[bi, :, cols(hh)] = (a / jnp.sum(x, axis=1, keepdims=True)).astype(o_ref.dtype)


def _mid_kernel(*refs, n_sub, layer):
    (hml_ref, hda_ref, x_ref, woml_ref, woda_ref, gpost1_ref, gpre1_ref, wq_ref, mk_ref, mv_ref, wmo_ref,
     gpost2_ref, gpre2_ref, qs_ref, ck_hbm, cv_hbm, x2_ref, hf_ref, os_ref, kbuf, vbuf, sem) = refs
    slot = _cache_fetch(ck_hbm, cv_hbm, kbuf, vbuf, sem, layer)
    sub = x_ref.shape[0] // n_sub
    prompt = refs[:13] + (x2_ref, hf_ref)
    _run_phases(_mem_attn_cache_phases(slot, qs_ref, os_ref, kbuf, vbuf),
                *[_mid_phases(slice(j * sub, (j + 1) * sub), *prompt) for j in range(n_sub)])


def _mid(h_ml, h_da, x, wo_ml, wo_da, g_post1, g_pre1, wq, mk, mv, wmo, g_post2, g_pre2, T, tm,
         q_s, cache_k, cache_v, layer):
    rows, d = x.shape
    tiles = T // tm
    steps = rows // tm
    Bs, Ts, _ = q_s.shape
    assert Bs % steps == 0, "one group of sample batch elements per prompt row tile"
    bb = Bs // steps
    M, H, hd = cache_k.shape[2:]
    row_spec = lambda w: pl.BlockSpec((tm, w), lambda i: (i, 0))
    kv_spec = pl.BlockSpec((1,) + mk.shape[1:], lambda i: (i // tiles, 0, 0))
    grp_spec = pl.BlockSpec((bb, Ts, d), lambda i: (i, 0, 0))
    hbm = pl.BlockSpec(memory_space=pl.ANY)
    ins = (h_ml, h_da, x, wo_ml, wo_da, g_post1, g_pre1, wq, mk, mv, wmo, g_post2, g_pre2)
    in_specs = [row_spec(h_ml.shape[1]), row_spec(h_da.shape[1]), row_spec(d)] + [
        kv_spec if a is mk or a is mv else _resident(a) for a in ins[3:]] + [grp_spec, hbm, hbm]
    return pl.pallas_call(
        functools.partial(_mid_kernel, n_sub=2, layer=layer), grid=(steps,),
        in_specs=in_specs, out_specs=[row_spec(d), row_spec(d), grp_spec],
        out_shape=[jax.ShapeDtypeStruct((rows, d), F32), jax.ShapeDtypeStruct((rows, d), BF16),
                   jax.ShapeDtypeStruct((Bs, Ts, d), BF16)],
        scratch_shapes=[pltpu.VMEM((2, bb, H, M, hd), F32), pltpu.VMEM((2, bb, H, M, hd), F32),
                        pltpu.SemaphoreType.DMA((2, 2))],
        compiler_params=_params("arbitrary"), name="mid")(*ins, q_s, cache_k, cache_v)


def _ffn_chunks(nh):
    nchunk = 2 if (nh // LANES) % 2 == 0 else 1
    cw = nh // nchunk
    return [(j * cw, cw) for j in range(nchunk)]


def _ffn_prompt_kernel(hf_ref, x_ref, wup_ref, wdw_ref, bdw_ref, wdn_ref, gpost_ref, y_ref, ulast_ref, ubuf,
                       *, tiles_per_seq):
    i = pl.program_id(0)
    tm = hf_ref.shape[0]
    nh = wdn_ref.shape[0]
    halo = ubuf.shape[0] - tm

    @pl.when(i % tiles_per_seq == 0)
    def _():
        ubuf[0:halo, :] = jnp.zeros((halo, ubuf.shape[1]), F32)

    hf = hf_ref[...]
    f = jnp.zeros((tm, y_ref.shape[1]), F32)
    for c0, cw in _ffn_chunks(nh):
        cg = []
        for base in (c0, nh + c0):
            cs = slice(base, base + cw)
            ubuf[halo:halo + tm, cs] = _dot(hf, wup_ref[:, cs])
            c = bdw_ref[:, cs]
            for j in range(CONV_W):
                lo = halo - (CONV_W - 1) + j
                c = c + ubuf[lo:lo + tm, cs] * wdw_ref[j:j + 1, cs]
            cg.append(c)
        act = (jax.nn.silu(cg[1]) * cg[0]).astype(BF16)
        f = f + _dot(act, wdn_ref[c0:c0 + cw, :])
    y_ref[...] = x_ref[...] + _rms(f, gpost_ref[...])
    tail = ubuf[tm:tm + halo, :]
    ubuf[0:halo, :] = tail
    ulast_ref[0] = tail


def _ffn_prompt(hf, x, w_up, w_dw, b_dw, w_down, g_post, B, T, tm):
    rows, d = x.shape
    npad = w_up.shape[1]
    tiles = T // tm
    halo = 8
    row_spec = lambda w: pl.BlockSpec((tm, w), lambda i: (i, 0))
    full = _resident
    return pl.pallas_call(
        functools.partial(_ffn_prompt_kernel, tiles_per_seq=tiles), grid=(rows // tm,),
        in_specs=[row_spec(d), row_spec(d), full(w_up), full(w_dw), full(b_dw), full(w_down), full(g_post)],
        out_specs=[row_spec(d), pl.BlockSpec((1, halo, npad), lambda i: (i // tiles, 0, 0))],
        out_shape=[jax.ShapeDtypeStruct((rows, d), F32), jax.ShapeDtypeStruct((B, halo, npad), F32)],
        scratch_shapes=[pltpu.VMEM((tm + halo, npad), F32)],
        compiler_params=_params("arbitrary"), name="ffn_prompt")(hf, x, w_up, w_dw, b_dw, w_down, g_post)


def _ffn_sample_kernel(hf_ref, x_ref, cb_ref, wup_ref, wdw_ref, bdw_ref, wdn_ref, gpost_ref, y_ref, unew_ref,
                       *, T):
    nb = hf_ref.shape[0] // T
    nh = wdn_ref.shape[0]
    hf = hf_ref[...]
    f = jnp.zeros(y_ref.shape, F32)
    for c0, cw in _ffn_chunks(nh):
        cg = []
        for base in (c0, nh + c0):
            cs = slice(base, base + cw)
            u = _dot(hf, wup_ref[:, cs])
            ext = [cb_ref[j, :, cs] for j in range(CONV_W - 1)] + [u[t * nb:(t + 1) * nb] for t in range(T)]
            for j in range(CONV_W - 1):
                unew_ref[j, :, cs] = ext[len(ext) - (CONV_W - 1) + j]
            rows = []
            for t in range(T):
                c = bdw_ref[:, cs]
                for j in range(CONV_W):
                    c = c + ext[t + j] * wdw_ref[j:j + 1, cs]
                rows.append(c)
            cg.append(jnp.concatenate(rows, axis=0))
        act = (jax.nn.silu(cg[1]) * cg[0]).astype(BF16)
        f = f + _dot(act, wdn_ref[c0:c0 + cw, :])
    y_ref[...] = x_ref[...] + _rms(f, gpost_ref[...])


def _ffn_sample(hf, x, cb, w_up, w_dw, b_dw, w_down, g_post, T):
    rows, d = x.shape
    npad = w_up.shape[1]
    ins = (hf, x, cb, w_up, w_dw, b_dw, w_down, g_post)
    full = lambda a: pl.BlockSpec(a.shape, lambda i: (0,) * a.ndim)
    return pl.pallas_call(
        functools.partial(_ffn_sample_kernel, T=T), grid=(1,),
        in_specs=[full(a) for a in ins],
        out_specs=[pl.BlockSpec((rows, d), lambda i: (0, 0)),
                   pl.BlockSpec((CONV_W - 1, rows // T, npad), lambda i: (0, 0, 0))],
        out_shape=[jax.ShapeDtypeStruct((rows, d), F32),
                   jax.ShapeDtypeStruct((CONV_W - 1, rows // T, npad), F32)],
        compiler_params=_params("arbitrary"), name="ffn_sample")(*ins)


def _pad_halves(a, nh, nh_pad):
    pad = [(0, 0)] * (a.ndim - 1) + [(0, nh_pad - nh)]
    return jnp.concatenate([jnp.pad(a[..., :nh], pad), jnp.pad(a[..., nh:], pad)], axis=-1)


def _unpad_halves(a, nh, nh_pad):
    return jnp.concatenate([a[..., :nh], a[..., nh_pad:nh_pad + nh]], axis=-1)


def _row_tile(rows, want):
    t = min(rows, want)
    while rows % t:
        t //= 2
    return t


def kernel(x_prompt, x_sample, cache_dk, cache_dv, cache_mem_k, cache_mem_v, state_ml_C, state_ml_n, state_ml_m, state_conv, page_table, mem_prompt, g_mix_pre, g_mix_post, w_in, b_if, g_ml_head, da_lambda, g_da_head, w_out, g_mem_pre, g_mem_post, g_mem_src, w_mq, w_mk, w_mv, w_mo, g_ffn_pre, g_ffn_post, w_up, w_dw, b_dw, w_down):
    depth = w_in.shape[0]
    Bp, Tp, d = x_prompt.shape
    Bs, Ts, _ = x_sample.shape
    n_pages, page = page_table.shape[1], cache_dk.shape[2]
    past_len = n_pages * page
    n_mem = mem_prompt.shape[1]
    nh = w_down.shape[1]
    nh_pad = -(-nh // LANES) * LANES
    H, D = ML_HEADS, ML_HEAD_DIM
    rows_p, rows_s = Bp * Tp, Bs * Ts

    tm_p = _row_tile(Tp, 512)
    tm_s = _row_tile(rows_s, 256)
    tab_p = _rope_tables(Tp, Tp, 0)
    tab_s = _rope_tables(tm_s, Ts, past_len)

    yp = x_prompt.reshape(rows_p, d)
    ys = x_sample.reshape(rows_s, d)
    outs = [[] for _ in range(14)]
    row = lambda a: a.reshape(1, -1)
    for l in range(depth):
        lam_init = 0.8 - 0.6 * math.exp(-0.3 * l)
        wi = w_in[l]
        w_ml = wi[:, :4 * ML_WIDTH].astype(BF16)
        w_da = wi[:, 4 * ML_WIDTH + 2 * H:].astype(BF16)
        w_gate = wi[:, 4 * ML_WIDTH:4 * ML_WIDTH + 2 * H]
        w_g = jnp.pad(w_gate, ((0, 0), (0, LANES - 2 * H))).astype(BF16)
        w_gt = w_gate.T.astype(BF16)
        b_col = jnp.pad(b_if[l], (0, LANES - 2 * H)).reshape(1, LANES)
        b_row = b_if[l].reshape(2 * H, 1)
        wo_ml, wo_da = w_out[l][:ML_WIDTH].astype(BF16), w_out[l][ML_WIDTH:].astype(BF16)
        wq_b, wo_b = w_mq[l].astype(BF16), w_mo[l].astype(BF16)
        wk_b, wv_b = w_mk[l].astype(BF16), w_mv[l].astype(BF16)
        wup_b = _pad_halves(w_up[l], nh, nh_pad).astype(BF16)
        wdw_p = _pad_halves(w_dw[l], nh, nh_pad)
        bdw_p = _pad_halves(b_dw[l].reshape(1, -1), nh, nh_pad)
        wdn_b = jnp.pad(w_down[l], ((0, nh_pad - nh), (0, 0))).astype(BF16)
        g_da3 = g_da_head[l].reshape(DA_HEADS, 1, DA_V_DIM)

        q, k, v, o, gc, gr, dq, dk, dv, dkb, dvb = _in_proj(
            yp, row(g_mix_pre[l]), w_ml, w_da, w_g, w_gt, b_col, b_row, tab_p, BF16, tm_p,
            DA_QK_DIM ** -0.5 * math.log2(math.e))
        qs, ks, vs, os_, gcs, grs, dqs, dk_s, dv_s, _, _ = _in_proj(
            ys, row(g_mix_pre[l]), w_ml, w_da, w_g, w_gt, b_col, b_row, tab_s, F32, tm_s, 1.0)
        r3 = lambda a: a.reshape(Bs, Ts, a.shape[-1])
        r4 = lambda a: a.reshape(Bs, Ts, DA_HEADS, DA_V_DIM)
        h_da_s, h_ml, C_p, n_p, m_p = _da_sample_mlstm_prompt(
            page_table, r3(dqs), r4(dk_s), r4(dv_s), da_lambda[l], g_da_head[l],
            cache_dk.reshape(depth, -1, page * DA_HEADS, DA_V_DIM),
            cache_dv.reshape(depth, -1, page * DA_HEADS, DA_V_DIM), lam_init, l,
            q, k, v, o, gc, gr, g_ml_head[l], Bp, Tp)

        gr3 = grs.reshape(2 * H, Bs, Ts).transpose(1, 0, 2)
        h_ml_s, C_s, n_s, m_s = _mlstm_sample(
            r3(qs), r3(ks), r3(vs), r3(os_), r3(gcs), gr3, g_ml_head[l],
            state_ml_C[l], state_ml_n[l].reshape(Bs, H, 1, D), state_ml_m[l].reshape(Bs, H, 1, 1),
            _row_tile(Bs, 8))
        x1_s, qm_s = _proj_norm([h_ml_s.reshape(rows_s, ML_WIDTH), h_da_s.reshape(rows_s, DA_WIDTH)],
                                [wo_ml, wo_da], ys, row(g_mix_post[l]), row(g_mem_pre[l]), wq_b, tm_s,
                                h_dtype=F32)

        h_da = _da_prompt(dq, dkb, dvb, da_lambda[l], g_da3, Bp, Tp, _row_tile(Tp, 512), 2, lam_init)
        mk_p, mv_p, mkb, mvb = _mem_kv(mem_prompt.reshape(Bp * n_mem, d), row(g_mem_src[l]), wk_b, wv_b,
                                       _row_tile(Bp * n_mem, 256))
        x2, hf, om_s = _mid(h_ml, h_da, yp, wo_ml, wo_da, row(g_mix_post[l]), row(g_mem_pre[l]), wq_b,
                            mkb.reshape(Bp, n_mem, d), mvb.reshape(Bp, n_mem, d), wo_b,
                            row(g_mem_post[l]), row(g_ffn_pre[l]), Tp, tm_p,
                            qm_s.reshape(Bs, Ts, d), cache_mem_k, cache_mem_v, l)
        yp, ulast = _ffn_prompt(hf, x2, wup_b, wdw_p, bdw_p, wdn_b, row(g_ffn_post[l]), Bp, Tp, tm_p)
        cv_p = _unpad_halves(ulast[:, ulast.shape[1] - (CONV_W - 1):], nh, nh_pad)

        x2, hf = _proj_norm([om_s.reshape(rows_s, d)], [wo_b], x1_s, row(g_mem_post[l]), row(g_ffn_pre[l]),
                            None, tm_s)
        tmaj = lambda a: a.reshape(Bs, Ts, -1).transpose(1, 0, 2).reshape(rows_s, -1)
        cb = _pad_halves(state_conv[l], nh, nh_pad).transpose(1, 0, 2)
        y_t, unew = _ffn_sample(tmaj(hf), tmaj(x2), cb, wup_b, wdw_p, bdw_p, wdn_b, row(g_ffn_post[l]), Ts)
        ys = y_t.reshape(Ts, Bs, d).transpose(1, 0, 2).reshape(rows_s, d)
        cv_s = _unpad_halves(unew.transpose(1, 0, 2), nh, nh_pad)

        vals = (dk.reshape(Bp, Tp, DA_HEADS, DA_V_DIM), dv.reshape(Bp, Tp, DA_HEADS, DA_V_DIM),
                mk_p.reshape(Bp, n_mem, MEM_HEADS, d // MEM_HEADS), mv_p.reshape(Bp, n_mem, MEM_HEADS, d // MEM_HEADS),
                C_p, n_p.reshape(Bp, H, D), m_p.reshape(Bp, H), cv_p,
                dk_s.reshape(Bs, Ts, DA_HEADS, DA_V_DIM), dv_s.reshape(Bs, Ts, DA_HEADS, DA_V_DIM),
                C_s, n_s.reshape(Bs, H, D), m_s.reshape(Bs, H), cv_s)
        for acc, val in zip(outs, vals):
            acc.append(val)
    return (yp.reshape(Bp, Tp, d), ys.reshape(Bs, Ts, d)) + tuple(jnp.stack(a) for a in outs)
```

```python
import functools
import math

import jax
import jax.numpy as jnp
from jax import lax
from jax.experimental import pallas as pl
from jax.experimental.pallas import tpu as pltpu

F32 = jnp.float32
BF16 = jnp.bfloat16

ML_HEADS = 4
ML_HEAD_DIM = 128
ML_WIDTH = ML_HEADS * ML_HEAD_DIM
DA_HEADS = 4
DA_V_DIM = 128
DA_QK_DIM = 64
DA_WIDTH = DA_HEADS * DA_V_DIM
ROPE_DIM = 16
ROPE_THETA = 500000.0
MEM_HEADS = 4
CONV_W = 3
RMS_EPS = 1e-6
LANES = 128
NEG_BIG = -1e30
VMEM_LIMIT = 56 * 1024 * 1024


def _params(*sem, vmem=VMEM_LIMIT):
    return pltpu.CompilerParams(dimension_semantics=sem, vmem_limit_bytes=vmem)


def _resident(a):
    return pl.BlockSpec(a.shape, lambda *_: (0,) * a.ndim, pipeline_mode=pl.Buffered(1))


def _rms(x, g):
    return x * lax.rsqrt(jnp.mean(x * x, axis=-1, keepdims=True) + RMS_EPS) * g


def _log_sigmoid(x):
    return jnp.minimum(x, 0.0) - jnp.log1p(jnp.exp(-jnp.abs(x)))


def _dot(a, b):
    return jnp.dot(a, b, preferred_element_type=F32)


def _dot_nt(a, b):
    return lax.dot_general(a, b, (((1,), (1,)), ((), ())), preferred_element_type=F32)


def _dot_tn(a, b):
    return lax.dot_general(a, b, (((0,), (0,)), ((), ())), preferred_element_type=F32)


def _rope_table_kernel(cos_ref, sa_ref, sb_ref, *, period, offset):
    rows = cos_ref.shape[0]
    half = ROPE_DIM // 2
    r = lax.broadcasted_iota(jnp.int32, (rows, LANES), 0) + pl.program_id(0) * rows
    lane = lax.broadcasted_iota(jnp.int32, (rows, LANES), 1)
    pos = (offset + r % period).astype(F32)
    c = lane % DA_QK_DIM
    j = (c % half).astype(F32)
    inv = jnp.exp(-math.log(ROPE_THETA) * (2.0 * j / ROPE_DIM))
    ang = pos * inv
    cos, sin = jnp.cos(ang), jnp.sin(ang)
    cos_ref[...] = jnp.where(c < ROPE_DIM, cos, 1.0)
    sa_ref[...] = jnp.where(c < half, -sin, 0.0)
    sb_ref[...] = jnp.where((c >= half) & (c < ROPE_DIM), sin, 0.0)


def _rope_tables(rows, period, offset):
    blk = min(rows, 512)
    spec = pl.BlockSpec((blk, LANES), lambda i: (i, 0))
    shp = jax.ShapeDtypeStruct((rows, LANES), F32)
    return pl.pallas_call(
        functools.partial(_rope_table_kernel, period=period, offset=offset),
        grid=(rows // blk,), out_specs=[spec] * 3, out_shape=[shp] * 3,
        compiler_params=_params("parallel"), name="rope_tables")()


def _rope(x, cos, sa, sb):
    outs = []
    for j in range(x.shape[1] // LANES):
        xj = x[:, j * LANES:(j + 1) * LANES]
        up = pltpu.roll(xj, LANES - ROPE_DIM // 2, axis=1)
        dn = pltpu.roll(xj, ROPE_DIM // 2, axis=1)
        outs.append(xj * cos + up * sa + dn * sb)
    return outs


def _in_proj_kernel(x_ref, g_ref, w_ref, wda_ref, wg_ref, wgt_ref, bcol_ref, brow_ref, cos_ref, sa_ref, sb_ref,
                    q_ref, k_ref, v_ref, o_ref, gc_ref, gr_ref, dq_ref, dk_ref, dv_ref, dkb_ref, dvb_ref,
                    *, dq_scale, n_sub):
    sub = x_ref.shape[0] // n_sub
    for u in range(n_sub):
        rows = slice(u * sub, (u + 1) * sub)
        h = _rms(x_ref[rows, :], g_ref[...]).astype(BF16)
        for j, ref in enumerate((q_ref, k_ref, v_ref, o_ref)):
            ref[rows, :] = _dot(h, w_ref[:, j * ML_WIDTH:(j + 1) * ML_WIDTH]).astype(ref.dtype)
        gc = _dot(h, wg_ref[...]) + bcol_ref[...]
        lane = lax.broadcasted_iota(jnp.int32, gc.shape, 1)
        gc_ref[rows, :] = jnp.where(lane < ML_HEADS, gc, _log_sigmoid(gc))
        gr = _dot_nt(wgt_ref[...], h) + brow_ref[...]
        row = lax.broadcasted_iota(jnp.int32, gr.shape, 0)
        gr_ref[:, rows] = jnp.where(row < ML_HEADS, gr, _log_sigmoid(gr))
        cos, sa, sb = cos_ref[rows, :], sa_ref[rows, :], sb_ref[rows, :]
        dq = _rope(_dot(h, wda_ref[:, :DA_WIDTH]), cos, sa, sb)
        dk = _rope(_dot(h, wda_ref[:, DA_WIDTH:2 * DA_WIDTH]), cos, sa, sb)
        dv = _dot(h, wda_ref[:, 2 * DA_WIDTH:3 * DA_WIDTH])
        dvb_ref[rows, :] = dv.astype(BF16)
        for j in range(DA_HEADS):
            cols = slice(j * LANES, (j + 1) * LANES)
            dq_ref[rows, cols] = (dq[j] * dq_scale).astype(dq_ref.dtype)
            dk_ref[rows, j, :] = dk[j]
            dkb_ref[rows, cols] = dk[j].astype(BF16)
            dv_ref[rows, j, :] = dv[:, cols]


def _in_proj(x, g, w_ml, w_da, w_g, w_gt, b_col, b_row, tables, act_dtype, tm, dq_scale):
    rows, d = x.shape
    cos, sa, sb = tables
    nt = cos.shape[0] // tm
    row_spec = lambda w: pl.BlockSpec((tm, w), lambda i: (i, 0))
    full = lambda a: pl.BlockSpec(a.shape, lambda i: (0,) * a.ndim)
    tab_spec = pl.BlockSpec((tm, LANES), lambda i: (i % nt, 0))
    out_shape = [jax.ShapeDtypeStruct((rows, ML_WIDTH), act_dtype)] * 4 + [
        jax.ShapeDtypeStruct((rows, LANES), F32), jax.ShapeDtypeStruct((8, rows), F32),
        jax.ShapeDtypeStruct((rows, DA_WIDTH), act_dtype),
        jax.ShapeDtypeStruct((rows, DA_HEADS, DA_V_DIM), F32), jax.ShapeDtypeStruct((rows, DA_HEADS, DA_V_DIM), F32),
        jax.ShapeDtypeStruct((rows, DA_WIDTH), BF16), jax.ShapeDtypeStruct((rows, DA_WIDTH), BF16)]
    kv_spec = pl.BlockSpec((tm, DA_HEADS, DA_V_DIM), lambda i: (i, 0, 0))
    out_specs = [row_spec(ML_WIDTH)] * 4 + [row_spec(LANES), pl.BlockSpec((8, tm), lambda i: (0, i)),
                                            row_spec(DA_WIDTH), kv_spec, kv_spec,
                                            row_spec(DA_WIDTH), row_spec(DA_WIDTH)]
    return pl.pallas_call(
        functools.partial(_in_proj_kernel, dq_scale=dq_scale, n_sub=1),
        grid=(rows // tm,),
        in_specs=[row_spec(d), full(g), full(w_ml), full(w_da), full(w_g), full(w_gt), full(b_col), full(b_row),
                  tab_spec, tab_spec, tab_spec],
        out_specs=out_specs, out_shape=out_shape,
        compiler_params=_params("parallel"), name="in_proj")(
            x, g, w_ml, w_da, w_g, w_gt, b_col, b_row, cos, sa, sb)


def _run_phases(*gens):
    live = list(gens)
    while live:
        for g in list(live):
            try:
                next(g)
            except StopIteration:
                live.remove(g)


def _mlstm_phases(heads, out):
    L, D = heads[0][0].shape
    scale = D ** -0.5
    r = lax.broadcasted_iota(jnp.int32, (L, L), 0)
    c = lax.broadcasted_iota(jnp.int32, (L, L), 1)
    tri = c <= r
    gate = []
    for q, k, v, i_col, f_col, i_row, f_row, C, n, m in heads:
        b_col = jnp.sum(jnp.where(tri, f_row, 0.0), axis=1, keepdims=True)
        b_row = jnp.sum(jnp.where(r <= c, f_col, 0.0), axis=0, keepdims=True)
        log_d = jnp.where(tri, b_col - b_row + i_row, -jnp.inf)
        inter = b_col + m
        m_row = jnp.maximum(jnp.max(log_d, axis=1, keepdims=True), inter)
        b_last = b_col[L - 1:L, :]
        log_w = b_last - b_col + i_col
        m_new = jnp.maximum(b_last + m, jnp.max(log_w, axis=0, keepdims=True))
        gate.append(dict(
            m_row=m_row, w_inter=jnp.exp(inter - m_row), d=scale * jnp.exp(log_d - m_row), m_new=m_new,
            decay=jnp.exp(b_last + m - m_new), wk=(scale * jnp.exp(log_w - m_new)) * k.astype(F32)))
    yield
    qb = [h[0].astype(BF16) for h in heads]
    vb = [h[2].astype(BF16) for h in heads]
    qk = [_dot_nt(qb[j], heads[j][1].astype(BF16)) for j in range(len(heads))]
    qc = [_dot(qb[j], heads[j][7].astype(BF16)) for j in range(len(heads))]
    kv = [_dot_tn(gate[j]["wk"].astype(BF16), vb[j]) for j in range(len(heads))]
    yield
    s = [qk[j] * gate[j]["d"] for j in range(len(heads))]
    yield
    sv = [_dot(s[j].astype(BF16), vb[j]) for j in range(len(heads))]
    yield
    for j, (q, k, v, i_col, f_col, i_row, f_row, C, n, m) in enumerate(heads):
        g = gate[j]
        num = g["w_inter"] * qc[j] + sv[j]
        den = (g["w_inter"] * jnp.sum(q.astype(F32) * n, axis=1, keepdims=True)
               + jnp.sum(s[j], axis=1, keepdims=True))
        h = num / jnp.maximum(jnp.abs(den), jnp.exp(-g["m_row"]))
        c_new = g["decay"] * C + kv[j]
        n_new = g["decay"] * n + jnp.sum(g["wk"], axis=0, keepdims=True)
        out.append((h, c_new, n_new, g["m_new"]))


def _ml_head_out(h, o, g):
    return _rms(h, g) * jax.nn.sigmoid(o.astype(F32))


def _mlstm_prompt_phases(first_chunk, q_ref, k_ref, v_ref, o_ref, gc_ref, gr_ref, gh_ref,
                         h_ref, c_out, n_out, m_out, c_scr, n_scr, m_scr):
    gc, gr = gc_ref[...], gr_ref[...]
    cols = [slice(hh * ML_HEAD_DIM, (hh + 1) * ML_HEAD_DIM) for hh in range(ML_HEADS)]
    prev = lambda ref, hh: jnp.where(first_chunk, 0.0, ref[hh])
    res = []
    yield from _mlstm_phases([
        (q_ref[:, cols[hh]], k_ref[:, cols[hh]], v_ref[:, cols[hh]],
         gc[:, hh:hh + 1], gc[:, ML_HEADS + hh:ML_HEADS + hh + 1],
         gr[hh:hh + 1, :], gr[ML_HEADS + hh:ML_HEADS + hh + 1, :],
         prev(c_scr, hh), prev(n_scr, hh), prev(m_scr, hh)) for hh in range(ML_HEADS)], res)
    for hh, (h, c_new, n_new, m_new) in enumerate(res):
        c_scr[hh], n_scr[hh], m_scr[hh] = c_new, n_new, m_new
        c_out[0, hh], n_out[0, hh], m_out[0, hh] = c_new, n_new, m_new
        h_ref[:, cols[hh]] = _ml_head_out(h, o_ref[:, cols[hh]], gh_ref[hh:hh + 1, :]).astype(h_ref.dtype)


def _mlstm_sample_phases(q_ref, k_ref, v_ref, o_ref, gc_ref, gr_ref, gh_ref, c_in, n_in, m_in,
                         h_ref, c_out, n_out, m_out):
    gc, gr = gc_ref[0], gr_ref[0]
    cols = [slice(hh * ML_HEAD_DIM, (hh + 1) * ML_HEAD_DIM) for hh in range(ML_HEADS)]
    res = []
    yield from _mlstm_phases([
        (q_ref[0, :, cols[hh]], k_ref[0, :, cols[hh]], v_ref[0, :, cols[hh]],
         gc[:, hh:hh + 1], gc[:, ML_HEADS + hh:ML_HEADS + hh + 1],
         gr[hh:hh + 1, :], gr[ML_HEADS + hh:ML_HEADS + hh + 1, :],
         c_in[0, hh], n_in[0, hh], m_in[0, hh]) for hh in range(ML_HEADS)], res)
    for hh, (h, c_new, n_new, m_new) in enumerate(res):
        c_out[0, hh], n_out[0, hh], m_out[0, hh] = c_new, n_new, m_new
        h_ref[0, :, cols[hh]] = _ml_head_out(h, o_ref[0, :, cols[hh]], gh_ref[hh:hh + 1, :]).astype(h_ref.dtype)


def _da_lambda(lam_ref, lam_init):
    lv = lam_ref[...]
    a = jnp.sum(lv[0:1, :] * lv[1:2, :], axis=1, keepdims=True)
    b = jnp.sum(lv[2:3, :] * lv[3:4, :], axis=1, keepdims=True)
    return jnp.exp(a) - jnp.exp(b) + lam_init


def _stack_components(q):
    lane = lax.broadcasted_iota(jnp.int32, q.shape, 1)
    zero = jnp.zeros_like(q)
    return jnp.concatenate([jnp.where(lane < DA_QK_DIM, q, zero), jnp.where(lane >= DA_QK_DIM, q, zero)], axis=0)


def _da_prompt_kernel(q_ref, k_ref, v_ref, lam_ref, gh_ref, out_ref, vt_scr, s_a, s_b, m_scr, l_scr, acc_scr,
                      *, lam_init):
    i = pl.program_id(2)
    tq = q_ref.shape[0]
    nblk = vt_scr.shape[1]
    heads = range(vt_scr.shape[0])
    hcols = [slice(hd * DA_V_DIM, (hd + 1) * DA_V_DIM) for hd in heads]

    @pl.when(i == 0)
    def _():
        for hd in heads:
            for j in range(nblk):
                vt_scr[hd, j] = v_ref[j * tq:(j + 1) * tq, hcols[hd]].astype(F32).T.astype(BF16)

    qqt = []
    for hd in heads:
        qt = q_ref[:, hcols[hd]].astype(F32).T
        dim = lax.broadcasted_iota(jnp.int32, qt.shape, 0)
        qqt.append(jnp.concatenate(
            [jnp.where(dim < DA_QK_DIM, qt, 0.0), jnp.where(dim >= DA_QK_DIM, qt, 0.0)],
            axis=1).astype(BF16))

    def scores(j, s_ref):
        off = pl.multiple_of(j * tq, tq)
        for hd in heads:
            s_ref[hd] = _dot(k_ref[pl.ds(off, tq), hcols[hd]], qqt[hd])

    def accumulate(j, s_ref, masked):
        stats = []
        for hd in heads:
            st = s_ref[hd]
            if masked:
                key = lax.broadcasted_iota(jnp.int32, st.shape, 0)
                query = lax.broadcasted_iota(jnp.int32, st.shape, 1) % tq
                st = jnp.where(key <= query, st, NEG_BIG)
            m_old = m_scr[hd]
            m_new = jnp.maximum(m_old, jnp.max(st, axis=0, keepdims=True))
            alpha = jnp.exp2(m_old - m_new)
            p = jnp.exp2(st - m_new)
            l_scr[hd] = alpha * l_scr[hd] + jnp.sum(p, axis=0, keepdims=True)
            m_scr[hd] = m_new
            stats.append((alpha, p.astype(BF16)))
        for hd, (alpha, p) in zip(heads, stats):
            acc_scr[hd] = alpha * acc_scr[hd] + _dot(vt_scr[hd, j], p)

    m_scr[...] = jnp.full_like(m_scr, NEG_BIG)
    l_scr[...] = jnp.zeros_like(l_scr)
    acc_scr[...] = jnp.zeros_like(acc_scr)
    scores(0, s_a)

    def body(t, carry):
        scores(2 * t + 1, s_b)
        accumulate(2 * t, s_a, False)
        scores(2 * t + 2, s_a)
        accumulate(2 * t + 1, s_b, False)
        return carry

    lax.fori_loop(0, i // 2, body, 0)

    @pl.when(i % 2 == 1)
    def _():
        scores(i, s_b)
        accumulate(i - 1, s_a, False)
        accumulate(i, s_b, True)

    @pl.when(i % 2 == 0)
    def _():
        accumulate(i, s_a, True)

    lam = _da_lambda(lam_ref, lam_init)
    for hd in heads:
        ot = acc_scr[hd] / l_scr[hd]
        at = ot[:, :tq] - lam * ot[:, tq:]
        norm = at * lax.rsqrt(jnp.mean(at * at, axis=0, keepdims=True) + RMS_EPS)
        out_ref[:, hcols[hd]] = ((norm.T * gh_ref[hd]) * (1.0 - lam_init)).astype(out_ref.dtype)


def _da_prompt(dq, dk, dv, da_lambda, g_head3, B, T, tq, hp, lam_init):
    nq = T // tq
    rows = B * T
    w = hp * DA_V_DIM
    kv_spec = pl.BlockSpec((T, w), lambda b, h, i: (b, h))
    return pl.pallas_call(
        functools.partial(_da_prompt_kernel, lam_init=lam_init), grid=(B, DA_HEADS // hp, nq),
        in_specs=[pl.BlockSpec((tq, w), lambda b, h, i: (b * nq + i, h)), kv_spec, kv_spec,
                  pl.BlockSpec(da_lambda.shape, lambda b, h, i: (0, 0)),
                  pl.BlockSpec((hp, 1, DA_V_DIM), lambda b, h, i: (h, 0, 0))],
        out_specs=pl.BlockSpec((tq, w), lambda b, h, i: (b * nq + i, h)),
        out_shape=jax.ShapeDtypeStruct((rows, DA_WIDTH), BF16),
        scratch_shapes=[pltpu.VMEM((hp, nq, DA_V_DIM, tq), BF16),
                        pltpu.VMEM((hp, tq, 2 * tq), F32), pltpu.VMEM((hp, tq, 2 * tq), F32),
                        pltpu.VMEM((hp, 1, 2 * tq), F32), pltpu.VMEM((hp, 1, 2 * tq), F32),
                        pltpu.VMEM((hp, DA_V_DIM, 2 * tq), F32)],
        compiler_params=_params("parallel", "parallel", "arbitrary"), name="da_prompt")(
            dq, dk, dv, da_lambda, g_head3)


def _paged_fetch(pt_ref, ck_hbm, cv_hbm, kbuf, vbuf, sem, layer):
    b = pl.program_id(0)
    n_pages = kbuf.shape[1]

    def copies(bi, slot):
        out = []
        for p in range(n_pages):
            pg = pt_ref[bi, p]
            out.append(pltpu.make_async_copy(ck_hbm.at[layer, pg], kbuf.at[slot, p], sem.at[slot, 0]))
            out.append(pltpu.make_async_copy(cv_hbm.at[layer, pg], vbuf.at[slot, p], sem.at[slot, 1]))
        return out

    @pl.when(b == 0)
    def _():
        for cp in copies(0, 0):
            cp.start()

    @pl.when(b + 1 < pl.num_programs(0))
    def _():
        for cp in copies(b + 1, (b + 1) % 2):
            cp.start()

    slot = b % 2
    for cp in copies(b, slot):
        cp.wait()
    return slot


def _da_sample_phases(slot, q_ref, kn_ref, vn_ref, lam_ref, gh_ref, out_ref, kbuf, vbuf, lam_init):
    n_pages, page = kbuf.shape[1], kbuf.shape[2] // DA_HEADS
    T = q_ref.shape[1]
    scale = DA_QK_DIM ** -0.5
    lam = _da_lambda(lam_ref, lam_init)
    q_all = q_ref[0]
    n_rows = n_pages * page * DA_HEADS
    qq = jnp.concatenate([_stack_components(q_all[:, hh * DA_V_DIM:(hh + 1) * DA_V_DIM])
                          for hh in range(DA_HEADS)], axis=0)
    k_all = kbuf[slot].reshape(n_rows, DA_V_DIM).astype(BF16)
    s_raw = _dot_nt(qq.astype(BF16), k_all)
    yield
    rq = 2 * T * DA_HEADS
    own_head = (lax.broadcasted_iota(jnp.int32, (rq, n_rows), 1) % DA_HEADS
                == lax.broadcasted_iota(jnp.int32, (rq, n_rows), 0) // (2 * T))
    s_past = jnp.where(own_head, s_raw * scale, NEG_BIG)
    trow = lax.broadcasted_iota(jnp.int32, (rq, 1), 0) % T

    def per_query_row(new_ref, t):
        x = new_ref[0, t]
        return jnp.concatenate([jnp.broadcast_to(x[hh:hh + 1, :], (2 * T, DA_V_DIM))
                                for hh in range(DA_HEADS)], axis=0)

    s_new = [jnp.where(trow >= t,
                       jnp.sum(qq * per_query_row(kn_ref, t), axis=1, keepdims=True) * scale, NEG_BIG)
             for t in range(T)]
    m = jnp.max(s_past, axis=1, keepdims=True)
    for t in range(T):
        m = jnp.maximum(m, s_new[t])
    p_past = jnp.exp(s_past - m)
    l = jnp.sum(p_past, axis=1, keepdims=True)
    v_all = vbuf[slot].reshape(n_rows, DA_V_DIM).astype(BF16)
    yield
    acc = _dot(p_past.astype(BF16), v_all)
    yield
    for t in range(T):
        p_t = jnp.exp(s_new[t] - m)
        l = l + p_t
        acc = acc + p_t * per_query_row(vn_ref, t)
    o = acc / l
    for hh in range(DA_HEADS):
        r0 = hh * 2 * T
        a = o[r0:r0 + T] - lam * o[r0 + T:r0 + 2 * T]
        out_ref[0, :, hh * DA_V_DIM:(hh + 1) * DA_V_DIM] = (
            _rms(a, gh_ref[hh:hh + 1, :]) * (1.0 - lam_init)).astype(out_ref.dtype)


def _da_sample_mlstm_prompt_kernel(
        pt_ref, q_ref, kn_ref, vn_ref, lam_ref, gh_ref, ck_hbm, cv_hbm,
        mq_ref, mk_ref, mv_ref, mo_ref, gc_ref, gr_ref, mgh_ref,
        sq_ref, sk_ref, sv_ref, so_ref, sgc_ref, sgr_ref, sc_in, sn_in, sm_in,
        out_ref, h_ref, c_out, n_out, m_out, sh_ref, sc_out, sn_out, sm_out,
        kbuf, vbuf, sem, c_scr, n_scr, m_scr, *, lam_init, layer, chunks):
    i = pl.program_id(0)

    @pl.when(i == 0)
    def _():
        c_scr[...] = jnp.zeros_like(c_scr)
        n_scr[...] = jnp.zeros_like(n_scr)
        m_scr[...] = jnp.zeros_like(m_scr)

    slot = _paged_fetch(pt_ref, ck_hbm, cv_hbm, kbuf, vbuf, sem, layer)
    _run_phases(
        _da_sample_phases(slot, q_ref, kn_ref, vn_ref, lam_ref, gh_ref, out_ref, kbuf, vbuf, lam_init),
        _mlstm_prompt_phases(i % chunks == 0, mq_ref, mk_ref, mv_ref, mo_ref, gc_ref, gr_ref, mgh_ref,
                             h_ref, c_out, n_out, m_out, c_scr, n_scr, m_scr),
        _mlstm_sample_phases(sq_ref, sk_ref, sv_ref, so_ref, sgc_ref, sgr_ref, mgh_ref, sc_in, sn_in, sm_in,
                             sh_ref, sc_out, sn_out, sm_out))


def _da_sample_mlstm_prompt(page_table, dq, dk, dv, da_lambda, g_da, cache_k, cache_v, lam_init, layer,
                            mq, mk, mv, mo, gc, gr, g_ml, Bp, Tp, sq, sk, sv, so, sgc, sgr, c0, n0, m0):
    B, T, _ = dq.shape
    rows = Bp * Tp
    assert rows % B == 0 and Tp % (rows // B) == 0, "one mLSTM chunk per sample batch element"
    L = rows // B
    chunks = Tp // L
    H, D = ML_HEADS, ML_HEAD_DIM
    n_pages = page_table.shape[1]
    page_rows = cache_k.shape[2]
    blk = pl.BlockSpec((1, T, DA_WIDTH), lambda i, pt: (i, 0, 0))
    kv_blk = pl.BlockSpec((1, T, DA_HEADS, DA_V_DIM), lambda i, pt: (i, 0, 0, 0))
    const = lambda a: pl.BlockSpec(a.shape, lambda i, pt: (0,) * a.ndim)
    mblk = lambda w: pl.BlockSpec((L, w), lambda i, pt: (i, 0))
    state = lambda *s: pl.BlockSpec((1, H) + s, lambda i, pt: (i // chunks, 0, 0, 0))
    s3 = lambda *s: pl.BlockSpec((1,) + s, lambda i, pt: (i, 0, 0))
    s4 = lambda *s: pl.BlockSpec((1, H) + s, lambda i, pt: (i, 0, 0, 0))
    sample_state = [s4(D, D), s4(1, D), s4(1, 1)]
    grid_spec = pltpu.PrefetchScalarGridSpec(
        num_scalar_prefetch=1, grid=(B,),
        in_specs=[blk, kv_blk, kv_blk, const(da_lambda), const(g_da),
                  pl.BlockSpec(memory_space=pl.ANY), pl.BlockSpec(memory_space=pl.ANY),
                  mblk(ML_WIDTH), mblk(ML_WIDTH), mblk(ML_WIDTH), mblk(ML_WIDTH), mblk(LANES),
                  pl.BlockSpec((8, L), lambda i, pt: (0, i)), const(g_ml),
                  s3(T, ML_WIDTH), s3(T, ML_WIDTH), s3(T, ML_WIDTH), s3(T, ML_WIDTH), s3(T, LANES), s3(8, T)]
                 + sample_state,
        out_specs=[blk, mblk(ML_WIDTH), state(D, D), state(1, D), state(1, 1), s3(T, ML_WIDTH)] + sample_state,
        scratch_shapes=[pltpu.VMEM((2, n_pages, page_rows, DA_V_DIM), F32),
                        pltpu.VMEM((2, n_pages, page_rows, DA_V_DIM), F32),
                        pltpu.SemaphoreType.DMA((2, 2)),
                        pltpu.VMEM((H, D, D), F32), pltpu.VMEM((H, 1, D), F32), pltpu.VMEM((H, 1, 1), F32)])
    return pl.pallas_call(
        functools.partial(_da_sample_mlstm_prompt_kernel, lam_init=lam_init, layer=layer, chunks=chunks),
        grid_spec=grid_spec,
        out_shape=[jax.ShapeDtypeStruct((B, T, DA_WIDTH), BF16),
                   jax.ShapeDtypeStruct((rows, ML_WIDTH), BF16),
                   jax.ShapeDtypeStruct((Bp, H, D, D), F32),
                   jax.ShapeDtypeStruct((Bp, H, 1, D), F32),
                   jax.ShapeDtypeStruct((Bp, H, 1, 1), F32),
                   jax.ShapeDtypeStruct((B, T, ML_WIDTH), BF16),
                   jax.ShapeDtypeStruct((B, H, D, D), F32),
                   jax.ShapeDtypeStruct((B, H, 1, D), F32),
                   jax.ShapeDtypeStruct((B, H, 1, 1), F32)],
        compiler_params=_params("arbitrary"), name="da_sample_mlstm_prompt")(
            page_table, dq, dk, dv, da_lambda, g_da, cache_k, cache_v, mq, mk, mv, mo, gc, gr, g_ml,
            sq, sk, sv, so, sgc, sgr, c0, n0, m0)


def _proj_norm_kernel(*refs, n_in, has_next):
    a_refs, w_refs = refs[:n_in], refs[n_in:2 * n_in]
    x_ref, gpost_ref, gpre_ref = refs[2 * n_in:2 * n_in + 3]
    rest = refs[2 * n_in + 3:]
    wn_ref = rest[0] if has_next else None
    xo_ref, ho_ref = rest[-2:]
    acc = _dot(a_refs[0][...], w_refs[0][...])
    for a, w in zip(a_refs[1:], w_refs[1:]):
        acc = acc + _dot(a[...], w[...])
    x1 = x_ref[...] + _rms(acc, gpost_ref[...])
    xo_ref[...] = x1
    hn = _rms(x1, gpre_ref[...]).astype(BF16)
    ho_ref[...] = (_dot(hn, wn_ref[...]) if has_next else hn).astype(ho_ref.dtype)


def _proj_norm(a_list, w_list, x, g_post, g_pre, w_next, tm, h_dtype=BF16):
    rows, d = x.shape
    n_in = len(a_list)
    has_next = w_next is not None
    row_spec = lambda w: pl.BlockSpec((tm, w), lambda i: (i, 0))
    full = lambda a: pl.BlockSpec(a.shape, lambda i: (0,) * a.ndim)
    ins = list(a_list) + list(w_list) + [x, g_post, g_pre] + ([w_next] if has_next else [])
    in_specs = ([row_spec(a.shape[1]) for a in a_list] + [full(w) for w in w_list]
                + [row_spec(d), full(g_post), full(g_pre)] + ([full(w_next)] if has_next else []))
    n_out = w_next.shape[1] if has_next else d
    return pl.pallas_call(
        functools.partial(_proj_norm_kernel, n_in=n_in, has_next=has_next), grid=(rows // tm,),
        in_specs=in_specs, out_specs=[row_spec(d), row_spec(n_out)],
        out_shape=[jax.ShapeDtypeStruct((rows, d), F32), jax.ShapeDtypeStruct((rows, n_out), h_dtype)],
        compiler_params=_params("parallel"), name="proj_norm")(*ins)


def _mem_kv_kernel(x_ref, g_ref, wk_ref, wv_ref, k_ref, v_ref, kb_ref, vb_ref):
    h = _rms(x_ref[...], g_ref[...]).astype(BF16)
    k = _dot(h, wk_ref[...])
    v = _dot(h, wv_ref[...])
    k_ref[...], v_ref[...] = k, v
    kb_ref[...], vb_ref[...] = k.astype(BF16), v.astype(BF16)


def _mem_kv(x, g, wk, wv, tm):
    rows, d = x.shape
    n = wk.shape[1]
    row_spec = lambda w: pl.BlockSpec((tm, w), lambda i: (i, 0))
    return pl.pallas_call(
        _mem_kv_kernel, grid=(rows // tm,),
        in_specs=[row_spec(d), _resident(g), _resident(wk), _resident(wv)],
        out_specs=[row_spec(n)] * 4,
        out_shape=[jax.ShapeDtypeStruct((rows, n), F32)] * 2 + [jax.ShapeDtypeStruct((rows, n), BF16)] * 2,
        compiler_params=_params("parallel"), name="mem_kv")(x, g, wk, wv)


def _mid_phases(rows, hml_ref, hda_ref, x_ref, woml_ref, woda_ref, gpost1_ref, gpre1_ref, wq_ref,
                mk_ref, mv_ref, wmo_ref, gpost2_ref, gpre2_ref, x2_ref, hf_ref):
    acc = _dot(hml_ref[rows, :], woml_ref[...]) + _dot(hda_ref[rows, :], woda_ref[...])
    yield
    x1 = x_ref[rows, :] + _rms(acc, gpost1_ref[...])
    qm = _dot(_rms(x1, gpre1_ref[...]).astype(BF16), wq_ref[...]).astype(BF16)
    yield
    hd = qm.shape[1] // MEM_HEADS
    hcols = [slice(hh * hd, (hh + 1) * hd) for hh in range(MEM_HEADS)]
    s = [_dot_nt(qm[:, c], mk_ref[0, :, c]) * (hd ** -0.5) for c in hcols]
    yield
    p = [jnp.exp(x - jnp.max(x, axis=1, keepdims=True)) for x in s]
    pv = [_dot(x.astype(BF16), mv_ref[0, :, c]) for x, c in zip(p, hcols)]
    yield
    o = jnp.concatenate([(a / jnp.sum(x, axis=1, keepdims=True)).astype(BF16) for a, x in zip(pv, p)], axis=1)
    acc2 = _dot(o, wmo_ref[...])
    yield
    x2 = x1 + _rms(acc2, gpost2_ref[...])
    x2_ref[rows, :] = x2
    hf_ref[rows, :] = _rms(x2, gpre2_ref[...]).astype(hf_ref.dtype)


def _cache_fetch(mk_hbm, mv_hbm, kbuf, vbuf, sem, layer):
    g = pl.program_id(0)
    bb = kbuf.shape[1]

    def copies(gi, slot):
        out = []
        for bi in range(bb):
            for hh in range(MEM_HEADS):
                b = gi * bb + bi
                out.append(pltpu.make_async_copy(mk_hbm.at[layer, b, :, hh, :], kbuf.at[slot, bi, hh],
                                                 sem.at[slot, 0]))
                out.append(pltpu.make_async_copy(mv_hbm.at[layer, b, :, hh, :], vbuf.at[slot, bi, hh],
                                                 sem.at[slot, 1]))
        return out

    @pl.when(g == 0)
    def _():
        for cp in copies(0, 0):
            cp.start()

    @pl.when(g + 1 < pl.num_programs(0))
    def _():
        for cp in copies(g + 1, (g + 1) % 2):
            cp.start()

    slot = g % 2
    for cp in copies(g, slot):
        cp.wait()
    return slot


def _mem_attn_cache_phases(slot, q_ref, o_ref, kbuf, vbuf):
    bb, hd = q_ref.shape[0], kbuf.shape[-1]
    items = [(bi, hh) for bi in range(bb) for hh in range(MEM_HEADS)]
    cols = lambda hh: slice(hh * hd, (hh + 1) * hd)
    s = [_dot_nt(q_ref[bi, :, cols(hh)].astype(BF16), kbuf[slot, bi, hh].astype(BF16)) * (hd ** -0.5)
         for bi, hh in items]
    yield
    p = [jnp.exp(x - jnp.max(x, axis=1, keepdims=True)) for x in s]
    pv = [_dot(x.astype(BF16), vbuf[slot, bi, hh].astype(BF16)) for x, (bi, hh) in zip(p, items)]
    yield
    for (bi, hh), a, x in zip(items, pv, p):
        o_ref[bi, :, cols(hh)] = (a / jnp.sum(x, axis=1, keepdims=True)).astype(o_ref.dtype)


def _mid_kernel(*refs, n_sub, layer):
    (hml_ref, hda_ref, x_ref, woml_ref, woda_ref, gpost1_ref, gpre1_ref, wq_ref, mk_ref, mv_ref, wmo_ref,
     gpost2_ref, gpre2_ref, qs_ref, ck_hbm, cv_hbm, x2_ref, hf_ref, os_ref, kbuf, vbuf, sem) = refs
    slot = _cache_fetch(ck_hbm, cv_hbm, kbuf, vbuf, sem, layer)
    sub = x_ref.shape[0] // n_sub
    prompt = refs[:13] + (x2_ref, hf_ref)
    _run_phases(_mem_attn_cache_phases(slot, qs_ref, os_ref, kbuf, vbuf),
                *[_mid_phases(slice(j * sub, (j + 1) * sub), *prompt) for j in range(n_sub)])


def _mid(h_ml, h_da, x, wo_ml, wo_da, g_post1, g_pre1, wq, mk, mv, wmo, g_post2, g_pre2, T, tm,
         q_s, cache_k, cache_v, layer):
    rows, d = x.shape
    tiles = T // tm
    steps = rows // tm
    Bs, Ts, _ = q_s.shape
    assert Bs % steps == 0, "one group of sample batch elements per prompt row tile"
    bb = Bs // steps
    M, H, hd = cache_k.shape[2:]
    row_spec = lambda w: pl.BlockSpec((tm, w), lambda i: (i, 0))
    kv_spec = pl.BlockSpec((1,) + mk.shape[1:], lambda i: (i // tiles, 0, 0))
    grp_spec = pl.BlockSpec((bb, Ts, d), lambda i: (i, 0, 0))
    hbm = pl.BlockSpec(memory_space=pl.ANY)
    ins = (h_ml, h_da, x, wo_ml, wo_da, g_post1, g_pre1, wq, mk, mv, wmo, g_post2, g_pre2)
    in_specs = [row_spec(h_ml.shape[1]), row_spec(h_da.shape[1]), row_spec(d)] + [
        kv_spec if a is mk or a is mv else _resident(a) for a in ins[3:]] + [grp_spec, hbm, hbm]
    return pl.pallas_call(
        functools.partial(_mid_kernel, n_sub=2, layer=layer), grid=(steps,),
        in_specs=in_specs, out_specs=[row_spec(d), row_spec(d), grp_spec],
        out_shape=[jax.ShapeDtypeStruct((rows, d), F32), jax.ShapeDtypeStruct((rows, d), BF16),
                   jax.ShapeDtypeStruct((Bs, Ts, d), BF16)],
        scratch_shapes=[pltpu.VMEM((2, bb, H, M, hd), F32), pltpu.VMEM((2, bb, H, M, hd), F32),
                        pltpu.SemaphoreType.DMA((2, 2))],
        compiler_params=_params("arbitrary"), name="mid")(*ins, q_s, cache_k, cache_v)


def _ffn_chunks(nh):
    nchunk = 2 if (nh // LANES) % 2 == 0 else 1
    cw = nh // nchunk
    return [(j * cw, cw) for j in range(nchunk)]


def _ffn_prompt_kernel(hf_ref, x_ref, wup_ref, wdw_ref, bdw_ref, wdn_ref, gpost_ref, y_ref, ulast_ref, ubuf,
                       *, tiles_per_seq):
    i = pl.program_id(0)
    tm = hf_ref.shape[0]
    nh = wdn_ref.shape[0]
    halo = ubuf.shape[0] - tm

    @pl.when(i % tiles_per_seq == 0)
    def _():
        ubuf[0:halo, :] = jnp.zeros((halo, ubuf.shape[1]), F32)

    hf = hf_ref[...]
    f = jnp.zeros((tm, y_ref.shape[1]), F32)
    for c0, cw in _ffn_chunks(nh):
        cg = []
        for base in (c0, nh + c0):
            cs = slice(base, base + cw)
            ubuf[halo:halo + tm, cs] = _dot(hf, wup_ref[:, cs])
            c = bdw_ref[:, cs]
            for j in range(CONV_W):
                lo = halo - (CONV_W - 1) + j
                c = c + ubuf[lo:lo + tm, cs] * wdw_ref[j:j + 1, cs]
            cg.append(c)
        act = (jax.nn.silu(cg[1]) * cg[0]).astype(BF16)
        f = f + _dot(act, wdn_ref[c0:c0 + cw, :])
    y_ref[...] = x_ref[...] + _rms(f, gpost_ref[...])
    tail = ubuf[tm:tm + halo, :]
    ubuf[0:halo, :] = tail
    ulast_ref[0] = tail


def _ffn_prompt(hf, x, w_up, w_dw, b_dw, w_down, g_post, B, T, tm):
    rows, d = x.shape
    npad = w_up.shape[1]
    tiles = T // tm
    halo = 8
    row_spec = lambda w: pl.BlockSpec((tm, w), lambda i: (i, 0))
    full = _resident
    return pl.pallas_call(
        functools.partial(_ffn_prompt_kernel, tiles_per_seq=tiles), grid=(rows // tm,),
        in_specs=[row_spec(d), row_spec(d), full(w_up), full(w_dw), full(b_dw), full(w_down), full(g_post)],
        out_specs=[row_spec(d), pl.BlockSpec((1, halo, npad), lambda i: (i // tiles, 0, 0))],
        out_shape=[jax.ShapeDtypeStruct((rows, d), F32), jax.ShapeDtypeStruct((B, halo, npad), F32)],
        scratch_shapes=[pltpu.VMEM((tm + halo, npad), F32)],
        compiler_params=_params("arbitrary"), name="ffn_prompt")(hf, x, w_up, w_dw, b_dw, w_down, g_post)


def _ffn_sample_kernel(hf_ref, x_ref, cb_ref, wup_ref, wdw_ref, bdw_ref, wdn_ref, gpost_ref, y_ref, unew_ref,
                       *, T):
    nb = hf_ref.shape[0] // T
    nh = wdn_ref.shape[0]
    hf = hf_ref[...]
    f = jnp.zeros(y_ref.shape, F32)
    for c0, cw in _ffn_chunks(nh):
        cg = []
        for base in (c0, nh + c0):
            cs = slice(base, base + cw)
            u = _dot(hf, wup_ref[:, cs])
            ext = [cb_ref[j, :, cs] for j in range(CONV_W - 1)] + [u[t * nb:(t + 1) * nb] for t in range(T)]
            for j in range(CONV_W - 1):
                unew_ref[j, :, cs] = ext[len(ext) - (CONV_W - 1) + j]
            rows = []
            for t in range(T):
                c = bdw_ref[:, cs]
                for j in range(CONV_W):
                    c = c + ext[t + j] * wdw_ref[j:j + 1, cs]
                rows.append(c)
            cg.append(jnp.concatenate(rows, axis=0))
        act = (jax.nn.silu(cg[1]) * cg[0]).astype(BF16)
        f = f + _dot(act, wdn_ref[c0:c0 + cw, :])
    y_ref[...] = x_ref[...] + _rms(f, gpost_ref[...])


def _ffn_sample(hf, x, cb, w_up, w_dw, b_dw, w_down, g_post, T):
    rows, d = x.shape
    npad = w_up.shape[1]
    ins = (hf, x, cb, w_up, w_dw, b_dw, w_down, g_post)
    full = lambda a: pl.BlockSpec(a.shape, lambda i: (0,) * a.ndim)
    return pl.pallas_call(
        functools.partial(_ffn_sample_kernel, T=T), grid=(1,),
        in_specs=[full(a) for a in ins],
        out_specs=[pl.BlockSpec((rows, d), lambda i: (0, 0)),
                   pl.BlockSpec((CONV_W - 1, rows // T, npad), lambda i: (0, 0, 0))],
        out_shape=[jax.ShapeDtypeStruct((rows, d), F32),
                   jax.ShapeDtypeStruct((CONV_W - 1, rows // T, npad), F32)],
        compiler_params=_params("arbitrary"), name="ffn_sample")(*ins)


def _pad_halves(a, nh, nh_pad):
    pad = [(0, 0)] * (a.ndim - 1) + [(0, nh_pad - nh)]
    return jnp.concatenate([jnp.pad(a[..., :nh], pad), jnp.pad(a[..., nh:], pad)], axis=-1)


def _unpad_halves(a, nh, nh_pad):
    return jnp.concatenate([a[..., :nh], a[..., nh_pad:nh_pad + nh]], axis=-1)


def _row_tile(rows, want):
    t = min(rows, want)
    while rows % t:
        t //= 2
    return t


def kernel(x_prompt, x_sample, cache_dk, cache_dv, cache_mem_k, cache_mem_v, state_ml_C, state_ml_n, state_ml_m, state_conv, page_table, mem_prompt, g_mix_pre, g_mix_post, w_in, b_if, g_ml_head, da_lambda, g_da_head, w_out, g_mem_pre, g_mem_post, g_mem_src, w_mq, w_mk, w_mv, w_mo, g_ffn_pre, g_ffn_post, w_up, w_dw, b_dw, w_down):
    depth = w_in.shape[0]
    Bp, Tp, d = x_prompt.shape
    Bs, Ts, _ = x_sample.shape
    n_pages, page = page_table.shape[1], cache_dk.shape[2]
    past_len = n_pages * page
    n_mem = mem_prompt.shape[1]
    nh = w_down.shape[1]
    nh_pad = -(-nh // LANES) * LANES
    H, D = ML_HEADS, ML_HEAD_DIM
    rows_p, rows_s = Bp * Tp, Bs * Ts

    tm_p = _row_tile(Tp, 512)
    tm_s = _row_tile(rows_s, 256)
    tab_p = _rope_tables(Tp, Tp, 0)
    tab_s = _rope_tables(tm_s, Ts, past_len)

    yp = x_prompt.reshape(rows_p, d)
    ys = x_sample.reshape(rows_s, d)
    outs = [[] for _ in range(14)]
    row = lambda a: a.reshape(1, -1)
    for l in range(depth):
        lam_init = 0.8 - 0.6 * math.exp(-0.3 * l)
        wi = w_in[l]
        w_ml = wi[:, :4 * ML_WIDTH].astype(BF16)
        w_da = wi[:, 4 * ML_WIDTH + 2 * H:].astype(BF16)
        w_gate = wi[:, 4 * ML_WIDTH:4 * ML_WIDTH + 2 * H]
        w_g = jnp.pad(w_gate, ((0, 0), (0, LANES - 2 * H))).astype(BF16)
        w_gt = w_gate.T.astype(BF16)
        b_col = jnp.pad(b_if[l], (0, LANES - 2 * H)).reshape(1, LANES)
        b_row = b_if[l].reshape(2 * H, 1)
        wo_ml, wo_da = w_out[l][:ML_WIDTH].astype(BF16), w_out[l][ML_WIDTH:].astype(BF16)
        wq_b, wo_b = w_mq[l].astype(BF16), w_mo[l].astype(BF16)
        wk_b, wv_b = w_mk[l].astype(BF16), w_mv[l].astype(BF16)
        wup_b = _pad_halves(w_up[l], nh, nh_pad).astype(BF16)
        wdw_p = _pad_halves(w_dw[l], nh, nh_pad)
        bdw_p = _pad_halves(b_dw[l].reshape(1, -1), nh, nh_pad)
        wdn_b = jnp.pad(w_down[l], ((0, nh_pad - nh), (0, 0))).astype(BF16)
        g_da3 = g_da_head[l].reshape(DA_HEADS, 1, DA_V_DIM)

        q, k, v, o, gc, gr, dq, dk, dv, dkb, dvb = _in_proj(
            yp, row(g_mix_pre[l]), w_ml, w_da, w_g, w_gt, b_col, b_row, tab_p, BF16, _row_tile(Tp, 1024),
            DA_QK_DIM ** -0.5 * math.log2(math.e))
        qs, ks, vs, os_, gcs, grs, dqs, dk_s, dv_s, _, _ = _in_proj(
            ys, row(g_mix_pre[l]), w_ml, w_da, w_g, w_gt, b_col, b_row, tab_s, F32, tm_s, 1.0)
        r3 = lambda a: a.reshape(Bs, Ts, a.shape[-1])
        r4 = lambda a: a.reshape(Bs, Ts, DA_HEADS, DA_V_DIM)
        gr3 = grs.reshape(2 * H, Bs, Ts).transpose(1, 0, 2)
        h_da_s, h_ml, C_p, n_p, m_p, h_ml_s, C_s, n_s, m_s = _da_sample_mlstm_prompt(
            page_table, r3(dqs), r4(dk_s), r4(dv_s), da_lambda[l], g_da_head[l],
            cache_dk.reshape(depth, -1, page * DA_HEADS, DA_V_DIM),
            cache_dv.reshape(depth, -1, page * DA_HEADS, DA_V_DIM), lam_init, l,
            q, k, v, o, gc, gr, g_ml_head[l], Bp, Tp,
            r3(qs), r3(ks), r3(vs), r3(os_), r3(gcs), gr3,
            state_ml_C[l], state_ml_n[l].reshape(Bs, H, 1, D), state_ml_m[l].reshape(Bs, H, 1, 1))

        x1_s, qm_s = _proj_norm([h_ml_s.reshape(rows_s, ML_WIDTH), h_da_s.reshape(rows_s, DA_WIDTH)],
                                [wo_ml, wo_da], ys, row(g_mix_post[l]), row(g_mem_pre[l]), wq_b, tm_s,
                                h_dtype=F32)

        h_da = _da_prompt(dq, dkb, dvb, da_lambda[l], g_da3, Bp, Tp, _row_tile(Tp, 512), 2, lam_init)
        mk_p, mv_p, mkb, mvb = _mem_kv(mem_prompt.reshape(Bp * n_mem, d), row(g_mem_src[l]), wk_b, wv_b,
                                       _row_tile(Bp * n_mem, 256))
        x2, hf, om_s = _mid(h_ml, h_da, yp, wo_ml, wo_da, row(g_mix_post[l]), row(g_mem_pre[l]), wq_b,
                            mkb.reshape(Bp, n_mem, d), mvb.reshape(Bp, n_mem, d), wo_b,
                            row(g_mem_post[l]), row(g_ffn_pre[l]), Tp, tm_p,
                            qm_s.reshape(Bs, Ts, d), cache_mem_k, cache_mem_v, l)
        yp, ulast = _ffn_prompt(hf, x2, wup_b, wdw_p, bdw_p, wdn_b, row(g_ffn_post[l]), Bp, Tp, tm_p)
        cv_p = _unpad_halves(ulast[:, ulast.shape[1] - (CONV_W - 1):], nh, nh_pad)

        x2, hf = _proj_norm([om_s.reshape(rows_s, d)], [wo_b], x1_s, row(g_mem_post[l]), row(g_ffn_pre[l]),
                            None, tm_s)
        tmaj = lambda a: a.reshape(Bs, Ts, -1).transpose(1, 0, 2).reshape(rows_s, -1)
        cb = _pad_halves(state_conv[l], nh, nh_pad).transpose(1, 0, 2)
        y_t, unew = _ffn_sample(tmaj(hf), tmaj(x2), cb, wup_b, wdw_p, bdw_p, wdn_b, row(g_ffn_post[l]), Ts)
        ys = y_t.reshape(Ts, Bs, d).transpose(1, 0, 2).reshape(rows_s, d)
        cv_s = _unpad_halves(unew.transpose(1, 0, 2), nh, nh_pad)

        vals = (dk.reshape(Bp, Tp, DA_HEADS, DA_V_DIM), dv.reshape(Bp, Tp, DA_HEADS, DA_V_DIM),
                mk_p.reshape(Bp, n_mem, MEM_HEADS, d // MEM_HEADS), mv_p.reshape(Bp, n_mem, MEM_HEADS, d // MEM_HEADS),
                C_p, n_p.reshape(Bp, H, D), m_p.reshape(Bp, H), cv_p,
                dk_s.reshape(Bs, Ts, DA_HEADS, DA_V_DIM), dv_s.reshape(Bs, Ts, DA_HEADS, DA_V_DIM),
                C_s, n_s.reshape(Bs, H, D), m_s.reshape(Bs, H), cv_s)
        for acc, val in zip(outs, vals):
            acc.append(val)
    return (yp.reshape(Bp, Tp, d), ys.reshape(Bs, Ts, d)) + tuple(jnp.stack(a) for a in outs)
```

```python
import functools
import math

import jax
import jax.numpy as jnp
from jax import lax
from jax.experimental import pallas as pl
from jax.experimental.pallas import tpu as pltpu

F32 = jnp.float32
BF16 = jnp.bfloat16

ML_HEADS = 4
ML_HEAD_DIM = 128
ML_WIDTH = ML_HEADS * ML_HEAD_DIM
DA_HEADS = 4
DA_V_DIM = 128
DA_QK_DIM = 64
DA_WIDTH = DA_HEADS * DA_V_DIM
ROPE_DIM = 16
ROPE_THETA = 500000.0
MEM_HEADS = 4
CONV_W = 3
RMS_EPS = 1e-6
LANES = 128
NEG_BIG = -1e30
VMEM_LIMIT = 56 * 1024 * 1024


def _params(*sem, vmem=VMEM_LIMIT):
    return pltpu.CompilerParams(dimension_semantics=sem, vmem_limit_bytes=vmem)


def _resident(a):
    return pl.BlockSpec(a.shape, lambda *_: (0,) * a.ndim, pipeline_mode=pl.Buffered(1))


def _rms(x, g):
    return x * lax.rsqrt(jnp.mean(x * x, axis=-1, keepdims=True) + RMS_EPS) * g


def _log_sigmoid(x):
    return jnp.minimum(x, 0.0) - jnp.log1p(jnp.exp(-jnp.abs(x)))


def _dot(a, b):
    return jnp.dot(a, b, preferred_element_type=F32)


def _dot_nt(a, b):
    return lax.dot_general(a, b, (((1,), (1,)), ((), ())), preferred_element_type=F32)


def _dot_tn(a, b):
    return lax.dot_general(a, b, (((0,), (0,)), ((), ())), preferred_element_type=F32)


def _rope_table_kernel(cos_ref, sa_ref, sb_ref, *, period, offset):
    rows = cos_ref.shape[0]
    half = ROPE_DIM // 2
    r = lax.broadcasted_iota(jnp.int32, (rows, LANES), 0) + pl.program_id(0) * rows
    lane = lax.broadcasted_iota(jnp.int32, (rows, LANES), 1)
    pos = (offset + r % period).astype(F32)
    c = lane % DA_QK_DIM
    j = (c % half).astype(F32)
    inv = jnp.exp(-math.log(ROPE_THETA) * (2.0 * j / ROPE_DIM))
    ang = pos * inv
    cos, sin = jnp.cos(ang), jnp.sin(ang)
    cos_ref[...] = jnp.where(c < ROPE_DIM, cos, 1.0)
    sa_ref[...] = jnp.where(c < half, -sin, 0.0)
    sb_ref[...] = jnp.where((c >= half) & (c < ROPE_DIM), sin, 0.0)


def _rope_tables(rows, period, offset):
    blk = min(rows, 512)
    spec = pl.BlockSpec((blk, LANES), lambda i: (i, 0))
    shp = jax.ShapeDtypeStruct((rows, LANES), F32)
    return pl.pallas_call(
        functools.partial(_rope_table_kernel, period=period, offset=offset),
        grid=(rows // blk,), out_specs=[spec] * 3, out_shape=[shp] * 3,
        compiler_params=_params("parallel"), name="rope_tables")()


def _rope(x, cos, sa, sb):
    outs = []
    for j in range(x.shape[1] // LANES):
        xj = x[:, j * LANES:(j + 1) * LANES]
        up = pltpu.roll(xj, LANES - ROPE_DIM // 2, axis=1)
        dn = pltpu.roll(xj, ROPE_DIM // 2, axis=1)
        outs.append(xj * cos + up * sa + dn * sb)
    return outs


def _in_proj_kernel(x_ref, g_ref, w_ref, wda_ref, wg_ref, wgt_ref, bcol_ref, brow_ref, cos_ref, sa_ref, sb_ref,
                    q_ref, k_ref, v_ref, o_ref, gc_ref, gr_ref, dq_ref, dk_ref, dv_ref, dkb_ref, dvb_ref,
                    *, dq_scale, n_sub):
    sub = x_ref.shape[0] // n_sub
    for u in range(n_sub):
        rows = slice(u * sub, (u + 1) * sub)
        h = _rms(x_ref[rows, :], g_ref[...]).astype(BF16)
        for j, ref in enumerate((q_ref, k_ref, v_ref, o_ref)):
            ref[rows, :] = _dot(h, w_ref[:, j * ML_WIDTH:(j + 1) * ML_WIDTH]).astype(ref.dtype)
        gc = _dot(h, wg_ref[...]) + bcol_ref[...]
        lane = lax.broadcasted_iota(jnp.int32, gc.shape, 1)
        gc_ref[rows, :] = jnp.where(lane < ML_HEADS, gc, _log_sigmoid(gc))
        gr = _dot_nt(wgt_ref[...], h) + brow_ref[...]
        row = lax.broadcasted_iota(jnp.int32, gr.shape, 0)
        gr_ref[:, rows] = jnp.where(row < ML_HEADS, gr, _log_sigmoid(gr))
        cos, sa, sb = cos_ref[rows, :], sa_ref[rows, :], sb_ref[rows, :]
        dq = _rope(_dot(h, wda_ref[:, :DA_WIDTH]), cos, sa, sb)
        dk = _rope(_dot(h, wda_ref[:, DA_WIDTH:2 * DA_WIDTH]), cos, sa, sb)
        dv = _dot(h, wda_ref[:, 2 * DA_WIDTH:3 * DA_WIDTH])
        dvb_ref[rows, :] = dv.astype(BF16)
        for j in range(DA_HEADS):
            cols = slice(j * LANES, (j + 1) * LANES)
            dq_ref[rows, cols] = (dq[j] * dq_scale).astype(dq_ref.dtype)
            dk_ref[rows, j, :] = dk[j]
            dkb_ref[rows, cols] = dk[j].astype(BF16)
            dv_ref[rows, j, :] = dv[:, cols]


def _in_proj(x, g, w_ml, w_da, w_g, w_gt, b_col, b_row, tables, act_dtype, tm, dq_scale):
    rows, d = x.shape
    cos, sa, sb = tables
    nt = cos.shape[0] // tm
    row_spec = lambda w: pl.BlockSpec((tm, w), lambda i: (i, 0))
    full = lambda a: pl.BlockSpec(a.shape, lambda i: (0,) * a.ndim)
    tab_spec = pl.BlockSpec((tm, LANES), lambda i: (i % nt, 0))
    out_shape = [jax.ShapeDtypeStruct((rows, ML_WIDTH), act_dtype)] * 4 + [
        jax.ShapeDtypeStruct((rows, LANES), F32), jax.ShapeDtypeStruct((8, rows), F32),
        jax.ShapeDtypeStruct((rows, DA_WIDTH), act_dtype),
        jax.ShapeDtypeStruct((rows, DA_HEADS, DA_V_DIM), F32), jax.ShapeDtypeStruct((rows, DA_HEADS, DA_V_DIM), F32),
        jax.ShapeDtypeStruct((rows, DA_WIDTH), BF16), jax.ShapeDtypeStruct((rows, DA_WIDTH), BF16)]
    kv_spec = pl.BlockSpec((tm, DA_HEADS, DA_V_DIM), lambda i: (i, 0, 0))
    out_specs = [row_spec(ML_WIDTH)] * 4 + [row_spec(LANES), pl.BlockSpec((8, tm), lambda i: (0, i)),
                                            row_spec(DA_WIDTH), kv_spec, kv_spec,
                                            row_spec(DA_WIDTH), row_spec(DA_WIDTH)]
    return pl.pallas_call(
        functools.partial(_in_proj_kernel, dq_scale=dq_scale, n_sub=1),
        grid=(rows // tm,),
        in_specs=[row_spec(d), full(g), full(w_ml), full(w_da), full(w_g), full(w_gt), full(b_col), full(b_row),
                  tab_spec, tab_spec, tab_spec],
        out_specs=out_specs, out_shape=out_shape,
        compiler_params=_params("parallel"), name="in_proj")(
            x, g, w_ml, w_da, w_g, w_gt, b_col, b_row, cos, sa, sb)


def _run_phases(*gens):
    live = list(gens)
    while live:
        for g in list(live):
            try:
                next(g)
            except StopIteration:
                live.remove(g)


def _mlstm_phases(heads, out):
    L, D = heads[0][0].shape
    scale = D ** -0.5
    qb = [h[0].astype(BF16) for h in heads]
    vb = [h[2].astype(BF16) for h in heads]
    qk = [_dot_nt(qb[j], heads[j][1].astype(BF16)) for j in range(len(heads))]
    qc = [_dot(qb[j], heads[j][7].astype(BF16)) for j in range(len(heads))]
    r = lax.broadcasted_iota(jnp.int32, (L, L), 0)
    c = lax.broadcasted_iota(jnp.int32, (L, L), 1)
    tri = c <= r
    gate = []
    for q, k, v, i_col, f_col, i_row, f_row, C, n, m in heads:
        b_col = jnp.sum(jnp.where(tri, f_row, 0.0), axis=1, keepdims=True)
        b_row = jnp.sum(jnp.where(r <= c, f_col, 0.0), axis=0, keepdims=True)
        log_d = jnp.where(tri, b_col - b_row + i_row, -jnp.inf)
        inter = b_col + m
        m_row = jnp.maximum(jnp.max(log_d, axis=1, keepdims=True), inter)
        b_last = b_col[L - 1:L, :]
        log_w = b_last - b_col + i_col
        m_new = jnp.maximum(b_last + m, jnp.max(log_w, axis=0, keepdims=True))
        gate.append(dict(
            m_row=m_row, w_inter=jnp.exp(inter - m_row), d=scale * jnp.exp(log_d - m_row), m_new=m_new,
            decay=jnp.exp(b_last + m - m_new), wk=(scale * jnp.exp(log_w - m_new)) * k.astype(F32)))
    yield
    s = [qk[j] * gate[j]["d"] for j in range(len(heads))]
    yield
    sv = [_dot(s[j].astype(BF16), vb[j]) for j in range(len(heads))]
    kv = [_dot_tn(gate[j]["wk"].astype(BF16), vb[j]) for j in range(len(heads))]
    yield
    for j, (q, k, v, i_col, f_col, i_row, f_row, C, n, m) in enumerate(heads):
        g = gate[j]
        num = g["w_inter"] * qc[j] + sv[j]
        den = (g["w_inter"] * jnp.sum(q.astype(F32) * n, axis=1, keepdims=True)
               + jnp.sum(s[j], axis=1, keepdims=True))
        h = num / jnp.maximum(jnp.abs(den), jnp.exp(-g["m_row"]))
        c_new = g["decay"] * C + kv[j]
        n_new = g["decay"] * n + jnp.sum(g["wk"], axis=0, keepdims=True)
        out.append((h, c_new, n_new, g["m_new"]))


def _ml_head_out(h, o, g):
    return _rms(h, g) * jax.nn.sigmoid(o.astype(F32))


def _mlstm_prompt_phases(first_chunk, q_ref, k_ref, v_ref, o_ref, gc_ref, gr_ref, gh_ref,
                         h_ref, c_out, n_out, m_out, c_scr, n_scr, m_scr):
    gc, gr = gc_ref[...], gr_ref[...]
    cols = [slice(hh * ML_HEAD_DIM, (hh + 1) * ML_HEAD_DIM) for hh in range(ML_HEADS)]
    prev = lambda ref, hh: jnp.where(first_chunk, 0.0, ref[hh])
    res = []
    yield from _mlstm_phases([
        (q_ref[:, cols[hh]], k_ref[:, cols[hh]], v_ref[:, cols[hh]],
         gc[:, hh:hh + 1], gc[:, ML_HEADS + hh:ML_HEADS + hh + 1],
         gr[hh:hh + 1, :], gr[ML_HEADS + hh:ML_HEADS + hh + 1, :],
         prev(c_scr, hh), prev(n_scr, hh), prev(m_scr, hh)) for hh in range(ML_HEADS)], res)
    for hh, (h, c_new, n_new, m_new) in enumerate(res):
        c_scr[hh], n_scr[hh], m_scr[hh] = c_new, n_new, m_new
        c_out[0, hh], n_out[0, hh], m_out[0, hh] = c_new, n_new, m_new
        h_ref[:, cols[hh]] = _ml_head_out(h, o_ref[:, cols[hh]], gh_ref[hh:hh + 1, :]).astype(h_ref.dtype)


def _mlstm_sample_phases(q_ref, k_ref, v_ref, o_ref, gc_ref, gr_ref, gh_ref, c_in, n_in, m_in,
                         h_ref, c_out, n_out, m_out):
    gc, gr = gc_ref[0], gr_ref[0]
    cols = [slice(hh * ML_HEAD_DIM, (hh + 1) * ML_HEAD_DIM) for hh in range(ML_HEADS)]
    res = []
    yield from _mlstm_phases([
        (q_ref[0, :, cols[hh]], k_ref[0, :, cols[hh]], v_ref[0, :, cols[hh]],
         gc[:, hh:hh + 1], gc[:, ML_HEADS + hh:ML_HEADS + hh + 1],
         gr[hh:hh + 1, :], gr[ML_HEADS + hh:ML_HEADS + hh + 1, :],
         c_in[0, hh], n_in[0, hh], m_in[0, hh]) for hh in range(ML_HEADS)], res)
    for hh, (h, c_new, n_new, m_new) in enumerate(res):
        c_out[0, hh], n_out[0, hh], m_out[0, hh] = c_new, n_new, m_new
        h_ref[0, :, cols[hh]] = _ml_head_out(h, o_ref[0, :, cols[hh]], gh_ref[hh:hh + 1, :]).astype(h_ref.dtype)


def _da_lambda(lam_ref, lam_init):
    lv = lam_ref[...]
    a = jnp.sum(lv[0:1, :] * lv[1:2, :], axis=1, keepdims=True)
    b = jnp.sum(lv[2:3, :] * lv[3:4, :], axis=1, keepdims=True)
    return jnp.exp(a) - jnp.exp(b) + lam_init


def _stack_components(q):
    lane = lax.broadcasted_iota(jnp.int32, q.shape, 1)
    zero = jnp.zeros_like(q)
    return jnp.concatenate([jnp.where(lane < DA_QK_DIM, q, zero), jnp.where(lane >= DA_QK_DIM, q, zero)], axis=0)


def _da_prompt_kernel(q_ref, k_ref, v_ref, lam_ref, gh_ref, out_ref, vt_scr, s_a, s_b, m_scr, l_scr, acc_scr,
                      *, lam_init):
    i = pl.program_id(2)
    tq = q_ref.shape[0]
    nblk = vt_scr.shape[1]
    heads = range(vt_scr.shape[0])
    hcols = [slice(hd * DA_V_DIM, (hd + 1) * DA_V_DIM) for hd in heads]

    @pl.when(i == 0)
    def _():
        for hd in heads:
            for j in range(nblk):
                vt_scr[hd, j] = v_ref[j * tq:(j + 1) * tq, hcols[hd]].astype(F32).T.astype(BF16)

    qqt = []
    for hd in heads:
        qt = q_ref[:, hcols[hd]].astype(F32).T
        dim = lax.broadcasted_iota(jnp.int32, qt.shape, 0)
        qqt.append(jnp.concatenate(
            [jnp.where(dim < DA_QK_DIM, qt, 0.0), jnp.where(dim >= DA_QK_DIM, qt, 0.0)],
            axis=1).astype(BF16))

    def scores(j, s_ref):
        off = pl.multiple_of(j * tq, tq)
        for hd in heads:
            s_ref[hd] = _dot(k_ref[pl.ds(off, tq), hcols[hd]], qqt[hd])

    def accumulate(j, s_ref, masked):
        stats = []
        for hd in heads:
            st = s_ref[hd]
            if masked:
                key = lax.broadcasted_iota(jnp.int32, st.shape, 0)
                query = lax.broadcasted_iota(jnp.int32, st.shape, 1) % tq
                st = jnp.where(key <= query, st, NEG_BIG)
            m_old = m_scr[hd]
            m_new = jnp.maximum(m_old, jnp.max(st, axis=0, keepdims=True))
            alpha = jnp.exp2(m_old - m_new)
            p = jnp.exp2(st - m_new)
            l_scr[hd] = alpha * l_scr[hd] + jnp.sum(p, axis=0, keepdims=True)
            m_scr[hd] = m_new
            stats.append((alpha, p.astype(BF16)))
        for hd, (alpha, p) in zip(heads, stats):
            acc_scr[hd] = alpha * acc_scr[hd] + _dot(vt_scr[hd, j], p)

    m_scr[...] = jnp.full_like(m_scr, NEG_BIG)
    l_scr[...] = jnp.zeros_like(l_scr)
    acc_scr[...] = jnp.zeros_like(acc_scr)
    scores(0, s_a)

    def body(t, carry):
        scores(2 * t + 1, s_b)
        accumulate(2 * t, s_a, False)
        scores(2 * t + 2, s_a)
        accumulate(2 * t + 1, s_b, False)
        return carry

    lax.fori_loop(0, i // 2, body, 0)

    @pl.when(i % 2 == 1)
    def _():
        scores(i, s_b)
        accumulate(i - 1, s_a, False)
        accumulate(i, s_b, True)

    @pl.when(i % 2 == 0)
    def _():
        accumulate(i, s_a, True)

    lam = _da_lambda(lam_ref, lam_init)
    for hd in heads:
        ot = acc_scr[hd] / l_scr[hd]
        at = ot[:, :tq] - lam * ot[:, tq:]
        norm = at * lax.rsqrt(jnp.mean(at * at, axis=0, keepdims=True) + RMS_EPS)
        out_ref[:, hcols[hd]] = ((norm.T * gh_ref[hd]) * (1.0 - lam_init)).astype(out_ref.dtype)


def _da_prompt(dq, dk, dv, da_lambda, g_head3, B, T, tq, hp, lam_init):
    nq = T // tq
    rows = B * T
    w = hp * DA_V_DIM
    kv_spec = pl.BlockSpec((T, w), lambda b, h, i: (b, h))
    return pl.pallas_call(
        functools.partial(_da_prompt_kernel, lam_init=lam_init), grid=(B, DA_HEADS // hp, nq),
        in_specs=[pl.BlockSpec((tq, w), lambda b, h, i: (b * nq + i, h)), kv_spec, kv_spec,
                  pl.BlockSpec(da_lambda.shape, lambda b, h, i: (0, 0)),
                  pl.BlockSpec((hp, 1, DA_V_DIM), lambda b, h, i: (h, 0, 0))],
        out_specs=pl.BlockSpec((tq, w), lambda b, h, i: (b * nq + i, h)),
        out_shape=jax.ShapeDtypeStruct((rows, DA_WIDTH), BF16),
        scratch_shapes=[pltpu.VMEM((hp, nq, DA_V_DIM, tq), BF16),
                        pltpu.VMEM((hp, tq, 2 * tq), F32), pltpu.VMEM((hp, tq, 2 * tq), F32),
                        pltpu.VMEM((hp, 1, 2 * tq), F32), pltpu.VMEM((hp, 1, 2 * tq), F32),
                        pltpu.VMEM((hp, DA_V_DIM, 2 * tq), F32)],
        compiler_params=_params("parallel", "parallel", "arbitrary"), name="da_prompt")(
            dq, dk, dv, da_lambda, g_head3)


def _paged_fetch(pt_ref, ck_hbm, cv_hbm, kbuf, vbuf, sem, layer):
    b = pl.program_id(0)
    n_pages = kbuf.shape[1]

    def copies(bi, slot):
        out = []
        for p in range(n_pages):
            pg = pt_ref[bi, p]
            out.append(pltpu.make_async_copy(ck_hbm.at[layer, pg], kbuf.at[slot, p], sem.at[slot, 0]))
            out.append(pltpu.make_async_copy(cv_hbm.at[layer, pg], vbuf.at[slot, p], sem.at[slot, 1]))
        return out

    @pl.when(b == 0)
    def _():
        for cp in copies(0, 0):
            cp.start()

    @pl.when(b + 1 < pl.num_programs(0))
    def _():
        for cp in copies(b + 1, (b + 1) % 2):
            cp.start()

    slot = b % 2
    for cp in copies(b, slot):
        cp.wait()
    return slot


def _da_sample_phases(slot, q_ref, kn_ref, vn_ref, lam_ref, gh_ref, out_ref, kbuf, vbuf, lam_init):
    n_pages, page = kbuf.shape[1], kbuf.shape[2] // DA_HEADS
    T = q_ref.shape[1]
    scale = DA_QK_DIM ** -0.5
    lam = _da_lambda(lam_ref, lam_init)
    q_all = q_ref[0]
    n_rows = n_pages * page * DA_HEADS
    qq = jnp.concatenate([_stack_components(q_all[:, hh * DA_V_DIM:(hh + 1) * DA_V_DIM])
                          for hh in range(DA_HEADS)], axis=0)
    k_all = kbuf[slot].reshape(n_rows, DA_V_DIM).astype(BF16)
    yield
    s_raw = _dot_nt(qq.astype(BF16), k_all)
    yield
    rq = 2 * T * DA_HEADS
    own_head = (lax.broadcasted_iota(jnp.int32, (rq, n_rows), 1) % DA_HEADS
                == lax.broadcasted_iota(jnp.int32, (rq, n_rows), 0) // (2 * T))
    s_past = jnp.where(own_head, s_raw * scale, NEG_BIG)
    trow = lax.broadcasted_iota(jnp.int32, (rq, 1), 0) % T

    def per_query_row(new_ref, t):
        x = new_ref[0, t]
        return jnp.concatenate([jnp.broadcast_to(x[hh:hh + 1, :], (2 * T, DA_V_DIM))
                                for hh in range(DA_HEADS)], axis=0)

    s_new = [jnp.where(trow >= t,
                       jnp.sum(qq * per_query_row(kn_ref, t), axis=1, keepdims=True) * scale, NEG_BIG)
             for t in range(T)]
    m = jnp.max(s_past, axis=1, keepdims=True)
    for t in range(T):
        m = jnp.maximum(m, s_new[t])
    p_past = jnp.exp(s_past - m)
    l = jnp.sum(p_past, axis=1, keepdims=True)
    v_all = vbuf[slot].reshape(n_rows, DA_V_DIM).astype(BF16)
    yield
    acc = _dot(p_past.astype(BF16), v_all)
    yield
    for t in range(T):
        p_t = jnp.exp(s_new[t] - m)
        l = l + p_t
        acc = acc + p_t * per_query_row(vn_ref, t)
    o = acc / l
    for hh in range(DA_HEADS):
        r0 = hh * 2 * T
        a = o[r0:r0 + T] - lam * o[r0 + T:r0 + 2 * T]
        out_ref[0, :, hh * DA_V_DIM:(hh + 1) * DA_V_DIM] = (
            _rms(a, gh_ref[hh:hh + 1, :]) * (1.0 - lam_init)).astype(out_ref.dtype)


def _da_sample_mlstm_prompt_kernel(
        pt_ref, q_ref, kn_ref, vn_ref, lam_ref, gh_ref, ck_hbm, cv_hbm,
        mq_ref, mk_ref, mv_ref, mo_ref, gc_ref, gr_ref, mgh_ref,
        sq_ref, sk_ref, sv_ref, so_ref, sgc_ref, sgr_ref, sc_in, sn_in, sm_in,
        out_ref, h_ref, c_out, n_out, m_out, sh_ref, sc_out, sn_out, sm_out,
        kbuf, vbuf, sem, c_scr, n_scr, m_scr, *, lam_init, layer, chunks):
    i = pl.program_id(0)

    @pl.when(i == 0)
    def _():
        c_scr[...] = jnp.zeros_like(c_scr)
        n_scr[...] = jnp.zeros_like(n_scr)
        m_scr[...] = jnp.zeros_like(m_scr)

    slot = _paged_fetch(pt_ref, ck_hbm, cv_hbm, kbuf, vbuf, sem, layer)
    _run_phases(
        _mlstm_prompt_phases(i % chunks == 0, mq_ref, mk_ref, mv_ref, mo_ref, gc_ref, gr_ref, mgh_ref,
                             h_ref, c_out, n_out, m_out, c_scr, n_scr, m_scr),
        _mlstm_sample_phases(sq_ref, sk_ref, sv_ref, so_ref, sgc_ref, sgr_ref, mgh_ref, sc_in, sn_in, sm_in,
                             sh_ref, sc_out, sn_out, sm_out),
        _da_sample_phases(slot, q_ref, kn_ref, vn_ref, lam_ref, gh_ref, out_ref, kbuf, vbuf, lam_init))


def _da_sample_mlstm_prompt(page_table, dq, dk, dv, da_lambda, g_da, cache_k, cache_v, lam_init, layer,
                            mq, mk, mv, mo, gc, gr, g_ml, Bp, Tp, sq, sk, sv, so, sgc, sgr, c0, n0, m0):
    B, T, _ = dq.shape
    rows = Bp * Tp
    assert rows % B == 0 and Tp % (rows // B) == 0, "one mLSTM chunk per sample batch element"
    L = rows // B
    chunks = Tp // L
    H, D = ML_HEADS, ML_HEAD_DIM
    n_pages = page_table.shape[1]
    page_rows = cache_k.shape[2]
    blk = pl.BlockSpec((1, T, DA_WIDTH), lambda i, pt: (i, 0, 0))
    kv_blk = pl.BlockSpec((1, T, DA_HEADS, DA_V_DIM), lambda i, pt: (i, 0, 0, 0))
    const = lambda a: pl.BlockSpec(a.shape, lambda i, pt: (0,) * a.ndim)
    mblk = lambda w: pl.BlockSpec((L, w), lambda i, pt: (i, 0))
    state = lambda *s: pl.BlockSpec((1, H) + s, lambda i, pt: (i // chunks, 0, 0, 0))
    s3 = lambda *s: pl.BlockSpec((1,) + s, lambda i, pt: (i, 0, 0))
    s4 = lambda *s: pl.BlockSpec((1, H) + s, lambda i, pt: (i, 0, 0, 0))
    sample_state = [s4(D, D), s4(1, D), s4(1, 1)]
    grid_spec = pltpu.PrefetchScalarGridSpec(
        num_scalar_prefetch=1, grid=(B,),
        in_specs=[blk, kv_blk, kv_blk, const(da_lambda), const(g_da),
                  pl.BlockSpec(memory_space=pl.ANY), pl.BlockSpec(memory_space=pl.ANY),
                  mblk(ML_WIDTH), mblk(ML_WIDTH), mblk(ML_WIDTH), mblk(ML_WIDTH), mblk(LANES),
                  pl.BlockSpec((8, L), lambda i, pt: (0, i)), const(g_ml),
                  s3(T, ML_WIDTH), s3(T, ML_WIDTH), s3(T, ML_WIDTH), s3(T, ML_WIDTH), s3(T, LANES), s3(8, T)]
                 + sample_state,
        out_specs=[blk, mblk(ML_WIDTH), state(D, D), state(1, D), state(1, 1), s3(T, ML_WIDTH)] + sample_state,
        scratch_shapes=[pltpu.VMEM((2, n_pages, page_rows, DA_V_DIM), F32),
                        pltpu.VMEM((2, n_pages, page_rows, DA_V_DIM), F32),
                        pltpu.SemaphoreType.DMA((2, 2)),
                        pltpu.VMEM((H, D, D), F32), pltpu.VMEM((H, 1, D), F32), pltpu.VMEM((H, 1, 1), F32)])
    return pl.pallas_call(
        functools.partial(_da_sample_mlstm_prompt_kernel, lam_init=lam_init, layer=layer, chunks=chunks),
        grid_spec=grid_spec,
        out_shape=[jax.ShapeDtypeStruct((B, T, DA_WIDTH), BF16),
                   jax.ShapeDtypeStruct((rows, ML_WIDTH), BF16),
                   jax.ShapeDtypeStruct((Bp, H, D, D), F32),
                   jax.ShapeDtypeStruct((Bp, H, 1, D), F32),
                   jax.ShapeDtypeStruct((Bp, H, 1, 1), F32),
                   jax.ShapeDtypeStruct((B, T, ML_WIDTH), BF16),
                   jax.ShapeDtypeStruct((B, H, D, D), F32),
                   jax.ShapeDtypeStruct((B, H, 1, D), F32),
                   jax.ShapeDtypeStruct((B, H, 1, 1), F32)],
        compiler_params=_params("arbitrary"), name="da_sample_mlstm_prompt")(
            page_table, dq, dk, dv, da_lambda, g_da, cache_k, cache_v, mq, mk, mv, mo, gc, gr, g_ml,
            sq, sk, sv, so, sgc, sgr, c0, n0, m0)


def _proj_norm_kernel(*refs, n_in, has_next):
    a_refs, w_refs = refs[:n_in], refs[n_in:2 * n_in]
    x_ref, gpost_ref, gpre_ref = refs[2 * n_in:2 * n_in + 3]
    rest = refs[2 * n_in + 3:]
    wn_ref = rest[0] if has_next else None
    xo_ref, ho_ref = rest[-2:]
    acc = _dot(a_refs[0][...], w_refs[0][...])
    for a, w in zip(a_refs[1:], w_refs[1:]):
        acc = acc + _dot(a[...], w[...])
    x1 = x_ref[...] + _rms(acc, gpost_ref[...])
    xo_ref[...] = x1
    hn = _rms(x1, gpre_ref[...]).astype(BF16)
    ho_ref[...] = (_dot(hn, wn_ref[...]) if has_next else hn).astype(ho_ref.dtype)


def _proj_norm(a_list, w_list, x, g_post, g_pre, w_next, tm, h_dtype=BF16):
    rows, d = x.shape
    n_in = len(a_list)
    has_next = w_next is not None
    row_spec = lambda w: pl.BlockSpec((tm, w), lambda i: (i, 0))
    full = lambda a: pl.BlockSpec(a.shape, lambda i: (0,) * a.ndim)
    ins = list(a_list) + list(w_list) + [x, g_post, g_pre] + ([w_next] if has_next else [])
    in_specs = ([row_spec(a.shape[1]) for a in a_list] + [full(w) for w in w_list]
                + [row_spec(d), full(g_post), full(g_pre)] + ([full(w_next)] if has_next else []))
    n_out = w_next.shape[1] if has_next else d
    return pl.pallas_call(
        functools.partial(_proj_norm_kernel, n_in=n_in, has_next=has_next), grid=(rows // tm,),
        in_specs=in_specs, out_specs=[row_spec(d), row_spec(n_out)],
        out_shape=[jax.ShapeDtypeStruct((rows, d), F32), jax.ShapeDtypeStruct((rows, n_out), h_dtype)],
        compiler_params=_params("parallel"), name="proj_norm")(*ins)


def _mem_kv_kernel(x_ref, g_ref, wk_ref, wv_ref, k_ref, v_ref, kb_ref, vb_ref):
    h = _rms(x_ref[...], g_ref[...]).astype(BF16)
    k = _dot(h, wk_ref[...])
    v = _dot(h, wv_ref[...])
    k_ref[...], v_ref[...] = k, v
    kb_ref[...], vb_ref[...] = k.astype(BF16), v.astype(BF16)


def _mem_kv(x, g, wk, wv, tm):
    rows, d = x.shape
    n = wk.shape[1]
    row_spec = lambda w: pl.BlockSpec((tm, w), lambda i: (i, 0))
    return pl.pallas_call(
        _mem_kv_kernel, grid=(rows // tm,),
        in_specs=[row_spec(d), _resident(g), _resident(wk), _resident(wv)],
        out_specs=[row_spec(n)] * 4,
        out_shape=[jax.ShapeDtypeStruct((rows, n), F32)] * 2 + [jax.ShapeDtypeStruct((rows, n), BF16)] * 2,
        compiler_params=_params("parallel"), name="mem_kv")(x, g, wk, wv)


def _mid_phases(rows, hml_ref, hda_ref, x_ref, woml_ref, woda_ref, gpost1_ref, gpre1_ref, wq_ref,
                mk_ref, mv_ref, wmo_ref, gpost2_ref, gpre2_ref, x2_ref, hf_ref):
    acc = _dot(hml_ref[rows, :], woml_ref[...]) + _dot(hda_ref[rows, :], woda_ref[...])
    yield
    x1 = x_ref[rows, :] + _rms(acc, gpost1_ref[...])
    qm = _dot(_rms(x1, gpre1_ref[...]).astype(BF16), wq_ref[...]).astype(BF16)
    yield
    hd = qm.shape[1] // MEM_HEADS
    hcols = [slice(hh * hd, (hh + 1) * hd) for hh in range(MEM_HEADS)]
    s = [_dot_nt(qm[:, c], mk_ref[0, :, c]) * (hd ** -0.5) for c in hcols]
    yield
    p = [jnp.exp(x - jnp.max(x, axis=1, keepdims=True)) for x in s]
    pv = [_dot(x.astype(BF16), mv_ref[0, :, c]) for x, c in zip(p, hcols)]
    yield
    o = jnp.concatenate([(a / jnp.sum(x, axis=1, keepdims=True)).astype(BF16) for a, x in zip(pv, p)], axis=1)
    acc2 = _dot(o, wmo_ref[...])
    yield
    x2 = x1 + _rms(acc2, gpost2_ref[...])
    x2_ref[rows, :] = x2
    hf_ref[rows, :] = _rms(x2, gpre2_ref[...]).astype(hf_ref.dtype)


def _cache_fetch(mk_hbm, mv_hbm, kbuf, vbuf, sem, layer):
    g = pl.program_id(0)
    bb = kbuf.shape[1]

    def copies(gi, slot):
        out = []
        for bi in range(bb):
            for hh in range(MEM_HEADS):
                b = gi * bb + bi
                out.append(pltpu.make_async_copy(mk_hbm.at[layer, b, :, hh, :], kbuf.at[slot, bi, hh],
                                                 sem.at[slot, 0]))
                out.append(pltpu.make_async_copy(mv_hbm.at[layer, b, :, hh, :], vbuf.at[slot, bi, hh],
                                                 sem.at[slot, 1]))
        return out

    @pl.when(g == 0)
    def _():
        for cp in copies(0, 0):
            cp.start()

    @pl.when(g + 1 < pl.num_programs(0))
    def _():
        for cp in copies(g + 1, (g + 1) % 2):
            cp.start()

    slot = g % 2
    for cp in copies(g, slot):
        cp.wait()
    return slot


def _mem_attn_cache_phases(slot, q_ref, o_ref, kbuf, vbuf):
    bb, hd = q_ref.shape[0], kbuf.shape[-1]
    items = [(bi, hh) for bi in range(bb) for hh in range(MEM_HEADS)]
    cols = lambda hh: slice(hh * hd, (hh + 1) * hd)
    s = [_dot_nt(q_ref[bi, :, cols(hh)].astype(BF16), kbuf[slot, bi, hh].astype(BF16)) * (hd ** -0.5)
         for bi, hh in items]
    yield
    p = [jnp.exp(x - jnp.max(x, axis=1, keepdims=True)) for x in s]
    pv = [_dot(x.astype(BF16), vbuf[slot, bi, hh].astype(BF16)) for x, (bi, hh) in zip(p, items)]
    yield
    for (bi, hh), a, x in zip(items, pv, p):
        o_ref[bi, :, cols(hh)] = (a / jnp.sum(x, axis=1, keepdims=True)).astype(o_ref.dtype)


def _mid_kernel(*refs, n_sub, layer):
    (hml_ref, hda_ref, x_ref, woml_ref, woda_ref, gpost1_ref, gpre1_ref, wq_ref, mk_ref, mv_ref, wmo_ref,
     gpost2_ref, gpre2_ref, qs_ref, ck_hbm, cv_hbm, x2_ref, hf_ref, os_ref, kbuf, vbuf, sem) = refs
    slot = _cache_fetch(ck_hbm, cv_hbm, kbuf, vbuf, sem, layer)
    sub = x_ref.shape[0] // n_sub
    prompt = refs[:13] + (x2_ref, hf_ref)
    _run_phases(_mem_attn_cache_phases(slot, qs_ref, os_ref, kbuf, vbuf),
                *[_mid_phases(slice(j * sub, (j + 1) * sub), *prompt) for j in range(n_sub)])


def _mid(h_ml, h_da, x, wo_ml, wo_da, g_post1, g_pre1, wq, mk, mv, wmo, g_post2, g_pre2, T, tm,
         q_s, cache_k, cache_v, layer):
    rows, d = x.shape
    tiles = T // tm
    steps = rows // tm
    Bs, Ts, _ = q_s.shape
    assert Bs % steps == 0, "one group of sample batch elements per prompt row tile"
    bb = Bs // steps
    M, H, hd = cache_k.shape[2:]
    row_spec = lambda w: pl.BlockSpec((tm, w), lambda i: (i, 0))
    kv_spec = pl.BlockSpec((1,) + mk.shape[1:], lambda i: (i // tiles, 0, 0))
    grp_spec = pl.BlockSpec((bb, Ts, d), lambda i: (i, 0, 0))
    hbm = pl.BlockSpec(memory_space=pl.ANY)
    ins = (h_ml, h_da, x, wo_ml, wo_da, g_post1, g_pre1, wq, mk, mv, wmo, g_post2, g_pre2)
    in_specs = [row_spec(h_ml.shape[1]), row_spec(h_da.shape[1]), row_spec(d)] + [
        kv_spec if a is mk or a is mv else _resident(a) for a in ins[3:]] + [grp_spec, hbm, hbm]
    return pl.pallas_call(
        functools.partial(_mid_kernel, n_sub=2, layer=layer), grid=(steps,),
        in_specs=in_specs, out_specs=[row_spec(d), row_spec(d), grp_spec],
        out_shape=[jax.ShapeDtypeStruct((rows, d), F32), jax.ShapeDtypeStruct((rows, d), BF16),
                   jax.ShapeDtypeStruct((Bs, Ts, d), BF16)],
        scratch_shapes=[pltpu.VMEM((2, bb, H, M, hd), F32), pltpu.VMEM((2, bb, H, M, hd), F32),
                        pltpu.SemaphoreType.DMA((2, 2))],
        compiler_params=_params("arbitrary"), name="mid")(*ins, q_s, cache_k, cache_v)


def _ffn_chunks(nh):
    nchunk = 2 if (nh // LANES) % 2 == 0 else 1
    cw = nh // nchunk
    return [(j * cw, cw) for j in range(nchunk)]


def _ffn_prompt_kernel(hf_ref, x_ref, wup_ref, wdw_ref, bdw_ref, wdn_ref, gpost_ref, y_ref, ulast_ref, ubuf,
                       *, tiles_per_seq):
    i = pl.program_id(0)
    tm = hf_ref.shape[0]
    nh = wdn_ref.shape[0]
    halo = ubuf.shape[0] - tm

    @pl.when(i % tiles_per_seq == 0)
    def _():
        ubuf[0:halo, :] = jnp.zeros((halo, ubuf.shape[1]), F32)

    hf = hf_ref[...]
    f = jnp.zeros((tm, y_ref.shape[1]), F32)
    for c0, cw in _ffn_chunks(nh):
        cg = []
        for base in (c0, nh + c0):
            cs = slice(base, base + cw)
            ubuf[halo:halo + tm, cs] = _dot(hf, wup_ref[:, cs])
            c = bdw_ref[:, cs]
            for j in range(CONV_W):
                lo = halo - (CONV_W - 1) + j
                c = c + ubuf[lo:lo + tm, cs] * wdw_ref[j:j + 1, cs]
            cg.append(c)
        act = (jax.nn.silu(cg[1]) * cg[0]).astype(BF16)
        f = f + _dot(act, wdn_ref[c0:c0 + cw, :])
    y_ref[...] = x_ref[...] + _rms(f, gpost_ref[...])
    tail = ubuf[tm:tm + halo, :]
    ubuf[0:halo, :] = tail
    ulast_ref[0] = tail


def _ffn_prompt(hf, x, w_up, w_dw, b_dw, w_down, g_post, B, T, tm):
    rows, d = x.shape
    npad = w_up.shape[1]
    tiles = T // tm
    halo = 8
    row_spec = lambda w: pl.BlockSpec((tm, w), lambda i: (i, 0))
    full = _resident
    return pl.pallas_call(
        functools.partial(_ffn_prompt_kernel, tiles_per_seq=tiles), grid=(rows // tm,),
        in_specs=[row_spec(d), row_spec(d), full(w_up), full(w_dw), full(b_dw), full(w_down), full(g_post)],
        out_specs=[row_spec(d), pl.BlockSpec((1, halo, npad), lambda i: (i // tiles, 0, 0))],
        out_shape=[jax.ShapeDtypeStruct((rows, d), F32), jax.ShapeDtypeStruct((B, halo, npad), F32)],
        scratch_shapes=[pltpu.VMEM((tm + halo, npad), F32)],
        compiler_params=_params("arbitrary"), name="ffn_prompt")(hf, x, w_up, w_dw, b_dw, w_down, g_post)


def _ffn_sample_kernel(hf_ref, x_ref, cb_ref, wup_ref, wdw_ref, bdw_ref, wdn_ref, gpost_ref, y_ref, unew_ref,
                       *, T):
    nb = hf_ref.shape[0] // T
    nh = wdn_ref.shape[0]
    hf = hf_ref[...]
    f = jnp.zeros(y_ref.shape, F32)
    for c0, cw in _ffn_chunks(nh):
        cg = []
        for base in (c0, nh + c0):
            cs = slice(base, base + cw)
            u = _dot(hf, wup_ref[:, cs])
            ext = [cb_ref[j, :, cs] for j in range(CONV_W - 1)] + [u[t * nb:(t + 1) * nb] for t in range(T)]
            for j in range(CONV_W - 1):
                unew_ref[j, :, cs] = ext[len(ext) - (CONV_W - 1) + j]
            rows = []
            for t in range(T):
                c = bdw_ref[:, cs]
                for j in range(CONV_W):
                    c = c + ext[t + j] * wdw_ref[j:j + 1, cs]
                rows.append(c)
            cg.append(jnp.concatenate(rows, axis=0))
        act = (jax.nn.silu(cg[1]) * cg[0]).astype(BF16)
        f = f + _dot(act, wdn_ref[c0:c0 + cw, :])
    y_ref[...] = x_ref[...] + _rms(f, gpost_ref[...])


def _ffn_sample(hf, x, cb, w_up, w_dw, b_dw, w_down, g_post, T):
    rows, d = x.shape
    npad = w_up.shape[1]
    ins = (hf, x, cb, w_up, w_dw, b_dw, w_down, g_post)
    full = lambda a: pl.BlockSpec(a.shape, lambda i: (0,) * a.ndim)
    return pl.pallas_call(
        functools.partial(_ffn_sample_kernel, T=T), grid=(1,),
        in_specs=[full(a) for a in ins],
        out_specs=[pl.BlockSpec((rows, d), lambda i: (0, 0)),
                   pl.BlockSpec((CONV_W - 1, rows // T, npad), lambda i: (0, 0, 0))],
        out_shape=[jax.ShapeDtypeStruct((rows, d), F32),
                   jax.ShapeDtypeStruct((CONV_W - 1, rows // T, npad), F32)],
        compiler_params=_params("arbitrary"), name="ffn_sample")(*ins)


def _pad_halves(a, nh, nh_pad):
    pad = [(0, 0)] * (a.ndim - 1) + [(0, nh_pad - nh)]
    return jnp.concatenate([jnp.pad(a[..., :nh], pad), jnp.pad(a[..., nh:], pad)], axis=-1)


def _unpad_halves(a, nh, nh_pad):
    return jnp.concatenate([a[..., :nh], a[..., nh_pad:nh_pad + nh]], axis=-1)


def _row_tile(rows, want):
    t = min(rows, want)
    while rows % t:
        t //= 2
    return t


def kernel(x_prompt, x_sample, cache_dk, cache_dv, cache_mem_k, cache_mem_v, state_ml_C, state_ml_n, state_ml_m, state_conv, page_table, mem_prompt, g_mix_pre, g_mix_post, w_in, b_if, g_ml_head, da_lambda, g_da_head, w_out, g_mem_pre, g_mem_post, g_mem_src, w_mq, w_mk, w_mv, w_mo, g_ffn_pre, g_ffn_post, w_up, w_dw, b_dw, w_down):
    depth = w_in.shape[0]
    Bp, Tp, d = x_prompt.shape
    Bs, Ts, _ = x_sample.shape
    n_pages, page = page_table.shape[1], cache_dk.shape[2]
    past_len = n_pages * page
    n_mem = mem_prompt.shape[1]
    nh = w_down.shape[1]
    nh_pad = -(-nh // LANES) * LANES
    H, D = ML_HEADS, ML_HEAD_DIM
    rows_p, rows_s = Bp * Tp, Bs * Ts

    tm_p = _row_tile(Tp, 512)
    tm_s = _row_tile(rows_s, 256)
    tab_p = _rope_tables(Tp, Tp, 0)
    tab_s = _rope_tables(tm_s, Ts, past_len)

    yp = x_prompt.reshape(rows_p, d)
    ys = x_sample.reshape(rows_s, d)
    outs = [[] for _ in range(14)]
    row = lambda a: a.reshape(1, -1)
    for l in range(depth):
        lam_init = 0.8 - 0.6 * math.exp(-0.3 * l)
        wi = w_in[l]
        w_ml = wi[:, :4 * ML_WIDTH].astype(BF16)
        w_da = wi[:, 4 * ML_WIDTH + 2 * H:].astype(BF16)
        w_gate = wi[:, 4 * ML_WIDTH:4 * ML_WIDTH + 2 * H]
        w_g = jnp.pad(w_gate, ((0, 0), (0, LANES - 2 * H))).astype(BF16)
        w_gt = w_gate.T.astype(BF16)
        b_col = jnp.pad(b_if[l], (0, LANES - 2 * H)).reshape(1, LANES)
        b_row = b_if[l].reshape(2 * H, 1)
        wo_ml, wo_da = w_out[l][:ML_WIDTH].astype(BF16), w_out[l][ML_WIDTH:].astype(BF16)
        wq_b, wo_b = w_mq[l].astype(BF16), w_mo[l].astype(BF16)
        wk_b, wv_b = w_mk[l].astype(BF16), w_mv[l].astype(BF16)
        wup_b = _pad_halves(w_up[l], nh, nh_pad).astype(BF16)
        wdw_p = _pad_halves(w_dw[l], nh, nh_pad)
        bdw_p = _pad_halves(b_dw[l].reshape(1, -1), nh, nh_pad)
        wdn_b = jnp.pad(w_down[l], ((0, nh_pad - nh), (0, 0))).astype(BF16)
        g_da3 = g_da_head[l].reshape(DA_HEADS, 1, DA_V_DIM)

        q, k, v, o, gc, gr, dq, dk, dv, dkb, dvb = _in_proj(
            yp, row(g_mix_pre[l]), w_ml, w_da, w_g, w_gt, b_col, b_row, tab_p, BF16, _row_tile(Tp, 1024),
            DA_QK_DIM ** -0.5 * math.log2(math.e))
        qs, ks, vs, os_, gcs, grs, dqs, dk_s, dv_s, _, _ = _in_proj(
            ys, row(g_mix_pre[l]), w_ml, w_da, w_g, w_gt, b_col, b_row, tab_s, F32, tm_s, 1.0)
        r3 = lambda a: a.reshape(Bs, Ts, a.shape[-1])
        r4 = lambda a: a.reshape(Bs, Ts, DA_HEADS, DA_V_DIM)
        gr3 = grs.reshape(2 * H, Bs, Ts).transpose(1, 0, 2)
        h_da_s, h_ml, C_p, n_p, m_p, h_ml_s, C_s, n_s, m_s = _da_sample_mlstm_prompt(
            page_table, r3(dqs), r4(dk_s), r4(dv_s), da_lambda[l], g_da_head[l],
            cache_dk.reshape(depth, -1, page * DA_HEADS, DA_V_DIM),
            cache_dv.reshape(depth, -1, page * DA_HEADS, DA_V_DIM), lam_init, l,
            q, k, v, o, gc, gr, g_ml_head[l], Bp, Tp,
            r3(qs), r3(ks), r3(vs), r3(os_), r3(gcs), gr3,
            state_ml_C[l], state_ml_n[l].reshape(Bs, H, 1, D), state_ml_m[l].reshape(Bs, H, 1, 1))

        x1_s, qm_s = _proj_norm([h_ml_s.reshape(rows_s, ML_WIDTH), h_da_s.reshape(rows_s, DA_WIDTH)],
                                [wo_ml, wo_da], ys, row(g_mix_post[l]), row(g_mem_pre[l]), wq_b, tm_s,
                                h_dtype=F32)

        h_da = _da_prompt(dq, dkb, dvb, da_lambda[l], g_da3, Bp, Tp, _row_tile(Tp, 512), 2, lam_init)
        mk_p, mv_p, mkb, mvb = _mem_kv(mem_prompt.reshape(Bp * n_mem, d), row(g_mem_src[l]), wk_b, wv_b,
                                       _row_tile(Bp * n_mem, 256))
        x2, hf, om_s = _mid(h_ml, h_da, yp, wo_ml, wo_da, row(g_mix_post[l]), row(g_mem_pre[l]), wq_b,
                            mkb.reshape(Bp, n_mem, d), mvb.reshape(Bp, n_mem, d), wo_b,
                            row(g_mem_post[l]), row(g_ffn_pre[l]), Tp, tm_p,
                            qm_s.reshape(Bs, Ts, d), cache_mem_k, cache_mem_v, l)
        yp, ulast = _ffn_prompt(hf, x2, wup_b, wdw_p, bdw_p, wdn_b, row(g_ffn_post[l]), Bp, Tp, tm_p)
        cv_p = _unpad_halves(ulast[:, ulast.shape[1] - (CONV_W - 1):], nh, nh_pad)

        x2, hf = _proj_norm([om_s.reshape(rows_s, d)], [wo_b], x1_s, row(g_mem_post[l]), row(g_ffn_pre[l]),
                            None, tm_s)
        tmaj = lambda a: a.reshape(Bs, Ts, -1).transpose(1, 0, 2).reshape(rows_s, -1)
        cb = _pad_halves(state_conv[l], nh, nh_pad).transpose(1, 0, 2)
        y_t, unew = _ffn_sample(tmaj(hf), tmaj(x2), cb, wup_b, wdw_p, bdw_p, wdn_b, row(g_ffn_post[l]), Ts)
        ys = y_t.reshape(Ts, Bs, d).transpose(1, 0, 2).reshape(rows_s, d)
        cv_s = _unpad_halves(unew.transpose(1, 0, 2), nh, nh_pad)

        vals = (dk.reshape(Bp, Tp, DA_HEADS, DA_V_DIM), dv.reshape(Bp, Tp, DA_HEADS, DA_V_DIM),
                mk_p.reshape(Bp, n_mem, MEM_HEADS, d // MEM_HEADS), mv_p.reshape(Bp, n_mem, MEM_HEADS, d // MEM_HEADS),
                C_p, n_p.reshape(Bp, H, D), m_p.reshape(Bp, H), cv_p,
                dk_s.reshape(Bs, Ts, DA_HEADS, DA_V_DIM), dv_s.reshape(Bs, Ts, DA_HEADS, DA_V_DIM),
                C_s, n_s.reshape(Bs, H, D), m_s.reshape(Bs, H), cv_s)
        for acc, val in zip(outs, vals):
            acc.append(val)
    return (yp.reshape(Bp, Tp, d), ys.reshape(Bs, Ts, d)) + tuple(jnp.stack(a) for a in outs)
```

```python
import functools
import math

import jax
import jax.numpy as jnp
from jax import lax
from jax.experimental import pallas as pl
from jax.experimental.pallas import tpu as pltpu

F32 = jnp.float32
BF16 = jnp.bfloat16

ML_HEADS = 4
ML_HEAD_DIM = 128
ML_WIDTH = ML_HEADS * ML_HEAD_DIM
DA_HEADS = 4
DA_V_DIM = 128
DA_QK_DIM = 64
DA_WIDTH = DA_HEADS * DA_V_DIM
ROPE_DIM = 16
ROPE_THETA = 500000.0
MEM_HEADS = 4
CONV_W = 3
RMS_EPS = 1e-6
LANES = 128
NEG_BIG = -1e30
VMEM_LIMIT = 56 * 1024 * 1024


def _params(*sem, vmem=VMEM_LIMIT):
    return pltpu.CompilerParams(dimension_semantics=sem, vmem_limit_bytes=vmem)


def _resident(a):
    return pl.BlockSpec(a.shape, lambda *_: (0,) * a.ndim, pipeline_mode=pl.Buffered(1))


def _rms(x, g):
    return x * lax.rsqrt(jnp.mean(x * x, axis=-1, keepdims=True) + RMS_EPS) * g


def _log_sigmoid(x):
    return jnp.minimum(x, 0.0) - jnp.log1p(jnp.exp(-jnp.abs(x)))


def _dot(a, b):
    return jnp.dot(a, b, preferred_element_type=F32)


def _dot_nt(a, b):
    return lax.dot_general(a, b, (((1,), (1,)), ((), ())), preferred_element_type=F32)


def _dot_tn(a, b):
    return lax.dot_general(a, b, (((0,), (0,)), ((), ())), preferred_element_type=F32)


def _rope_table_kernel(cos_ref, sa_ref, sb_ref, *, period, offset):
    rows = cos_ref.shape[0]
    half = ROPE_DIM // 2
    r = lax.broadcasted_iota(jnp.int32, (rows, LANES), 0) + pl.program_id(0) * rows
    lane = lax.broadcasted_iota(jnp.int32, (rows, LANES), 1)
    pos = (offset + r % period).astype(F32)
    c = lane % DA_QK_DIM
    j = (c % half).astype(F32)
    inv = jnp.exp(-math.log(ROPE_THETA) * (2.0 * j / ROPE_DIM))
    ang = pos * inv
    cos, sin = jnp.cos(ang), jnp.sin(ang)
    cos_ref[...] = jnp.where(c < ROPE_DIM, cos, 1.0)
    sa_ref[...] = jnp.where(c < half, -sin, 0.0)
    sb_ref[...] = jnp.where((c >= half) & (c < ROPE_DIM), sin, 0.0)


def _rope_tables(rows, period, offset):
    blk = min(rows, 512)
    spec = pl.BlockSpec((blk, LANES), lambda i: (i, 0))
    shp = jax.ShapeDtypeStruct((rows, LANES), F32)
    return pl.pallas_call(
        functools.partial(_rope_table_kernel, period=period, offset=offset),
        grid=(rows // blk,), out_specs=[spec] * 3, out_shape=[shp] * 3,
        compiler_params=_params("parallel"), name="rope_tables")()


def _rope(x, cos, sa, sb):
    outs = []
    for j in range(x.shape[1] // LANES):
        xj = x[:, j * LANES:(j + 1) * LANES]
        up = pltpu.roll(xj, LANES - ROPE_DIM // 2, axis=1)
        dn = pltpu.roll(xj, ROPE_DIM // 2, axis=1)
        outs.append(xj * cos + up * sa + dn * sb)
    return outs


def _in_proj_kernel(x_ref, g_ref, w_ref, wda_ref, wg_ref, wgt_ref, bcol_ref, brow_ref, cos_ref, sa_ref, sb_ref,
                    q_ref, k_ref, v_ref, o_ref, gc_ref, gr_ref, dq_ref, dk_ref, dv_ref, dkb_ref, dvb_ref,
                    *, dq_scale, n_sub):
    sub = x_ref.shape[0] // n_sub
    for u in range(n_sub):
        rows = slice(u * sub, (u + 1) * sub)
        h = _rms(x_ref[rows, :], g_ref[...]).astype(BF16)
        for j, ref in enumerate((q_ref, k_ref, v_ref, o_ref)):
            ref[rows, :] = _dot(h, w_ref[:, j * ML_WIDTH:(j + 1) * ML_WIDTH]).astype(ref.dtype)
        gc = _dot(h, wg_ref[...]) + bcol_ref[...]
        lane = lax.broadcasted_iota(jnp.int32, gc.shape, 1)
        gc_ref[rows, :] = jnp.where(lane < ML_HEADS, gc, _log_sigmoid(gc))
        gr = _dot_nt(wgt_ref[...], h) + brow_ref[...]
        row = lax.broadcasted_iota(jnp.int32, gr.shape, 0)
        gr_ref[:, rows] = jnp.where(row < ML_HEADS, gr, _log_sigmoid(gr))
        cos, sa, sb = cos_ref[rows, :], sa_ref[rows, :], sb_ref[rows, :]
        dq = _rope(_dot(h, wda_ref[:, :DA_WIDTH]), cos, sa, sb)
        dk = _rope(_dot(h, wda_ref[:, DA_WIDTH:2 * DA_WIDTH]), cos, sa, sb)
        dv = _dot(h, wda_ref[:, 2 * DA_WIDTH:3 * DA_WIDTH])
        dvb_ref[rows, :] = dv.astype(BF16)
        for j in range(DA_HEADS):
            cols = slice(j * LANES, (j + 1) * LANES)
            dq_ref[rows, cols] = (dq[j] * dq_scale).astype(dq_ref.dtype)
            dk_ref[rows, j, :] = dk[j]
            dkb_ref[rows, cols] = dk[j].astype(BF16)
            dv_ref[rows, j, :] = dv[:, cols]


def _in_proj(x, g, w_ml, w_da, w_g, w_gt, b_col, b_row, tables, act_dtype, tm, dq_scale):
    rows, d = x.shape
    cos, sa, sb = tables
    nt = cos.shape[0] // tm
    row_spec = lambda w: pl.BlockSpec((tm, w), lambda i: (i, 0))
    full = lambda a: pl.BlockSpec(a.shape, lambda i: (0,) * a.ndim)
    tab_spec = pl.BlockSpec((tm, LANES), lambda i: (i % nt, 0))
    out_shape = [jax.ShapeDtypeStruct((rows, ML_WIDTH), act_dtype)] * 4 + [
        jax.ShapeDtypeStruct((rows, LANES), F32), jax.ShapeDtypeStruct((8, rows), F32),
        jax.ShapeDtypeStruct((rows, DA_WIDTH), act_dtype),
        jax.ShapeDtypeStruct((rows, DA_HEADS, DA_V_DIM), F32), jax.ShapeDtypeStruct((rows, DA_HEADS, DA_V_DIM), F32),
        jax.ShapeDtypeStruct((rows, DA_WIDTH), BF16), jax.ShapeDtypeStruct((rows, DA_WIDTH), BF16)]
    kv_spec = pl.BlockSpec((tm, DA_HEADS, DA_V_DIM), lambda i: (i, 0, 0))
    out_specs = [row_spec(ML_WIDTH)] * 4 + [row_spec(LANES), pl.BlockSpec((8, tm), lambda i: (0, i)),
                                            row_spec(DA_WIDTH), kv_spec, kv_spec,
                                            row_spec(DA_WIDTH), row_spec(DA_WIDTH)]
    return pl.pallas_call(
        functools.partial(_in_proj_kernel, dq_scale=dq_scale, n_sub=1),
        grid=(rows // tm,),
        in_specs=[row_spec(d), full(g), full(w_ml), full(w_da), full(w_g), full(w_gt), full(b_col), full(b_row),
                  tab_spec, tab_spec, tab_spec],
        out_specs=out_specs, out_shape=out_shape,
        compiler_params=_params("parallel"), name="in_proj")(
            x, g, w_ml, w_da, w_g, w_gt, b_col, b_row, cos, sa, sb)


def _run_phases(*gens):
    live = list(gens)
    while live:
        for g in list(live):
            try:
                next(g)
            except StopIteration:
                live.remove(g)


def _mlstm_phases(heads, out):
    L, D = heads[0][0].shape
    scale = D ** -0.5
    qb = [h[0].astype(BF16) for h in heads]
    vb = [h[2].astype(BF16) for h in heads]
    qk = [_dot_nt(qb[j], heads[j][1].astype(BF16)) for j in range(len(heads))]
    qc = [_dot(qb[j], heads[j][7].astype(BF16)) for j in range(len(heads))]
    r = lax.broadcasted_iota(jnp.int32, (L, L), 0)
    c = lax.broadcasted_iota(jnp.int32, (L, L), 1)
    tri = c <= r
    gate = []
    for q, k, v, i_col, f_col, i_row, f_row, C, n, m in heads:
        b_col = jnp.sum(jnp.where(tri, f_row, 0.0), axis=1, keepdims=True)
        b_row = jnp.sum(jnp.where(r <= c, f_col, 0.0), axis=0, keepdims=True)
        log_d = jnp.where(tri, b_col - b_row + i_row, -jnp.inf)
        inter = b_col + m
        m_row = jnp.maximum(jnp.max(log_d, axis=1, keepdims=True), inter)
        b_last = b_col[L - 1:L, :]
        log_w = b_last - b_col + i_col
        m_new = jnp.maximum(b_last + m, jnp.max(log_w, axis=0, keepdims=True))
        gate.append(dict(
            m_row=m_row, w_inter=jnp.exp(inter - m_row), d=scale * jnp.exp(log_d - m_row), m_new=m_new,
            decay=jnp.exp(b_last + m - m_new), wk=(scale * jnp.exp(log_w - m_new)) * k.astype(F32)))
    yield
    s = [qk[j] * gate[j]["d"] for j in range(len(heads))]
    yield
    sv = [_dot(s[j].astype(BF16), vb[j]) for j in range(len(heads))]
    kv = [_dot_tn(gate[j]["wk"].astype(BF16), vb[j]) for j in range(len(heads))]
    yield
    for j, (q, k, v, i_col, f_col, i_row, f_row, C, n, m) in enumerate(heads):
        g = gate[j]
        num = g["w_inter"] * qc[j] + sv[j]
        den = (g["w_inter"] * jnp.sum(q.astype(F32) * n, axis=1, keepdims=True)
               + jnp.sum(s[j], axis=1, keepdims=True))
        h = num / jnp.maximum(jnp.abs(den), jnp.exp(-g["m_row"]))
        c_new = g["decay"] * C + kv[j]
        n_new = g["decay"] * n + jnp.sum(g["wk"], axis=0, keepdims=True)
        out.append((h, c_new, n_new, g["m_new"]))


def _ml_head_out(h, o, g):
    return _rms(h, g) * jax.nn.sigmoid(o.astype(F32))


def _mlstm_prompt_phases(first_chunk, q_ref, k_ref, v_ref, o_ref, gc_ref, gr_ref, gh_ref,
                         h_ref, c_out, n_out, m_out, c_scr, n_scr, m_scr):
    gc, gr = gc_ref[...], gr_ref[...]
    cols = [slice(hh * ML_HEAD_DIM, (hh + 1) * ML_HEAD_DIM) for hh in range(ML_HEADS)]
    prev = lambda ref, hh: jnp.where(first_chunk, 0.0, ref[hh])
    res = []
    yield from _mlstm_phases([
        (q_ref[:, cols[hh]], k_ref[:, cols[hh]], v_ref[:, cols[hh]],
         gc[:, hh:hh + 1], gc[:, ML_HEADS + hh:ML_HEADS + hh + 1],
         gr[hh:hh + 1, :], gr[ML_HEADS + hh:ML_HEADS + hh + 1, :],
         prev(c_scr, hh), prev(n_scr, hh), prev(m_scr, hh)) for hh in range(ML_HEADS)], res)
    for hh, (h, c_new, n_new, m_new) in enumerate(res):
        c_scr[hh], n_scr[hh], m_scr[hh] = c_new, n_new, m_new
        c_out[0, hh], n_out[0, hh], m_out[0, hh] = c_new, n_new, m_new
        h_ref[:, cols[hh]] = _ml_head_out(h, o_ref[:, cols[hh]], gh_ref[hh:hh + 1, :]).astype(h_ref.dtype)


def _mlstm_sample_phases(q_ref, k_ref, v_ref, o_ref, gc_ref, gr_ref, gh_ref, c_in, n_in, m_in,
                         h_ref, c_out, n_out, m_out):
    gc, gr = gc_ref[0], gr_ref[0]
    cols = [slice(hh * ML_HEAD_DIM, (hh + 1) * ML_HEAD_DIM) for hh in range(ML_HEADS)]
    res = []
    yield from _mlstm_phases([
        (q_ref[0, :, cols[hh]], k_ref[0, :, cols[hh]], v_ref[0, :, cols[hh]],
         gc[:, hh:hh + 1], gc[:, ML_HEADS + hh:ML_HEADS + hh + 1],
         gr[hh:hh + 1, :], gr[ML_HEADS + hh:ML_HEADS + hh + 1, :],
         c_in[0, hh], n_in[0, hh], m_in[0, hh]) for hh in range(ML_HEADS)], res)
    for hh, (h, c_new, n_new, m_new) in enumerate(res):
        c_out[0, hh], n_out[0, hh], m_out[0, hh] = c_new, n_new, m_new
        h_ref[0, :, cols[hh]] = _ml_head_out(h, o_ref[0, :, cols[hh]], gh_ref[hh:hh + 1, :]).astype(h_ref.dtype)


def _da_lambda(lam_ref, lam_init):
    lv = lam_ref[...]
    a = jnp.sum(lv[0:1, :] * lv[1:2, :], axis=1, keepdims=True)
    b = jnp.sum(lv[2:3, :] * lv[3:4, :], axis=1, keepdims=True)
    return jnp.exp(a) - jnp.exp(b) + lam_init


def _stack_components(q):
    lane = lax.broadcasted_iota(jnp.int32, q.shape, 1)
    zero = jnp.zeros_like(q)
    return jnp.concatenate([jnp.where(lane < DA_QK_DIM, q, zero), jnp.where(lane >= DA_QK_DIM, q, zero)], axis=0)


def _da_prompt_kernel(q_ref, k_ref, v_ref, lam_ref, gh_ref, out_ref, vt_scr, s_a, s_b, m_scr, l_scr, acc_scr,
                      *, lam_init):
    i = pl.program_id(2)
    tq = q_ref.shape[0]
    nblk = vt_scr.shape[1]
    heads = range(vt_scr.shape[0])
    hcols = [slice(hd * DA_V_DIM, (hd + 1) * DA_V_DIM) for hd in heads]

    @pl.when(i == 0)
    def _():
        for hd in heads:
            for j in range(nblk):
                vt_scr[hd, j] = v_ref[j * tq:(j + 1) * tq, hcols[hd]].astype(F32).T.astype(BF16)

    qqt = []
    for hd in heads:
        qt = q_ref[:, hcols[hd]].astype(F32).T
        dim = lax.broadcasted_iota(jnp.int32, qt.shape, 0)
        qqt.append(jnp.concatenate(
            [jnp.where(dim < DA_QK_DIM, qt, 0.0), jnp.where(dim >= DA_QK_DIM, qt, 0.0)],
            axis=1).astype(BF16))

    def scores(j, s_ref):
        off = pl.multiple_of(j * tq, tq)
        for hd in heads:
            s_ref[hd] = _dot(k_ref[pl.ds(off, tq), hcols[hd]], qqt[hd])

    def accumulate(j, s_ref, masked):
        stats = []
        for hd in heads:
            st = s_ref[hd]
            if masked:
                key = lax.broadcasted_iota(jnp.int32, st.shape, 0)
                query = lax.broadcasted_iota(jnp.int32, st.shape, 1) % tq
                st = jnp.where(key <= query, st, NEG_BIG)
            m_old = m_scr[hd]
            m_new = jnp.maximum(m_old, jnp.max(st, axis=0, keepdims=True))
            alpha = jnp.exp2(m_old - m_new)
            p = jnp.exp2(st - m_new)
            l_scr[hd] = alpha * l_scr[hd] + jnp.sum(p, axis=0, keepdims=True)
            m_scr[hd] = m_new
            stats.append((alpha, p.astype(BF16)))
        for hd, (alpha, p) in zip(heads, stats):
            acc_scr[hd] = alpha * acc_scr[hd] + _dot(vt_scr[hd, j], p)

    m_scr[...] = jnp.full_like(m_scr, NEG_BIG)
    l_scr[...] = jnp.zeros_like(l_scr)
    acc_scr[...] = jnp.zeros_like(acc_scr)
    scores(0, s_a)

    def body(t, carry):
        scores(2 * t + 1, s_b)
        accumulate(2 * t, s_a, False)
        scores(2 * t + 2, s_a)
        accumulate(2 * t + 1, s_b, False)
        return carry

    lax.fori_loop(0, i // 2, body, 0)

    @pl.when(i % 2 == 1)
    def _():
        scores(i, s_b)
        accumulate(i - 1, s_a, False)
        accumulate(i, s_b, True)

    @pl.when(i % 2 == 0)
    def _():
        accumulate(i, s_a, True)

    lam = _da_lambda(lam_ref, lam_init)
    for hd in heads:
        ot = acc_scr[hd] / l_scr[hd]
        at = ot[:, :tq] - lam * ot[:, tq:]
        norm = at * lax.rsqrt(jnp.mean(at * at, axis=0, keepdims=True) + RMS_EPS)
        out_ref[:, hcols[hd]] = ((norm.T * gh_ref[hd]) * (1.0 - lam_init)).astype(out_ref.dtype)


def _da_prompt(dq, dk, dv, da_lambda, g_head3, B, T, tq, hp, lam_init):
    nq = T // tq
    rows = B * T
    w = hp * DA_V_DIM
    kv_spec = pl.BlockSpec((T, w), lambda b, h, i: (b, h))
    return pl.pallas_call(
        functools.partial(_da_prompt_kernel, lam_init=lam_init), grid=(B, DA_HEADS // hp, nq),
        in_specs=[pl.BlockSpec((tq, w), lambda b, h, i: (b * nq + i, h)), kv_spec, kv_spec,
                  pl.BlockSpec(da_lambda.shape, lambda b, h, i: (0, 0)),
                  pl.BlockSpec((hp, 1, DA_V_DIM), lambda b, h, i: (h, 0, 0))],
        out_specs=pl.BlockSpec((tq, w), lambda b, h, i: (b * nq + i, h)),
        out_shape=jax.ShapeDtypeStruct((rows, DA_WIDTH), BF16),
        scratch_shapes=[pltpu.VMEM((hp, nq, DA_V_DIM, tq), BF16),
                        pltpu.VMEM((hp, tq, 2 * tq), F32), pltpu.VMEM((hp, tq, 2 * tq), F32),
                        pltpu.VMEM((hp, 1, 2 * tq), F32), pltpu.VMEM((hp, 1, 2 * tq), F32),
                        pltpu.VMEM((hp, DA_V_DIM, 2 * tq), F32)],
        compiler_params=_params("parallel", "parallel", "arbitrary"), name="da_prompt")(
            dq, dk, dv, da_lambda, g_head3)


def _paged_fetch(pt_ref, ck_hbm, cv_hbm, kbuf, vbuf, sem, layer):
    b = pl.program_id(0)
    n_pages = kbuf.shape[1]

    def copies(bi, slot):
        out = []
        for p in range(n_pages):
            pg = pt_ref[bi, p]
            out.append(pltpu.make_async_copy(ck_hbm.at[layer, pg], kbuf.at[slot, p], sem.at[slot, 0]))
            out.append(pltpu.make_async_copy(cv_hbm.at[layer, pg], vbuf.at[slot, p], sem.at[slot, 1]))
        return out

    @pl.when(b == 0)
    def _():
        for n, cp in enumerate(copies(0, 0)):
            cp.start(priority=n % 2)

    @pl.when(b + 1 < pl.num_programs(0))
    def _():
        for n, cp in enumerate(copies(b + 1, (b + 1) % 2)):
            cp.start(priority=n % 2)

    slot = b % 2
    for cp in copies(b, slot):
        cp.wait()
    return slot


def _da_sample_phases(slot, q_ref, kn_ref, vn_ref, lam_ref, gh_ref, out_ref, kbuf, vbuf, lam_init):
    n_pages, page = kbuf.shape[1], kbuf.shape[2] // DA_HEADS
    T = q_ref.shape[1]
    scale = DA_QK_DIM ** -0.5
    lam = _da_lambda(lam_ref, lam_init)
    q_all = q_ref[0]
    n_rows = n_pages * page * DA_HEADS
    qq = jnp.concatenate([_stack_components(q_all[:, hh * DA_V_DIM:(hh + 1) * DA_V_DIM])
                          for hh in range(DA_HEADS)], axis=0)
    k_all = kbuf[slot].reshape(n_rows, DA_V_DIM).astype(BF16)
    yield
    s_raw = _dot_nt(qq.astype(BF16), k_all)
    yield
    rq = 2 * T * DA_HEADS
    own_head = (lax.broadcasted_iota(jnp.int32, (rq, n_rows), 1) % DA_HEADS
                == lax.broadcasted_iota(jnp.int32, (rq, n_rows), 0) // (2 * T))
    s_past = jnp.where(own_head, s_raw * scale, NEG_BIG)
    trow = lax.broadcasted_iota(jnp.int32, (rq, 1), 0) % T

    def per_query_row(new_ref, t):
        x = new_ref[0, t]
        return jnp.concatenate([jnp.broadcast_to(x[hh:hh + 1, :], (2 * T, DA_V_DIM))
                                for hh in range(DA_HEADS)], axis=0)

    s_new = [jnp.where(trow >= t,
                       jnp.sum(qq * per_query_row(kn_ref, t), axis=1, keepdims=True) * scale, NEG_BIG)
             for t in range(T)]
    m = jnp.max(s_past, axis=1, keepdims=True)
    for t in range(T):
        m = jnp.maximum(m, s_new[t])
    p_past = jnp.exp(s_past - m)
    l = jnp.sum(p_past, axis=1, keepdims=True)
    v_all = vbuf[slot].reshape(n_rows, DA_V_DIM).astype(BF16)
    yield
    acc = _dot(p_past.astype(BF16), v_all)
    yield
    for t in range(T):
        p_t = jnp.exp(s_new[t] - m)
        l = l + p_t
        acc = acc + p_t * per_query_row(vn_ref, t)
    o = acc / l
    for hh in range(DA_HEADS):
        r0 = hh * 2 * T
        a = o[r0:r0 + T] - lam * o[r0 + T:r0 + 2 * T]
        out_ref[0, :, hh * DA_V_DIM:(hh + 1) * DA_V_DIM] = (
            _rms(a, gh_ref[hh:hh + 1, :]) * (1.0 - lam_init)).astype(out_ref.dtype)


def _da_sample_mlstm_prompt_kernel(
        pt_ref, q_ref, kn_ref, vn_ref, lam_ref, gh_ref, ck_hbm, cv_hbm,
        mq_ref, mk_ref, mv_ref, mo_ref, gc_ref, gr_ref, mgh_ref,
        sq_ref, sk_ref, sv_ref, so_ref, sgc_ref, sgr_ref, sc_in, sn_in, sm_in,
        out_ref, h_ref, c_out, n_out, m_out, sh_ref, sc_out, sn_out, sm_out,
        kbuf, vbuf, sem, c_scr, n_scr, m_scr, *, lam_init, layer, chunks):
    i = pl.program_id(0)

    @pl.when(i == 0)
    def _():
        c_scr[...] = jnp.zeros_like(c_scr)
        n_scr[...] = jnp.zeros_like(n_scr)
        m_scr[...] = jnp.zeros_like(m_scr)

    slot = _paged_fetch(pt_ref, ck_hbm, cv_hbm, kbuf, vbuf, sem, layer)
    _run_phases(
        _mlstm_prompt_phases(i % chunks == 0, mq_ref, mk_ref, mv_ref, mo_ref, gc_ref, gr_ref, mgh_ref,
                             h_ref, c_out, n_out, m_out, c_scr, n_scr, m_scr),
        _mlstm_sample_phases(sq_ref, sk_ref, sv_ref, so_ref, sgc_ref, sgr_ref, mgh_ref, sc_in, sn_in, sm_in,
                             sh_ref, sc_out, sn_out, sm_out),
        _da_sample_phases(slot, q_ref, kn_ref, vn_ref, lam_ref, gh_ref, out_ref, kbuf, vbuf, lam_init))


def _da_sample_mlstm_prompt(page_table, dq, dk, dv, da_lambda, g_da, cache_k, cache_v, lam_init, layer,
                            mq, mk, mv, mo, gc, gr, g_ml, Bp, Tp, sq, sk, sv, so, sgc, sgr, c0, n0, m0):
    B, T, _ = dq.shape
    rows = Bp * Tp
    assert rows % B == 0 and Tp % (rows // B) == 0, "one mLSTM chunk per sample batch element"
    L = rows // B
    chunks = Tp // L
    H, D = ML_HEADS, ML_HEAD_DIM
    n_pages = page_table.shape[1]
    page_rows = cache_k.shape[2]
    blk = pl.BlockSpec((1, T, DA_WIDTH), lambda i, pt: (i, 0, 0))
    kv_blk = pl.BlockSpec((1, T, DA_HEADS, DA_V_DIM), lambda i, pt: (i, 0, 0, 0))
    const = lambda a: pl.BlockSpec(a.shape, lambda i, pt: (0,) * a.ndim)
    mblk = lambda w: pl.BlockSpec((L, w), lambda i, pt: (i, 0))
    state = lambda *s: pl.BlockSpec((1, H) + s, lambda i, pt: (i // chunks, 0, 0, 0))
    s3 = lambda *s: pl.BlockSpec((1,) + s, lambda i, pt: (i, 0, 0))
    s4 = lambda *s: pl.BlockSpec((1, H) + s, lambda i, pt: (i, 0, 0, 0))
    sample_state = [s4(D, D), s4(1, D), s4(1, 1)]
    grid_spec = pltpu.PrefetchScalarGridSpec(
        num_scalar_prefetch=1, grid=(B,),
        in_specs=[blk, kv_blk, kv_blk, const(da_lambda), const(g_da),
                  pl.BlockSpec(memory_space=pl.ANY), pl.BlockSpec(memory_space=pl.ANY),
                  mblk(ML_WIDTH), mblk(ML_WIDTH), mblk(ML_WIDTH), mblk(ML_WIDTH), mblk(LANES),
                  pl.BlockSpec((8, L), lambda i, pt: (0, i)), const(g_ml),
                  s3(T, ML_WIDTH), s3(T, ML_WIDTH), s3(T, ML_WIDTH), s3(T, ML_WIDTH), s3(T, LANES), s3(8, T)]
                 + sample_state,
        out_specs=[blk, mblk(ML_WIDTH), state(D, D), state(1, D), state(1, 1), s3(T, ML_WIDTH)] + sample_state,
        scratch_shapes=[pltpu.VMEM((2, n_pages, page_rows, DA_V_DIM), F32),
                        pltpu.VMEM((2, n_pages, page_rows, DA_V_DIM), F32),
                        pltpu.SemaphoreType.DMA((2, 2)),
                        pltpu.VMEM((H, D, D), F32), pltpu.VMEM((H, 1, D), F32), pltpu.VMEM((H, 1, 1), F32)])
    return pl.pallas_call(
        functools.partial(_da_sample_mlstm_prompt_kernel, lam_init=lam_init, layer=layer, chunks=chunks),
        grid_spec=grid_spec,
        out_shape=[jax.ShapeDtypeStruct((B, T, DA_WIDTH), BF16),
                   jax.ShapeDtypeStruct((rows, ML_WIDTH), BF16),
                   jax.ShapeDtypeStruct((Bp, H, D, D), F32),
                   jax.ShapeDtypeStruct((Bp, H, 1, D), F32),
                   jax.ShapeDtypeStruct((Bp, H, 1, 1), F32),
                   jax.ShapeDtypeStruct((B, T, ML_WIDTH), BF16),
                   jax.ShapeDtypeStruct((B, H, D, D), F32),
                   jax.ShapeDtypeStruct((B, H, 1, D), F32),
                   jax.ShapeDtypeStruct((B, H, 1, 1), F32)],
        compiler_params=_params("arbitrary"), name="da_sample_mlstm_prompt")(
            page_table, dq, dk, dv, da_lambda, g_da, cache_k, cache_v, mq, mk, mv, mo, gc, gr, g_ml,
            sq, sk, sv, so, sgc, sgr, c0, n0, m0)


def _proj_norm_kernel(*refs, n_in, has_next):
    a_refs, w_refs = refs[:n_in], refs[n_in:2 * n_in]
    x_ref, gpost_ref, gpre_ref = refs[2 * n_in:2 * n_in + 3]
    rest = refs[2 * n_in + 3:]
    wn_ref = rest[0] if has_next else None
    xo_ref, ho_ref = rest[-2:]
    acc = _dot(a_refs[0][...], w_refs[0][...])
    for a, w in zip(a_refs[1:], w_refs[1:]):
        acc = acc + _dot(a[...], w[...])
    x1 = x_ref[...] + _rms(acc, gpost_ref[...])
    xo_ref[...] = x1
    hn = _rms(x1, gpre_ref[...]).astype(BF16)
    ho_ref[...] = (_dot(hn, wn_ref[...]) if has_next else hn).astype(ho_ref.dtype)


def _proj_norm(a_list, w_list, x, g_post, g_pre, w_next, tm, h_dtype=BF16):
    rows, d = x.shape
    n_in = len(a_list)
    has_next = w_next is not None
    row_spec = lambda w: pl.BlockSpec((tm, w), lambda i: (i, 0))
    full = lambda a: pl.BlockSpec(a.shape, lambda i: (0,) * a.ndim)
    ins = list(a_list) + list(w_list) + [x, g_post, g_pre] + ([w_next] if has_next else [])
    in_specs = ([row_spec(a.shape[1]) for a in a_list] + [full(w) for w in w_list]
                + [row_spec(d), full(g_post), full(g_pre)] + ([full(w_next)] if has_next else []))
    n_out = w_next.shape[1] if has_next else d
    return pl.pallas_call(
        functools.partial(_proj_norm_kernel, n_in=n_in, has_next=has_next), grid=(rows // tm,),
        in_specs=in_specs, out_specs=[row_spec(d), row_spec(n_out)],
        out_shape=[jax.ShapeDtypeStruct((rows, d), F32), jax.ShapeDtypeStruct((rows, n_out), h_dtype)],
        compiler_params=_params("parallel"), name="proj_norm")(*ins)


def _mem_kv_kernel(x_ref, g_ref, wk_ref, wv_ref, k_ref, v_ref, kb_ref, vb_ref):
    h = _rms(x_ref[...], g_ref[...]).astype(BF16)
    k = _dot(h, wk_ref[...])
    v = _dot(h, wv_ref[...])
    k_ref[...], v_ref[...] = k, v
    kb_ref[...], vb_ref[...] = k.astype(BF16), v.astype(BF16)


def _mem_kv(x, g, wk, wv, tm):
    rows, d = x.shape
    n = wk.shape[1]
    row_spec = lambda w: pl.BlockSpec((tm, w), lambda i: (i, 0))
    return pl.pallas_call(
        _mem_kv_kernel, grid=(rows // tm,),
        in_specs=[row_spec(d), _resident(g), _resident(wk), _resident(wv)],
        out_specs=[row_spec(n)] * 4,
        out_shape=[jax.ShapeDtypeStruct((rows, n), F32)] * 2 + [jax.ShapeDtypeStruct((rows, n), BF16)] * 2,
        compiler_params=_params("parallel"), name="mem_kv")(x, g, wk, wv)


def _mid_phases(rows, hml_ref, hda_ref, x_ref, woml_ref, woda_ref, gpost1_ref, gpre1_ref, wq_ref,
                mk_ref, mv_ref, wmo_ref, gpost2_ref, gpre2_ref, x2_ref, hf_ref):
    acc = _dot(hml_ref[rows, :], woml_ref[...]) + _dot(hda_ref[rows, :], woda_ref[...])
    yield
    x1 = x_ref[rows, :] + _rms(acc, gpost1_ref[...])
    qm = _dot(_rms(x1, gpre1_ref[...]).astype(BF16), wq_ref[...]).astype(BF16)
    yield
    hd = qm.shape[1] // MEM_HEADS
    hcols = [slice(hh * hd, (hh + 1) * hd) for hh in range(MEM_HEADS)]
    s = [_dot_nt(qm[:, c], mk_ref[0, :, c]) * (hd ** -0.5) for c in hcols]
    yield
    p = [jnp.exp(x - jnp.max(x, axis=1, keepdims=True)) for x in s]
    pv = [_dot(x.astype(BF16), mv_ref[0, :, c]) for x, c in zip(p, hcols)]
    yield
    o = jnp.concatenate([(a / jnp.sum(x, axis=1, keepdims=True)).astype(BF16) for a, x in zip(pv, p)], axis=1)
    acc2 = _dot(o, wmo_ref[...])
    yield
    x2 = x1 + _rms(acc2, gpost2_ref[...])
    x2_ref[rows, :] = x2
    hf_ref[rows, :] = _rms(x2, gpre2_ref[...]).astype(hf_ref.dtype)


def _cache_fetch(mk_hbm, mv_hbm, kbuf, vbuf, sem, layer):
    g = pl.program_id(0)
    bb = kbuf.shape[1]

    def copies(gi, slot):
        out = []
        for bi in range(bb):
            for hh in range(MEM_HEADS):
                b = gi * bb + bi
                out.append(pltpu.make_async_copy(mk_hbm.at[layer, b, :, hh, :], kbuf.at[slot, bi, hh],
                                                 sem.at[slot, 0]))
                out.append(pltpu.make_async_copy(mv_hbm.at[layer, b, :, hh, :], vbuf.at[slot, bi, hh],
                                                 sem.at[slot, 1]))
        return out

    @pl.when(g == 0)
    def _():
        for cp in copies(0, 0):
            cp.start()

    @pl.when(g + 1 < pl.num_programs(0))
    def _():
        for cp in copies(g + 1, (g + 1) % 2):
            cp.start()

    slot = g % 2
    for cp in copies(g, slot):
        cp.wait()
    return slot


def _mem_attn_cache_phases(slot, q_ref, o_ref, kbuf, vbuf):
    bb, hd = q_ref.shape[0], kbuf.shape[-1]
    items = [(bi, hh) for bi in range(bb) for hh in range(MEM_HEADS)]
    cols = lambda hh: slice(hh * hd, (hh + 1) * hd)
    s = [_dot_nt(q_ref[bi, :, cols(hh)].astype(BF16), kbuf[slot, bi, hh].astype(BF16)) * (hd ** -0.5)
         for bi, hh in items]
    yield
    p = [jnp.exp(x - jnp.max(x, axis=1, keepdims=True)) for x in s]
    pv = [_dot(x.astype(BF16), vbuf[slot, bi, hh].astype(BF16)) for x, (bi, hh) in zip(p, items)]
    yield
    for (bi, hh), a, x in zip(items, pv, p):
        o_ref[bi, :, cols(hh)] = (a / jnp.sum(x, axis=1, keepdims=True)).astype(o_ref.dtype)


def _mid_kernel(*refs, n_sub, layer):
    (hml_ref, hda_ref, x_ref, woml_ref, woda_ref, gpost1_ref, gpre1_ref, wq_ref, mk_ref, mv_ref, wmo_ref,
     gpost2_ref, gpre2_ref, qs_ref, ck_hbm, cv_hbm, x2_ref, hf_ref, os_ref, kbuf, vbuf, sem) = refs
    slot = _cache_fetch(ck_hbm, cv_hbm, kbuf, vbuf, sem, layer)
    sub = x_ref.shape[0] // n_sub
    prompt = refs[:13] + (x2_ref, hf_ref)
    _run_phases(_mem_attn_cache_phases(slot, qs_ref, os_ref, kbuf, vbuf),
                *[_mid_phases(slice(j * sub, (j + 1) * sub), *prompt) for j in range(n_sub)])


def _mid(h_ml, h_da, x, wo_ml, wo_da, g_post1, g_pre1, wq, mk, mv, wmo, g_post2, g_pre2, T, tm,
         q_s, cache_k, cache_v, layer):
    rows, d = x.shape
    tiles = T // tm
    steps = rows // tm
    Bs, Ts, _ = q_s.shape
    assert Bs % steps == 0, "one group of sample batch elements per prompt row tile"
    bb = Bs // steps
    M, H, hd = cache_k.shape[2:]
    row_spec = lambda w: pl.BlockSpec((tm, w), lambda i: (i, 0))
    kv_spec = pl.BlockSpec((1,) + mk.shape[1:], lambda i: (i // tiles, 0, 0))
    grp_spec = pl.BlockSpec((bb, Ts, d), lambda i: (i, 0, 0))
    hbm = pl.BlockSpec(memory_space=pl.ANY)
    ins = (h_ml, h_da, x, wo_ml, wo_da, g_post1, g_pre1, wq, mk, mv, wmo, g_post2, g_pre2)
    in_specs = [row_spec(h_ml.shape[1]), row_spec(h_da.shape[1]), row_spec(d)] + [
        kv_spec if a is mk or a is mv else _resident(a) for a in ins[3:]] + [grp_spec, hbm, hbm]
    return pl.pallas_call(
        functools.partial(_mid_kernel, n_sub=2, layer=layer), grid=(steps,),
        in_specs=in_specs, out_specs=[row_spec(d), row_spec(d), grp_spec],
        out_shape=[jax.ShapeDtypeStruct((rows, d), F32), jax.ShapeDtypeStruct((rows, d), BF16),
                   jax.ShapeDtypeStruct((Bs, Ts, d), BF16)],
        scratch_shapes=[pltpu.VMEM((2, bb, H, M, hd), F32), pltpu.VMEM((2, bb, H, M, hd), F32),
                        pltpu.SemaphoreType.DMA((2, 2))],
        compiler_params=_params("arbitrary"), name="mid")(*ins, q_s, cache_k, cache_v)


def _ffn_chunks(nh):
    nchunk = 2 if (nh // LANES) % 2 == 0 else 1
    cw = nh // nchunk
    return [(j * cw, cw) for j in range(nchunk)]


def _ffn_prompt_kernel(hf_ref, x_ref, wup_ref, wdw_ref, bdw_ref, wdn_ref, gpost_ref, y_ref, ulast_ref, ubuf,
                       *, tiles_per_seq):
    i = pl.program_id(0)
    tm = hf_ref.shape[0]
    nh = wdn_ref.shape[0]
    halo = ubuf.shape[0] - tm

    @pl.when(i % tiles_per_seq == 0)
    def _():
        ubuf[0:halo, :] = jnp.zeros((halo, ubuf.shape[1]), F32)

    hf = hf_ref[...]
    f = jnp.zeros((tm, y_ref.shape[1]), F32)
    for c0, cw in _ffn_chunks(nh):
        cg = []
        for base in (c0, nh + c0):
            cs = slice(base, base + cw)
            ubuf[halo:halo + tm, cs] = _dot(hf, wup_ref[:, cs])
            c = bdw_ref[:, cs]
            for j in range(CONV_W):
                lo = halo - (CONV_W - 1) + j
                c = c + ubuf[lo:lo + tm, cs] * wdw_ref[j:j + 1, cs]
            cg.append(c)
        act = (jax.nn.silu(cg[1]) * cg[0]).astype(BF16)
        f = f + _dot(act, wdn_ref[c0:c0 + cw, :])
    y_ref[...] = x_ref[...] + _rms(f, gpost_ref[...])
    tail = ubuf[tm:tm + halo, :]
    ubuf[0:halo, :] = tail
    ulast_ref[0] = tail


def _ffn_prompt(hf, x, w_up, w_dw, b_dw, w_down, g_post, B, T, tm):
    rows, d = x.shape
    npad = w_up.shape[1]
    tiles = T // tm
    halo = 8
    row_spec = lambda w: pl.BlockSpec((tm, w), lambda i: (i, 0))
    full = _resident
    return pl.pallas_call(
        functools.partial(_ffn_prompt_kernel, tiles_per_seq=tiles), grid=(rows // tm,),
        in_specs=[row_spec(d), row_spec(d), full(w_up), full(w_dw), full(b_dw), full(w_down), full(g_post)],
        out_specs=[row_spec(d), pl.BlockSpec((1, halo, npad), lambda i: (i // tiles, 0, 0))],
        out_shape=[jax.ShapeDtypeStruct((rows, d), F32), jax.ShapeDtypeStruct((B, halo, npad), F32)],
        scratch_shapes=[pltpu.VMEM((tm + halo, npad), F32)],
        compiler_params=_params("arbitrary"), name="ffn_prompt")(hf, x, w_up, w_dw, b_dw, w_down, g_post)


def _ffn_sample_kernel(hf_ref, x_ref, cb_ref, wup_ref, wdw_ref, bdw_ref, wdn_ref, gpost_ref, y_ref, unew_ref,
                       *, T):
    nb = hf_ref.shape[0] // T
    nh = wdn_ref.shape[0]
    hf = hf_ref[...]
    f = jnp.zeros(y_ref.shape, F32)
    for c0, cw in _ffn_chunks(nh):
        cg = []
        for base in (c0, nh + c0):
            cs = slice(base, base + cw)
            u = _dot(hf, wup_ref[:, cs])
            ext = [cb_ref[j, :, cs] for j in range(CONV_W - 1)] + [u[t * nb:(t + 1) * nb] for t in range(T)]
            for j in range(CONV_W - 1):
                unew_ref[j, :, cs] = ext[len(ext) - (CONV_W - 1) + j]
            rows = []
            for t in range(T):
                c = bdw_ref[:, cs]
                for j in range(CONV_W):
                    c = c + ext[t + j] * wdw_ref[j:j + 1, cs]
                rows.append(c)
            cg.append(jnp.concatenate(rows, axis=0))
        act = (jax.nn.silu(cg[1]) * cg[0]).astype(BF16)
        f = f + _dot(act, wdn_ref[c0:c0 + cw, :])
    y_ref[...] = x_ref[...] + _rms(f, gpost_ref[...])


def _ffn_sample(hf, x, cb, w_up, w_dw, b_dw, w_down, g_post, T):
    rows, d = x.shape
    npad = w_up.shape[1]
    ins = (hf, x, cb, w_up, w_dw, b_dw, w_down, g_post)
    full = lambda a: pl.BlockSpec(a.shape, lambda i: (0,) * a.ndim)
    return pl.pallas_call(
        functools.partial(_ffn_sample_kernel, T=T), grid=(1,),
        in_specs=[full(a) for a in ins],
        out_specs=[pl.BlockSpec((rows, d), lambda i: (0, 0)),
                   pl.BlockSpec((CONV_W - 1, rows // T, npad), lambda i: (0, 0, 0))],
        out_shape=[jax.ShapeDtypeStruct((rows, d), F32),
                   jax.ShapeDtypeStruct((CONV_W - 1, rows // T, npad), F32)],
        compiler_params=_params("arbitrary"), name="ffn_sample")(*ins)


def _pad_halves(a, nh, nh_pad):
    pad = [(0, 0)] * (a.ndim - 1) + [(0, nh_pad - nh)]
    return jnp.concatenate([jnp.pad(a[..., :nh], pad), jnp.pad(a[..., nh:], pad)], axis=-1)


def _unpad_halves(a, nh, nh_pad):
    return jnp.concatenate([a[..., :nh], a[..., nh_pad:nh_pad + nh]], axis=-1)


def _row_tile(rows, want):
    t = min(rows, want)
    while rows % t:
        t //= 2
    return t


def kernel(x_prompt, x_sample, cache_dk, cache_dv, cache_mem_k, cache_mem_v, state_ml_C, state_ml_n, state_ml_m, state_conv, page_table, mem_prompt, g_mix_pre, g_mix_post, w_in, b_if, g_ml_head, da_lambda, g_da_head, w_out, g_mem_pre, g_mem_post, g_mem_src, w_mq, w_mk, w_mv, w_mo, g_ffn_pre, g_ffn_post, w_up, w_dw, b_dw, w_down):
    depth = w_in.shape[0]
    Bp, Tp, d = x_prompt.shape
    Bs, Ts, _ = x_sample.shape
    n_pages, page = page_table.shape[1], cache_dk.shape[2]
    past_len = n_pages * page
    n_mem = mem_prompt.shape[1]
    nh = w_down.shape[1]
    nh_pad = -(-nh // LANES) * LANES
    H, D = ML_HEADS, ML_HEAD_DIM
    rows_p, rows_s = Bp * Tp, Bs * Ts

    tm_p = _row_tile(Tp, 512)
    tm_s = _row_tile(rows_s, 256)
    tab_p = _rope_tables(Tp, Tp, 0)
    tab_s = _rope_tables(tm_s, Ts, past_len)

    yp = x_prompt.reshape(rows_p, d)
    ys = x_sample.reshape(rows_s, d)
    outs = [[] for _ in range(14)]
    row = lambda a: a.reshape(1, -1)
    for l in range(depth):
        lam_init = 0.8 - 0.6 * math.exp(-0.3 * l)
        wi = w_in[l]
        w_ml = wi[:, :4 * ML_WIDTH].astype(BF16)
        w_da = wi[:, 4 * ML_WIDTH + 2 * H:].astype(BF16)
        w_gate = wi[:, 4 * ML_WIDTH:4 * ML_WIDTH + 2 * H]
        w_g = jnp.pad(w_gate, ((0, 0), (0, LANES - 2 * H))).astype(BF16)
        w_gt = w_gate.T.astype(BF16)
        b_col = jnp.pad(b_if[l], (0, LANES - 2 * H)).reshape(1, LANES)
        b_row = b_if[l].reshape(2 * H, 1)
        wo_ml, wo_da = w_out[l][:ML_WIDTH].astype(BF16), w_out[l][ML_WIDTH:].astype(BF16)
        wq_b, wo_b = w_mq[l].astype(BF16), w_mo[l].astype(BF16)
        wk_b, wv_b = w_mk[l].astype(BF16), w_mv[l].astype(BF16)
        wup_b = _pad_halves(w_up[l], nh, nh_pad).astype(BF16)
        wdw_p = _pad_halves(w_dw[l], nh, nh_pad)
        bdw_p = _pad_halves(b_dw[l].reshape(1, -1), nh, nh_pad)
        wdn_b = jnp.pad(w_down[l], ((0, nh_pad - nh), (0, 0))).astype(BF16)
        g_da3 = g_da_head[l].reshape(DA_HEADS, 1, DA_V_DIM)

        q, k, v, o, gc, gr, dq, dk, dv, dkb, dvb = _in_proj(
            yp, row(g_mix_pre[l]), w_ml, w_da, w_g, w_gt, b_col, b_row, tab_p, BF16, _row_tile(Tp, 1024),
            DA_QK_DIM ** -0.5 * math.log2(math.e))
        qs, ks, vs, os_, gcs, grs, dqs, dk_s, dv_s, _, _ = _in_proj(
            ys, row(g_mix_pre[l]), w_ml, w_da, w_g, w_gt, b_col, b_row, tab_s, F32, tm_s, 1.0)
        r3 = lambda a: a.reshape(Bs, Ts, a.shape[-1])
        r4 = lambda a: a.reshape(Bs, Ts, DA_HEADS, DA_V_DIM)
        gr3 = grs.reshape(2 * H, Bs, Ts).transpose(1, 0, 2)
        h_da_s, h_ml, C_p, n_p, m_p, h_ml_s, C_s, n_s, m_s = _da_sample_mlstm_prompt(
            page_table, r3(dqs), r4(dk_s), r4(dv_s), da_lambda[l], g_da_head[l],
            cache_dk.reshape(depth, -1, page * DA_HEADS, DA_V_DIM),
            cache_dv.reshape(depth, -1, page * DA_HEADS, DA_V_DIM), lam_init, l,
            q, k, v, o, gc, gr, g_ml_head[l], Bp, Tp,
            r3(qs), r3(ks), r3(vs), r3(os_), r3(gcs), gr3,
            state_ml_C[l], state_ml_n[l].reshape(Bs, H, 1, D), state_ml_m[l].reshape(Bs, H, 1, 1))

        x1_s, qm_s = _proj_norm([h_ml_s.reshape(rows_s, ML_WIDTH), h_da_s.reshape(rows_s, DA_WIDTH)],
                                [wo_ml, wo_da], ys, row(g_mix_post[l]), row(g_mem_pre[l]), wq_b, tm_s,
                                h_dtype=F32)

        h_da = _da_prompt(dq, dkb, dvb, da_lambda[l], g_da3, Bp, Tp, _row_tile(Tp, 512), 2, lam_init)
        mk_p, mv_p, mkb, mvb = _mem_kv(mem_prompt.reshape(Bp * n_mem, d), row(g_mem_src[l]), wk_b, wv_b,
                                       _row_tile(Bp * n_mem, 256))
        x2, hf, om_s = _mid(h_ml, h_da, yp, wo_ml, wo_da, row(g_mix_post[l]), row(g_mem_pre[l]), wq_b,
                            mkb.reshape(Bp, n_mem, d), mvb.reshape(Bp, n_mem, d), wo_b,
                            row(g_mem_post[l]), row(g_ffn_pre[l]), Tp, tm_p,
                            qm_s.reshape(Bs, Ts, d), cache_mem_k, cache_mem_v, l)
        yp, ulast = _ffn_prompt(hf, x2, wup_b, wdw_p, bdw_p, wdn_b, row(g_ffn_post[l]), Bp, Tp, tm_p)
        cv_p = _unpad_halves(ulast[:, ulast.shape[1] - (CONV_W - 1):], nh, nh_pad)

        x2, hf = _proj_norm([om_s.reshape(rows_s, d)], [wo_b], x1_s, row(g_mem_post[l]), row(g_ffn_pre[l]),
                            None, tm_s)
        tmaj = lambda a: a.reshape(Bs, Ts, -1).transpose(1, 0, 2).reshape(rows_s, -1)
        cb = _pad_halves(state_conv[l], nh, nh_pad).transpose(1, 0, 2)
        y_t, unew = _ffn_sample(tmaj(hf), tmaj(x2), cb, wup_b, wdw_p, bdw_p, wdn_b, row(g_ffn_post[l]), Ts)
        ys = y_t.reshape(Ts, Bs, d).transpose(1, 0, 2).reshape(rows_s, d)
        cv_s = _unpad_halves(unew.transpose(1, 0, 2), nh, nh_pad)

        vals = (dk.reshape(Bp, Tp, DA_HEADS, DA_V_DIM), dv.reshape(Bp, Tp, DA_HEADS, DA_V_DIM),
                mk_p.reshape(Bp, n_mem, MEM_HEADS, d // MEM_HEADS), mv_p.reshape(Bp, n_mem, MEM_HEADS, d // MEM_HEADS),
                C_p, n_p.reshape(Bp, H, D), m_p.reshape(Bp, H), cv_p,
                dk_s.reshape(Bs, Ts, DA_HEADS, DA_V_DIM), dv_s.reshape(Bs, Ts, DA_HEADS, DA_V_DIM),
                C_s, n_s.reshape(Bs, H, D), m_s.reshape(Bs, H), cv_s)
        for acc, val in zip(outs, vals):
            acc.append(val)
    return (yp.reshape(Bp, Tp, d), ys.reshape(Bs, Ts, d)) + tuple(jnp.stack(a) for a in outs)
```

```python
import functools
import math

import jax
import jax.numpy as jnp
from jax import lax
from jax.experimental import pallas as pl
from jax.experimental.pallas import tpu as pltpu

F32 = jnp.float32
BF16 = jnp.bfloat16

ML_HEADS = 4
ML_HEAD_DIM = 128
ML_WIDTH = ML_HEADS * ML_HEAD_DIM
DA_HEADS = 4
DA_V_DIM = 128
DA_QK_DIM = 64
DA_WIDTH = DA_HEADS * DA_V_DIM
ROPE_DIM = 16
ROPE_THETA = 500000.0
MEM_HEADS = 4
CONV_W = 3
RMS_EPS = 1e-6
LANES = 128
NEG_BIG = -1e30
VMEM_LIMIT = 56 * 1024 * 1024


def _params(*sem, vmem=VMEM_LIMIT):
    return pltpu.CompilerParams(dimension_semantics=sem, vmem_limit_bytes=vmem)


def _resident(a):
    return pl.BlockSpec(a.shape, lambda *_: (0,) * a.ndim, pipeline_mode=pl.Buffered(1))


def _rms(x, g):
    return x * lax.rsqrt(jnp.mean(x * x, axis=-1, keepdims=True) + RMS_EPS) * g


def _log_sigmoid(x):
    return jnp.minimum(x, 0.0) - jnp.log1p(jnp.exp(-jnp.abs(x)))


def _dot(a, b):
    return jnp.dot(a, b, preferred_element_type=F32)


def _dot_nt(a, b):
    return lax.dot_general(a, b, (((1,), (1,)), ((), ())), preferred_element_type=F32)


def _dot_tn(a, b):
    return lax.dot_general(a, b, (((0,), (0,)), ((), ())), preferred_element_type=F32)


def _rope_table_kernel(cos_ref, sa_ref, sb_ref, *, period, offset):
    rows = cos_ref.shape[0]
    half = ROPE_DIM // 2
    r = lax.broadcasted_iota(jnp.int32, (rows, LANES), 0) + pl.program_id(0) * rows
    lane = lax.broadcasted_iota(jnp.int32, (rows, LANES), 1)
    pos = (offset + r % period).astype(F32)
    c = lane % DA_QK_DIM
    j = (c % half).astype(F32)
    inv = jnp.exp(-math.log(ROPE_THETA) * (2.0 * j / ROPE_DIM))
    ang = pos * inv
    cos, sin = jnp.cos(ang), jnp.sin(ang)
    cos_ref[...] = jnp.where(c < ROPE_DIM, cos, 1.0)
    sa_ref[...] = jnp.where(c < half, -sin, 0.0)
    sb_ref[...] = jnp.where((c >= half) & (c < ROPE_DIM), sin, 0.0)


def _rope_tables(rows, period, offset):
    blk = min(rows, 512)
    spec = pl.BlockSpec((blk, LANES), lambda i: (i, 0))
    shp = jax.ShapeDtypeStruct((rows, LANES), F32)
    return pl.pallas_call(
        functools.partial(_rope_table_kernel, period=period, offset=offset),
        grid=(rows // blk,), out_specs=[spec] * 3, out_shape=[shp] * 3,
        compiler_params=_params("parallel"), name="rope_tables")()


def _rope(x, cos, sa, sb):
    outs = []
    for j in range(x.shape[1] // LANES):
        xj = x[:, j * LANES:(j + 1) * LANES]
        up = pltpu.roll(xj, LANES - ROPE_DIM // 2, axis=1)
        dn = pltpu.roll(xj, ROPE_DIM // 2, axis=1)
        outs.append(xj * cos + up * sa + dn * sb)
    return outs


def _in_proj_kernel(x_ref, g_ref, w_ref, wda_ref, wg_ref, bcol_ref, cos_ref, sa_ref, sb_ref,
                    q_ref, k_ref, v_ref, o_ref, gc_ref, gr_ref, dq_ref, dk_ref, dv_ref, dkb_ref, dvb_ref,
                    *, dq_scale, n_sub):
    sub = x_ref.shape[0] // n_sub
    for u in range(n_sub):
        rows = slice(u * sub, (u + 1) * sub)
        h = _rms(x_ref[rows, :], g_ref[...]).astype(BF16)
        for j, ref in enumerate((q_ref, k_ref, v_ref, o_ref)):
            ref[rows, :] = _dot(h, w_ref[:, j * ML_WIDTH:(j + 1) * ML_WIDTH]).astype(ref.dtype)
        gc = _dot(h, wg_ref[...]) + bcol_ref[...]
        lane = lax.broadcasted_iota(jnp.int32, gc.shape, 1)
        gc = jnp.where(lane < ML_HEADS, gc, _log_sigmoid(gc))
        gc_ref[rows, :] = gc
        gr_ref[:, rows] = gc.T[:2 * ML_HEADS, :]
        cos, sa, sb = cos_ref[rows, :], sa_ref[rows, :], sb_ref[rows, :]
        dq = _rope(_dot(h, wda_ref[:, :DA_WIDTH]), cos, sa, sb)
        dk = _rope(_dot(h, wda_ref[:, DA_WIDTH:2 * DA_WIDTH]), cos, sa, sb)
        dv = _dot(h, wda_ref[:, 2 * DA_WIDTH:3 * DA_WIDTH])
        dvb_ref[rows, :] = dv.astype(BF16)
        for j in range(DA_HEADS):
            cols = slice(j * LANES, (j + 1) * LANES)
            dq_ref[rows, cols] = (dq[j] * dq_scale).astype(dq_ref.dtype)
            dk_ref[rows, j, :] = dk[j]
            dkb_ref[rows, cols] = dk[j].astype(BF16)
            dv_ref[rows, j, :] = dv[:, cols]


def _in_proj(x, g, w_ml, w_da, w_g, b_col, tables, act_dtype, tm, dq_scale):
    rows, d = x.shape
    cos, sa, sb = tables
    nt = cos.shape[0] // tm
    row_spec = lambda w: pl.BlockSpec((tm, w), lambda i: (i, 0))
    full = lambda a: pl.BlockSpec(a.shape, lambda i: (0,) * a.ndim)
    tab_spec = pl.BlockSpec((tm, LANES), lambda i: (i % nt, 0))
    out_shape = [jax.ShapeDtypeStruct((rows, ML_WIDTH), act_dtype)] * 4 + [
        jax.ShapeDtypeStruct((rows, LANES), F32), jax.ShapeDtypeStruct((8, rows), F32),
        jax.ShapeDtypeStruct((rows, DA_WIDTH), act_dtype),
        jax.ShapeDtypeStruct((rows, DA_HEADS, DA_V_DIM), F32), jax.ShapeDtypeStruct((rows, DA_HEADS, DA_V_DIM), F32),
        jax.ShapeDtypeStruct((rows, DA_WIDTH), BF16), jax.ShapeDtypeStruct((rows, DA_WIDTH), BF16)]
    kv_spec = pl.BlockSpec((tm, DA_HEADS, DA_V_DIM), lambda i: (i, 0, 0))
    out_specs = [row_spec(ML_WIDTH)] * 4 + [row_spec(LANES), pl.BlockSpec((8, tm), lambda i: (0, i)),
                                            row_spec(DA_WIDTH), kv_spec, kv_spec,
                                            row_spec(DA_WIDTH), row_spec(DA_WIDTH)]
    return pl.pallas_call(
        functools.partial(_in_proj_kernel, dq_scale=dq_scale, n_sub=1),
        grid=(rows // tm,),
        in_specs=[row_spec(d), full(g), full(w_ml), full(w_da), full(w_g), full(b_col),
                  tab_spec, tab_spec, tab_spec],
        out_specs=out_specs, out_shape=out_shape,
        compiler_params=_params("parallel"), name="in_proj")(
            x, g, w_ml, w_da, w_g, b_col, cos, sa, sb)


def _run_phases(*gens):
    live = list(gens)
    while live:
        for g in list(live):
            try:
                next(g)
            except StopIteration:
                live.remove(g)


def _mlstm_phases(heads, out):
    L, D = heads[0][0].shape
    scale = D ** -0.5
    qb = [h[0].astype(BF16) for h in heads]
    vb = [h[2].astype(BF16) for h in heads]
    qk = [_dot_nt(qb[j], heads[j][1].astype(BF16)) for j in range(len(heads))]
    qc = [_dot(qb[j], heads[j][7].astype(BF16)) for j in range(len(heads))]
    r = lax.broadcasted_iota(jnp.int32, (L, L), 0)
    c = lax.broadcasted_iota(jnp.int32, (L, L), 1)
    tri = c <= r
    gate = []
    for q, k, v, i_col, f_col, i_row, f_row, C, n, m in heads:
        b_col = jnp.sum(jnp.where(tri, f_row, 0.0), axis=1, keepdims=True)
        b_row = jnp.sum(jnp.where(r <= c, f_col, 0.0), axis=0, keepdims=True)
        log_d = jnp.where(tri, b_col - b_row + i_row, -jnp.inf)
        inter = b_col + m
        m_row = jnp.maximum(jnp.max(log_d, axis=1, keepdims=True), inter)
        b_last = b_col[L - 1:L, :]
        log_w = b_last - b_col + i_col
        m_new = jnp.maximum(b_last + m, jnp.max(log_w, axis=0, keepdims=True))
        gate.append(dict(
            m_row=m_row, w_inter=jnp.exp(inter - m_row), d=scale * jnp.exp(log_d - m_row), m_new=m_new,
            decay=jnp.exp(b_last + m - m_new), wk=(scale * jnp.exp(log_w - m_new)) * k.astype(F32)))
    yield
    s = [qk[j] * gate[j]["d"] for j in range(len(heads))]
    yield
    sv = [_dot(s[j].astype(BF16), vb[j]) for j in range(len(heads))]
    kv = [_dot_tn(gate[j]["wk"].astype(BF16), vb[j]) for j in range(len(heads))]
    yield
    for j, (q, k, v, i_col, f_col, i_row, f_row, C, n, m) in enumerate(heads):
        g = gate[j]
        num = g["w_inter"] * qc[j] + sv[j]
        den = (g["w_inter"] * jnp.sum(q.astype(F32) * n, axis=1, keepdims=True)
               + jnp.sum(s[j], axis=1, keepdims=True))
        h = num / jnp.maximum(jnp.abs(den), jnp.exp(-g["m_row"]))
        c_new = g["decay"] * C + kv[j]
        n_new = g["decay"] * n + jnp.sum(g["wk"], axis=0, keepdims=True)
        out.append((h, c_new, n_new, g["m_new"]))


def _ml_head_out(h, o, g):
    return _rms(h, g) * jax.nn.sigmoid(o.astype(F32))


def _mlstm_prompt_phases(first_chunk, q_ref, k_ref, v_ref, o_ref, gc_ref, gr_ref, gh_ref,
                         h_ref, c_out, n_out, m_out, c_scr, n_scr, m_scr):
    gc, gr = gc_ref[...], gr_ref[...]
    cols = [slice(hh * ML_HEAD_DIM, (hh + 1) * ML_HEAD_DIM) for hh in range(ML_HEADS)]
    prev = lambda ref, hh: jnp.where(first_chunk, 0.0, ref[hh])
    res = []
    yield from _mlstm_phases([
        (q_ref[:, cols[hh]], k_ref[:, cols[hh]], v_ref[:, cols[hh]],
         gc[:, hh:hh + 1], gc[:, ML_HEADS + hh:ML_HEADS + hh + 1],
         gr[hh:hh + 1, :], gr[ML_HEADS + hh:ML_HEADS + hh + 1, :],
         prev(c_scr, hh), prev(n_scr, hh), prev(m_scr, hh)) for hh in range(ML_HEADS)], res)
    for hh, (h, c_new, n_new, m_new) in enumerate(res):
        c_scr[hh], n_scr[hh], m_scr[hh] = c_new, n_new, m_new
        c_out[0, hh], n_out[0, hh], m_out[0, hh] = c_new, n_new, m_new
        h_ref[:, cols[hh]] = _ml_head_out(h, o_ref[:, cols[hh]], gh_ref[hh:hh + 1, :]).astype(h_ref.dtype)


def _mlstm_sample_phases(q_ref, k_ref, v_ref, o_ref, gc_ref, gr_ref, gh_ref, c_in, n_in, m_in,
                         h_ref, c_out, n_out, m_out):
    gc, gr = gc_ref[0], gr_ref[0]
    cols = [slice(hh * ML_HEAD_DIM, (hh + 1) * ML_HEAD_DIM) for hh in range(ML_HEADS)]
    res = []
    yield from _mlstm_phases([
        (q_ref[0, :, cols[hh]], k_ref[0, :, cols[hh]], v_ref[0, :, cols[hh]],
         gc[:, hh:hh + 1], gc[:, ML_HEADS + hh:ML_HEADS + hh + 1],
         gr[hh:hh + 1, :], gr[ML_HEADS + hh:ML_HEADS + hh + 1, :],
         c_in[0, hh], n_in[0, hh], m_in[0, hh]) for hh in range(ML_HEADS)], res)
    for hh, (h, c_new, n_new, m_new) in enumerate(res):
        c_out[0, hh], n_out[0, hh], m_out[0, hh] = c_new, n_new, m_new
        h_ref[0, :, cols[hh]] = _ml_head_out(h, o_ref[0, :, cols[hh]], gh_ref[hh:hh + 1, :]).astype(h_ref.dtype)


def _da_lambda(lam_ref, lam_init):
    lv = lam_ref[...]
    a = jnp.sum(lv[0:1, :] * lv[1:2, :], axis=1, keepdims=True)
    b = jnp.sum(lv[2:3, :] * lv[3:4, :], axis=1, keepdims=True)
    return jnp.exp(a) - jnp.exp(b) + lam_init


def _stack_components(q):
    lane = lax.broadcasted_iota(jnp.int32, q.shape, 1)
    zero = jnp.zeros_like(q)
    return jnp.concatenate([jnp.where(lane < DA_QK_DIM, q, zero), jnp.where(lane >= DA_QK_DIM, q, zero)], axis=0)


def _da_prompt_kernel(q_ref, k_ref, v_ref, lam_ref, gh_ref, out_ref, vt_scr, s_a, s_b, m_scr, l_scr, acc_scr,
                      *, lam_init):
    i = pl.program_id(2)
    tq = q_ref.shape[0]
    nblk = vt_scr.shape[1]
    heads = range(vt_scr.shape[0])
    hcols = [slice(hd * DA_V_DIM, (hd + 1) * DA_V_DIM) for hd in heads]

    @pl.when(i == 0)
    def _():
        for hd in heads:
            for j in range(nblk):
                vt_scr[hd, j] = v_ref[j * tq:(j + 1) * tq, hcols[hd]].astype(F32).T.astype(BF16)

    qqt = []
    for hd in heads:
        qt = q_ref[:, hcols[hd]].astype(F32).T
        dim = lax.broadcasted_iota(jnp.int32, qt.shape, 0)
        qqt.append(jnp.concatenate(
            [jnp.where(dim < DA_QK_DIM, qt, 0.0), jnp.where(dim >= DA_QK_DIM, qt, 0.0)],
            axis=1).astype(BF16))

    def scores(j, s_ref):
        off = pl.multiple_of(j * tq, tq)
        for hd in heads:
            s_ref[hd] = _dot(k_ref[pl.ds(off, tq), hcols[hd]], qqt[hd])

    def accumulate(j, s_ref, masked):
        stats = []
        for hd in heads:
            st = s_ref[hd]
            if masked:
                key = lax.broadcasted_iota(jnp.int32, st.shape, 0)
                query = lax.broadcasted_iota(jnp.int32, st.shape, 1) % tq
                st = jnp.where(key <= query, st, NEG_BIG)
            m_old = m_scr[hd]
            m_new = jnp.maximum(m_old, jnp.max(st, axis=0, keepdims=True))
            alpha = jnp.exp2(m_old - m_new)
            p = jnp.exp2(st - m_new)
            l_scr[hd] = alpha * l_scr[hd] + jnp.sum(p, axis=0, keepdims=True)
            m_scr[hd] = m_new
            stats.append((alpha, p.astype(BF16)))
        for hd, (alpha, p) in zip(heads, stats):
            acc_scr[hd] = alpha * acc_scr[hd] + _dot(vt_scr[hd, j], p)

    m_scr[...] = jnp.full_like(m_scr, NEG_BIG)
    l_scr[...] = jnp.zeros_like(l_scr)
    acc_scr[...] = jnp.zeros_like(acc_scr)
    scores(0, s_a)

    def body(t, carry):
        scores(2 * t + 1, s_b)
        accumulate(2 * t, s_a, False)
        scores(2 * t + 2, s_a)
        accumulate(2 * t + 1, s_b, False)
        return carry

    lax.fori_loop(0, i // 2, body, 0)

    @pl.when(i % 2 == 1)
    def _():
        scores(i, s_b)
        accumulate(i - 1, s_a, False)
        accumulate(i, s_b, True)

    @pl.when(i % 2 == 0)
    def _():
        accumulate(i, s_a, True)

    lam = _da_lambda(lam_ref, lam_init)
    for hd in heads:
        ot = acc_scr[hd] / l_scr[hd]
        at = ot[:, :tq] - lam * ot[:, tq:]
        norm = at * lax.rsqrt(jnp.mean(at * at, axis=0, keepdims=True) + RMS_EPS)
        out_ref[:, hcols[hd]] = ((norm.T * gh_ref[hd]) * (1.0 - lam_init)).astype(out_ref.dtype)


def _da_prompt(dq, dk, dv, da_lambda, g_head3, B, T, tq, hp, lam_init):
    nq = T // tq
    rows = B * T
    w = hp * DA_V_DIM
    kv_spec = pl.BlockSpec((T, w), lambda b, h, i: (b, h))
    return pl.pallas_call(
        functools.partial(_da_prompt_kernel, lam_init=lam_init), grid=(B, DA_HEADS // hp, nq),
        in_specs=[pl.BlockSpec((tq, w), lambda b, h, i: (b * nq + i, h)), kv_spec, kv_spec,
                  pl.BlockSpec(da_lambda.shape, lambda b, h, i: (0, 0)),
                  pl.BlockSpec((hp, 1, DA_V_DIM), lambda b, h, i: (h, 0, 0))],
        out_specs=pl.BlockSpec((tq, w), lambda b, h, i: (b * nq + i, h)),
        out_shape=jax.ShapeDtypeStruct((rows, DA_WIDTH), BF16),
        scratch_shapes=[pltpu.VMEM((hp, nq, DA_V_DIM, tq), BF16),
                        pltpu.VMEM((hp, tq, 2 * tq), F32), pltpu.VMEM((hp, tq, 2 * tq), F32),
                        pltpu.VMEM((hp, 1, 2 * tq), F32), pltpu.VMEM((hp, 1, 2 * tq), F32),
                        pltpu.VMEM((hp, DA_V_DIM, 2 * tq), F32)],
        compiler_params=_params("parallel", "parallel", "arbitrary"), name="da_prompt")(
            dq, dk, dv, da_lambda, g_head3)


def _paged_fetch(pt_ref, ck_hbm, cv_hbm, kbuf, vbuf, sem, layer):
    b = pl.program_id(0)
    n_pages = kbuf.shape[1]

    def copies(bi, slot):
        out = []
        for p in range(n_pages):
            pg = pt_ref[bi, p]
            out.append(pltpu.make_async_copy(ck_hbm.at[layer, pg], kbuf.at[slot, p], sem.at[slot, 0]))
            out.append(pltpu.make_async_copy(cv_hbm.at[layer, pg], vbuf.at[slot, p], sem.at[slot, 1]))
        return out

    @pl.when(b == 0)
    def _():
        for n, cp in enumerate(copies(0, 0)):
            cp.start(priority=n % 2)

    @pl.when(b + 1 < pl.num_programs(0))
    def _():
        for n, cp in enumerate(copies(b + 1, (b + 1) % 2)):
            cp.start(priority=n % 2)

    slot = b % 2
    for cp in copies(b, slot):
        cp.wait()
    return slot


def _da_sample_phases(slot, q_ref, kn_ref, vn_ref, lam_ref, gh_ref, out_ref, kbuf, vbuf, lam_init):
    n_pages, page = kbuf.shape[1], kbuf.shape[2] // DA_HEADS
    T = q_ref.shape[1]
    scale = DA_QK_DIM ** -0.5
    lam = _da_lambda(lam_ref, lam_init)
    q_all = q_ref[0]
    n_rows = n_pages * page * DA_HEADS
    qq = jnp.concatenate([_stack_components(q_all[:, hh * DA_V_DIM:(hh + 1) * DA_V_DIM])
                          for hh in range(DA_HEADS)], axis=0)
    k_all = kbuf[slot].reshape(n_rows, DA_V_DIM).astype(BF16)
    yield
    s_raw = _dot_nt(qq.astype(BF16), k_all)
    yield
    rq = 2 * T * DA_HEADS
    own_head = (lax.broadcasted_iota(jnp.int32, (rq, n_rows), 1) % DA_HEADS
                == lax.broadcasted_iota(jnp.int32, (rq, n_rows), 0) // (2 * T))
    s_past = jnp.where(own_head, s_raw * scale, NEG_BIG)
    trow = lax.broadcasted_iota(jnp.int32, (rq, 1), 0) % T

    def per_query_row(new_ref, t):
        x = new_ref[0, t]
        return jnp.concatenate([jnp.broadcast_to(x[hh:hh + 1, :], (2 * T, DA_V_DIM))
                                for hh in range(DA_HEADS)], axis=0)

    s_new = [jnp.where(trow >= t,
                       jnp.sum(qq * per_query_row(kn_ref, t), axis=1, keepdims=True) * scale, NEG_BIG)
             for t in range(T)]
    m = jnp.max(s_past, axis=1, keepdims=True)
    for t in range(T):
        m = jnp.maximum(m, s_new[t])
    p_past = jnp.exp(s_past - m)
    l = jnp.sum(p_past, axis=1, keepdims=True)
    v_all = vbuf[slot].reshape(n_rows, DA_V_DIM).astype(BF16)
    yield
    acc = _dot(p_past.astype(BF16), v_all)
    yield
    for t in range(T):
        p_t = jnp.exp(s_new[t] - m)
        l = l + p_t
        acc = acc + p_t * per_query_row(vn_ref, t)
    o = acc / l
    for hh in range(DA_HEADS):
        r0 = hh * 2 * T
        a = o[r0:r0 + T] - lam * o[r0 + T:r0 + 2 * T]
        out_ref[0, :, hh * DA_V_DIM:(hh + 1) * DA_V_DIM] = (
            _rms(a, gh_ref[hh:hh + 1, :]) * (1.0 - lam_init)).astype(out_ref.dtype)


def _da_sample_mlstm_prompt_kernel(
        pt_ref, q_ref, kn_ref, vn_ref, lam_ref, gh_ref, ck_hbm, cv_hbm,
        mq_ref, mk_ref, mv_ref, mo_ref, gc_ref, gr_ref, mgh_ref,
        sq_ref, sk_ref, sv_ref, so_ref, sgc_ref, sgr_ref, sc_in, sn_in, sm_in,
        out_ref, h_ref, c_out, n_out, m_out, sh_ref, sc_out, sn_out, sm_out,
        kbuf, vbuf, sem, c_scr, n_scr, m_scr, *, lam_init, layer, chunks):
    i = pl.program_id(0)

    @pl.when(i == 0)
    def _():
        c_scr[...] = jnp.zeros_like(c_scr)
        n_scr[...] = jnp.zeros_like(n_scr)
        m_scr[...] = jnp.zeros_like(m_scr)

    slot = _paged_fetch(pt_ref, ck_hbm, cv_hbm, kbuf, vbuf, sem, layer)
    _run_phases(
        _mlstm_prompt_phases(i % chunks == 0, mq_ref, mk_ref, mv_ref, mo_ref, gc_ref, gr_ref, mgh_ref,
                             h_ref, c_out, n_out, m_out, c_scr, n_scr, m_scr),
        _mlstm_sample_phases(sq_ref, sk_ref, sv_ref, so_ref, sgc_ref, sgr_ref, mgh_ref, sc_in, sn_in, sm_in,
                             sh_ref, sc_out, sn_out, sm_out),
        _da_sample_phases(slot, q_ref, kn_ref, vn_ref, lam_ref, gh_ref, out_ref, kbuf, vbuf, lam_init))


def _da_sample_mlstm_prompt(page_table, dq, dk, dv, da_lambda, g_da, cache_k, cache_v, lam_init, layer,
                            mq, mk, mv, mo, gc, gr, g_ml, Bp, Tp, sq, sk, sv, so, sgc, sgr, c0, n0, m0):
    B, T, _ = dq.shape
    rows = Bp * Tp
    assert rows % B == 0 and Tp % (rows // B) == 0, "one mLSTM chunk per sample batch element"
    L = rows // B
    chunks = Tp // L
    H, D = ML_HEADS, ML_HEAD_DIM
    n_pages = page_table.shape[1]
    page_rows = cache_k.shape[2]
    blk = pl.BlockSpec((1, T, DA_WIDTH), lambda i, pt: (i, 0, 0))
    kv_blk = pl.BlockSpec((1, T, DA_HEADS, DA_V_DIM), lambda i, pt: (i, 0, 0, 0))
    const = lambda a: pl.BlockSpec(a.shape, lambda i, pt: (0,) * a.ndim)
    mblk = lambda w: pl.BlockSpec((L, w), lambda i, pt: (i, 0))
    state = lambda *s: pl.BlockSpec((1, H) + s, lambda i, pt: (i // chunks, 0, 0, 0))
    s3 = lambda *s: pl.BlockSpec((1,) + s, lambda i, pt: (i, 0, 0))
    s4 = lambda *s: pl.BlockSpec((1, H) + s, lambda i, pt: (i, 0, 0, 0))
    sample_state = [s4(D, D), s4(1, D), s4(1, 1)]
    grid_spec = pltpu.PrefetchScalarGridSpec(
        num_scalar_prefetch=1, grid=(B,),
        in_specs=[blk, kv_blk, kv_blk, const(da_lambda), const(g_da),
                  pl.BlockSpec(memory_space=pl.ANY), pl.BlockSpec(memory_space=pl.ANY),
                  mblk(ML_WIDTH), mblk(ML_WIDTH), mblk(ML_WIDTH), mblk(ML_WIDTH), mblk(LANES),
                  pl.BlockSpec((8, L), lambda i, pt: (0, i)), const(g_ml),
                  s3(T, ML_WIDTH), s3(T, ML_WIDTH), s3(T, ML_WIDTH), s3(T, ML_WIDTH), s3(T, LANES), s3(8, T)]
                 + sample_state,
        out_specs=[blk, mblk(ML_WIDTH), state(D, D), state(1, D), state(1, 1), s3(T, ML_WIDTH)] + sample_state,
        scratch_shapes=[pltpu.VMEM((2, n_pages, page_rows, DA_V_DIM), F32),
                        pltpu.VMEM((2, n_pages, page_rows, DA_V_DIM), F32),
                        pltpu.SemaphoreType.DMA((2, 2)),
                        pltpu.VMEM((H, D, D), F32), pltpu.VMEM((H, 1, D), F32), pltpu.VMEM((H, 1, 1), F32)])
    return pl.pallas_call(
        functools.partial(_da_sample_mlstm_prompt_kernel, lam_init=lam_init, layer=layer, chunks=chunks),
        grid_spec=grid_spec,
        out_shape=[jax.ShapeDtypeStruct((B, T, DA_WIDTH), BF16),
                   jax.ShapeDtypeStruct((rows, ML_WIDTH), BF16),
                   jax.ShapeDtypeStruct((Bp, H, D, D), F32),
                   jax.ShapeDtypeStruct((Bp, H, 1, D), F32),
                   jax.ShapeDtypeStruct((Bp, H, 1, 1), F32),
                   jax.ShapeDtypeStruct((B, T, ML_WIDTH), BF16),
                   jax.ShapeDtypeStruct((B, H, D, D), F32),
                   jax.ShapeDtypeStruct((B, H, 1, D), F32),
                   jax.ShapeDtypeStruct((B, H, 1, 1), F32)],
        compiler_params=_params("arbitrary"), name="da_sample_mlstm_prompt")(
            page_table, dq, dk, dv, da_lambda, g_da, cache_k, cache_v, mq, mk, mv, mo, gc, gr, g_ml,
            sq, sk, sv, so, sgc, sgr, c0, n0, m0)


def _proj_norm_kernel(*refs, n_in, has_next):
    a_refs, w_refs = refs[:n_in], refs[n_in:2 * n_in]
    x_ref, gpost_ref, gpre_ref = refs[2 * n_in:2 * n_in + 3]
    rest = refs[2 * n_in + 3:]
    wn_ref = rest[0] if has_next else None
    xo_ref, ho_ref = rest[-2:]
    acc = _dot(a_refs[0][...], w_refs[0][...])
    for a, w in zip(a_refs[1:], w_refs[1:]):
        acc = acc + _dot(a[...], w[...])
    x1 = x_ref[...] + _rms(acc, gpost_ref[...])
    xo_ref[...] = x1
    hn = _rms(x1, gpre_ref[...]).astype(BF16)
    ho_ref[...] = (_dot(hn, wn_ref[...]) if has_next else hn).astype(ho_ref.dtype)


def _proj_norm(a_list, w_list, x, g_post, g_pre, w_next, tm, h_dtype=BF16):
    rows, d = x.shape
    n_in = len(a_list)
    has_next = w_next is not None
    row_spec = lambda w: pl.BlockSpec((tm, w), lambda i: (i, 0))
    full = lambda a: pl.BlockSpec(a.shape, lambda i: (0,) * a.ndim)
    ins = list(a_list) + list(w_list) + [x, g_post, g_pre] + ([w_next] if has_next else [])
    in_specs = ([row_spec(a.shape[1]) for a in a_list] + [full(w) for w in w_list]
                + [row_spec(d), full(g_post), full(g_pre)] + ([full(w_next)] if has_next else []))
    n_out = w_next.shape[1] if has_next else d
    return pl.pallas_call(
        functools.partial(_proj_norm_kernel, n_in=n_in, has_next=has_next), grid=(rows // tm,),
        in_specs=in_specs, out_specs=[row_spec(d), row_spec(n_out)],
        out_shape=[jax.ShapeDtypeStruct((rows, d), F32), jax.ShapeDtypeStruct((rows, n_out), h_dtype)],
        compiler_params=_params("parallel"), name="proj_norm")(*ins)


def _mem_kv_kernel(x_ref, g_ref, wk_ref, wv_ref, k_ref, v_ref, kb_ref, vb_ref):
    h = _rms(x_ref[...], g_ref[...]).astype(BF16)
    k = _dot(h, wk_ref[...])
    v = _dot(h, wv_ref[...])
    k_ref[...], v_ref[...] = k, v
    kb_ref[...], vb_ref[...] = k.astype(BF16), v.astype(BF16)


def _mem_kv(x, g, wk, wv, tm):
    rows, d = x.shape
    n = wk.shape[1]
    row_spec = lambda w: pl.BlockSpec((tm, w), lambda i: (i, 0))
    return pl.pallas_call(
        _mem_kv_kernel, grid=(rows // tm,),
        in_specs=[row_spec(d), _resident(g), _resident(wk), _resident(wv)],
        out_specs=[row_spec(n)] * 4,
        out_shape=[jax.ShapeDtypeStruct((rows, n), F32)] * 2 + [jax.ShapeDtypeStruct((rows, n), BF16)] * 2,
        compiler_params=_params("parallel"), name="mem_kv")(x, g, wk, wv)


def _mid_phases(rows, hml_ref, hda_ref, x_ref, woml_ref, woda_ref, gpost1_ref, gpre1_ref, wq_ref,
                mk_ref, mv_ref, wmo_ref, gpost2_ref, gpre2_ref, x2_ref, hf_ref):
    acc = _dot(hml_ref[rows, :], woml_ref[...]) + _dot(hda_ref[rows, :], woda_ref[...])
    yield
    x1 = x_ref[rows, :] + _rms(acc, gpost1_ref[...])
    qm = _dot(_rms(x1, gpre1_ref[...]).astype(BF16), wq_ref[...]).astype(BF16)
    yield
    hd = qm.shape[1] // MEM_HEADS
    hcols = [slice(hh * hd, (hh + 1) * hd) for hh in range(MEM_HEADS)]
    s = [_dot_nt(qm[:, c], mk_ref[0, :, c]) * (hd ** -0.5) for c in hcols]
    yield
    p = [jnp.exp(x - jnp.max(x, axis=1, keepdims=True)) for x in s]
    pv = [_dot(x.astype(BF16), mv_ref[0, :, c]) for x, c in zip(p, hcols)]
    yield
    o = jnp.concatenate([(a / jnp.sum(x, axis=1, keepdims=True)).astype(BF16) for a, x in zip(pv, p)], axis=1)
    acc2 = _dot(o, wmo_ref[...])
    yield
    x2 = x1 + _rms(acc2, gpost2_ref[...])
    x2_ref[rows, :] = x2
    hf_ref[rows, :] = _rms(x2, gpre2_ref[...]).astype(hf_ref.dtype)


def _cache_fetch(mk_hbm, mv_hbm, kbuf, vbuf, sem, layer):
    g = pl.program_id(0)
    bb = kbuf.shape[1]

    def copies(gi, slot):
        out = []
        for bi in range(bb):
            for hh in range(MEM_HEADS):
                b = gi * bb + bi
                out.append(pltpu.make_async_copy(mk_hbm.at[layer, b, :, hh, :], kbuf.at[slot, bi, hh],
                                                 sem.at[slot, 0]))
                out.append(pltpu.make_async_copy(mv_hbm.at[layer, b, :, hh, :], vbuf.at[slot, bi, hh],
                                                 sem.at[slot, 1]))
        return out

    @pl.when(g == 0)
    def _():
        for cp in copies(0, 0):
            cp.start()

    @pl.when(g + 1 < pl.num_programs(0))
    def _():
        for cp in copies(g + 1, (g + 1) % 2):
            cp.start()

    slot = g % 2
    for cp in copies(g, slot):
        cp.wait()
    return slot


def _mem_attn_cache_phases(slot, q_ref, o_ref, kbuf, vbuf):
    bb, hd = q_ref.shape[0], kbuf.shape[-1]
    items = [(bi, hh) for bi in range(bb) for hh in range(MEM_HEADS)]
    cols = lambda hh: slice(hh * hd, (hh + 1) * hd)
    s = [_dot_nt(q_ref[bi, :, cols(hh)].astype(BF16), kbuf[slot, bi, hh].astype(BF16)) * (hd ** -0.5)
         for bi, hh in items]
    yield
    p = [jnp.exp(x - jnp.max(x, axis=1, keepdims=True)) for x in s]
    pv = [_dot(x.astype(BF16), vbuf[slot, bi, hh].astype(BF16)) for x, (bi, hh) in zip(p, items)]
    yield
    for (bi, hh), a, x in zip(items, pv, p):
        o_ref[bi, :, cols(hh)] = (a / jnp.sum(x, axis=1, keepdims=True)).astype(o_ref.dtype)


def _mid_kernel(*refs, n_sub, layer):
    (hml_ref, hda_ref, x_ref, woml_ref, woda_ref, gpost1_ref, gpre1_ref, wq_ref, mk_ref, mv_ref, wmo_ref,
     gpost2_ref, gpre2_ref, qs_ref, ck_hbm, cv_hbm, x2_ref, hf_ref, os_ref, kbuf, vbuf, sem) = refs
    slot = _cache_fetch(ck_hbm, cv_hbm, kbuf, vbuf, sem, layer)
    sub = x_ref.shape[0] // n_sub
    prompt = refs[:13] + (x2_ref, hf_ref)
    _run_phases(_mem_attn_cache_phases(slot, qs_ref, os_ref, kbuf, vbuf),
                *[_mid_phases(slice(j * sub, (j + 1) * sub), *prompt) for j in range(n_sub)])


def _mid(h_ml, h_da, x, wo_ml, wo_da, g_post1, g_pre1, wq, mk, mv, wmo, g_post2, g_pre2, T, tm,
         q_s, cache_k, cache_v, layer):
    rows, d = x.shape
    tiles = T // tm
    steps = rows // tm
    Bs, Ts, _ = q_s.shape
    assert Bs % steps == 0, "one group of sample batch elements per prompt row tile"
    bb = Bs // steps
    M, H, hd = cache_k.shape[2:]
    row_spec = lambda w: pl.BlockSpec((tm, w), lambda i: (i, 0))
    kv_spec = pl.BlockSpec((1,) + mk.shape[1:], lambda i: (i // tiles, 0, 0))
    grp_spec = pl.BlockSpec((bb, Ts, d), lambda i: (i, 0, 0))
    hbm = pl.BlockSpec(memory_space=pl.ANY)
    ins = (h_ml, h_da, x, wo_ml, wo_da, g_post1, g_pre1, wq, mk, mv, wmo, g_post2, g_pre2)
    in_specs = [row_spec(h_ml.shape[1]), row_spec(h_da.shape[1]), row_spec(d)] + [
        kv_spec if a is mk or a is mv else _resident(a) for a in ins[3:]] + [grp_spec, hbm, hbm]
    return pl.pallas_call(
        functools.partial(_mid_kernel, n_sub=2, layer=layer), grid=(steps,),
        in_specs=in_specs, out_specs=[row_spec(d), row_spec(d), grp_spec],
        out_shape=[jax.ShapeDtypeStruct((rows, d), F32), jax.ShapeDtypeStruct((rows, d), BF16),
                   jax.ShapeDtypeStruct((Bs, Ts, d), BF16)],
        scratch_shapes=[pltpu.VMEM((2, bb, H, M, hd), F32), pltpu.VMEM((2, bb, H, M, hd), F32),
                        pltpu.SemaphoreType.DMA((2, 2))],
        compiler_params=_params("arbitrary"), name="mid")(*ins, q_s, cache_k, cache_v)


def _ffn_chunks(nh):
    nchunk = 2 if (nh // LANES) % 2 == 0 else 1
    cw = nh // nchunk
    return [(j * cw, cw) for j in range(nchunk)]


def _ffn_prompt_kernel(hf_ref, x_ref, wup_ref, wdw_ref, bdw_ref, wdn_ref, gpost_ref, y_ref, ulast_ref, ubuf,
                       *, tiles_per_seq):
    i = pl.program_id(0)
    tm = hf_ref.shape[0]
    nh = wdn_ref.shape[0]
    halo = ubuf.shape[0] - tm

    @pl.when(i % tiles_per_seq == 0)
    def _():
        ubuf[0:halo, :] = jnp.zeros((halo, ubuf.shape[1]), F32)

    hf = hf_ref[...]
    f = jnp.zeros((tm, y_ref.shape[1]), F32)
    for c0, cw in _ffn_chunks(nh):
        cg = []
        for base in (c0, nh + c0):
            cs = slice(base, base + cw)
            ubuf[halo:halo + tm, cs] = _dot(hf, wup_ref[:, cs])
            c = bdw_ref[:, cs]
            for j in range(CONV_W):
                lo = halo - (CONV_W - 1) + j
                c = c + ubuf[lo:lo + tm, cs] * wdw_ref[j:j + 1, cs]
            cg.append(c)
        act = (jax.nn.silu(cg[1]) * cg[0]).astype(BF16)
        f = f + _dot(act, wdn_ref[c0:c0 + cw, :])
    y_ref[...] = x_ref[...] + _rms(f, gpost_ref[...])
    tail = ubuf[tm:tm + halo, :]
    ubuf[0:halo, :] = tail
    ulast_ref[0] = tail


def _ffn_prompt(hf, x, w_up, w_dw, b_dw, w_down, g_post, B, T, tm):
    rows, d = x.shape
    npad = w_up.shape[1]
    tiles = T // tm
    halo = 8
    row_spec = lambda w: pl.BlockSpec((tm, w), lambda i: (i, 0))
    full = _resident
    return pl.pallas_call(
        functools.partial(_ffn_prompt_kernel, tiles_per_seq=tiles), grid=(rows // tm,),
        in_specs=[row_spec(d), row_spec(d), full(w_up), full(w_dw), full(b_dw), full(w_down), full(g_post)],
        out_specs=[row_spec(d), pl.BlockSpec((1, halo, npad), lambda i: (i // tiles, 0, 0))],
        out_shape=[jax.ShapeDtypeStruct((rows, d), F32), jax.ShapeDtypeStruct((B, halo, npad), F32)],
        scratch_shapes=[pltpu.VMEM((tm + halo, npad), F32)],
        compiler_params=_params("arbitrary"), name="ffn_prompt")(hf, x, w_up, w_dw, b_dw, w_down, g_post)


def _ffn_sample_kernel(hf_ref, x_ref, cb_ref, wup_ref, wdw_ref, bdw_ref, wdn_ref, gpost_ref, y_ref, unew_ref,
                       *, T):
    nb = hf_ref.shape[0] // T
    nh = wdn_ref.shape[0]
    hf = hf_ref[...]
    f = jnp.zeros(y_ref.shape, F32)
    for c0, cw in _ffn_chunks(nh):
        cg = []
        for base in (c0, nh + c0):
            cs = slice(base, base + cw)
            u = _dot(hf, wup_ref[:, cs])
            ext = [cb_ref[j, :, cs] for j in range(CONV_W - 1)] + [u[t * nb:(t + 1) * nb] for t in range(T)]
            for j in range(CONV_W - 1):
                unew_ref[j, :, cs] = ext[len(ext) - (CONV_W - 1) + j]
            rows = []
            for t in range(T):
                c = bdw_ref[:, cs]
                for j in range(CONV_W):
                    c = c + ext[t + j] * wdw_ref[j:j + 1, cs]
                rows.append(c)
            cg.append(jnp.concatenate(rows, axis=0))
        act = (jax.nn.silu(cg[1]) * cg[0]).astype(BF16)
        f = f + _dot(act, wdn_ref[c0:c0 + cw, :])
    y_ref[...] = x_ref[...] + _rms(f, gpost_ref[...])


def _ffn_sample(hf, x, cb, w_up, w_dw, b_dw, w_down, g_post, T):
    rows, d = x.shape
    npad = w_up.shape[1]
    ins = (hf, x, cb, w_up, w_dw, b_dw, w_down, g_post)
    full = lambda a: pl.BlockSpec(a.shape, lambda i: (0,) * a.ndim)
    return pl.pallas_call(
        functools.partial(_ffn_sample_kernel, T=T), grid=(1,),
        in_specs=[full(a) for a in ins],
        out_specs=[pl.BlockSpec((rows, d), lambda i: (0, 0)),
                   pl.BlockSpec((CONV_W - 1, rows // T, npad), lambda i: (0, 0, 0))],
        out_shape=[jax.ShapeDtypeStruct((rows, d), F32),
                   jax.ShapeDtypeStruct((CONV_W - 1, rows // T, npad), F32)],
        compiler_params=_params("arbitrary"), name="ffn_sample")(*ins)


def _pad_halves(a, nh, nh_pad):
    pad = [(0, 0)] * (a.ndim - 1) + [(0, nh_pad - nh)]
    return jnp.concatenate([jnp.pad(a[..., :nh], pad), jnp.pad(a[..., nh:], pad)], axis=-1)


def _unpad_halves(a, nh, nh_pad):
    return jnp.concatenate([a[..., :nh], a[..., nh_pad:nh_pad + nh]], axis=-1)


def _row_tile(rows, want):
    t = min(rows, want)
    while rows % t:
        t //= 2
    return t


def kernel(x_prompt, x_sample, cache_dk, cache_dv, cache_mem_k, cache_mem_v, state_ml_C, state_ml_n, state_ml_m, state_conv, page_table, mem_prompt, g_mix_pre, g_mix_post, w_in, b_if, g_ml_head, da_lambda, g_da_head, w_out, g_mem_pre, g_mem_post, g_mem_src, w_mq, w_mk, w_mv, w_mo, g_ffn_pre, g_ffn_post, w_up, w_dw, b_dw, w_down):
    depth = w_in.shape[0]
    Bp, Tp, d = x_prompt.shape
    Bs, Ts, _ = x_sample.shape
    n_pages, page = page_table.shape[1], cache_dk.shape[2]
    past_len = n_pages * page
    n_mem = mem_prompt.shape[1]
    nh = w_down.shape[1]
    nh_pad = -(-nh // LANES) * LANES
    H, D = ML_HEADS, ML_HEAD_DIM
    rows_p, rows_s = Bp * Tp, Bs * Ts

    tm_p = _row_tile(Tp, 512)
    tm_s = _row_tile(rows_s, 256)
    tab_p = _rope_tables(Tp, Tp, 0)
    tab_s = _rope_tables(tm_s, Ts, past_len)

    yp = x_prompt.reshape(rows_p, d)
    ys = x_sample.reshape(rows_s, d)
    outs = [[] for _ in range(14)]
    row = lambda a: a.reshape(1, -1)
    for l in range(depth):
        lam_init = 0.8 - 0.6 * math.exp(-0.3 * l)
        wi = w_in[l]
        w_ml = wi[:, :4 * ML_WIDTH].astype(BF16)
        w_da = wi[:, 4 * ML_WIDTH + 2 * H:].astype(BF16)
        w_gate = wi[:, 4 * ML_WIDTH:4 * ML_WIDTH + 2 * H]
        w_g = jnp.pad(w_gate, ((0, 0), (0, LANES - 2 * H))).astype(BF16)
        b_col = jnp.pad(b_if[l], (0, LANES - 2 * H)).reshape(1, LANES)
        wo_ml, wo_da = w_out[l][:ML_WIDTH].astype(BF16), w_out[l][ML_WIDTH:].astype(BF16)
        wq_b, wo_b = w_mq[l].astype(BF16), w_mo[l].astype(BF16)
        wk_b, wv_b = w_mk[l].astype(BF16), w_mv[l].astype(BF16)
        wup_b = _pad_halves(w_up[l], nh, nh_pad).astype(BF16)
        wdw_p = _pad_halves(w_dw[l], nh, nh_pad)
        bdw_p = _pad_halves(b_dw[l].reshape(1, -1), nh, nh_pad)
        wdn_b = jnp.pad(w_down[l], ((0, nh_pad - nh), (0, 0))).astype(BF16)
        g_da3 = g_da_head[l].reshape(DA_HEADS, 1, DA_V_DIM)

        q, k, v, o, gc, gr, dq, dk, dv, dkb, dvb = _in_proj(
            yp, row(g_mix_pre[l]), w_ml, w_da, w_g, b_col, tab_p, BF16, _row_tile(Tp, 1024),
            DA_QK_DIM ** -0.5 * math.log2(math.e))
        qs, ks, vs, os_, gcs, grs, dqs, dk_s, dv_s, _, _ = _in_proj(
            ys, row(g_mix_pre[l]), w_ml, w_da, w_g, b_col, tab_s, F32, tm_s, 1.0)
        r3 = lambda a: a.reshape(Bs, Ts, a.shape[-1])
        r4 = lambda a: a.reshape(Bs, Ts, DA_HEADS, DA_V_DIM)
        gr3 = grs.reshape(2 * H, Bs, Ts).transpose(1, 0, 2)
        h_da_s, h_ml, C_p, n_p, m_p, h_ml_s, C_s, n_s, m_s = _da_sample_mlstm_prompt(
            page_table, r3(dqs), r4(dk_s), r4(dv_s), da_lambda[l], g_da_head[l],
            cache_dk.reshape(depth, -1, page * DA_HEADS, DA_V_DIM),
            cache_dv.reshape(depth, -1, page * DA_HEADS, DA_V_DIM), lam_init, l,
            q, k, v, o, gc, gr, g_ml_head[l], Bp, Tp,
            r3(qs), r3(ks), r3(vs), r3(os_), r3(gcs), gr3,
            state_ml_C[l], state_ml_n[l].reshape(Bs, H, 1, D), state_ml_m[l].reshape(Bs, H, 1, 1))

        x1_s, qm_s = _proj_norm([h_ml_s.reshape(rows_s, ML_WIDTH), h_da_s.reshape(rows_s, DA_WIDTH)],
                                [wo_ml, wo_da], ys, row(g_mix_post[l]), row(g_mem_pre[l]), wq_b, tm_s,
                                h_dtype=F32)

        h_da = _da_prompt(dq, dkb, dvb, da_lambda[l], g_da3, Bp, Tp, _row_tile(Tp, 512), 2, lam_init)
        mk_p, mv_p, mkb, mvb = _mem_kv(mem_prompt.reshape(Bp * n_mem, d), row(g_mem_src[l]), wk_b, wv_b,
                                       _row_tile(Bp * n_mem, 256))
        x2, hf, om_s = _mid(h_ml, h_da, yp, wo_ml, wo_da, row(g_mix_post[l]), row(g_mem_pre[l]), wq_b,
                            mkb.reshape(Bp, n_mem, d), mvb.reshape(Bp, n_mem, d), wo_b,
                            row(g_mem_post[l]), row(g_ffn_pre[l]), Tp, tm_p,
                            qm_s.reshape(Bs, Ts, d), cache_mem_k, cache_mem_v, l)
        yp, ulast = _ffn_prompt(hf, x2, wup_b, wdw_p, bdw_p, wdn_b, row(g_ffn_post[l]), Bp, Tp, tm_p)
        cv_p = _unpad_halves(ulast[:, ulast.shape[1] - (CONV_W - 1):], nh, nh_pad)

        x2, hf = _proj_norm([om_s.reshape(rows_s, d)], [wo_b], x1_s, row(g_mem_post[l]), row(g_ffn_pre[l]),
                            None, tm_s)
        tmaj = lambda a: a.reshape(Bs, Ts, -1).transpose(1, 0, 2).reshape(rows_s, -1)
        cb = _pad_halves(state_conv[l], nh, nh_pad).transpose(1, 0, 2)
        y_t, unew = _ffn_sample(tmaj(hf), tmaj(x2), cb, wup_b, wdw_p, bdw_p, wdn_b, row(g_ffn_post[l]), Ts)
        ys = y_t.reshape(Ts, Bs, d).transpose(1, 0, 2).reshape(rows_s, d)
        cv_s = _unpad_halves(unew.transpose(1, 0, 2), nh, nh_pad)

        vals = (dk.reshape(Bp, Tp, DA_HEADS, DA_V_DIM), dv.reshape(Bp, Tp, DA_HEADS, DA_V_DIM),
                mk_p.reshape(Bp, n_mem, MEM_HEADS, d // MEM_HEADS), mv_p.reshape(Bp, n_mem, MEM_HEADS, d // MEM_HEADS),
                C_p, n_p.reshape(Bp, H, D), m_p.reshape(Bp, H), cv_p,
                dk_s.reshape(Bs, Ts, DA_HEADS, DA_V_DIM), dv_s.reshape(Bs, Ts, DA_HEADS, DA_V_DIM),
                C_s, n_s.reshape(Bs, H, D), m_s.reshape(Bs, H), cv_s)
        for acc, val in zip(outs, vals):
            acc.append(val)
    return (yp.reshape(Bp, Tp, d), ys.reshape(Bs, Ts, d)) + tuple(jnp.stack(a) for a in outs)
```

```python
import functools
import math

import jax
import jax.numpy as jnp
from jax import lax
from jax.experimental import pallas as pl
from jax.experimental.pallas import tpu as pltpu

F32 = jnp.float32
BF16 = jnp.bfloat16

ML_HEADS = 4
ML_HEAD_DIM = 128
ML_WIDTH = ML_HEADS * ML_HEAD_DIM
DA_HEADS = 4
DA_V_DIM = 128
DA_QK_DIM = 64
DA_WIDTH = DA_HEADS * DA_V_DIM
ROPE_DIM = 16
ROPE_THETA = 500000.0
MEM_HEADS = 4
CONV_W = 3
RMS_EPS = 1e-6
LANES = 128
NEG_BIG = -1e30
VMEM_LIMIT = 56 * 1024 * 1024


def _params(*sem, vmem=VMEM_LIMIT):
    return pltpu.CompilerParams(dimension_semantics=sem, vmem_limit_bytes=vmem)


def _resident(a):
    return pl.BlockSpec(a.shape, lambda *_: (0,) * a.ndim, pipeline_mode=pl.Buffered(1))


def _rms(x, g):
    return x * lax.rsqrt(jnp.mean(x * x, axis=-1, keepdims=True) + RMS_EPS) * g


def _log_sigmoid(x):
    return jnp.minimum(x, 0.0) - jnp.log1p(jnp.exp(-jnp.abs(x)))


def _dot(a, b):
    return jnp.dot(a, b, preferred_element_type=F32)


def _dot_nt(a, b):
    return lax.dot_general(a, b, (((1,), (1,)), ((), ())), preferred_element_type=F32)


def _dot_tn(a, b):
    return lax.dot_general(a, b, (((0,), (0,)), ((), ())), preferred_element_type=F32)


def _rope_table_kernel(cos_ref, sa_ref, sb_ref, *, period, offset):
    rows = cos_ref.shape[0]
    half = ROPE_DIM // 2
    r = lax.broadcasted_iota(jnp.int32, (rows, LANES), 0) + pl.program_id(0) * rows
    lane = lax.broadcasted_iota(jnp.int32, (rows, LANES), 1)
    pos = (offset + r % period).astype(F32)
    c = lane % DA_QK_DIM
    j = (c % half).astype(F32)
    inv = jnp.exp(-math.log(ROPE_THETA) * (2.0 * j / ROPE_DIM))
    ang = pos * inv
    cos, sin = jnp.cos(ang), jnp.sin(ang)
    cos_ref[...] = jnp.where(c < ROPE_DIM, cos, 1.0)
    sa_ref[...] = jnp.where(c < half, -sin, 0.0)
    sb_ref[...] = jnp.where((c >= half) & (c < ROPE_DIM), sin, 0.0)


def _rope_tables(rows, period, offset):
    blk = min(rows, 512)
    spec = pl.BlockSpec((blk, LANES), lambda i: (i, 0))
    shp = jax.ShapeDtypeStruct((rows, LANES), F32)
    return pl.pallas_call(
        functools.partial(_rope_table_kernel, period=period, offset=offset),
        grid=(rows // blk,), out_specs=[spec] * 3, out_shape=[shp] * 3,
        compiler_params=_params("parallel"), name="rope_tables")()


def _rope(x, cos, sa, sb):
    outs = []
    for j in range(x.shape[1] // LANES):
        xj = x[:, j * LANES:(j + 1) * LANES]
        up = pltpu.roll(xj, LANES - ROPE_DIM // 2, axis=1)
        dn = pltpu.roll(xj, ROPE_DIM // 2, axis=1)
        outs.append(xj * cos + up * sa + dn * sb)
    return outs


def _in_proj_kernel(x_ref, g_ref, w_ref, wda_ref, wg_ref, bcol_ref, cos_ref, sa_ref, sb_ref,
                    q_ref, k_ref, v_ref, o_ref, gc_ref, gr_ref, dq_ref, dk_ref, dv_ref, dkb_ref, dvb_ref,
                    *, dq_scale, n_sub):
    sub = x_ref.shape[0] // n_sub
    for u in range(n_sub):
        rows = slice(u * sub, (u + 1) * sub)
        h = _rms(x_ref[rows, :], g_ref[...]).astype(BF16)
        for j, ref in enumerate((q_ref, k_ref, v_ref, o_ref)):
            ref[rows, :] = _dot(h, w_ref[:, j * ML_WIDTH:(j + 1) * ML_WIDTH]).astype(ref.dtype)
        gc = _dot(h, wg_ref[...]) + bcol_ref[...]
        lane = lax.broadcasted_iota(jnp.int32, gc.shape, 1)
        gc = jnp.where(lane < ML_HEADS, gc, _log_sigmoid(gc))
        gc_ref[rows, :] = gc
        gr_ref[:, rows] = gc.T[:2 * ML_HEADS, :]
        cos, sa, sb = cos_ref[rows, :], sa_ref[rows, :], sb_ref[rows, :]
        dq = _rope(_dot(h, wda_ref[:, :DA_WIDTH]), cos, sa, sb)
        dk = _rope(_dot(h, wda_ref[:, DA_WIDTH:2 * DA_WIDTH]), cos, sa, sb)
        dv = _dot(h, wda_ref[:, 2 * DA_WIDTH:3 * DA_WIDTH])
        dvb_ref[rows, :] = dv.astype(BF16)
        for j in range(DA_HEADS):
            cols = slice(j * LANES, (j + 1) * LANES)
            dq_ref[rows, cols] = (dq[j] * dq_scale).astype(dq_ref.dtype)
            dk_ref[rows, j, :] = dk[j]
            dkb_ref[rows, cols] = dk[j].astype(BF16)
            dv_ref[rows, j, :] = dv[:, cols]


def _in_proj(x, g, w_ml, w_da, w_g, b_col, tables, act_dtype, tm, dq_scale):
    rows, d = x.shape
    cos, sa, sb = tables
    nt = cos.shape[0] // tm
    row_spec = lambda w: pl.BlockSpec((tm, w), lambda i: (i, 0))
    full = lambda a: pl.BlockSpec(a.shape, lambda i: (0,) * a.ndim)
    tab_spec = pl.BlockSpec((tm, LANES), lambda i: (i % nt, 0))
    out_shape = [jax.ShapeDtypeStruct((rows, ML_WIDTH), act_dtype)] * 4 + [
        jax.ShapeDtypeStruct((rows, LANES), F32), jax.ShapeDtypeStruct((8, rows), F32),
        jax.ShapeDtypeStruct((rows, DA_WIDTH), act_dtype),
        jax.ShapeDtypeStruct((rows, DA_HEADS, DA_V_DIM), F32), jax.ShapeDtypeStruct((rows, DA_HEADS, DA_V_DIM), F32),
        jax.ShapeDtypeStruct((rows, DA_WIDTH), BF16), jax.ShapeDtypeStruct((rows, DA_WIDTH), BF16)]
    kv_spec = pl.BlockSpec((tm, DA_HEADS, DA_V_DIM), lambda i: (i, 0, 0))
    out_specs = [row_spec(ML_WIDTH)] * 4 + [row_spec(LANES), pl.BlockSpec((8, tm), lambda i: (0, i)),
                                            row_spec(DA_WIDTH), kv_spec, kv_spec,
                                            row_spec(DA_WIDTH), row_spec(DA_WIDTH)]
    return pl.pallas_call(
        functools.partial(_in_proj_kernel, dq_scale=dq_scale, n_sub=1),
        grid=(rows // tm,),
        in_specs=[row_spec(d), full(g), full(w_ml), full(w_da), full(w_g), full(b_col),
                  tab_spec, tab_spec, tab_spec],
        out_specs=out_specs, out_shape=out_shape,
        compiler_params=_params("parallel"), name="in_proj")(
            x, g, w_ml, w_da, w_g, b_col, cos, sa, sb)


def _run_phases(*gens):
    live = list(gens)
    while live:
        for g in list(live):
            try:
                next(g)
            except StopIteration:
                live.remove(g)


def _mlstm_phases(heads, out):
    L, D = heads[0][0].shape
    scale = D ** -0.5
    qb = [h[0].astype(BF16) for h in heads]
    vb = [h[2].astype(BF16) for h in heads]
    qk = [_dot_nt(qb[j], heads[j][1].astype(BF16)) for j in range(len(heads))]
    qc = [_dot(qb[j], heads[j][7].astype(BF16)) for j in range(len(heads))]
    r = lax.broadcasted_iota(jnp.int32, (L, L), 0)
    c = lax.broadcasted_iota(jnp.int32, (L, L), 1)
    tri = c <= r
    gate = []
    for q, k, v, i_col, f_col, i_row, f_row, C, n, m in heads:
        b_col = jnp.sum(jnp.where(tri, f_row, 0.0), axis=1, keepdims=True)
        b_row = jnp.sum(jnp.where(r <= c, f_col, 0.0), axis=0, keepdims=True)
        log_d = jnp.where(tri, b_col - b_row + i_row, -jnp.inf)
        inter = b_col + m
        m_row = jnp.maximum(jnp.max(log_d, axis=1, keepdims=True), inter)
        b_last = b_col[L - 1:L, :]
        log_w = b_last - b_col + i_col
        m_new = jnp.maximum(b_last + m, jnp.max(log_w, axis=0, keepdims=True))
        gate.append(dict(
            m_row=m_row, w_inter=jnp.exp(inter - m_row), d=scale * jnp.exp(log_d - m_row), m_new=m_new,
            decay=jnp.exp(b_last + m - m_new), wk=(scale * jnp.exp(log_w - m_new)) * k.astype(F32)))
    yield
    s = [qk[j] * gate[j]["d"] for j in range(len(heads))]
    yield
    sv = [_dot(s[j].astype(BF16), vb[j]) for j in range(len(heads))]
    kv = [_dot_tn(gate[j]["wk"].astype(BF16), vb[j]) for j in range(len(heads))]
    yield
    for j, (q, k, v, i_col, f_col, i_row, f_row, C, n, m) in enumerate(heads):
        g = gate[j]
        num = g["w_inter"] * qc[j] + sv[j]
        den = (g["w_inter"] * jnp.sum(q.astype(F32) * n, axis=1, keepdims=True)
               + jnp.sum(s[j], axis=1, keepdims=True))
        h = num / jnp.maximum(jnp.abs(den), jnp.exp(-g["m_row"]))
        c_new = g["decay"] * C + kv[j]
        n_new = g["decay"] * n + jnp.sum(g["wk"], axis=0, keepdims=True)
        out.append((h, c_new, n_new, g["m_new"]))


def _ml_head_out(h, o, g):
    return _rms(h, g) * jax.nn.sigmoid(o.astype(F32))


def _mlstm_prompt_phases(first_chunk, q_ref, k_ref, v_ref, o_ref, gc_ref, gr_ref, gh_ref,
                         h_ref, c_out, n_out, m_out, c_scr, n_scr, m_scr):
    gc, gr = gc_ref[...], gr_ref[...]
    cols = [slice(hh * ML_HEAD_DIM, (hh + 1) * ML_HEAD_DIM) for hh in range(ML_HEADS)]
    prev = lambda ref, hh: jnp.where(first_chunk, 0.0, ref[hh])
    res = []
    yield from _mlstm_phases([
        (q_ref[:, cols[hh]], k_ref[:, cols[hh]], v_ref[:, cols[hh]],
         gc[:, hh:hh + 1], gc[:, ML_HEADS + hh:ML_HEADS + hh + 1],
         gr[hh:hh + 1, :], gr[ML_HEADS + hh:ML_HEADS + hh + 1, :],
         prev(c_scr, hh), prev(n_scr, hh), prev(m_scr, hh)) for hh in range(ML_HEADS)], res)
    for hh, (h, c_new, n_new, m_new) in enumerate(res):
        c_scr[hh], n_scr[hh], m_scr[hh] = c_new, n_new, m_new
        c_out[0, hh], n_out[0, hh], m_out[0, hh] = c_new, n_new, m_new
        h_ref[:, cols[hh]] = _ml_head_out(h, o_ref[:, cols[hh]], gh_ref[hh:hh + 1, :]).astype(h_ref.dtype)


def _mlstm_sample_phases(q_ref, k_ref, v_ref, o_ref, gc_ref, gr_ref, gh_ref, c_in, n_in, m_in,
                         h_ref, c_out, n_out, m_out):
    gc, gr = gc_ref[0], gr_ref[0]
    cols = [slice(hh * ML_HEAD_DIM, (hh + 1) * ML_HEAD_DIM) for hh in range(ML_HEADS)]
    res = []
    yield from _mlstm_phases([
        (q_ref[0, :, cols[hh]], k_ref[0, :, cols[hh]], v_ref[0, :, cols[hh]],
         gc[:, hh:hh + 1], gc[:, ML_HEADS + hh:ML_HEADS + hh + 1],
         gr[hh:hh + 1, :], gr[ML_HEADS + hh:ML_HEADS + hh + 1, :],
         c_in[0, hh], n_in[0, hh], m_in[0, hh]) for hh in range(ML_HEADS)], res)
    for hh, (h, c_new, n_new, m_new) in enumerate(res):
        c_out[0, hh], n_out[0, hh], m_out[0, hh] = c_new, n_new, m_new
        h_ref[0, :, cols[hh]] = _ml_head_out(h, o_ref[0, :, cols[hh]], gh_ref[hh:hh + 1, :]).astype(h_ref.dtype)


def _da_lambda(lam_ref, lam_init):
    lv = lam_ref[...]
    a = jnp.sum(lv[0:1, :] * lv[1:2, :], axis=1, keepdims=True)
    b = jnp.sum(lv[2:3, :] * lv[3:4, :], axis=1, keepdims=True)
    return jnp.exp(a) - jnp.exp(b) + lam_init


def _stack_components(q):
    lane = lax.broadcasted_iota(jnp.int32, q.shape, 1)
    zero = jnp.zeros_like(q)
    return jnp.concatenate([jnp.where(lane < DA_QK_DIM, q, zero), jnp.where(lane >= DA_QK_DIM, q, zero)], axis=0)


def _da_prompt_kernel(q_ref, k_ref, v_ref, lam_ref, gh_ref, out_ref, vt_scr, s_a, s_b, m_scr, l_scr, acc_scr,
                      *, lam_init):
    i = pl.program_id(2)
    tq = q_ref.shape[0]
    nblk = vt_scr.shape[1]
    heads = range(vt_scr.shape[0])
    hcols = [slice(hd * DA_V_DIM, (hd + 1) * DA_V_DIM) for hd in heads]

    @pl.when(i == 0)
    def _():
        for hd in heads:
            for j in range(nblk):
                vt_scr[hd, j] = v_ref[j * tq:(j + 1) * tq, hcols[hd]].astype(F32).T.astype(BF16)

    qqt = []
    for hd in heads:
        qt = q_ref[:, hcols[hd]].astype(F32).T
        dim = lax.broadcasted_iota(jnp.int32, qt.shape, 0)
        qqt.append(jnp.concatenate(
            [jnp.where(dim < DA_QK_DIM, qt, 0.0), jnp.where(dim >= DA_QK_DIM, qt, 0.0)],
            axis=1).astype(BF16))

    def scores(j, s_ref):
        off = pl.multiple_of(j * tq, tq)
        for hd in heads:
            s_ref[hd] = _dot(k_ref[pl.ds(off, tq), hcols[hd]], qqt[hd])

    def accumulate(j, s_ref, masked):
        stats = []
        for hd in heads:
            st = s_ref[hd]
            if masked:
                key = lax.broadcasted_iota(jnp.int32, st.shape, 0)
                query = lax.broadcasted_iota(jnp.int32, st.shape, 1) % tq
                st = jnp.where(key <= query, st, NEG_BIG)
            m_old = m_scr[hd]
            m_new = jnp.maximum(m_old, jnp.max(st, axis=0, keepdims=True))
            alpha = jnp.exp2(m_old - m_new)
            p = jnp.exp2(st - m_new)
            l_scr[hd] = alpha * l_scr[hd] + jnp.sum(p, axis=0, keepdims=True)
            m_scr[hd] = m_new
            stats.append((alpha, p.astype(BF16)))
        for hd, (alpha, p) in zip(heads, stats):
            acc_scr[hd] = alpha * acc_scr[hd] + _dot(vt_scr[hd, j], p)

    m_scr[...] = jnp.full_like(m_scr, NEG_BIG)
    l_scr[...] = jnp.zeros_like(l_scr)
    acc_scr[...] = jnp.zeros_like(acc_scr)
    scores(0, s_a)

    def body(t, carry):
        scores(2 * t + 1, s_b)
        accumulate(2 * t, s_a, False)
        scores(2 * t + 2, s_a)
        accumulate(2 * t + 1, s_b, False)
        return carry

    lax.fori_loop(0, i // 2, body, 0)

    @pl.when(i % 2 == 1)
    def _():
        scores(i, s_b)
        accumulate(i - 1, s_a, False)
        accumulate(i, s_b, True)

    @pl.when(i % 2 == 0)
    def _():
        accumulate(i, s_a, True)

    lam = _da_lambda(lam_ref, lam_init)
    for hd in heads:
        ot = acc_scr[hd] / l_scr[hd]
        at = ot[:, :tq] - lam * ot[:, tq:]
        norm = at * lax.rsqrt(jnp.mean(at * at, axis=0, keepdims=True) + RMS_EPS)
        out_ref[:, hcols[hd]] = ((norm.T * gh_ref[hd]) * (1.0 - lam_init)).astype(out_ref.dtype)


def _da_prompt(dq, dk, dv, da_lambda, g_head3, B, T, tq, hp, lam_init):
    nq = T // tq
    rows = B * T
    w = hp * DA_V_DIM
    kv_spec = pl.BlockSpec((T, w), lambda b, h, i: (b, h))
    return pl.pallas_call(
        functools.partial(_da_prompt_kernel, lam_init=lam_init), grid=(B, DA_HEADS // hp, nq),
        in_specs=[pl.BlockSpec((tq, w), lambda b, h, i: (b * nq + i, h)), kv_spec, kv_spec,
                  pl.BlockSpec(da_lambda.shape, lambda b, h, i: (0, 0)),
                  pl.BlockSpec((hp, 1, DA_V_DIM), lambda b, h, i: (h, 0, 0))],
        out_specs=pl.BlockSpec((tq, w), lambda b, h, i: (b * nq + i, h)),
        out_shape=jax.ShapeDtypeStruct((rows, DA_WIDTH), BF16),
        scratch_shapes=[pltpu.VMEM((hp, nq, DA_V_DIM, tq), BF16),
                        pltpu.VMEM((hp, tq, 2 * tq), F32), pltpu.VMEM((hp, tq, 2 * tq), F32),
                        pltpu.VMEM((hp, 1, 2 * tq), F32), pltpu.VMEM((hp, 1, 2 * tq), F32),
                        pltpu.VMEM((hp, DA_V_DIM, 2 * tq), F32)],
        compiler_params=_params("parallel", "parallel", "arbitrary"), name="da_prompt")(
            dq, dk, dv, da_lambda, g_head3)


def _paged_fetch(pt_ref, ck_hbm, cv_hbm, kbuf, vbuf, sem, layer):
    b = pl.program_id(0)
    n_pages = kbuf.shape[1]

    def copies(bi, slot):
        out = []
        for p in range(n_pages):
            pg = pt_ref[bi, p]
            out.append(pltpu.make_async_copy(ck_hbm.at[layer, pg], kbuf.at[slot, p], sem.at[slot, 0]))
            out.append(pltpu.make_async_copy(cv_hbm.at[layer, pg], vbuf.at[slot, p], sem.at[slot, 1]))
        return out

    @pl.when(b == 0)
    def _():
        for n, cp in enumerate(copies(0, 0)):
            cp.start(priority=n % 2)

    @pl.when(b + 1 < pl.num_programs(0))
    def _():
        for n, cp in enumerate(copies(b + 1, (b + 1) % 2)):
            cp.start(priority=n % 2)

    slot = b % 2
    for cp in copies(b, slot):
        cp.wait()
    return slot


def _da_sample_phases(slot, q_ref, kn_ref, vn_ref, lam_ref, gh_ref, out_ref, kbuf, vbuf, lam_init):
    n_pages, page = kbuf.shape[1], kbuf.shape[2] // DA_HEADS
    T = q_ref.shape[1]
    scale = DA_QK_DIM ** -0.5
    lam = _da_lambda(lam_ref, lam_init)
    q_all = q_ref[0]
    n_rows = n_pages * page * DA_HEADS
    qq = jnp.concatenate([_stack_components(q_all[:, hh * DA_V_DIM:(hh + 1) * DA_V_DIM])
                          for hh in range(DA_HEADS)], axis=0)
    k_all = kbuf[slot].reshape(n_rows, DA_V_DIM).astype(BF16)
    yield
    s_raw = _dot_nt(qq.astype(BF16), k_all)
    yield
    rq = 2 * T * DA_HEADS
    own_head = (lax.broadcasted_iota(jnp.int32, (rq, n_rows), 1) % DA_HEADS
                == lax.broadcasted_iota(jnp.int32, (rq, n_rows), 0) // (2 * T))
    s_past = jnp.where(own_head, s_raw * scale, NEG_BIG)
    trow = lax.broadcasted_iota(jnp.int32, (rq, 1), 0) % T

    def per_query_row(new_ref, t):
        x = new_ref[0, t]
        return jnp.concatenate([jnp.broadcast_to(x[hh:hh + 1, :], (2 * T, DA_V_DIM))
                                for hh in range(DA_HEADS)], axis=0)

    s_new = [jnp.where(trow >= t,
                       jnp.sum(qq * per_query_row(kn_ref, t), axis=1, keepdims=True) * scale, NEG_BIG)
             for t in range(T)]
    m = jnp.max(s_past, axis=1, keepdims=True)
    for t in range(T):
        m = jnp.maximum(m, s_new[t])
    p_past = jnp.exp(s_past - m)
    l = jnp.sum(p_past, axis=1, keepdims=True)
    v_all = vbuf[slot].reshape(n_rows, DA_V_DIM).astype(BF16)
    yield
    acc = _dot(p_past.astype(BF16), v_all)
    yield
    for t in range(T):
        p_t = jnp.exp(s_new[t] - m)
        l = l + p_t
        acc = acc + p_t * per_query_row(vn_ref, t)
    o = acc / l
    for hh in range(DA_HEADS):
        r0 = hh * 2 * T
        a = o[r0:r0 + T] - lam * o[r0 + T:r0 + 2 * T]
        out_ref[0, :, hh * DA_V_DIM:(hh + 1) * DA_V_DIM] = (
            _rms(a, gh_ref[hh:hh + 1, :]) * (1.0 - lam_init)).astype(out_ref.dtype)


def _da_sample_mlstm_prompt_kernel(
        pt_ref, q_ref, kn_ref, vn_ref, lam_ref, gh_ref, ck_hbm, cv_hbm,
        mq_ref, mk_ref, mv_ref, mo_ref, gc_ref, gr_ref, mgh_ref,
        sq_ref, sk_ref, sv_ref, so_ref, sgc_ref, sgr_ref, sc_in, sn_in, sm_in,
        out_ref, h_ref, c_out, n_out, m_out, sh_ref, sc_out, sn_out, sm_out,
        kbuf, vbuf, sem, c_scr, n_scr, m_scr, *, lam_init, layer, chunks):
    i = pl.program_id(0)

    @pl.when(i == 0)
    def _():
        c_scr[...] = jnp.zeros_like(c_scr)
        n_scr[...] = jnp.zeros_like(n_scr)
        m_scr[...] = jnp.zeros_like(m_scr)

    slot = _paged_fetch(pt_ref, ck_hbm, cv_hbm, kbuf, vbuf, sem, layer)
    _run_phases(
        _mlstm_prompt_phases(i % chunks == 0, mq_ref, mk_ref, mv_ref, mo_ref, gc_ref, gr_ref, mgh_ref,
                             h_ref, c_out, n_out, m_out, c_scr, n_scr, m_scr),
        _mlstm_sample_phases(sq_ref, sk_ref, sv_ref, so_ref, sgc_ref, sgr_ref, mgh_ref, sc_in, sn_in, sm_in,
                             sh_ref, sc_out, sn_out, sm_out),
        _da_sample_phases(slot, q_ref, kn_ref, vn_ref, lam_ref, gh_ref, out_ref, kbuf, vbuf, lam_init))


def _da_sample_mlstm_prompt(page_table, dq, dk, dv, da_lambda, g_da, cache_k, cache_v, lam_init, layer,
                            mq, mk, mv, mo, gc, gr, g_ml, Bp, Tp, sq, sk, sv, so, sgc, sgr, c0, n0, m0):
    B, T, _ = dq.shape
    rows = Bp * Tp
    assert rows % B == 0 and Tp % (rows // B) == 0, "one mLSTM chunk per sample batch element"
    L = rows // B
    chunks = Tp // L
    H, D = ML_HEADS, ML_HEAD_DIM
    n_pages = page_table.shape[1]
    page_rows = cache_k.shape[2]
    blk = pl.BlockSpec((1, T, DA_WIDTH), lambda i, pt: (i, 0, 0))
    kv_blk = pl.BlockSpec((1, T, DA_HEADS, DA_V_DIM), lambda i, pt: (i, 0, 0, 0))
    const = lambda a: pl.BlockSpec(a.shape, lambda i, pt: (0,) * a.ndim)
    mblk = lambda w: pl.BlockSpec((L, w), lambda i, pt: (i, 0))
    state = lambda *s: pl.BlockSpec((1, H) + s, lambda i, pt: (i // chunks, 0, 0, 0))
    s3 = lambda *s: pl.BlockSpec((1,) + s, lambda i, pt: (i, 0, 0))
    s4 = lambda *s: pl.BlockSpec((1, H) + s, lambda i, pt: (i, 0, 0, 0))
    sample_state = [s4(D, D), s4(1, D), s4(1, 1)]
    grid_spec = pltpu.PrefetchScalarGridSpec(
        num_scalar_prefetch=1, grid=(B,),
        in_specs=[blk, kv_blk, kv_blk, const(da_lambda), const(g_da),
                  pl.BlockSpec(memory_space=pl.ANY), pl.BlockSpec(memory_space=pl.ANY),
                  mblk(ML_WIDTH), mblk(ML_WIDTH), mblk(ML_WIDTH), mblk(ML_WIDTH), mblk(LANES),
                  pl.BlockSpec((8, L), lambda i, pt: (0, i)), const(g_ml),
                  s3(T, ML_WIDTH), s3(T, ML_WIDTH), s3(T, ML_WIDTH), s3(T, ML_WIDTH), s3(T, LANES), s3(8, T)]
                 + sample_state,
        out_specs=[blk, mblk(ML_WIDTH), state(D, D), state(1, D), state(1, 1), s3(T, ML_WIDTH)] + sample_state,
        scratch_shapes=[pltpu.VMEM((2, n_pages, page_rows, DA_V_DIM), F32),
                        pltpu.VMEM((2, n_pages, page_rows, DA_V_DIM), F32),
                        pltpu.SemaphoreType.DMA((2, 2)),
                        pltpu.VMEM((H, D, D), F32), pltpu.VMEM((H, 1, D), F32), pltpu.VMEM((H, 1, 1), F32)])
    return pl.pallas_call(
        functools.partial(_da_sample_mlstm_prompt_kernel, lam_init=lam_init, layer=layer, chunks=chunks),
        grid_spec=grid_spec,
        out_shape=[jax.ShapeDtypeStruct((B, T, DA_WIDTH), BF16),
                   jax.ShapeDtypeStruct((rows, ML_WIDTH), BF16),
                   jax.ShapeDtypeStruct((Bp, H, D, D), F32),
                   jax.ShapeDtypeStruct((Bp, H, 1, D), F32),
                   jax.ShapeDtypeStruct((Bp, H, 1, 1), F32),
                   jax.ShapeDtypeStruct((B, T, ML_WIDTH), BF16),
                   jax.ShapeDtypeStruct((B, H, D, D), F32),
                   jax.ShapeDtypeStruct((B, H, 1, D), F32),
                   jax.ShapeDtypeStruct((B, H, 1, 1), F32)],
        compiler_params=_params("arbitrary"), name="da_sample_mlstm_prompt")(
            page_table, dq, dk, dv, da_lambda, g_da, cache_k, cache_v, mq, mk, mv, mo, gc, gr, g_ml,
            sq, sk, sv, so, sgc, sgr, c0, n0, m0)


def _proj_norm_kernel(*refs, n_in, has_next):
    a_refs, w_refs = refs[:n_in], refs[n_in:2 * n_in]
    x_ref, gpost_ref, gpre_ref = refs[2 * n_in:2 * n_in + 3]
    rest = refs[2 * n_in + 3:]
    wn_ref = rest[0] if has_next else None
    xo_ref, ho_ref = rest[-2:]
    acc = _dot(a_refs[0][...], w_refs[0][...])
    for a, w in zip(a_refs[1:], w_refs[1:]):
        acc = acc + _dot(a[...], w[...])
    x1 = x_ref[...] + _rms(acc, gpost_ref[...])
    xo_ref[...] = x1
    hn = _rms(x1, gpre_ref[...]).astype(BF16)
    ho_ref[...] = (_dot(hn, wn_ref[...]) if has_next else hn).astype(ho_ref.dtype)


def _proj_norm(a_list, w_list, x, g_post, g_pre, w_next, tm, h_dtype=BF16):
    rows, d = x.shape
    n_in = len(a_list)
    has_next = w_next is not None
    row_spec = lambda w: pl.BlockSpec((tm, w), lambda i: (i, 0))
    full = lambda a: pl.BlockSpec(a.shape, lambda i: (0,) * a.ndim)
    ins = list(a_list) + list(w_list) + [x, g_post, g_pre] + ([w_next] if has_next else [])
    in_specs = ([row_spec(a.shape[1]) for a in a_list] + [full(w) for w in w_list]
                + [row_spec(d), full(g_post), full(g_pre)] + ([full(w_next)] if has_next else []))
    n_out = w_next.shape[1] if has_next else d
    return pl.pallas_call(
        functools.partial(_proj_norm_kernel, n_in=n_in, has_next=has_next), grid=(rows // tm,),
        in_specs=in_specs, out_specs=[row_spec(d), row_spec(n_out)],
        out_shape=[jax.ShapeDtypeStruct((rows, d), F32), jax.ShapeDtypeStruct((rows, n_out), h_dtype)],
        compiler_params=_params("parallel"), name="proj_norm")(*ins)


def _mem_kv_kernel(x_ref, g_ref, wk_ref, wv_ref, k_ref, v_ref, kb_ref, vb_ref):
    h = _rms(x_ref[...], g_ref[...]).astype(BF16)
    k = _dot(h, wk_ref[...])
    v = _dot(h, wv_ref[...])
    hd = k_ref.shape[2]
    for hh in range(MEM_HEADS):
        k_ref[:, hh, :] = k[:, hh * hd:(hh + 1) * hd]
        v_ref[:, hh, :] = v[:, hh * hd:(hh + 1) * hd]
    kb_ref[...], vb_ref[...] = k.astype(BF16), v.astype(BF16)


def _mem_kv(x, g, wk, wv, tm):
    rows, d = x.shape
    n = wk.shape[1]
    hd = n // MEM_HEADS
    row_spec = lambda w: pl.BlockSpec((tm, w), lambda i: (i, 0))
    state_spec = pl.BlockSpec((tm, MEM_HEADS, hd), lambda i: (i, 0, 0))
    return pl.pallas_call(
        _mem_kv_kernel, grid=(rows // tm,),
        in_specs=[row_spec(d), _resident(g), _resident(wk), _resident(wv)],
        out_specs=[state_spec, state_spec, row_spec(n), row_spec(n)],
        out_shape=[jax.ShapeDtypeStruct((rows, MEM_HEADS, hd), F32)] * 2
                  + [jax.ShapeDtypeStruct((rows, n), BF16)] * 2,
        compiler_params=_params("parallel"), name="mem_kv")(x, g, wk, wv)


def _mid_phases(rows, hml_ref, hda_ref, x_ref, woml_ref, woda_ref, gpost1_ref, gpre1_ref, wq_ref,
                mk_ref, mv_ref, wmo_ref, gpost2_ref, gpre2_ref, x2_ref, hf_ref):
    acc = _dot(hml_ref[rows, :], woml_ref[...]) + _dot(hda_ref[rows, :], woda_ref[...])
    yield
    x1 = x_ref[rows, :] + _rms(acc, gpost1_ref[...])
    qm = _dot(_rms(x1, gpre1_ref[...]).astype(BF16), wq_ref[...]).astype(BF16)
    yield
    hd = qm.shape[1] // MEM_HEADS
    hcols = [slice(hh * hd, (hh + 1) * hd) for hh in range(MEM_HEADS)]
    s = [_dot_nt(qm[:, c], mk_ref[0, :, c]) * (hd ** -0.5) for c in hcols]
    yield
    p = [jnp.exp(x - jnp.max(x, axis=1, keepdims=True)) for x in s]
    pv = [_dot(x.astype(BF16), mv_ref[0, :, c]) for x, c in zip(p, hcols)]
    yield
    o = jnp.concatenate([(a / jnp.sum(x, axis=1, keepdims=True)).astype(BF16) for a, x in zip(pv, p)], axis=1)
    acc2 = _dot(o, wmo_ref[...])
    yield
    x2 = x1 + _rms(acc2, gpost2_ref[...])
    x2_ref[rows, :] = x2
    hf_ref[rows, :] = _rms(x2, gpre2_ref[...]).astype(hf_ref.dtype)


def _cache_fetch(mk_hbm, mv_hbm, kbuf, vbuf, sem, layer):
    g = pl.program_id(0)
    bb = kbuf.shape[1]

    def copies(gi, slot):
        out = []
        for bi in range(bb):
            for hh in range(MEM_HEADS):
                b = gi * bb + bi
                out.append(pltpu.make_async_copy(mk_hbm.at[layer, b, :, hh, :], kbuf.at[slot, bi, hh],
                                                 sem.at[slot, 0]))
                out.append(pltpu.make_async_copy(mv_hbm.at[layer, b, :, hh, :], vbuf.at[slot, bi, hh],
                                                 sem.at[slot, 1]))
        return out

    @pl.when(g == 0)
    def _():
        for cp in copies(0, 0):
            cp.start()

    @pl.when(g + 1 < pl.num_programs(0))
    def _():
        for cp in copies(g + 1, (g + 1) % 2):
            cp.start()

    slot = g % 2
    for cp in copies(g, slot):
        cp.wait()
    return slot


def _mem_attn_cache_phases(slot, q_ref, o_ref, kbuf, vbuf):
    bb, hd = q_ref.shape[0], kbuf.shape[-1]
    items = [(bi, hh) for bi in range(bb) for hh in range(MEM_HEADS)]
    cols = lambda hh: slice(hh * hd, (hh + 1) * hd)
    s = [_dot_nt(q_ref[bi, :, cols(hh)].astype(BF16), kbuf[slot, bi, hh].astype(BF16)) * (hd ** -0.5)
         for bi, hh in items]
    yield
    p = [jnp.exp(x - jnp.max(x, axis=1, keepdims=True)) for x in s]
    pv = [_dot(x.astype(BF16), vbuf[slot, bi, hh].astype(BF16)) for x, (bi, hh) in zip(p, items)]
    yield
    for (bi, hh), a, x in zip(items, pv, p):
        o_ref[bi, :, cols(hh)] = (a / jnp.sum(x, axis=1, keepdims=True)).astype(o_ref.dtype)


def _mid_kernel(*refs, n_sub, layer):
    (hml_ref, hda_ref, x_ref, woml_ref, woda_ref, gpost1_ref, gpre1_ref, wq_ref, mk_ref, mv_ref, wmo_ref,
     gpost2_ref, gpre2_ref, qs_ref, ck_hbm, cv_hbm, x2_ref, hf_ref, os_ref, kbuf, vbuf, sem) = refs
    slot = _cache_fetch(ck_hbm, cv_hbm, kbuf, vbuf, sem, layer)
    sub = x_ref.shape[0] // n_sub
    prompt = refs[:13] + (x2_ref, hf_ref)
    _run_phases(_mem_attn_cache_phases(slot, qs_ref, os_ref, kbuf, vbuf),
                *[_mid_phases(slice(j * sub, (j + 1) * sub), *prompt) for j in range(n_sub)])


def _mid(h_ml, h_da, x, wo_ml, wo_da, g_post1, g_pre1, wq, mk, mv, wmo, g_post2, g_pre2, T, tm,
         q_s, cache_k, cache_v, layer):
    rows, d = x.shape
    tiles = T // tm
    steps = rows // tm
    Bs, Ts, _ = q_s.shape
    assert Bs % steps == 0, "one group of sample batch elements per prompt row tile"
    bb = Bs // steps
    M, H, hd = cache_k.shape[2:]
    row_spec = lambda w: pl.BlockSpec((tm, w), lambda i: (i, 0))
    kv_spec = pl.BlockSpec((1,) + mk.shape[1:], lambda i: (i // tiles, 0, 0))
    grp_spec = pl.BlockSpec((bb, Ts, d), lambda i: (i, 0, 0))
    hbm = pl.BlockSpec(memory_space=pl.ANY)
    ins = (h_ml, h_da, x, wo_ml, wo_da, g_post1, g_pre1, wq, mk, mv, wmo, g_post2, g_pre2)
    in_specs = [row_spec(h_ml.shape[1]), row_spec(h_da.shape[1]), row_spec(d)] + [
        kv_spec if a is mk or a is mv else _resident(a) for a in ins[3:]] + [grp_spec, hbm, hbm]
    return pl.pallas_call(
        functools.partial(_mid_kernel, n_sub=2, layer=layer), grid=(steps,),
        in_specs=in_specs, out_specs=[row_spec(d), row_spec(d), grp_spec],
        out_shape=[jax.ShapeDtypeStruct((rows, d), F32), jax.ShapeDtypeStruct((rows, d), BF16),
                   jax.ShapeDtypeStruct((Bs, Ts, d), BF16)],
        scratch_shapes=[pltpu.VMEM((2, bb, H, M, hd), F32), pltpu.VMEM((2, bb, H, M, hd), F32),
                        pltpu.SemaphoreType.DMA((2, 2))],
        compiler_params=_params("arbitrary"), name="mid")(*ins, q_s, cache_k, cache_v)


def _ffn_chunks(nh):
    nchunk = 2 if (nh // LANES) % 2 == 0 else 1
    cw = nh // nchunk
    return [(j * cw, cw) for j in range(nchunk)]


def _ffn_prompt_kernel(hf_ref, x_ref, wup_ref, wdw_ref, bdw_ref, wdn_ref, gpost_ref, y_ref, ulast_ref, ubuf,
                       *, tiles_per_seq):
    i = pl.program_id(0)
    tm = hf_ref.shape[0]
    nh = wdn_ref.shape[0]
    halo = ubuf.shape[0] - tm

    @pl.when(i % tiles_per_seq == 0)
    def _():
        ubuf[0:halo, :] = jnp.zeros((halo, ubuf.shape[1]), F32)

    hf = hf_ref[...]
    f = jnp.zeros((tm, y_ref.shape[1]), F32)
    for c0, cw in _ffn_chunks(nh):
        cg = []
        for base in (c0, nh + c0):
            cs = slice(base, base + cw)
            ubuf[halo:halo + tm, cs] = _dot(hf, wup_ref[:, cs])
            c = bdw_ref[:, cs]
            for j in range(CONV_W):
                lo = halo - (CONV_W - 1) + j
                c = c + ubuf[lo:lo + tm, cs] * wdw_ref[j:j + 1, cs]
            cg.append(c)
        act = (jax.nn.silu(cg[1]) * cg[0]).astype(BF16)
        f = f + _dot(act, wdn_ref[c0:c0 + cw, :])
    y_ref[...] = x_ref[...] + _rms(f, gpost_ref[...])
    tail = ubuf[tm:tm + halo, :]
    ubuf[0:halo, :] = tail
    ulast_ref[0] = tail


def _ffn_prompt(hf, x, w_up, w_dw, b_dw, w_down, g_post, B, T, tm):
    rows, d = x.shape
    npad = w_up.shape[1]
    tiles = T // tm
    halo = 8
    row_spec = lambda w: pl.BlockSpec((tm, w), lambda i: (i, 0))
    full = _resident
    return pl.pallas_call(
        functools.partial(_ffn_prompt_kernel, tiles_per_seq=tiles), grid=(rows // tm,),
        in_specs=[row_spec(d), row_spec(d), full(w_up), full(w_dw), full(b_dw), full(w_down), full(g_post)],
        out_specs=[row_spec(d), pl.BlockSpec((1, halo, npad), lambda i: (i // tiles, 0, 0))],
        out_shape=[jax.ShapeDtypeStruct((rows, d), F32), jax.ShapeDtypeStruct((B, halo, npad), F32)],
        scratch_shapes=[pltpu.VMEM((tm + halo, npad), F32)],
        compiler_params=_params("arbitrary"), name="ffn_prompt")(hf, x, w_up, w_dw, b_dw, w_down, g_post)


def _ffn_sample_kernel(hf_ref, x_ref, cb_ref, wup_ref, wdw_ref, bdw_ref, wdn_ref, gpost_ref, y_ref, unew_ref,
                       *, T):
    nb = hf_ref.shape[0] // T
    nh = wdn_ref.shape[0]
    hf = hf_ref[...]
    f = jnp.zeros(y_ref.shape, F32)
    for c0, cw in _ffn_chunks(nh):
        cg = []
        for base in (c0, nh + c0):
            cs = slice(base, base + cw)
            u = _dot(hf, wup_ref[:, cs])
            ext = [cb_ref[j, :, cs] for j in range(CONV_W - 1)] + [u[t * nb:(t + 1) * nb] for t in range(T)]
            for j in range(CONV_W - 1):
                unew_ref[j, :, cs] = ext[len(ext) - (CONV_W - 1) + j]
            rows = []
            for t in range(T):
                c = bdw_ref[:, cs]
                for j in range(CONV_W):
                    c = c + ext[t + j] * wdw_ref[j:j + 1, cs]
                rows.append(c)
            cg.append(jnp.concatenate(rows, axis=0))
        act = (jax.nn.silu(cg[1]) * cg[0]).astype(BF16)
        f = f + _dot(act, wdn_ref[c0:c0 + cw, :])
    y_ref[...] = x_ref[...] + _rms(f, gpost_ref[...])


def _ffn_sample(hf, x, cb, w_up, w_dw, b_dw, w_down, g_post, T):
    rows, d = x.shape
    npad = w_up.shape[1]
    ins = (hf, x, cb, w_up, w_dw, b_dw, w_down, g_post)
    full = lambda a: pl.BlockSpec(a.shape, lambda i: (0,) * a.ndim)
    return pl.pallas_call(
        functools.partial(_ffn_sample_kernel, T=T), grid=(1,),
        in_specs=[full(a) for a in ins],
        out_specs=[pl.BlockSpec((rows, d), lambda i: (0, 0)),
                   pl.BlockSpec((CONV_W - 1, rows // T, npad), lambda i: (0, 0, 0))],
        out_shape=[jax.ShapeDtypeStruct((rows, d), F32),
                   jax.ShapeDtypeStruct((CONV_W - 1, rows // T, npad), F32)],
        compiler_params=_params("arbitrary"), name="ffn_sample")(*ins)


def _pad_halves(a, nh, nh_pad):
    pad = [(0, 0)] * (a.ndim - 1) + [(0, nh_pad - nh)]
    return jnp.concatenate([jnp.pad(a[..., :nh], pad), jnp.pad(a[..., nh:], pad)], axis=-1)


def _unpad_halves(a, nh, nh_pad):
    return jnp.concatenate([a[..., :nh], a[..., nh_pad:nh_pad + nh]], axis=-1)


def _row_tile(rows, want):
    t = min(rows, want)
    while rows % t:
        t //= 2
    return t


def kernel(x_prompt, x_sample, cache_dk, cache_dv, cache_mem_k, cache_mem_v, state_ml_C, state_ml_n, state_ml_m, state_conv, page_table, mem_prompt, g_mix_pre, g_mix_post, w_in, b_if, g_ml_head, da_lambda, g_da_head, w_out, g_mem_pre, g_mem_post, g_mem_src, w_mq, w_mk, w_mv, w_mo, g_ffn_pre, g_ffn_post, w_up, w_dw, b_dw, w_down):
    depth = w_in.shape[0]
    Bp, Tp, d = x_prompt.shape
    Bs, Ts, _ = x_sample.shape
    n_pages, page = page_table.shape[1], cache_dk.shape[2]
    past_len = n_pages * page
    n_mem = mem_prompt.shape[1]
    nh = w_down.shape[1]
    nh_pad = -(-nh // LANES) * LANES
    H, D = ML_HEADS, ML_HEAD_DIM
    rows_p, rows_s = Bp * Tp, Bs * Ts

    tm_p = _row_tile(Tp, 512)
    tm_s = _row_tile(rows_s, 256)
    tab_p = _rope_tables(Tp, Tp, 0)
    tab_s = _rope_tables(tm_s, Ts, past_len)

    yp = x_prompt.reshape(rows_p, d)
    ys = x_sample.reshape(rows_s, d)
    outs = [[] for _ in range(14)]
    row = lambda a: a.reshape(1, -1)
    for l in range(depth):
        lam_init = 0.8 - 0.6 * math.exp(-0.3 * l)
        wi = w_in[l]
        w_ml = wi[:, :4 * ML_WIDTH].astype(BF16)
        w_da = wi[:, 4 * ML_WIDTH + 2 * H:].astype(BF16)
        w_gate = wi[:, 4 * ML_WIDTH:4 * ML_WIDTH + 2 * H]
        w_g = jnp.pad(w_gate, ((0, 0), (0, LANES - 2 * H))).astype(BF16)
        b_col = jnp.pad(b_if[l], (0, LANES - 2 * H)).reshape(1, LANES)
        wo_ml, wo_da = w_out[l][:ML_WIDTH].astype(BF16), w_out[l][ML_WIDTH:].astype(BF16)
        wq_b, wo_b = w_mq[l].astype(BF16), w_mo[l].astype(BF16)
        wk_b, wv_b = w_mk[l].astype(BF16), w_mv[l].astype(BF16)
        wup_b = _pad_halves(w_up[l], nh, nh_pad).astype(BF16)
        wdw_p = _pad_halves(w_dw[l], nh, nh_pad)
        bdw_p = _pad_halves(b_dw[l].reshape(1, -1), nh, nh_pad)
        wdn_b = jnp.pad(w_down[l], ((0, nh_pad - nh), (0, 0))).astype(BF16)
        g_da3 = g_da_head[l].reshape(DA_HEADS, 1, DA_V_DIM)

        q, k, v, o, gc, gr, dq, dk, dv, dkb, dvb = _in_proj(
            yp, row(g_mix_pre[l]), w_ml, w_da, w_g, b_col, tab_p, BF16, _row_tile(Tp, 1024),
            DA_QK_DIM ** -0.5 * math.log2(math.e))
        qs, ks, vs, os_, gcs, grs, dqs, dk_s, dv_s, _, _ = _in_proj(
            ys, row(g_mix_pre[l]), w_ml, w_da, w_g, b_col, tab_s, F32, tm_s, 1.0)
        r3 = lambda a: a.reshape(Bs, Ts, a.shape[-1])
        r4 = lambda a: a.reshape(Bs, Ts, DA_HEADS, DA_V_DIM)
        gr3 = grs.reshape(2 * H, Bs, Ts).transpose(1, 0, 2)
        h_da_s, h_ml, C_p, n_p, m_p, h_ml_s, C_s, n_s, m_s = _da_sample_mlstm_prompt(
            page_table, r3(dqs), r4(dk_s), r4(dv_s), da_lambda[l], g_da_head[l],
            cache_dk.reshape(depth, -1, page * DA_HEADS, DA_V_DIM),
            cache_dv.reshape(depth, -1, page * DA_HEADS, DA_V_DIM), lam_init, l,
            q, k, v, o, gc, gr, g_ml_head[l], Bp, Tp,
            r3(qs), r3(ks), r3(vs), r3(os_), r3(gcs), gr3,
            state_ml_C[l], state_ml_n[l].reshape(Bs, H, 1, D), state_ml_m[l].reshape(Bs, H, 1, 1))

        x1_s, qm_s = _proj_norm([h_ml_s.reshape(rows_s, ML_WIDTH), h_da_s.reshape(rows_s, DA_WIDTH)],
                                [wo_ml, wo_da], ys, row(g_mix_post[l]), row(g_mem_pre[l]), wq_b, tm_s,
                                h_dtype=F32)

        h_da = _da_prompt(dq, dkb, dvb, da_lambda[l], g_da3, Bp, Tp, _row_tile(Tp, 512), 2, lam_init)
        mk_p, mv_p, mkb, mvb = _mem_kv(mem_prompt.reshape(Bp * n_mem, d), row(g_mem_src[l]), wk_b, wv_b,
                                       _row_tile(Bp * n_mem, 256))
        x2, hf, om_s = _mid(h_ml, h_da, yp, wo_ml, wo_da, row(g_mix_post[l]), row(g_mem_pre[l]), wq_b,
                            mkb.reshape(Bp, n_mem, d), mvb.reshape(Bp, n_mem, d), wo_b,
                            row(g_mem_post[l]), row(g_ffn_pre[l]), Tp, tm_p,
                            qm_s.reshape(Bs, Ts, d), cache_mem_k, cache_mem_v, l)
        yp, ulast = _ffn_prompt(hf, x2, wup_b, wdw_p, bdw_p, wdn_b, row(g_ffn_post[l]), Bp, Tp, tm_p)
        cv_p = _unpad_halves(ulast[:, ulast.shape[1] - (CONV_W - 1):], nh, nh_pad)

        x2, hf = _proj_norm([om_s.reshape(rows_s, d)], [wo_b], x1_s, row(g_mem_post[l]), row(g_ffn_pre[l]),
                            None, tm_s)
        tmaj = lambda a: a.reshape(Bs, Ts, -1).transpose(1, 0, 2).reshape(rows_s, -1)
        cb = _pad_halves(state_conv[l], nh, nh_pad).transpose(1, 0, 2)
        y_t, unew = _ffn_sample(tmaj(hf), tmaj(x2), cb, wup_b, wdw_p, bdw_p, wdn_b, row(g_ffn_post[l]), Ts)
        ys = y_t.reshape(Ts, Bs, d).transpose(1, 0, 2).reshape(rows_s, d)
        cv_s = _unpad_halves(unew.transpose(1, 0, 2), nh, nh_pad)

        vals = (dk.reshape(Bp, Tp, DA_HEADS, DA_V_DIM), dv.reshape(Bp, Tp, DA_HEADS, DA_V_DIM),
                mk_p.reshape(Bp, n_mem, MEM_HEADS, d // MEM_HEADS), mv_p.reshape(Bp, n_mem, MEM_HEADS, d // MEM_HEADS),
                C_p, n_p.reshape(Bp, H, D), m_p.reshape(Bp, H), cv_p,
                dk_s.reshape(Bs, Ts, DA_HEADS, DA_V_DIM), dv_s.reshape(Bs, Ts, DA_HEADS, DA_V_DIM),
                C_s, n_s.reshape(Bs, H, D), m_s.reshape(Bs, H), cv_s)
        for acc, val in zip(outs, vals):
            acc.append(val)
    return (yp.reshape(Bp, Tp, d), ys.reshape(Bs, Ts, d)) + tuple(jnp.stack(a) for a in outs)
```
